```python
import math
import jax
import jax.numpy as jnp
from jax import lax
import numpy as np


D_MODEL = 2048
BATCH = 16
SEQ = 2048
DEPTH = 2

GRID_W = 64
CTX_LEN = 256
EPS = 1e-6
NEG_INF = -1e30

N_HEADS = 16
N_KV_HEADS = 4
GQA_GROUP = N_HEADS // N_KV_HEADS
HEAD_DIM = 64
ATTN_W = N_HEADS * HEAD_DIM
KV_W = N_KV_HEADS * HEAD_DIM
WINDOW = 128
BLOCK = 128
ROPE_FREQS = HEAD_DIM // 4
ROPE_BASE = 10000.0

HYENA_W = D_MODEL // 4
HYENA_ORDER = 2
FILTER_BANDS = 16
FILTER_EMB = 1 + 2 * FILTER_BANDS
FILTER_HIDDEN = 64
FILTER_INNER = 2
DECAY_TARGET = 1e-2
FAST_DECAY_PCT = 0.3
SLOW_DECAY_PCT = 1.5

POOL_W = D_MODEL // 4
POOL_WINDOWS = (2, 4, 8, 16)
POOL_GROUP = POOL_W // len(POOL_WINDOWS)

N_BRANCH = 3
D_FF = 4 * D_MODEL

Q_OFF = 0
K_OFF = Q_OFF + ATTN_W
V_OFF = K_OFF + KV_W
HY_OFF = V_OFF + KV_W
POOL_OFF = HY_OFF + 3 * HYENA_W
GATE_OFF = POOL_OFF + POOL_W
IN_W = GATE_OFF + N_BRANCH * D_MODEL

kernel_name = 'hybrid_dit_attn_hyena_pool'


def _rms_norm(x, g):
    xf = x.astype(jnp.float32)
    y = xf * lax.rsqrt(jnp.mean(xf * xf, axis=-1, keepdims=True) + EPS)
    return y.astype(x.dtype) * g


def _modulation(cvec, w_mod, b_mod):
    m = jax.nn.silu(cvec) @ w_mod + b_mod
    return jnp.split(m, 6, axis=-1)


def _split_in(z):
    return (z[..., Q_OFF:K_OFF], z[..., K_OFF:V_OFF], z[..., V_OFF:HY_OFF],
            z[..., HY_OFF:POOL_OFF], z[..., POOL_OFF:GATE_OFF], z[..., GATE_OFF:IN_W])


def _heads(z, n_heads):
    return z.reshape(z.shape[0], z.shape[1], n_heads, HEAD_DIM)


def _axial_rope_angles(L):
    rows = L // GRID_W
    row = jnp.repeat(jnp.arange(rows, dtype=jnp.float32), GRID_W)
    col = jnp.tile(jnp.arange(GRID_W, dtype=jnp.float32), rows)
    inv = ROPE_BASE ** (-jnp.arange(ROPE_FREQS, dtype=jnp.float32) / ROPE_FREQS)
    return jnp.stack([row[:, None] * inv, col[:, None] * inv], axis=1)


def _apply_rope(x, ang):
    B, L, H, D = x.shape
    xs = x.reshape(B, L, H, 2, 2, ROPE_FREQS)
    cos = jnp.cos(ang)[None, :, None].astype(x.dtype)
    sin = jnp.sin(ang)[None, :, None].astype(x.dtype)
    x1, x2 = xs[..., 0, :], xs[..., 1, :]
    out = jnp.stack([x1 * cos - x2 * sin, x2 * cos + x1 * sin], axis=-2)
    return out.reshape(B, L, H, D)


def _sink_logits(sink, B, Q):
    s = sink.astype(jnp.float32).reshape(N_KV_HEADS, GQA_GROUP)[None, :, :, None, None]
    return jnp.broadcast_to(s, (B, N_KV_HEADS, GQA_GROUP, Q, 1))


def _latent_window_attention(q, k, v, kc, vc, sink):
    B, L, H, D = q.shape
    Lc = kc.shape[1]
    nb = L // BLOCK
    scale = D ** -0.5
    pad = ((0, 0), (BLOCK, BLOCK), (0, 0), (0, 0))
    kp = jnp.pad(k, pad)
    vp = jnp.pad(v, pad)
    qi = jnp.arange(BLOCK)[:, None]
    kj = jnp.arange(3 * BLOCK)[None, :]
    band = jnp.abs(BLOCK + qi - kj) <= WINDOW
    sink_b = _sink_logits(sink, B, BLOCK)

    def block(n):
        start = n * BLOCK
        qn = lax.dynamic_slice_in_dim(q, start, BLOCK, axis=1).reshape(B, BLOCK, N_KV_HEADS, GQA_GROUP, D)
        kn = lax.dynamic_slice_in_dim(kp, start, 3 * BLOCK, axis=1)
        vn = lax.dynamic_slice_in_dim(vp, start, 3 * BLOCK, axis=1)
        kpos = start - BLOCK + kj
        mask = band & (kpos >= 0) & (kpos < L)
        s_loc = jnp.einsum('bqhgd,bkhd->bhgqk', qn, kn, preferred_element_type=jnp.float32) * scale
        s_loc = jnp.where(mask, s_loc, NEG_INF)
        s_ctx = jnp.einsum('bqhgd,bchd->bhgqc', qn, kc, preferred_element_type=jnp.float32) * scale
        p = jax.nn.softmax(jnp.concatenate([s_loc, s_ctx, sink_b], axis=-1), axis=-1).astype(v.dtype)
        o = (jnp.einsum('bhgqk,bkhd->bqhgd', p[..., :3 * BLOCK], vn)
             + jnp.einsum('bhgqc,bchd->bqhgd', p[..., 3 * BLOCK:3 * BLOCK + Lc], vc))
        return o.reshape(B, BLOCK, H * D)

    out = lax.map(block, jnp.arange(nb))
    return out.transpose(1, 0, 2, 3).reshape(B, L, H * D)


def _context_attention(qc, kc, vc, sink):
    B, Lc, H, D = qc.shape
    qg = qc.reshape(B, Lc, N_KV_HEADS, GQA_GROUP, D)
    s = jnp.einsum('bqhgd,bkhd->bhgqk', qg, kc, preferred_element_type=jnp.float32) * D ** -0.5
    p = jax.nn.softmax(jnp.concatenate([s, _sink_logits(sink, B, Lc)], axis=-1), axis=-1).astype(vc.dtype)
    o = jnp.einsum('bhgqk,bkhd->bqhgd', p[..., :Lc], vc)
    return o.reshape(B, Lc, H * D)


def _hyena_filters(L, w0, b0, w1, b1, freq, w2):
    t = jnp.linspace(0.0, 1.0, L, dtype=jnp.float32)[:, None]
    w = 2.0 * math.pi * jnp.arange(L, dtype=jnp.float32)[:, None] / L
    bands = jnp.linspace(1e-4, FILTER_BANDS - 1, FILTER_BANDS, dtype=jnp.float32)[None, :]
    z = jnp.concatenate([t, jnp.cos(bands * w), -jnp.sin(bands * w)], axis=-1)
    h = jnp.sin(freq * (z @ w0 + b0))
    for i in range(FILTER_INNER):
        h = jnp.sin(freq * (h @ w1[i] + b1[i]))
    h = (h @ w2).reshape(L, HYENA_ORDER, 2, HYENA_W)
    deltas = jnp.linspace(math.log(DECAY_TARGET) / SLOW_DECAY_PCT, math.log(DECAY_TARGET) / FAST_DECAY_PCT,
                          HYENA_W, dtype=jnp.float32)
    decay = jnp.exp(-t * jnp.abs(deltas))
    return h * decay[:, None, None, :]


def _bidir_fftconv(u, h_fwd, h_bwd, d_skip):
    L, C = h_fwd.shape
    n = 2 * L
    k = jnp.concatenate([h_fwd, jnp.zeros((1, C), h_fwd.dtype), h_bwd[:L - 1][::-1]], axis=0)
    k_f = jnp.fft.rfft(k.astype(jnp.float32), n=n, axis=0)
    u_f = jnp.fft.rfft(u.astype(jnp.float32), n=n, axis=1)
    y = jnp.fft.irfft(u_f * k_f[None], n=n, axis=1)[:, :L]
    return (y + u.astype(jnp.float32) * d_skip).astype(u.dtype)


def _hyena_mixer(u, conv_w, conv_b, filters, d_skip):
    L = u.shape[1]
    up = jnp.pad(u, ((0, 0), (1, 1), (0, 0)))
    uc = up[:, :L] * conv_w[0] + up[:, 1:L + 1] * conv_w[1] + up[:, 2:] * conv_w[2] + conv_b
    v, x1, x2 = jnp.split(uc, 3, axis=-1)
    z = v
    for o, gate in enumerate((x1, x2)):
        z = gate * _bidir_fftconv(z, filters[:, o, 0], filters[:, o, 1], d_skip[o])
    return z


def _multiscale_pool(u, w_grp, scale):
    B, L, _ = u.shape
    ug = u.reshape(B, L, len(POOL_WINDOWS), POOL_GROUP)
    csum = jnp.cumsum(ug.astype(jnp.float32), axis=1)
    S = jnp.concatenate([jnp.zeros((B, 1, len(POOL_WINDOWS), POOL_GROUP), jnp.float32), csum], axis=1)
    t = jnp.arange(L)
    pooled = []
    for g, win in enumerate(POOL_WINDOWS):
        a = jnp.clip(t - win // 2, 0, L)
        b = jnp.clip(t + win // 2, 0, L)
        cnt = (b - a).astype(jnp.float32)[None, :, None]
        pooled.append((S[:, b, g] - S[:, a, g]) / cnt)
    pooled = jnp.stack(pooled, axis=2)
    y = jnp.einsum('blgc,gcd->blgd', pooled - ug.astype(jnp.float32), w_grp)
    return (y.reshape(B, L, POOL_W) * scale).astype(u.dtype)


def _merge(y_att, y_hy, y_pool, gates, w_att_o, w_hy_o, w_pool_o, w_out):
    g_att, g_hy, g_pool = jnp.split(jax.nn.sigmoid(gates), N_BRANCH, axis=-1)
    m = g_att * (y_att @ w_att_o) + g_hy * (y_hy @ w_hy_o) + g_pool * (y_pool @ w_pool_o)
    return m @ w_out


def _sqrelu_mlp(h, w1, w2):
    return jnp.square(jax.nn.relu(h @ w1)) @ w2


def setup_inputs(seed: int = 0) -> dict:
    key = jax.random.key(seed)
    ks = jax.random.split(key, 32)

    def nrm(k, shape, scale):
        return jax.random.normal(k, shape, jnp.float32) * scale

    def gain(k, shape):
        return 1.0 + 0.1 * jax.random.normal(k, shape, jnp.float32)

    return {
        'x': nrm(ks[0], (BATCH, SEQ, D_MODEL), 1.0),
        'c': nrm(ks[1], (BATCH, D_MODEL), 1.0),
        'ctx': nrm(ks[2], (BATCH, CTX_LEN, D_MODEL), 1.0),
        'c_ctx': nrm(ks[3], (D_MODEL,), 1.0),
        'norm1_g': gain(ks[4], (DEPTH, D_MODEL)),
        'norm2_g': gain(ks[5], (DEPTH, D_MODEL)),
        'w_mod': nrm(ks[6], (DEPTH, D_MODEL, 6 * D_MODEL), 0.5 * D_MODEL ** -0.5),
        'b_mod': nrm(ks[7], (DEPTH, 6 * D_MODEL), 0.02),
        'w_in': nrm(ks[8], (DEPTH, D_MODEL, IN_W), D_MODEL ** -0.5),
        'q_norm_g': gain(ks[9], (DEPTH, HEAD_DIM)),
        'k_norm_g': gain(ks[10], (DEPTH, HEAD_DIM)),
        'sink': nrm(ks[11], (DEPTH, N_HEADS), 0.5),
        'hy_conv_w': nrm(ks[12], (DEPTH, 3, 3 * HYENA_W), 3.0 ** -0.5),
        'hy_conv_b': nrm(ks[13], (DEPTH, 3 * HYENA_W), 0.02),
        'filt_w0': nrm(ks[14], (DEPTH, FILTER_EMB, FILTER_HIDDEN), FILTER_EMB ** -0.5),
        'filt_b0': nrm(ks[15], (DEPTH, FILTER_HIDDEN), 0.1),
        'filt_w1': nrm(ks[16], (DEPTH, FILTER_INNER, FILTER_HIDDEN, FILTER_HIDDEN), FILTER_HIDDEN ** -0.5),
        'filt_b1': nrm(ks[17], (DEPTH, FILTER_INNER, FILTER_HIDDEN), 0.1),
        'filt_freq': gain(ks[18], (DEPTH, FILTER_HIDDEN)),
        'filt_w2': nrm(ks[19], (DEPTH, FILTER_HIDDEN, HYENA_ORDER * 2 * HYENA_W), 0.05 * FILTER_HIDDEN ** -0.5),
        'hy_bias': nrm(ks[20], (DEPTH, HYENA_ORDER, HYENA_W), 0.5),
        'pool_w': nrm(ks[21], (DEPTH, len(POOL_WINDOWS), POOL_GROUP, POOL_GROUP), POOL_GROUP ** -0.5),
        'pool_scale': gain(ks[22], (DEPTH, POOL_W)),
        'w_att_o': nrm(ks[23], (DEPTH, ATTN_W, D_MODEL), ATTN_W ** -0.5),
        'w_hy_o': nrm(ks[24], (DEPTH, HYENA_W, D_MODEL), HYENA_W ** -0.5),
        'w_pool_o': nrm(ks[25], (DEPTH, POOL_W, D_MODEL), POOL_W ** -0.5),
        'w_out': nrm(ks[26], (DEPTH, D_MODEL, D_MODEL), D_MODEL ** -0.5),
        'mlp_w1': nrm(ks[27], (DEPTH, D_MODEL, D_FF), D_MODEL ** -0.5),
        'mlp_w2': nrm(ks[28], (DEPTH, D_FF, D_MODEL), D_FF ** -0.5),
    }


def reference(x, c, ctx, c_ctx, norm1_g, norm2_g, w_mod, b_mod, w_in, q_norm_g, k_norm_g, sink,
              hy_conv_w, hy_conv_b, filt_w0, filt_b0, filt_w1, filt_b1, filt_freq, filt_w2, hy_bias,
              pool_w, pool_scale, w_att_o, w_hy_o, w_pool_o, w_out, mlp_w1, mlp_w2):
    L = x.shape[1]
    Lc = ctx.shape[1]
    ang = _axial_rope_angles(L)
    c_lat = c[:, None, :]
    c_con = c_ctx[None, None, :]
    for l in range(DEPTH):
        last = l == DEPTH - 1
        filt = (filt_w0[l], filt_b0[l], filt_w1[l], filt_b1[l], filt_freq[l], filt_w2[l])
        sh1, sc1, ga1, sh2, sc2, ga2 = _modulation(c_lat, w_mod[l], b_mod[l])
        csh1, csc1, cga1, csh2, csc2, cga2 = _modulation(c_con, w_mod[l], b_mod[l])
        hx = _rms_norm(x, norm1_g[l]) * (1.0 + sc1) + sh1
        hc = _rms_norm(ctx, norm1_g[l]) * (1.0 + csc1) + csh1

        if last:
            kc_raw, vc_raw = jnp.split(hc @ w_in[l][:, K_OFF:HY_OFF], 2, axis=-1)
        else:
            qc_raw, kc_raw, vc_raw, hyc, poolc, gc = _split_in(hc @ w_in[l])
        kc = _rms_norm(_heads(kc_raw, N_KV_HEADS), k_norm_g[l])
        vc = _heads(vc_raw, N_KV_HEADS)

        qx_raw, kx_raw, vx_raw, hyx, poolx, gx = _split_in(hx @ w_in[l])
        qx = _apply_rope(_rms_norm(_heads(qx_raw, N_HEADS), q_norm_g[l]), ang)
        kx = _apply_rope(_rms_norm(_heads(kx_raw, N_KV_HEADS), k_norm_g[l]), ang)
        vx = _heads(vx_raw, N_KV_HEADS)
        y_att = _latent_window_attention(qx, kx, vx, kc, vc, sink[l])
        y_hy = _hyena_mixer(hyx, hy_conv_w[l], hy_conv_b[l], _hyena_filters(L, *filt), hy_bias[l])
        y_pool = _multiscale_pool(poolx, pool_w[l], pool_scale[l])
        x_new = x + ga1 * _merge(y_att, y_hy, y_pool, gx, w_att_o[l], w_hy_o[l], w_pool_o[l], w_out[l])
        x_new = x_new + ga2 * _sqrelu_mlp(_rms_norm(x_new, norm2_g[l]) * (1.0 + sc2) + sh2, mlp_w1[l], mlp_w2[l])

        if not last:
            qc = _rms_norm(_heads(qc_raw, N_HEADS), q_norm_g[l])
            yc_att = _context_attention(qc, kc, vc, sink[l])
            yc_hy = _hyena_mixer(hyc, hy_conv_w[l], hy_conv_b[l], _hyena_filters(Lc, *filt), hy_bias[l])
            yc_pool = _multiscale_pool(poolc, pool_w[l], pool_scale[l])
            ctx_new = ctx + cga1 * _merge(yc_att, yc_hy, yc_pool, gc, w_att_o[l], w_hy_o[l], w_pool_o[l], w_out[l])
            ctx = ctx_new + cga2 * _sqrelu_mlp(_rms_norm(ctx_new, norm2_g[l]) * (1.0 + csc2) + csh2,
                                               mlp_w1[l], mlp_w2[l])
        x = x_new
    return x
```

```python
import functools
import math

import jax
import jax.numpy as jnp
from jax import lax
from jax.experimental import pallas as pl
from jax.experimental.pallas import tpu as pltpu

D_MODEL = 2048
DEPTH = 2
GRID_W = 64
EPS = 1e-6
NEG_INF = -1e30

N_HEADS = 16
N_KV_HEADS = 4
GQA_GROUP = N_HEADS // N_KV_HEADS
HEAD_DIM = 64
ATTN_W = N_HEADS * HEAD_DIM
KV_W = N_KV_HEADS * HEAD_DIM
WINDOW = 128
ROPE_FREQS = HEAD_DIM // 4
ROPE_BASE = 10000.0

HYENA_W = D_MODEL // 4
HYENA_ORDER = 2
FILTER_BANDS = 16
FILTER_EMB = 1 + 2 * FILTER_BANDS
FILTER_HIDDEN = 64
FILTER_INNER = 2
DECAY_TARGET = 1e-2
FAST_DECAY_PCT = 0.3
SLOW_DECAY_PCT = 1.5

POOL_W = D_MODEL // 4
POOL_WINDOWS = (2, 4, 8, 16)
POOL_GROUP = POOL_W // len(POOL_WINDOWS)

N_BRANCH = 3
D_FF = 4 * D_MODEL

Q_OFF = 0
K_OFF = Q_OFF + ATTN_W
V_OFF = K_OFF + KV_W
HY_OFF = V_OFF + KV_W
POOL_OFF = HY_OFF + 3 * HYENA_W
GATE_OFF = POOL_OFF + POOL_W
IN_W = GATE_OFF + N_BRANCH * D_MODEL

V7X_LANES = 128
V7X_VMEM_LIMIT = 56 * 1024 * 1024
KV_DUP_W = N_KV_HEADS * V7X_LANES
MOD_ROWS = 24

F32 = jnp.float32
BF16 = jnp.bfloat16
HIGHEST = lax.Precision.HIGHEST


def _params(*semantics):
    return pltpu.CompilerParams(dimension_semantics=semantics, vmem_limit_bytes=V7X_VMEM_LIMIT)


def _const_spec(shape):
    zeros = (0,) * len(shape)
    return pl.BlockSpec(shape, lambda *_: zeros, pipeline_mode=pl.Buffered(1))


def _mod_spec(arr, rows_per_mod_tile):
    d = arr.shape[-1]
    if arr.shape[0] == 1:
        return pl.BlockSpec((1, 1, d), lambda i, *_: (0, 0, 0))
    return pl.BlockSpec((1, 1, d), lambda i, *_: (i // rows_per_mod_tile, 0, 0))


def _norm_mod(xf, g, sc, sh):
    y = xf * lax.rsqrt(jnp.mean(xf * xf, axis=-1, keepdims=True) + EPS)
    return (y * g) * (1.0 + sc) + sh


def _mod_body(c_ref, w_ref, b_ref, o_ref):
    c = c_ref[...]
    s = c * jax.nn.sigmoid(c)
    o_ref[0] = jnp.dot(s.astype(BF16), w_ref[0].astype(BF16), preferred_element_type=F32) + b_ref[0]


def _modulation(cc, w_mod, b_mod):
    depth, d, n = w_mod.shape
    tn = 1024
    return pl.pallas_call(
        _mod_body,
        grid=(depth, n // tn),
        in_specs=[
            pl.BlockSpec((MOD_ROWS, d), lambda l, j: (0, 0)),
            pl.BlockSpec((1, d, tn), lambda l, j: (l, 0, j)),
            pl.BlockSpec((1, 1, tn), lambda l, j: (l, 0, j)),
        ],
        out_specs=pl.BlockSpec((1, MOD_ROWS, tn), lambda l, j: (l, 0, j)),
        out_shape=jax.ShapeDtypeStruct((depth, MOD_ROWS, n), F32),
        compiler_params=_params("arbitrary", "arbitrary"),
    )(cc, w_mod, b_mod.reshape(depth, 1, n))


def _norm_body(x_ref, g_ref, sc_ref, sh_ref, o_ref):
    o_ref[...] = _norm_mod(x_ref[...], g_ref[...], sc_ref[0], sh_ref[0]).astype(o_ref.dtype)


def _norm_call(x2, g, sc, sh, rows_per_batch):
    m, d = x2.shape
    tm = min(512, rows_per_batch)
    return pl.pallas_call(
        _norm_body,
        grid=(m // tm,),
        in_specs=[
            pl.BlockSpec((tm, d), lambda i: (i, 0)),
            pl.BlockSpec((1, d), lambda i: (0, 0)),
            _mod_spec(sc, rows_per_batch // tm),
            _mod_spec(sh, rows_per_batch // tm),
        ],
        out_specs=pl.BlockSpec((tm, d), lambda i: (i, 0)),
        out_shape=jax.ShapeDtypeStruct((m, d), BF16),
        compiler_params=_params("arbitrary"),
    )(x2, g.reshape(1, d), sc, sh)


def _proj_body(a_ref, w_ref, o_ref, *, sigmoid):
    z = jnp.dot(a_ref[...], w_ref[...], preferred_element_type=F32)
    if sigmoid:
        z = jax.nn.sigmoid(z)
    o_ref[...] = z.astype(o_ref.dtype)


def _proj_call(a, w, col0, n, out_dtype, sigmoid=False):
    m, k = a.shape
    tm = min(2048, m)
    tn = 512
    c0 = col0 // tn
    return pl.pallas_call(
        functools.partial(_proj_body, sigmoid=sigmoid),
        grid=(m // tm, n // tn),
        in_specs=[
            pl.BlockSpec((tm, k), lambda i, j: (i, 0)),
            pl.BlockSpec((k, tn), lambda i, j: (0, c0 + j)),
        ],
        out_specs=pl.BlockSpec((tm, tn), lambda i, j: (i, j)),
        out_shape=jax.ShapeDtypeStruct((m, n), out_dtype),
        compiler_params=_params("arbitrary", "arbitrary"),
    )(a, w)


def _pair_block_diag():
    r = lax.broadcasted_iota(jnp.int32, (V7X_LANES, V7X_LANES), 0) // HEAD_DIM
    c = lax.broadcasted_iota(jnp.int32, (V7X_LANES, V7X_LANES), 1) // HEAD_DIM
    return (r == c).astype(F32)


def _head_norm(x, g, bd):
    ss = jnp.dot(x * x, bd, precision=HIGHEST, preferred_element_type=F32)
    return (x * lax.rsqrt(ss * (1.0 / HEAD_DIM) + EPS)) * g


def _rope(x, cos, sin_up, sin_dn):
    up = pltpu.roll(x, V7X_LANES - ROPE_FREQS, 1)
    dn = pltpu.roll(x, ROPE_FREQS, 1)
    return x * cos + up * sin_up + dn * sin_dn


def _dup_pair(x, low):
    r = pltpu.roll(x, HEAD_DIM, 1)
    return jnp.where(low, x, r), jnp.where(low, r, x)


def _qkv_body(*refs, has_q, rope):
    it = iter(refs)
    z_ref = next(it)
    gq_ref = next(it) if has_q else None
    gk_ref = next(it)
    if rope:
        cos_ref, sup_ref, sdn_ref = next(it), next(it), next(it)
    q_ref = next(it) if has_q else None
    k_ref, v_ref = next(it), next(it)

    bd = _pair_block_diag()
    low = lax.broadcasted_iota(jnp.int32, (1, V7X_LANES), 1) < HEAD_DIM
    k0 = ATTN_W if has_q else 0
    v0 = k0 + KV_W

    def prep(x, g):
        y = _head_norm(x, g, bd)
        if rope:
            y = _rope(y, cos_ref[...], sup_ref[...], sdn_ref[...])
        return y

    if has_q:
        for s in range(ATTN_W // V7X_LANES):
            sl = slice(s * V7X_LANES, (s + 1) * V7X_LANES)
            y = prep(z_ref[0, :, sl], gq_ref[...])
            q_ref[0, :, sl] = (y * (HEAD_DIM ** -0.5)).astype(q_ref.dtype)
    for s in range(KV_W // V7X_LANES):
        kp = prep(z_ref[0, :, k0 + s * V7X_LANES:k0 + (s + 1) * V7X_LANES], gk_ref[...])
        vp = z_ref[0, :, v0 + s * V7X_LANES:v0 + (s + 1) * V7X_LANES]
        for src, dst in ((kp, k_ref), (vp, v_ref)):
            a, b = _dup_pair(src, low)
            base = 2 * s * V7X_LANES
            dst[0, :, base:base + V7X_LANES] = a.astype(dst.dtype)
            dst[0, :, base + V7X_LANES:base + 2 * V7X_LANES] = b.astype(dst.dtype)


def _qkv_call(z, gq, gk, rope_tabs, has_q):
    b, l, nz = z.shape
    tm = min(512, l)
    nq = ATTN_W + 2 * KV_W if has_q else 2 * KV_W
    rope = rope_tabs is not None
    lane_spec = pl.BlockSpec((1, V7X_LANES), lambda bi, i: (0, 0))
    in_specs = [pl.BlockSpec((1, tm, nq), lambda bi, i: (bi, i, 0))]
    args = [z]
    if has_q:
        in_specs.append(lane_spec)
        args.append(jnp.tile(gq, 2).reshape(1, V7X_LANES))
    in_specs.append(lane_spec)
    args.append(jnp.tile(gk, 2).reshape(1, V7X_LANES))
    if rope:
        in_specs += [pl.BlockSpec((tm, V7X_LANES), lambda bi, i: (i, 0))] * 3
        args += list(rope_tabs)
    out_specs, out_shape = [], []
    if has_q:
        out_specs.append(pl.BlockSpec((1, tm, ATTN_W), lambda bi, i: (bi, i, 0)))
        out_shape.append(jax.ShapeDtypeStruct((b, l, ATTN_W), BF16))
    for _ in range(2):
        out_specs.append(pl.BlockSpec((1, tm, KV_DUP_W), lambda bi, i: (bi, i, 0)))
        out_shape.append(jax.ShapeDtypeStruct((b, l, KV_DUP_W), BF16))
    outs = pl.pallas_call(
        functools.partial(_qkv_body, has_q=has_q, rope=rope),
        grid=(b, l // tm),
        in_specs=in_specs,
        out_specs=out_specs,
        out_shape=out_shape,
        compiler_params=_params("arbitrary", "arbitrary"),
    )(*args)
    return outs if has_q else (None, *outs)


def _rope_tables(l):
    rows = l // GRID_W
    row = jnp.repeat(jnp.arange(rows, dtype=F32), GRID_W)
    col = jnp.tile(jnp.arange(GRID_W, dtype=F32), rows)
    inv = ROPE_BASE ** (-jnp.arange(ROPE_FREQS, dtype=F32) / ROPE_FREQS)
    ang = jnp.stack([row[:, None] * inv, col[:, None] * inv], axis=1)
    cos, sin = jnp.cos(ang), jnp.sin(ang)
    zero = jnp.zeros_like(sin)
    cos_h = jnp.stack([cos, cos], axis=2).reshape(l, HEAD_DIM)
    sup_h = jnp.stack([-sin, zero], axis=2).reshape(l, HEAD_DIM)
    sdn_h = jnp.stack([zero, sin], axis=2).reshape(l, HEAD_DIM)
    return tuple(jnp.tile(t, (1, 2)) for t in (cos_h, sup_h, sdn_h))


def _attn_body(*refs, local, tq):
    it = iter(refs)
    sink_ref = next(it)
    q_ref = next(it)
    if local:
        kp_ref, kc_ref, kn_ref, vp_ref, vc_ref, vn_ref = (next(it) for _ in range(6))
    kx_ref, vx_ref = next(it), next(it)
    o_ref = next(it)

    i = pl.program_id(1)
    nb = pl.num_programs(1)
    low = lax.broadcasted_iota(jnp.int32, (1, V7X_LANES), 1) < HEAD_DIM
    rows = GQA_GROUP * tq
    if local:
        qi = lax.broadcasted_iota(jnp.int32, (rows, tq), 0) % tq
        kj = lax.broadcasted_iota(jnp.int32, (rows, tq), 1)
        mask_prev = (kj >= qi) & (i > 0)
        mask_next = (kj <= qi) & (i < nb - 1)
    row_head = lax.broadcasted_iota(jnp.int32, (rows, 1), 0) // tq
    zero = jnp.zeros((), q_ref.dtype)

    for h in range(N_KV_HEADS):
        hs = slice(h * V7X_LANES, (h + 1) * V7X_LANES)
        qa = q_ref[0, :, 2 * h * V7X_LANES:(2 * h + 1) * V7X_LANES]
        qb = q_ref[0, :, (2 * h + 1) * V7X_LANES:(2 * h + 2) * V7X_LANES]
        qs = jnp.concatenate([jnp.where(low, qa, zero), jnp.where(low, zero, qa),
                              jnp.where(low, qb, zero), jnp.where(low, zero, qb)], axis=0)
        pieces = []
        if local:
            pieces += [(kp_ref[0, :, hs], vp_ref[0, :, hs], mask_prev),
                       (kc_ref[0, :, hs], vc_ref[0, :, hs], None),
                       (kn_ref[0, :, hs], vn_ref[0, :, hs], mask_next)]
        pieces.append((kx_ref[0, :, hs], vx_ref[0, :, hs], None))

        sink = jnp.zeros((rows, 1), F32)
        for g in range(GQA_GROUP):
            sink = jnp.where(row_head == g, sink_ref[GQA_GROUP * h + g], sink)
        scores = []
        m = sink
        for k, _, mask in pieces:
            s = lax.dot_general(qs, k, (((1,), (1,)), ((), ())), preferred_element_type=F32)
            if mask is not None:
                s = jnp.where(mask, s, NEG_INF)
            scores.append(s)
            m = jnp.maximum(m, jnp.max(s, axis=-1, keepdims=True))
        denom = jnp.exp(sink - m)
        acc = jnp.zeros((rows, V7X_LANES), F32)
        for s, (_, v, _) in zip(scores, pieces):
            p = jnp.exp(s - m)
            denom = denom + jnp.sum(p, axis=-1, keepdims=True)
            acc = acc + jnp.dot(p.astype(v.dtype), v, preferred_element_type=F32)
        o = acc / denom
        oa = jnp.where(low, o[0:tq], o[tq:2 * tq])
        ob = jnp.where(low, o[2 * tq:3 * tq], o[3 * tq:4 * tq])
        o_ref[0, :, 2 * h * V7X_LANES:(2 * h + 1) * V7X_LANES] = oa.astype(o_ref.dtype)
        o_ref[0, :, (2 * h + 1) * V7X_LANES:(2 * h + 2) * V7X_LANES] = ob.astype(o_ref.dtype)


def _attn_call(q, k, v, kx, vx, sink, local):
    b, l, _ = q.shape
    lx = kx.shape[1]
    tq = WINDOW
    nb = l // tq
    blk = lambda w: (1, tq, w)
    in_specs = [pl.BlockSpec(memory_space=pltpu.SMEM),
                pl.BlockSpec(blk(ATTN_W), lambda bi, i: (bi, i, 0))]
    args = [sink, q]
    if local:
        maps = [lambda bi, i: (bi, jnp.maximum(i - 1, 0), 0),
                lambda bi, i: (bi, i, 0),
                lambda bi, i: (bi, jnp.minimum(i + 1, nb - 1), 0)]
        for arr in (k, v):
            for mp in maps:
                in_specs.append(pl.BlockSpec(blk(KV_DUP_W), mp))
                args.append(arr)
    for arr in (kx, vx):
        in_specs.append(pl.BlockSpec((1, lx, KV_DUP_W), lambda bi, i: (bi, 0, 0)))
        args.append(arr)
    return pl.pallas_call(
        functools.partial(_attn_body, local=local, tq=tq),
        grid=(b, nb),
        in_specs=in_specs,
        out_specs=pl.BlockSpec(blk(ATTN_W), lambda bi, i: (bi, i, 0)),
        out_shape=jax.ShapeDtypeStruct((b, l, ATTN_W), BF16),
        compiler_params=_params("arbitrary", "arbitrary"),
    )(*args)


def _filter_body(z_ref, w0_ref, b0_ref, w1_ref, b1_ref, fr_ref, w2_ref, dec_ref, o_ref):
    fr = fr_ref[...]
    dot = functools.partial(jnp.dot, precision=HIGHEST, preferred_element_type=F32)
    h = jnp.sin(fr * (dot(z_ref[...], w0_ref[...]) + b0_ref[...]))
    for i in range(FILTER_INNER):
        h = jnp.sin(fr * (dot(h, w1_ref[i]) + b1_ref[i]))
    dec = dec_ref[...]
    for s in range(2 * HYENA_ORDER):
        sl = slice(s * HYENA_W, (s + 1) * HYENA_W)
        o_ref[:, sl] = dot(h, w2_ref[:, sl]) * dec


def _filter_features(l):
    t = jnp.linspace(0.0, 1.0, l, dtype=F32)[:, None]
    w = 2.0 * math.pi * jnp.arange(l, dtype=F32)[:, None] / l
    bands = jnp.linspace(1e-4, FILTER_BANDS - 1, FILTER_BANDS, dtype=F32)[None, :]
    z = jnp.concatenate([t, jnp.cos(bands * w), -jnp.sin(bands * w)], axis=-1)
    deltas = jnp.linspace(math.log(DECAY_TARGET) / SLOW_DECAY_PCT, math.log(DECAY_TARGET) / FAST_DECAY_PCT,
                          HYENA_W, dtype=F32)
    decay = jnp.exp(-t * jnp.abs(deltas))
    return jnp.pad(z, ((0, 0), (0, V7X_LANES - FILTER_EMB))), decay


def _filter_call(l, w0, b0, w1, b1, freq, w2):
    zfeat, decay = _filter_features(l)
    w0p = jnp.pad(w0, ((0, V7X_LANES - FILTER_EMB), (0, 0)))
    tl = min(512, l)
    nf = 2 * HYENA_ORDER * HYENA_W
    full = lambda shape: pl.BlockSpec(shape, lambda i: (0,) * len(shape))
    return pl.pallas_call(
        _filter_body,
        grid=(l // tl,),
        in_specs=[
            pl.BlockSpec((tl, V7X_LANES), lambda i: (i, 0)),
            full((V7X_LANES, FILTER_HIDDEN)),
            full((1, FILTER_HIDDEN)),
            full((FILTER_INNER, FILTER_HIDDEN, FILTER_HIDDEN)),
            full((FILTER_INNER, 1, FILTER_HIDDEN)),
            full((1, FILTER_HIDDEN)),
            full((FILTER_HIDDEN, nf)),
            pl.BlockSpec((tl, HYENA_W), lambda i: (i, 0)),
        ],
        out_specs=pl.BlockSpec((tl, nf), lambda i: (i, 0)),
        out_shape=jax.ShapeDtypeStruct((l, nf), F32),
        compiler_params=_params("arbitrary"),
    )(zfeat, w0p, b0.reshape(1, -1), w1, b1.reshape(FILTER_INNER, 1, -1), freq.reshape(1, -1), w2, decay)


def _dft_matrices(l):
    n = 2 * l
    r = jnp.arange(l, dtype=jnp.int32)
    ang = ((r[:, None] * r[None, :]) % n).astype(F32) * (2.0 * math.pi / n)
    return jnp.cos(ang).astype(BF16), jnp.sin(ang).astype(BF16)


def _alternating(l):
    row = lax.broadcasted_iota(jnp.int32, (l, 1), 0)
    return row, jnp.where(row % 2 == 0, 1.0, -1.0).astype(F32)


def _spectrum_body(hf_ref, hb_ref, fc_ref, fs_ref, ka_ref, kb_ref, kn_ref):
    l = hf_ref.shape[0]
    n = 2 * l
    row, alt = _alternating(l)
    hf = hf_ref[...]
    hbs = jnp.where(row == 0, 0.0, pltpu.roll(hb_ref[...], 1, 0))
    ssum = (hf + hbs).astype(BF16)
    sdif = (hbs - hf).astype(BF16)
    kre = jnp.dot(fc_ref[...], ssum, preferred_element_type=F32)
    kim = jnp.dot(fs_ref[...], sdif, preferred_element_type=F32)
    ka_ref[0] = kre * jnp.where(row == 0, 1.0 / n, 2.0 / n)
    kb_ref[0] = kim * (2.0 / n)
    kn_ref[0] = jnp.sum((hf + hbs) * alt, axis=0, keepdims=True) * (1.0 / n)


def _spectrum_call(filt, fc, fs, tc):
    l = filt.shape[0]
    nct = HYENA_W // tc
    return pl.pallas_call(
        _spectrum_body,
        grid=(HYENA_ORDER, nct),
        in_specs=[
            pl.BlockSpec((l, tc), lambda o, c: (0, 2 * nct * o + c)),
            pl.BlockSpec((l, tc), lambda o, c: (0, 2 * nct * o + nct + c)),
            _const_spec((l, l)),
            _const_spec((l, l)),
        ],
        out_specs=[
            pl.BlockSpec((1, l, tc), lambda o, c: (o, 0, c)),
            pl.BlockSpec((1, l, tc), lambda o, c: (o, 0, c)),
            pl.BlockSpec((1, 1, tc), lambda o, c: (o, 0, c)),
        ],
        out_shape=[
            jax.ShapeDtypeStruct((HYENA_ORDER, l, HYENA_W), F32),
            jax.ShapeDtypeStruct((HYENA_ORDER, l, HYENA_W), F32),
            jax.ShapeDtypeStruct((HYENA_ORDER, 1, HYENA_W), F32),
        ],
        compiler_params=_params("arbitrary", "arbitrary"),
    )(filt, filt, fc, fs)


def _conv3(x, w_ref, b_ref, row):
    l = x.shape[0]
    xm = jnp.where(row == 0, 0.0, pltpu.roll(x, 1, 0))
    xp = jnp.where(row == l - 1, 0.0, pltpu.roll(x, l - 1, 0))
    return xm * w_ref[0:1, :] + x * w_ref[1:2, :] + xp * w_ref[2:3, :] + b_ref[...]


def _fftconv_body(*refs, conv_u):
    it = iter(refs)
    u_ref = next(it)
    if conv_u:
        uw_ref, ub_ref = next(it), next(it)
    g_ref, gw_ref, gb_ref = next(it), next(it), next(it)
    ka_ref, kb_ref, kn_ref, d_ref, fc_ref, fs_ref, o_ref = (next(it) for _ in range(7))

    l = u_ref.shape[1]
    row, alt = _alternating(l)
    u = u_ref[0].astype(F32)
    if conv_u:
        u = _conv3(u, uw_ref, ub_ref, row)

    ub = u.astype(BF16)
    nyq = jnp.sum(u * alt, axis=0, keepdims=True) * kn_ref[0]
    y = alt * nyq + u * d_ref[0]
    rc = min(l, 512)
    for f0 in range(0, l, rc):
        fr = slice(f0, f0 + rc)
        p = jnp.dot(fc_ref[fr, :], ub, preferred_element_type=F32)
        q = jnp.dot(fs_ref[fr, :], ub, preferred_element_type=F32)
        ka, kb = ka_ref[0, fr, :], kb_ref[0, fr, :]
        r = (p * ka + q * kb).astype(BF16)
        t = (q * ka - p * kb).astype(BF16)
        y = y + jnp.dot(fc_ref[:, fr], r, preferred_element_type=F32)
        y = y + jnp.dot(fs_ref[:, fr], t, preferred_element_type=F32)
    gate = _conv3(g_ref[0], gw_ref, gb_ref, row)
    o_ref[0] = (gate * y).astype(o_ref.dtype)


def _fftconv_call(u, u_col0, z, gate_col0, conv_w, conv_b, spectra, d_skip, order, fc, fs, tc, out_dtype):
    b, l, _ = z.shape
    conv_u = u is z
    nct = HYENA_W // tc
    ka, kb, kn = spectra
    col = lambda c0: (lambda c, bi: (bi, 0, c0 // tc + c))
    wcol = lambda c0: (lambda c, bi: (0, (c0 - HY_OFF) // tc + c))
    in_specs = [pl.BlockSpec((1, l, tc), col(u_col0))]
    args = [u]
    if conv_u:
        in_specs += [pl.BlockSpec((3, tc), wcol(u_col0)), pl.BlockSpec((1, tc), wcol(u_col0))]
        args += [conv_w, conv_b]
    in_specs += [pl.BlockSpec((1, l, tc), col(gate_col0)),
                 pl.BlockSpec((3, tc), wcol(gate_col0)), pl.BlockSpec((1, tc), wcol(gate_col0))]
    args += [z, conv_w, conv_b]
    spec = lambda rows: pl.BlockSpec((1, rows, tc), lambda c, bi: (order, 0, c), pipeline_mode=pl.Buffered(1))
    in_specs += [spec(l), spec(l), spec(1), spec(1), _const_spec((l, l)), _const_spec((l, l))]
    args += [ka, kb, kn, d_skip, fc, fs]
    return pl.pallas_call(
        functools.partial(_fftconv_body, conv_u=conv_u),
        grid=(nct, b),
        in_specs=in_specs,
        out_specs=pl.BlockSpec((1, l, tc), lambda c, bi: (bi, 0, c)),
        out_shape=jax.ShapeDtypeStruct((b, l, HYENA_W), out_dtype),
        compiler_params=_params("arbitrary", "arbitrary"),
    )(*args)


def _pool_body(x_ref, w_ref, s_ref, o_ref):
    l = x_ref.shape[1]
    row = lax.broadcasted_iota(jnp.int32, (l, 1), 0)
    for g, win in enumerate(POOL_WINDOWS):
        half = win // 2
        sl = slice(g * POOL_GROUP, (g + 1) * POOL_GROUP)
        x = x_ref[0, :, sl]
        acc = jnp.zeros_like(x)
        for k in range(-half, half):
            shifted = x if k == 0 else pltpu.roll(x, (-k) % l, 0)
            acc = acc + jnp.where((row + k >= 0) & (row + k < l), shifted, 0.0)
        cnt = (jnp.minimum(row + half, l) - jnp.maximum(row - half, 0)).astype(F32)
        d = acc / cnt - x
        y = jnp.dot(d.astype(BF16), w_ref[g], preferred_element_type=F32)
        o_ref[0, :, sl] = (y * s_ref[:, sl]).astype(o_ref.dtype)


def _pool_call(z, w_grp, scale):
    b, l, _ = z.shape
    ng = len(POOL_WINDOWS)
    return pl.pallas_call(
        _pool_body,
        grid=(b,),
        in_specs=[
            pl.BlockSpec((1, l, POOL_W), lambda bi: (bi, 0, POOL_OFF // POOL_W)),
            pl.BlockSpec((ng, POOL_GROUP, POOL_GROUP), lambda bi: (0, 0, 0)),
            pl.BlockSpec((1, POOL_W), lambda bi: (0, 0)),
        ],
        out_specs=pl.BlockSpec((1, l, POOL_W), lambda bi: (bi, 0, 0)),
        out_shape=jax.ShapeDtypeStruct((b, l, POOL_W), BF16),
        compiler_params=_params("arbitrary"),
    )(z, w_grp, scale.reshape(1, POOL_W))


def _merge_body(ya_ref, yh_ref, yp_ref, gt_ref, x_ref, ga_ref, g2_ref, sc_ref, sh_ref,
                wa_ref, wh_ref, wp_ref, wo_ref, xn_ref, h2_ref):
    d = x_ref.shape[1]
    cj = 512
    ya, yh, yp = ya_ref[...], yh_ref[...], yp_ref[...]
    acc = jnp.zeros(x_ref.shape, F32)
    for j in range(d // cj):
        sl = slice(j * cj, (j + 1) * cj)
        gate = lambda br: gt_ref[:, br * d + j * cj:br * d + (j + 1) * cj].astype(F32)
        m = (gate(0) * jnp.dot(ya, wa_ref[:, sl], preferred_element_type=F32)
             + gate(1) * jnp.dot(yh, wh_ref[:, sl], preferred_element_type=F32)
             + gate(2) * jnp.dot(yp, wp_ref[:, sl], preferred_element_type=F32))
        acc = acc + jnp.dot(m.astype(BF16), wo_ref[sl, :], preferred_element_type=F32)
    xn = x_ref[...] + ga_ref[0] * acc
    xn_ref[...] = xn
    h2_ref[...] = _norm_mod(xn, g2_ref[...], sc_ref[0], sh_ref[0]).astype(h2_ref.dtype)


def _merge_call(ya, yh, yp, gates, x2, ga1, g2, sc2, sh2, wa, wh, wp, wo, rows_per_batch):
    m, d = x2.shape
    tm = min(256, rows_per_batch)
    rpt = rows_per_batch // tm
    rows = lambda w: pl.BlockSpec((tm, w), lambda i: (i, 0))
    return pl.pallas_call(
        _merge_body,
        grid=(m // tm,),
        in_specs=[rows(ATTN_W), rows(HYENA_W), rows(POOL_W), rows(N_BRANCH * d), rows(d),
                  _mod_spec(ga1, rpt), pl.BlockSpec((1, d), lambda i: (0, 0)),
                  _mod_spec(sc2, rpt), _mod_spec(sh2, rpt),
                  _const_spec(wa.shape), _const_spec(wh.shape), _const_spec(wp.shape), _const_spec(wo.shape)],
        out_specs=[rows(d), rows(d)],
        out_shape=[jax.ShapeDtypeStruct((m, d), F32), jax.ShapeDtypeStruct((m, d), BF16)],
        compiler_params=_params("arbitrary"),
    )(ya, yh, yp, gates, x2, ga1, g2.reshape(1, d), sc2, sh2, wa, wh, wp, wo)


def _mlp_body(*refs, has_next):
    it = iter(refs)
    h_ref, w1_ref, w2_ref, x_ref, ga_ref = (next(it) for _ in range(5))
    if has_next:
        gn_ref, sc_ref, sh_ref = next(it), next(it), next(it)
    o_ref = next(it)
    hn_ref = next(it) if has_next else None
    acc_ref = next(it)

    f = pl.program_id(1)

    @pl.when(f == 0)
    def _():
        acc_ref[...] = jnp.zeros_like(acc_ref)

    a = jnp.dot(h_ref[...], w1_ref[...], preferred_element_type=F32)
    a = jnp.square(jnp.maximum(a, 0.0))
    acc_ref[...] += jnp.dot(a.astype(BF16), w2_ref[...], preferred_element_type=F32)

    @pl.when(f == pl.num_programs(1) - 1)
    def _():
        xo = x_ref[...] + ga_ref[0] * acc_ref[...]
        o_ref[...] = xo
        if has_next:
            hn_ref[...] = _norm_mod(xo, gn_ref[...], sc_ref[0], sh_ref[0]).astype(hn_ref.dtype)


def _mlp_call(h2, w1, w2, xn, ga2, nxt, rows_per_batch):
    m, d = xn.shape
    ff = w1.shape[1]
    tm, tf = min(512, rows_per_batch), 512
    rpt = rows_per_batch // tm
    has_next = nxt is not None
    rows = pl.BlockSpec((tm, d), lambda i, f: (i, 0))
    in_specs = [rows, pl.BlockSpec((d, tf), lambda i, f: (0, f)), pl.BlockSpec((tf, d), lambda i, f: (f, 0)),
                rows, _mod_spec(ga2, rpt)]
    args = [h2, w1, w2, xn, ga2]
    out_specs = [rows]
    out_shape = [jax.ShapeDtypeStruct((m, d), F32)]
    if has_next:
        gn, scn, shn = nxt
        in_specs += [pl.BlockSpec((1, d), lambda i, f: (0, 0)), _mod_spec(scn, rpt), _mod_spec(shn, rpt)]
        args += [gn.reshape(1, d), scn, shn]
        out_specs.append(rows)
        out_shape.append(jax.ShapeDtypeStruct((m, d), BF16))
    outs = pl.pallas_call(
        functools.partial(_mlp_body, has_next=has_next),
        grid=(m // tm, ff // tf),
        in_specs=in_specs,
        out_specs=out_specs,
        out_shape=out_shape,
        scratch_shapes=[pltpu.VMEM((tm, d), F32)],
        compiler_params=_params("arbitrary", "arbitrary"),
    )(*args)
    return (outs[0], outs[1]) if has_next else (outs[0], None)


def _hyena_tc(l):
    return 256 if l >= 256 else HYENA_W


def _mixers(z, q, k, v, kx, vx, sink, local, hy, pool_w, pool_scale):
    b, l, _ = z.shape
    y_att = _attn_call(q, k, v, kx, vx, sink, local)
    conv_w, conv_b, filt_params, d_skip, dft = hy
    fc, fs = dft
    tc = _hyena_tc(l)
    spectra = _spectrum_call(_filter_call(l, *filt_params), fc, fs, tc)
    z1 = _fftconv_call(z, HY_OFF, z, HY_OFF + HYENA_W, conv_w, conv_b, spectra, d_skip, 0, fc, fs, tc, F32)
    y_hy = _fftconv_call(z1, 0, z, HY_OFF + 2 * HYENA_W, conv_w, conv_b, spectra, d_skip, 1, fc, fs, tc, BF16)
    y_pool = _pool_call(z, pool_w, pool_scale)
    return (y_att.reshape(b * l, ATTN_W), y_hy.reshape(b * l, HYENA_W), y_pool.reshape(b * l, POOL_W))


def kernel(x, c, ctx, c_ctx, norm1_g, norm2_g, w_mod, b_mod, w_in, q_norm_g, k_norm_g, sink, hy_conv_w, hy_conv_b, filt_w0, filt_b0, filt_w1, filt_b1, filt_freq, filt_w2, hy_bias, pool_w, pool_scale, w_att_o, w_hy_o, w_pool_o, w_out, mlp_w1, mlp_w2):
    b, l, d = x.shape
    lc = ctx.shape[1]
    depth = w_mod.shape[0]

    cc = jnp.concatenate([c, c_ctx[None, :], jnp.zeros((MOD_ROWS - b - 1, d), F32)], axis=0)
    mods = _modulation(cc, w_mod, b_mod)

    def chunks(layer, lo, hi):
        return [mods[layer, lo:hi, i * d:(i + 1) * d].reshape(hi - lo, 1, d) for i in range(6)]

    as_bf16 = lambda w: w.astype(BF16)
    w_in_b, w_att_b, w_hy_b, w_pool_b, w_out_b = map(as_bf16, (w_in, w_att_o, w_hy_o, w_pool_o, w_out))
    w1_b, w2_b, pool_w_b = map(as_bf16, (mlp_w1, mlp_w2, pool_w))

    rope_tabs = _rope_tables(l)
    dft_x = _dft_matrices(l)
    dft_c = _dft_matrices(lc)

    x2 = x.reshape(b * l, d)
    c2 = ctx.reshape(b * lc, d)
    sh1, sc1 = chunks(0, 0, b)[:2]
    csh1, csc1 = chunks(0, b, b + 1)[:2]
    hx = _norm_call(x2, norm1_g[0], sc1, sh1, l)
    hc = _norm_call(c2, norm1_g[0], csc1, csh1, lc)

    for layer in range(depth):
        last = layer == depth - 1
        _, _, ga1, sh2, sc2, ga2 = chunks(layer, 0, b)
        _, _, cga1, csh2, csc2, cga2 = chunks(layer, b, b + 1)
        wl = w_in_b[layer]
        filt_params = (filt_w0[layer], filt_b0[layer], filt_w1[layer], filt_b1[layer], filt_freq[layer],
                       filt_w2[layer])
        conv_b = hy_conv_b[layer].reshape(1, -1)
        d_skip = hy_bias[layer].reshape(HYENA_ORDER, 1, HYENA_W)
        merge_w = (w_att_b[layer], w_hy_b[layer], w_pool_b[layer], w_out_b[layer])

        if last:
            zc = _proj_call(hc, wl, K_OFF, 2 * KV_W, F32).reshape(b, lc, 2 * KV_W)
        else:
            zc = _proj_call(hc, wl, 0, GATE_OFF, F32).reshape(b, lc, GATE_OFF)
        qc, kc, vc = _qkv_call(zc, q_norm_g[layer], k_norm_g[layer], None, has_q=not last)

        zx = _proj_call(hx, wl, 0, GATE_OFF, F32).reshape(b, l, GATE_OFF)
        gx = _proj_call(hx, wl, GATE_OFF, N_BRANCH * d, BF16, sigmoid=True)
        qx, kx, vx = _qkv_call(zx, q_norm_g[layer], k_norm_g[layer], rope_tabs, has_q=True)
        hy = (hy_conv_w[layer], conv_b, filt_params, d_skip, dft_x)
        ya, yh, yp = _mixers(zx, qx, kx, vx, kc, vc, sink[layer], True, hy, pool_w_b[layer], pool_scale[layer])
        xn, h2 = _merge_call(ya, yh, yp, gx, x2, ga1, norm2_g[layer], sc2, sh2, *merge_w, l)
        nxt = None if last else (norm1_g[layer + 1], *reversed(chunks(layer + 1, 0, b)[:2]))
        x2, hx = _mlp_call(h2, w1_b[layer], w2_b[layer], xn, ga2, nxt, l)

        if not last:
            gc = _proj_call(hc, wl, GATE_OFF, N_BRANCH * d, BF16, sigmoid=True)
            hyc = (hy_conv_w[layer], conv_b, filt_params, d_skip, dft_c)
            ya, yh, yp = _mixers(zc, qc, None, None, kc, vc, sink[layer], False, hyc, pool_w_b[layer],
                                 pool_scale[layer])
            cn, h2c = _merge_call(ya, yh, yp, gc, c2, cga1, norm2_g[layer], csc2, csh2, *merge_w, lc)
            nxt = (norm1_g[layer + 1], *reversed(chunks(layer + 1, b, b + 1)[:2]))
            c2, hc = _mlp_call(h2c, w1_b[layer], w2_b[layer], cn, cga2, nxt, lc)

    return x2.reshape(b, l, d)
```

```python
import functools
import math

import jax
import jax.numpy as jnp
from jax import lax
from jax.experimental import pallas as pl
from jax.experimental.pallas import tpu as pltpu

D_MODEL = 2048
DEPTH = 2
GRID_W = 64
EPS = 1e-6
NEG_INF = -1e30

N_HEADS = 16
N_KV_HEADS = 4
GQA_GROUP = N_HEADS // N_KV_HEADS
HEAD_DIM = 64
ATTN_W = N_HEADS * HEAD_DIM
KV_W = N_KV_HEADS * HEAD_DIM
WINDOW = 128
ROPE_FREQS = HEAD_DIM // 4
ROPE_BASE = 10000.0

HYENA_W = D_MODEL // 4
HYENA_ORDER = 2
FILTER_BANDS = 16
FILTER_EMB = 1 + 2 * FILTER_BANDS
FILTER_HIDDEN = 64
FILTER_INNER = 2
DECAY_TARGET = 1e-2
FAST_DECAY_PCT = 0.3
SLOW_DECAY_PCT = 1.5

POOL_W = D_MODEL // 4
POOL_WINDOWS = (2, 4, 8, 16)
POOL_GROUP = POOL_W // len(POOL_WINDOWS)

N_BRANCH = 3
D_FF = 4 * D_MODEL

Q_OFF = 0
K_OFF = Q_OFF + ATTN_W
V_OFF = K_OFF + KV_W
HY_OFF = V_OFF + KV_W
POOL_OFF = HY_OFF + 3 * HYENA_W
GATE_OFF = POOL_OFF + POOL_W
IN_W = GATE_OFF + N_BRANCH * D_MODEL

V7X_LANES = 128
V7X_VMEM_LIMIT = 60 * 1024 * 1024
KV_DUP_W = N_KV_HEADS * V7X_LANES
MOD_ROWS = 24

F32 = jnp.float32
BF16 = jnp.bfloat16
HIGHEST = lax.Precision.HIGHEST


def _params(*semantics):
    return pltpu.CompilerParams(dimension_semantics=semantics, vmem_limit_bytes=V7X_VMEM_LIMIT)


def _const_spec(shape):
    zeros = (0,) * len(shape)
    return pl.BlockSpec(shape, lambda *_: zeros, pipeline_mode=pl.Buffered(1))


def _mod_spec(arr, rows_per_mod_tile):
    d = arr.shape[-1]
    if arr.shape[0] == 1:
        return pl.BlockSpec((1, 1, d), lambda i, *_: (0, 0, 0))
    return pl.BlockSpec((1, 1, d), lambda i, *_: (i // rows_per_mod_tile, 0, 0))


def _norm_mod(xf, g, sc, sh):
    y = xf * lax.rsqrt(jnp.mean(xf * xf, axis=-1, keepdims=True) + EPS)
    return (y * g) * (1.0 + sc) + sh


def _mod_body(c_ref, w_ref, b_ref, o_ref):
    c = c_ref[...]
    s = c * jax.nn.sigmoid(c)
    o_ref[0] = jnp.dot(s.astype(BF16), w_ref[0].astype(BF16), preferred_element_type=F32) + b_ref[0]


def _modulation(cc, w_mod, b_mod):
    depth, d, n = w_mod.shape
    tn = 1024
    return pl.pallas_call(
        _mod_body,
        grid=(depth, n // tn),
        in_specs=[
            pl.BlockSpec((MOD_ROWS, d), lambda l, j: (0, 0)),
            pl.BlockSpec((1, d, tn), lambda l, j: (l, 0, j)),
            pl.BlockSpec((1, 1, tn), lambda l, j: (l, 0, j)),
        ],
        out_specs=pl.BlockSpec((1, MOD_ROWS, tn), lambda l, j: (l, 0, j)),
        out_shape=jax.ShapeDtypeStruct((depth, MOD_ROWS, n), F32),
        compiler_params=_params("arbitrary", "arbitrary"),
        name="modulation",
    )(cc, w_mod, b_mod.reshape(depth, 1, n))


def _norm_body(x_ref, g_ref, sc_ref, sh_ref, o_ref):
    o_ref[...] = _norm_mod(x_ref[...], g_ref[...], sc_ref[0], sh_ref[0]).astype(o_ref.dtype)


def _norm_call(x2, g, sc, sh, rows_per_batch):
    m, d = x2.shape
    tm = min(512, rows_per_batch)
    return pl.pallas_call(
        _norm_body,
        grid=(m // tm,),
        in_specs=[
            pl.BlockSpec((tm, d), lambda i: (i, 0)),
            pl.BlockSpec((1, d), lambda i: (0, 0)),
            _mod_spec(sc, rows_per_batch // tm),
            _mod_spec(sh, rows_per_batch // tm),
        ],
        out_specs=pl.BlockSpec((tm, d), lambda i: (i, 0)),
        out_shape=jax.ShapeDtypeStruct((m, d), BF16),
        compiler_params=_params("arbitrary"),
        name="norm_mod",
    )(x2, g.reshape(1, d), sc, sh)


def _proj_body(a_ref, w_ref, o_ref, *, sigmoid):
    z = jnp.dot(a_ref[...], w_ref[...], preferred_element_type=F32)
    if sigmoid:
        z = 0.5 * jnp.tanh(0.5 * z) + 0.5
    o_ref[...] = z.astype(o_ref.dtype)


def _proj_call(a, w, layer, col0, n, out_dtype, sigmoid=False):
    m, k = a.shape
    tm = min(2048, m)
    tn = 512
    c0 = col0 // tn
    return pl.pallas_call(
        functools.partial(_proj_body, sigmoid=sigmoid),
        grid=(m // tm, n // tn),
        in_specs=[
            pl.BlockSpec((tm, k), lambda i, j: (i, 0)),
            pl.BlockSpec((None, k, tn), lambda i, j: (layer, 0, c0 + j)),
        ],
        out_specs=pl.BlockSpec((tm, tn), lambda i, j: (i, j)),
        out_shape=jax.ShapeDtypeStruct((m, n), out_dtype),
        compiler_params=_params("arbitrary", "arbitrary"),
        name="gates_proj" if sigmoid else "in_proj",
    )(a, w)


def _pair_block_diag():
    r = lax.broadcasted_iota(jnp.int32, (V7X_LANES, V7X_LANES), 0) // HEAD_DIM
    c = lax.broadcasted_iota(jnp.int32, (V7X_LANES, V7X_LANES), 1) // HEAD_DIM
    return (r == c).astype(F32)


def _head_norm(x, g, bd):
    ss = jnp.dot(x * x, bd, precision=HIGHEST, preferred_element_type=F32)
    return (x * lax.rsqrt(ss * (1.0 / HEAD_DIM) + EPS)) * g


def _rope(x, cos, sin_up, sin_dn):
    up = pltpu.roll(x, V7X_LANES - ROPE_FREQS, 1)
    dn = pltpu.roll(x, ROPE_FREQS, 1)
    return x * cos + up * sin_up + dn * sin_dn


def _dup_pair(x, low):
    r = pltpu.roll(x, HEAD_DIM, 1)
    return jnp.where(low, x, r), jnp.where(low, r, x)


def _qkv_body(*refs, has_q, rope):
    it = iter(refs)
    z_ref = next(it)
    gq_ref = next(it) if has_q else None
    gk_ref = next(it)
    if rope:
        cos_ref, sup_ref, sdn_ref = next(it), next(it), next(it)
    q_ref = next(it) if has_q else None
    k_ref, v_ref = next(it), next(it)

    bd = _pair_block_diag()
    low = lax.broadcasted_iota(jnp.int32, (1, V7X_LANES), 1) < HEAD_DIM
    k0 = ATTN_W if has_q else 0
    v0 = k0 + KV_W

    def prep(x, g):
        y = _head_norm(x, g, bd)
        if rope:
            y = _rope(y, cos_ref[...], sup_ref[...], sdn_ref[...])
        return y

    if has_q:
        for s in range(ATTN_W // V7X_LANES):
            sl = slice(s * V7X_LANES, (s + 1) * V7X_LANES)
            y = prep(z_ref[0, :, sl], gq_ref[...])
            q_ref[0, :, sl] = (y * (HEAD_DIM ** -0.5)).astype(q_ref.dtype)
    for s in range(KV_W // V7X_LANES):
        kp = prep(z_ref[0, :, k0 + s * V7X_LANES:k0 + (s + 1) * V7X_LANES], gk_ref[...])
        vp = z_ref[0, :, v0 + s * V7X_LANES:v0 + (s + 1) * V7X_LANES]
        for src, dst in ((kp, k_ref), (vp, v_ref)):
            a, b = _dup_pair(src, low)
            base = 2 * s * V7X_LANES
            dst[0, :, base:base + V7X_LANES] = a.astype(dst.dtype)
            dst[0, :, base + V7X_LANES:base + 2 * V7X_LANES] = b.astype(dst.dtype)


def _qkv_call(z, gq, gk, rope_tabs, has_q):
    b, l, nz = z.shape
    tm = min(512, l)
    nq = ATTN_W + 2 * KV_W if has_q else 2 * KV_W
    rope = rope_tabs is not None
    lane_spec = pl.BlockSpec((1, V7X_LANES), lambda bi, i: (0, 0))
    in_specs = [pl.BlockSpec((1, tm, nq), lambda bi, i: (bi, i, 0))]
    args = [z]
    if has_q:
        in_specs.append(lane_spec)
        args.append(jnp.tile(gq, 2).reshape(1, V7X_LANES))
    in_specs.append(lane_spec)
    args.append(jnp.tile(gk, 2).reshape(1, V7X_LANES))
    if rope:
        in_specs += [pl.BlockSpec((tm, V7X_LANES), lambda bi, i: (i, 0))] * 3
        args += list(rope_tabs)
    out_specs, out_shape = [], []
    if has_q:
        out_specs.append(pl.BlockSpec((1, tm, ATTN_W), lambda bi, i: (bi, i, 0)))
        out_shape.append(jax.ShapeDtypeStruct((b, l, ATTN_W), BF16))
    for _ in range(2):
        out_specs.append(pl.BlockSpec((1, tm, KV_DUP_W), lambda bi, i: (bi, i, 0)))
        out_shape.append(jax.ShapeDtypeStruct((b, l, KV_DUP_W), BF16))
    outs = pl.pallas_call(
        functools.partial(_qkv_body, has_q=has_q, rope=rope),
        grid=(b, l // tm),
        in_specs=in_specs,
        out_specs=out_specs,
        out_shape=out_shape,
        compiler_params=_params("arbitrary", "arbitrary"),
        name="qkv_prep",
    )(*args)
    return outs if has_q else (None, *outs)


def _rope_tables(l):
    rows = l // GRID_W
    row = jnp.repeat(jnp.arange(rows, dtype=F32), GRID_W)
    col = jnp.tile(jnp.arange(GRID_W, dtype=F32), rows)
    inv = ROPE_BASE ** (-jnp.arange(ROPE_FREQS, dtype=F32) / ROPE_FREQS)
    ang = jnp.stack([row[:, None] * inv, col[:, None] * inv], axis=1)
    cos, sin = jnp.cos(ang), jnp.sin(ang)
    zero = jnp.zeros_like(sin)
    cos_h = jnp.stack([cos, cos], axis=2).reshape(l, HEAD_DIM)
    sup_h = jnp.stack([-sin, zero], axis=2).reshape(l, HEAD_DIM)
    sdn_h = jnp.stack([zero, sin], axis=2).reshape(l, HEAD_DIM)
    return tuple(jnp.tile(t, (1, 2)) for t in (cos_h, sup_h, sdn_h))


def _attn_body(*refs, local, tq):
    it = iter(refs)
    sink_ref = next(it)
    q_ref = next(it)
    if local:
        kp_ref, kc_ref, kn_ref, vp_ref, vc_ref, vn_ref = (next(it) for _ in range(6))
    kx_ref, vx_ref = next(it), next(it)
    o_ref = next(it)

    i = pl.program_id(1)
    nb = pl.num_programs(1)
    low = lax.broadcasted_iota(jnp.int32, (1, V7X_LANES), 1) < HEAD_DIM
    rows = GQA_GROUP * tq
    if local:
        qi = lax.broadcasted_iota(jnp.int32, (rows, tq), 0) % tq
        kj = lax.broadcasted_iota(jnp.int32, (rows, tq), 1)
        mask_prev = (kj >= qi) & (i > 0)
        mask_next = (kj <= qi) & (i < nb - 1)
    row_head = lax.broadcasted_iota(jnp.int32, (rows, 1), 0) // tq
    zero = jnp.zeros((), q_ref.dtype)

    for h in range(N_KV_HEADS):
        hs = slice(h * V7X_LANES, (h + 1) * V7X_LANES)
        qa = q_ref[0, :, 2 * h * V7X_LANES:(2 * h + 1) * V7X_LANES]
        qb = q_ref[0, :, (2 * h + 1) * V7X_LANES:(2 * h + 2) * V7X_LANES]
        qs = jnp.concatenate([jnp.where(low, qa, zero), jnp.where(low, zero, qa),
                              jnp.where(low, qb, zero), jnp.where(low, zero, qb)], axis=0)
        kparts, vparts, masks = [kx_ref[0, :, hs]], [vx_ref[0, :, hs]], {}
        if local:
            kparts = [kp_ref[0, :, hs], kc_ref[0, :, hs], kn_ref[0, :, hs]] + kparts
            vparts = [vp_ref[0, :, hs], vc_ref[0, :, hs], vn_ref[0, :, hs]] + vparts
            masks = {0: mask_prev, 2: mask_next}
        k_all = jnp.concatenate(kparts, axis=0)
        v_all = jnp.concatenate(vparts, axis=0)

        sink = jnp.zeros((rows, 1), F32)
        for g in range(GQA_GROUP):
            sink = jnp.where(row_head == g, sink_ref[GQA_GROUP * h + g], sink)
        s_all = lax.dot_general(qs, k_all, (((1,), (1,)), ((), ())), preferred_element_type=F32)
        chunks = []
        for c in range(k_all.shape[0] // tq):
            s = s_all[:, c * tq:(c + 1) * tq]
            chunks.append(jnp.where(masks[c], s, NEG_INF) if c in masks else s)
        m = jnp.maximum(sink, jnp.max(functools.reduce(jnp.maximum, chunks), axis=-1, keepdims=True))
        probs = [jnp.exp(s - m) for s in chunks]
        denom = jnp.exp(sink - m) + jnp.sum(functools.reduce(jnp.add, probs), axis=-1, keepdims=True)
        p_all = jnp.concatenate([p.astype(v_all.dtype) for p in probs], axis=1)
        o = jnp.dot(p_all, v_all, preferred_element_type=F32) / denom
        oa = jnp.where(low, o[0:tq], o[tq:2 * tq])
        ob = jnp.where(low, o[2 * tq:3 * tq], o[3 * tq:4 * tq])
        o_ref[0, :, 2 * h * V7X_LANES:(2 * h + 1) * V7X_LANES] = oa.astype(o_ref.dtype)
        o_ref[0, :, (2 * h + 1) * V7X_LANES:(2 * h + 2) * V7X_LANES] = ob.astype(o_ref.dtype)


def _attn_call(q, k, v, kx, vx, sink, local):
    b, l, _ = q.shape
    lx = kx.shape[1]
    tq = WINDOW
    nb = l // tq
    blk = lambda w: (1, tq, w)
    in_specs = [pl.BlockSpec(memory_space=pltpu.SMEM),
                pl.BlockSpec(blk(ATTN_W), lambda bi, i: (bi, i, 0))]
    args = [sink, q]
    if local:
        maps = [lambda bi, i: (bi, jnp.maximum(i - 1, 0), 0),
                lambda bi, i: (bi, i, 0),
                lambda bi, i: (bi, jnp.minimum(i + 1, nb - 1), 0)]
        for arr in (k, v):
            for mp in maps:
                in_specs.append(pl.BlockSpec(blk(KV_DUP_W), mp))
                args.append(arr)
    for arr in (kx, vx):
        in_specs.append(pl.BlockSpec((1, lx, KV_DUP_W), lambda bi, i: (bi, 0, 0)))
        args.append(arr)
    return pl.pallas_call(
        functools.partial(_attn_body, local=local, tq=tq),
        grid=(b, nb),
        in_specs=in_specs,
        out_specs=pl.BlockSpec(blk(ATTN_W), lambda bi, i: (bi, i, 0)),
        out_shape=jax.ShapeDtypeStruct((b, l, ATTN_W), BF16),
        compiler_params=_params("arbitrary", "arbitrary"),
        name="attention",
    )(*args)


def _filter_body(z_ref, w0_ref, b0_ref, w1_ref, b1_ref, fr_ref, w2_ref, dec_ref, o_ref):
    fr = fr_ref[...]
    dot = functools.partial(jnp.dot, precision=HIGHEST, preferred_element_type=F32)
    h = jnp.sin(fr * (dot(z_ref[...], w0_ref[...]) + b0_ref[...]))
    for i in range(FILTER_INNER):
        h = jnp.sin(fr * (dot(h, w1_ref[i]) + b1_ref[i]))
    dec = dec_ref[...]
    for s in range(2 * HYENA_ORDER):
        sl = slice(s * HYENA_W, (s + 1) * HYENA_W)
        o_ref[:, sl] = dot(h, w2_ref[:, sl]) * dec


def _filter_features(l):
    t = jnp.linspace(0.0, 1.0, l, dtype=F32)[:, None]
    w = 2.0 * math.pi * jnp.arange(l, dtype=F32)[:, None] / l
    bands = jnp.linspace(1e-4, FILTER_BANDS - 1, FILTER_BANDS, dtype=F32)[None, :]
    z = jnp.concatenate([t, jnp.cos(bands * w), -jnp.sin(bands * w)], axis=-1)
    deltas = jnp.linspace(math.log(DECAY_TARGET) / SLOW_DECAY_PCT, math.log(DECAY_TARGET) / FAST_DECAY_PCT,
                          HYENA_W, dtype=F32)
    decay = jnp.exp(-t * jnp.abs(deltas))
    return jnp.pad(z, ((0, 0), (0, V7X_LANES - FILTER_EMB))), decay


def _filter_call(l, w0, b0, w1, b1, freq, w2):
    zfeat, decay = _filter_features(l)
    w0p = jnp.pad(w0, ((0, V7X_LANES - FILTER_EMB), (0, 0)))
    tl = min(512, l)
    nf = 2 * HYENA_ORDER * HYENA_W
    full = lambda shape: pl.BlockSpec(shape, lambda i: (0,) * len(shape))
    return pl.pallas_call(
        _filter_body,
        grid=(l // tl,),
        in_specs=[
            pl.BlockSpec((tl, V7X_LANES), lambda i: (i, 0)),
            full((V7X_LANES, FILTER_HIDDEN)),
            full((1, FILTER_HIDDEN)),
            full((FILTER_INNER, FILTER_HIDDEN, FILTER_HIDDEN)),
            full((FILTER_INNER, 1, FILTER_HIDDEN)),
            full((1, FILTER_HIDDEN)),
            full((FILTER_HIDDEN, nf)),
            pl.BlockSpec((tl, HYENA_W), lambda i: (i, 0)),
        ],
        out_specs=pl.BlockSpec((tl, nf), lambda i: (i, 0)),
        out_shape=jax.ShapeDtypeStruct((l, nf), F32),
        compiler_params=_params("arbitrary"),
        name="hyena_filter",
    )(zfeat, w0p, b0.reshape(1, -1), w1, b1.reshape(FILTER_INNER, 1, -1), freq.reshape(1, -1), w2, decay)


def _hyena_blocks(l):
    return max(1, min(4, l // V7X_LANES))


def _dft_matrices(blk):
    n = 2 * blk
    r = jnp.arange(blk, dtype=jnp.int32)
    ang = ((r[:, None] * r[None, :]) % n).astype(F32) * (2.0 * math.pi / n)
    return jnp.cos(ang).astype(BF16), jnp.sin(ang).astype(BF16)


def _alternating(l):
    row = lax.broadcasted_iota(jnp.int32, (l, 1), 0)
    return row, jnp.where(row % 2 == 0, 1.0, -1.0).astype(F32)


def _spectrum_body(hf_ref, hb_ref, fc_ref, fs_ref, ka_ref, kb_ref, kn_ref, *, nblk):
    l = hf_ref.shape[0]
    b = l // nblk
    n = 2 * b
    row = lax.broadcasted_iota(jnp.int32, (l, 1), 0)
    _, alt = _alternating(b)
    hf = hf_ref[...]
    hbs = jnp.where(row == 0, 0.0, pltpu.roll(hb_ref[...], 1, 0))
    fc, fs = fc_ref[...], fs_ref[...]

    def transforms(h):
        out = []
        for k in range(nblk):
            hk = h[k * b:(k + 1) * b]
            hk16 = hk.astype(BF16)
            out.append(dict(
                c=jnp.dot(fc, hk16, preferred_element_type=F32),
                s=jnp.dot(fs, hk16, preferred_element_type=F32),
                first16=hk16[0:1].astype(F32),
                first=hk[0:1],
                alt=jnp.sum(hk * alt, axis=0, keepdims=True)))
        return out

    tf, tb = transforms(hf), transforms(hbs)
    brow = lax.broadcasted_iota(jnp.int32, (b, 1), 0)
    w_re = jnp.where(brow == 0, 1.0 / n, 2.0 / n)
    for d in range(-(nblk - 1), nblk):
        idx = d + nblk - 1
        if d == 0:
            kre = tf[0]["c"] + tb[0]["c"]
            kim = tb[0]["s"] - tf[0]["s"]
            kn = tf[0]["alt"] + tb[0]["alt"]
        else:
            t, e, sg = (tf, d, -1.0) if d > 0 else (tb, -d, 1.0)
            kre = t[e]["c"] + alt * (t[e - 1]["c"] - t[e - 1]["first16"])
            kim = sg * (t[e]["s"] + alt * t[e - 1]["s"])
            kn = t[e]["alt"] + t[e - 1]["alt"] - t[e - 1]["first"]
        ka_ref[0, idx] = kre * w_re
        kb_ref[0, idx] = kim * (2.0 / n)
        kn_ref[0, idx] = kn * (1.0 / n)


def _spectrum_call(filt, fc, fs, nblk, tc):
    l = filt.shape[0]
    b = l // nblk
    nct = HYENA_W // tc
    nlag = 2 * nblk - 1
    return pl.pallas_call(
        functools.partial(_spectrum_body, nblk=nblk),
        grid=(HYENA_ORDER, nct),
        in_specs=[
            pl.BlockSpec((l, tc), lambda o, c: (0, 2 * nct * o + c)),
            pl.BlockSpec((l, tc), lambda o, c: (0, 2 * nct * o + nct + c)),
            _const_spec((b, b)),
            _const_spec((b, b)),
        ],
        out_specs=[
            pl.BlockSpec((1, nlag, b, tc), lambda o, c: (o, 0, 0, c)),
            pl.BlockSpec((1, nlag, b, tc), lambda o, c: (o, 0, 0, c)),
            pl.BlockSpec((1, nlag, 1, tc), lambda o, c: (o, 0, 0, c)),
        ],
        out_shape=[
            jax.ShapeDtypeStruct((HYENA_ORDER, nlag, b, HYENA_W), F32),
            jax.ShapeDtypeStruct((HYENA_ORDER, nlag, b, HYENA_W), F32),
            jax.ShapeDtypeStruct((HYENA_ORDER, nlag, 1, HYENA_W), F32),
        ],
        compiler_params=_params("arbitrary", "arbitrary"),
        name="hyena_spectrum",
    )(filt, filt, fc, fs)


def _conv3(x, w_ref, b_ref, row):
    l = x.shape[0]
    xm = jnp.where(row == 0, 0.0, pltpu.roll(x, 1, 0))
    xp = jnp.where(row == l - 1, 0.0, pltpu.roll(x, l - 1, 0))
    return xm * w_ref[0:1, :] + x * w_ref[1:2, :] + xp * w_ref[2:3, :] + b_ref[...]


def _fftconv_body(*refs, conv_u, nblk):
    it = iter(refs)
    u_ref = next(it)
    if conv_u:
        uw_ref, ub_ref = next(it), next(it)
    g_ref, gw_ref, gb_ref = next(it), next(it), next(it)
    ka_ref, kb_ref, kn_ref, d_ref, fc_ref, fs_ref, o_ref = (next(it) for _ in range(7))

    l = u_ref.shape[1]
    b = l // nblk
    row = lax.broadcasted_iota(jnp.int32, (l, 1), 0)
    _, alt = _alternating(b)
    u = u_ref[0].astype(F32)
    if conv_u:
        u = _conv3(u, uw_ref, ub_ref, row)
    gate = _conv3(g_ref[0], gw_ref, gb_ref, row)
    fc, fs = fc_ref[...], fs_ref[...]

    ps, qs, ns = [], [], []
    for j in range(nblk):
        uj = u[j * b:(j + 1) * b]
        uj16 = uj.astype(BF16)
        ps.append(jnp.dot(fc, uj16, preferred_element_type=F32))
        qs.append(jnp.dot(fs, uj16, preferred_element_type=F32))
        ns.append(jnp.sum(uj * alt, axis=0, keepdims=True))
    for i in range(nblk):
        r = t = nyq = None
        for j in range(nblk):
            lag = i - j + nblk - 1
            ka, kb = ka_ref[0, lag], kb_ref[0, lag]
            dr = ps[j] * ka + qs[j] * kb
            dt = qs[j] * ka - ps[j] * kb
            dn = ns[j] * kn_ref[0, lag]
            r, t, nyq = (dr, dt, dn) if j == 0 else (r + dr, t + dt, nyq + dn)
        y = (jnp.dot(fc, r.astype(BF16), preferred_element_type=F32)
             + jnp.dot(fs, t.astype(BF16), preferred_element_type=F32))
        rows = slice(i * b, (i + 1) * b)
        y = y + alt * nyq + u[rows] * d_ref[0]
        o_ref[0, rows, :] = (gate[rows] * y).astype(o_ref.dtype)


def _fftconv_call(u, u_col0, z, gate_col0, conv_w, conv_b, spectra, d_skip, order, fc, fs, nblk, tc, out_dtype):
    b, l, _ = z.shape
    conv_u = u is z
    nct = HYENA_W // tc
    ka, kb, kn = spectra
    blk = l // nblk
    nlag = 2 * nblk - 1
    col = lambda c0: (lambda c, bi: (bi, 0, c0 // tc + c))
    wcol = lambda c0: (lambda c, bi: (0, (c0 - HY_OFF) // tc + c))
    in_specs = [pl.BlockSpec((1, l, tc), col(u_col0))]
    args = [u]
    if conv_u:
        in_specs += [pl.BlockSpec((3, tc), wcol(u_col0)), pl.BlockSpec((1, tc), wcol(u_col0))]
        args += [conv_w, conv_b]
    in_specs += [pl.BlockSpec((1, l, tc), col(gate_col0)),
                 pl.BlockSpec((3, tc), wcol(gate_col0)), pl.BlockSpec((1, tc), wcol(gate_col0))]
    args += [z, conv_w, conv_b]
    spec = lambda rows: pl.BlockSpec((1, nlag, rows, tc), lambda c, bi: (order, 0, 0, c),
                                     pipeline_mode=pl.Buffered(1))
    in_specs += [spec(blk), spec(blk), spec(1),
                 pl.BlockSpec((1, 1, tc), lambda c, bi: (order, 0, c), pipeline_mode=pl.Buffered(1)),
                 _const_spec((blk, blk)), _const_spec((blk, blk))]
    args += [ka, kb, kn, d_skip, fc, fs]
    return pl.pallas_call(
        functools.partial(_fftconv_body, conv_u=conv_u, nblk=nblk),
        grid=(nct, b),
        in_specs=in_specs,
        out_specs=pl.BlockSpec((1, l, tc), lambda c, bi: (bi, 0, c)),
        out_shape=jax.ShapeDtypeStruct((b, l, HYENA_W), out_dtype),
        compiler_params=_params("arbitrary", "arbitrary"),
        name="hyena_conv",
    )(*args)


def _pool_body(x_ref, w_ref, s_ref, o_ref):
    l = x_ref.shape[1]
    row = lax.broadcasted_iota(jnp.int32, (l, 1), 0)
    for g, win in enumerate(POOL_WINDOWS):
        half = win // 2
        sl = slice(g * POOL_GROUP, (g + 1) * POOL_GROUP)
        x = x_ref[0, :, sl]
        acc = jnp.zeros_like(x)
        for k in range(-half, half):
            shifted = x if k == 0 else pltpu.roll(x, (-k) % l, 0)
            acc = acc + jnp.where((row + k >= 0) & (row + k < l), shifted, 0.0)
        cnt = (jnp.minimum(row + half, l) - jnp.maximum(row - half, 0)).astype(F32)
        d = acc / cnt - x
        y = jnp.dot(d.astype(BF16), w_ref[g], preferred_element_type=F32)
        o_ref[0, :, sl] = (y * s_ref[:, sl]).astype(o_ref.dtype)


def _pool_call(z, w_grp, scale):
    b, l, _ = z.shape
    ng = len(POOL_WINDOWS)
    return pl.pallas_call(
        _pool_body,
        grid=(b,),
        in_specs=[
            pl.BlockSpec((1, l, POOL_W), lambda bi: (bi, 0, POOL_OFF // POOL_W)),
            pl.BlockSpec((ng, POOL_GROUP, POOL_GROUP), lambda bi: (0, 0, 0)),
            pl.BlockSpec((1, POOL_W), lambda bi: (0, 0)),
        ],
        out_specs=pl.BlockSpec((1, l, POOL_W), lambda bi: (bi, 0, 0)),
        out_shape=jax.ShapeDtypeStruct((b, l, POOL_W), BF16),
        compiler_params=_params("arbitrary"),
        name="pool",
    )(z, w_grp, scale.reshape(1, POOL_W))


def _merge_body(ya_ref, yh_ref, yp_ref, gt_ref, x_ref, ga_ref, g2_ref, sc_ref, sh_ref,
                wa_ref, wh_ref, wp_ref, wo_ref, xn_ref, h2_ref):
    d = x_ref.shape[1]
    cj = 512
    ya, yh, yp = ya_ref[...], yh_ref[...], yp_ref[...]
    acc = jnp.zeros(x_ref.shape, F32)
    for j in range(d // cj):
        sl = slice(j * cj, (j + 1) * cj)
        gate = lambda br: gt_ref[:, br * d + j * cj:br * d + (j + 1) * cj].astype(F32)
        m = (gate(0) * jnp.dot(ya, wa_ref[:, sl], preferred_element_type=F32)
             + gate(1) * jnp.dot(yh, wh_ref[:, sl], preferred_element_type=F32)
             + gate(2) * jnp.dot(yp, wp_ref[:, sl], preferred_element_type=F32))
        acc = acc + jnp.dot(m.astype(BF16), wo_ref[sl, :], preferred_element_type=F32)
    xn = x_ref[...] + ga_ref[0] * acc
    xn_ref[...] = xn
    h2_ref[...] = _norm_mod(xn, g2_ref[...], sc_ref[0], sh_ref[0]).astype(h2_ref.dtype)


def _merge_call(ya, yh, yp, gates, x2, ga1, g2, sc2, sh2, wa, wh, wp, wo, layer, rows_per_batch):
    m, d = x2.shape
    tm = min(256, rows_per_batch)
    rpt = rows_per_batch // tm
    rows = lambda w: pl.BlockSpec((tm, w), lambda i: (i, 0))
    weight = lambda w: pl.BlockSpec((None,) + w.shape[1:], lambda i: (layer, 0, 0), pipeline_mode=pl.Buffered(1))
    return pl.pallas_call(
        _merge_body,
        grid=(m // tm,),
        in_specs=[rows(ATTN_W), rows(HYENA_W), rows(POOL_W), rows(N_BRANCH * d), rows(d),
                  _mod_spec(ga1, rpt), pl.BlockSpec((1, d), lambda i: (0, 0)),
                  _mod_spec(sc2, rpt), _mod_spec(sh2, rpt),
                  weight(wa), weight(wh), weight(wp), weight(wo)],
        out_specs=[rows(d), rows(d)],
        out_shape=[jax.ShapeDtypeStruct((m, d), F32), jax.ShapeDtypeStruct((m, d), BF16)],
        compiler_params=_params("arbitrary"),
        name="merge",
    )(ya, yh, yp, gates, x2, ga1, g2.reshape(1, d), sc2, sh2, wa, wh, wp, wo)


def _mlp_body(*refs, has_next):
    it = iter(refs)
    h_ref, w1_ref, w2_ref, x_ref, ga_ref = (next(it) for _ in range(5))
    if has_next:
        gn_ref, sc_ref, sh_ref = next(it), next(it), next(it)
    o_ref = next(it)
    hn_ref = next(it) if has_next else None

    f = pl.program_id(1)
    d = o_ref.shape[1]
    cn = 512

    @pl.when(f == 0)
    def _():
        o_ref[...] = jnp.zeros_like(o_ref)

    a = jnp.dot(h_ref[...], w1_ref[...], preferred_element_type=F32)
    a = jnp.square(jnp.maximum(a, 0.0)).astype(BF16)
    for n0 in range(0, d, cn):
        o_ref[:, n0:n0 + cn] += jnp.dot(a, w2_ref[:, n0:n0 + cn], preferred_element_type=F32)

    @pl.when(f == pl.num_programs(1) - 1)
    def _():
        xo = x_ref[...] + ga_ref[0] * o_ref[...]
        o_ref[...] = xo
        if has_next:
            hn_ref[...] = _norm_mod(xo, gn_ref[...], sc_ref[0], sh_ref[0]).astype(hn_ref.dtype)


def _mlp_call(h2, w1, w2, layer, xn, ga2, nxt, rows_per_batch):
    m, d = xn.shape
    ff = w1.shape[2]
    tm, tf = min(1024, rows_per_batch), 512
    rpt = rows_per_batch // tm
    has_next = nxt is not None
    rows = pl.BlockSpec((tm, d), lambda i, f: (i, 0))
    in_specs = [rows, pl.BlockSpec((None, d, tf), lambda i, f: (layer, 0, f)),
                pl.BlockSpec((None, tf, d), lambda i, f: (layer, f, 0)),
                pl.BlockSpec((tm, d), lambda i, f: (i, 0), pipeline_mode=pl.Buffered(1)), _mod_spec(ga2, rpt)]
    args = [h2, w1, w2, xn, ga2]
    out_specs = [rows]
    out_shape = [jax.ShapeDtypeStruct((m, d), F32)]
    if has_next:
        gn, scn, shn = nxt
        in_specs += [pl.BlockSpec((1, d), lambda i, f: (0, 0)), _mod_spec(scn, rpt), _mod_spec(shn, rpt)]
        args += [gn.reshape(1, d), scn, shn]
        out_specs.append(rows)
        out_shape.append(jax.ShapeDtypeStruct((m, d), BF16))
    outs = pl.pallas_call(
        functools.partial(_mlp_body, has_next=has_next),
        grid=(m // tm, ff // tf),
        in_specs=in_specs,
        out_specs=out_specs,
        out_shape=out_shape,
        compiler_params=_params("arbitrary", "arbitrary"),
        name="mlp",
    )(*args)
    return (outs[0], outs[1]) if has_next else (outs[0], None)


def _mixers(z, q, k, v, kx, vx, sink, local, hy, pool_w, pool_scale):
    b, l, _ = z.shape
    y_att = _attn_call(q, k, v, kx, vx, sink, local)
    conv_w, conv_b, filt_params, d_skip, dft = hy
    fc, fs = dft
    nblk = _hyena_blocks(l)
    tc = 256
    spectra = _spectrum_call(_filter_call(l, *filt_params), fc, fs, nblk, tc)
    conv = functools.partial(_fftconv_call, conv_w=conv_w, conv_b=conv_b, spectra=spectra, d_skip=d_skip,
                             fc=fc, fs=fs, nblk=nblk, tc=tc)
    z1 = conv(z, HY_OFF, z, HY_OFF + HYENA_W, order=0, out_dtype=F32)
    y_hy = conv(z1, 0, z, HY_OFF + 2 * HYENA_W, order=1, out_dtype=BF16)
    y_pool = _pool_call(z, pool_w, pool_scale)
    return (y_att.reshape(b * l, ATTN_W), y_hy.reshape(b * l, HYENA_W), y_pool.reshape(b * l, POOL_W))


def kernel(x, c, ctx, c_ctx, norm1_g, norm2_g, w_mod, b_mod, w_in, q_norm_g, k_norm_g, sink, hy_conv_w, hy_conv_b, filt_w0, filt_b0, filt_w1, filt_b1, filt_freq, filt_w2, hy_bias, pool_w, pool_scale, w_att_o, w_hy_o, w_pool_o, w_out, mlp_w1, mlp_w2):
    b, l, d = x.shape
    lc = ctx.shape[1]
    depth = w_mod.shape[0]

    cc = jnp.concatenate([c, c_ctx[None, :], jnp.zeros((MOD_ROWS - b - 1, d), F32)], axis=0)
    mods = _modulation(cc, w_mod, b_mod)

    def chunks(layer, lo, hi):
        return [mods[layer, lo:hi, i * d:(i + 1) * d].reshape(hi - lo, 1, d) for i in range(6)]

    as_bf16 = lambda w: w.astype(BF16)
    w_in_b, w_att_b, w_hy_b, w_pool_b, w_out_b = map(as_bf16, (w_in, w_att_o, w_hy_o, w_pool_o, w_out))
    w1_b, w2_b, pool_w_b = map(as_bf16, (mlp_w1, mlp_w2, pool_w))

    rope_tabs = _rope_tables(l)
    dft_x = _dft_matrices(l // _hyena_blocks(l))
    dft_c = _dft_matrices(lc // _hyena_blocks(lc))

    x2 = x.reshape(b * l, d)
    c2 = ctx.reshape(b * lc, d)
    sh1, sc1 = chunks(0, 0, b)[:2]
    csh1, csc1 = chunks(0, b, b + 1)[:2]
    hx = _norm_call(x2, norm1_g[0], sc1, sh1, l)
    hc = _norm_call(c2, norm1_g[0], csc1, csh1, lc)

    for layer in range(depth):
        last = layer == depth - 1
        _, _, ga1, sh2, sc2, ga2 = chunks(layer, 0, b)
        _, _, cga1, csh2, csc2, cga2 = chunks(layer, b, b + 1)
        filt_params = (filt_w0[layer], filt_b0[layer], filt_w1[layer], filt_b1[layer], filt_freq[layer],
                       filt_w2[layer])
        conv_b = hy_conv_b[layer].reshape(1, -1)
        d_skip = hy_bias[layer].reshape(HYENA_ORDER, 1, HYENA_W)
        merge_w = (w_att_b, w_hy_b, w_pool_b, w_out_b, layer)

        if last:
            zc = _proj_call(hc, w_in_b, layer, K_OFF, 2 * KV_W, F32).reshape(b, lc, 2 * KV_W)
        else:
            zc = _proj_call(hc, w_in_b, layer, 0, GATE_OFF, F32).reshape(b, lc, GATE_OFF)
        qc, kc, vc = _qkv_call(zc, q_norm_g[layer], k_norm_g[layer], None, has_q=not last)

        zx = _proj_call(hx, w_in_b, layer, 0, GATE_OFF, F32).reshape(b, l, GATE_OFF)
        gx = _proj_call(hx, w_in_b, layer, GATE_OFF, N_BRANCH * d, BF16, sigmoid=True)
        qx, kx, vx = _qkv_call(zx, q_norm_g[layer], k_norm_g[layer], rope_tabs, has_q=True)
        hy = (hy_conv_w[layer], conv_b, filt_params, d_skip, dft_x)
        ya, yh, yp = _mixers(zx, qx, kx, vx, kc, vc, sink[layer], True, hy, pool_w_b[layer], pool_scale[layer])
        xn, h2 = _merge_call(ya, yh, yp, gx, x2, ga1, norm2_g[layer], sc2, sh2, *merge_w, l)
        nxt = None if last else (norm1_g[layer + 1], *reversed(chunks(layer + 1, 0, b)[:2]))
        x2, hx = _mlp_call(h2, w1_b, w2_b, layer, xn, ga2, nxt, l)

        if not last:
            gc = _proj_call(hc, w_in_b, layer, GATE_OFF, N_BRANCH * d, BF16, sigmoid=True)
            hyc = (hy_conv_w[layer], conv_b, filt_params, d_skip, dft_c)
            ya, yh, yp = _mixers(zc, qc, None, None, kc, vc, sink[layer], False, hyc, pool_w_b[layer],
                                 pool_scale[layer])
            cn, h2c = _merge_call(ya, yh, yp, gc, c2, cga1, norm2_g[layer], csc2, csh2, *merge_w, lc)
            nxt = (norm1_g[layer + 1], *reversed(chunks(layer + 1, b, b + 1)[:2]))
            c2, hc = _mlp_call(h2c, w1_b, w2_b, layer, cn, cga2, nxt, lc)

    return x2.reshape(b, l, d)
```

```python
import functools
import math

import jax
import jax.numpy as jnp
from jax import lax
from jax.experimental import pallas as pl
from jax.experimental.pallas import tpu as pltpu

D_MODEL = 2048
DEPTH = 2
GRID_W = 64
EPS = 1e-6
NEG_INF = -1e30

N_HEADS = 16
N_KV_HEADS = 4
GQA_GROUP = N_HEADS // N_KV_HEADS
HEAD_DIM = 64
ATTN_W = N_HEADS * HEAD_DIM
KV_W = N_KV_HEADS * HEAD_DIM
WINDOW = 128
ROPE_FREQS = HEAD_DIM // 4
ROPE_BASE = 10000.0

HYENA_W = D_MODEL // 4
HYENA_ORDER = 2
FILTER_BANDS = 16
FILTER_EMB = 1 + 2 * FILTER_BANDS
FILTER_HIDDEN = 64
FILTER_INNER = 2
DECAY_TARGET = 1e-2
FAST_DECAY_PCT = 0.3
SLOW_DECAY_PCT = 1.5

POOL_W = D_MODEL // 4
POOL_WINDOWS = (2, 4, 8, 16)
POOL_GROUP = POOL_W // len(POOL_WINDOWS)

N_BRANCH = 3
D_FF = 4 * D_MODEL

Q_OFF = 0
K_OFF = Q_OFF + ATTN_W
V_OFF = K_OFF + KV_W
HY_OFF = V_OFF + KV_W
POOL_OFF = HY_OFF + 3 * HYENA_W
GATE_OFF = POOL_OFF + POOL_W
IN_W = GATE_OFF + N_BRANCH * D_MODEL

V7X_LANES = 128
V7X_VMEM_LIMIT = 60 * 1024 * 1024
KV_DUP_W = N_KV_HEADS * V7X_LANES
MOD_ROWS = 24

F32 = jnp.float32
BF16 = jnp.bfloat16
HIGHEST = lax.Precision.HIGHEST


def _params(*semantics):
    return pltpu.CompilerParams(dimension_semantics=semantics, vmem_limit_bytes=V7X_VMEM_LIMIT)


def _const_spec(shape):
    zeros = (0,) * len(shape)
    return pl.BlockSpec(shape, lambda *_: zeros, pipeline_mode=pl.Buffered(1))


def _mod_spec(arr, rows_per_mod_tile):
    d = arr.shape[-1]
    if arr.shape[0] == 1:
        return pl.BlockSpec((1, 1, d), lambda i, *_: (0, 0, 0))
    return pl.BlockSpec((1, 1, d), lambda i, *_: (i // rows_per_mod_tile, 0, 0))


def _norm_mod(xf, g, sc, sh):
    y = xf * lax.rsqrt(jnp.mean(xf * xf, axis=-1, keepdims=True) + EPS)
    return (y * g) * (1.0 + sc) + sh


def _mod_body(c_ref, w_ref, b_ref, o_ref):
    c = c_ref[...]
    s = c * jax.nn.sigmoid(c)
    o_ref[0] = jnp.dot(s.astype(BF16), w_ref[0].astype(BF16), preferred_element_type=F32) + b_ref[0]


def _modulation(cc, w_mod, b_mod):
    depth, d, n = w_mod.shape
    tn = 1024
    return pl.pallas_call(
        _mod_body,
        grid=(depth, n // tn),
        in_specs=[
            pl.BlockSpec((MOD_ROWS, d), lambda l, j: (0, 0)),
            pl.BlockSpec((1, d, tn), lambda l, j: (l, 0, j)),
            pl.BlockSpec((1, 1, tn), lambda l, j: (l, 0, j)),
        ],
        out_specs=pl.BlockSpec((1, MOD_ROWS, tn), lambda l, j: (l, 0, j)),
        out_shape=jax.ShapeDtypeStruct((depth, MOD_ROWS, n), F32),
        compiler_params=_params("arbitrary", "arbitrary"),
        name="modulation",
    )(cc, w_mod, b_mod.reshape(depth, 1, n))


def _norm_body(x_ref, g_ref, sc_ref, sh_ref, o_ref):
    o_ref[...] = _norm_mod(x_ref[...], g_ref[...], sc_ref[0], sh_ref[0]).astype(o_ref.dtype)


def _norm_call(x2, g, sc, sh, rows_per_batch):
    m, d = x2.shape
    tm = min(512, rows_per_batch)
    return pl.pallas_call(
        _norm_body,
        grid=(m // tm,),
        in_specs=[
            pl.BlockSpec((tm, d), lambda i: (i, 0)),
            pl.BlockSpec((1, d), lambda i: (0, 0)),
            _mod_spec(sc, rows_per_batch // tm),
            _mod_spec(sh, rows_per_batch // tm),
        ],
        out_specs=pl.BlockSpec((tm, d), lambda i: (i, 0)),
        out_shape=jax.ShapeDtypeStruct((m, d), BF16),
        compiler_params=_params("arbitrary"),
        name="norm_mod",
    )(x2, g.reshape(1, d), sc, sh)


def _proj_body(a_ref, w_ref, o_ref, *, sigmoid):
    z = jnp.dot(a_ref[...], w_ref[...], preferred_element_type=F32)
    if sigmoid:
        z = 0.5 * jnp.tanh(0.5 * z) + 0.5
    o_ref[...] = z.astype(o_ref.dtype)


def _proj_call(a, w, layer, col0, n, out_dtype, sigmoid=False):
    m, k = a.shape
    tm = min(2048, m)
    tn = 512
    c0 = col0 // tn
    return pl.pallas_call(
        functools.partial(_proj_body, sigmoid=sigmoid),
        grid=(m // tm, n // tn),
        in_specs=[
            pl.BlockSpec((tm, k), lambda i, j: (i, 0)),
            pl.BlockSpec((None, k, tn), lambda i, j: (layer, 0, c0 + j)),
        ],
        out_specs=pl.BlockSpec((tm, tn), lambda i, j: (i, j)),
        out_shape=jax.ShapeDtypeStruct((m, n), out_dtype),
        compiler_params=_params("arbitrary", "arbitrary"),
        name="gates_proj" if sigmoid else "in_proj",
    )(a, w)


def _pair_block_diag():
    r = lax.broadcasted_iota(jnp.int32, (V7X_LANES, V7X_LANES), 0) // HEAD_DIM
    c = lax.broadcasted_iota(jnp.int32, (V7X_LANES, V7X_LANES), 1) // HEAD_DIM
    return (r == c).astype(F32)


def _head_norm(x, g, bd):
    ss = jnp.dot(x * x, bd, precision=HIGHEST, preferred_element_type=F32)
    return (x * lax.rsqrt(ss * (1.0 / HEAD_DIM) + EPS)) * g


def _rope(x, cos, sin_up, sin_dn):
    up = pltpu.roll(x, V7X_LANES - ROPE_FREQS, 1)
    dn = pltpu.roll(x, ROPE_FREQS, 1)
    return x * cos + up * sin_up + dn * sin_dn


def _dup_pair(x, low):
    r = pltpu.roll(x, HEAD_DIM, 1)
    return jnp.where(low, x, r), jnp.where(low, r, x)


def _qkv_body(*refs, has_q, rope):
    it = iter(refs)
    z_ref = next(it)
    gq_ref = next(it) if has_q else None
    gk_ref = next(it)
    if rope:
        cos_ref, sup_ref, sdn_ref = next(it), next(it), next(it)
    q_ref = next(it) if has_q else None
    k_ref, v_ref = next(it), next(it)

    bd = _pair_block_diag()
    low = lax.broadcasted_iota(jnp.int32, (1, V7X_LANES), 1) < HEAD_DIM
    k0 = ATTN_W if has_q else 0
    v0 = k0 + KV_W

    def prep(x, g):
        y = _head_norm(x, g, bd)
        if rope:
            y = _rope(y, cos_ref[...], sup_ref[...], sdn_ref[...])
        return y

    if has_q:
        for s in range(ATTN_W // V7X_LANES):
            sl = slice(s * V7X_LANES, (s + 1) * V7X_LANES)
            y = prep(z_ref[0, :, sl], gq_ref[...])
            q_ref[0, :, sl] = (y * (HEAD_DIM ** -0.5)).astype(q_ref.dtype)
    for s in range(KV_W // V7X_LANES):
        kp = prep(z_ref[0, :, k0 + s * V7X_LANES:k0 + (s + 1) * V7X_LANES], gk_ref[...])
        vp = z_ref[0, :, v0 + s * V7X_LANES:v0 + (s + 1) * V7X_LANES]
        for src, dst in ((kp, k_ref), (vp, v_ref)):
            a, b = _dup_pair(src, low)
            base = 2 * s * V7X_LANES
            dst[0, :, base:base + V7X_LANES] = a.astype(dst.dtype)
            dst[0, :, base + V7X_LANES:base + 2 * V7X_LANES] = b.astype(dst.dtype)


def _qkv_call(z, gq, gk, rope_tabs, has_q):
    b, l, nz = z.shape
    tm = min(512, l)
    nq = ATTN_W + 2 * KV_W if has_q else 2 * KV_W
    rope = rope_tabs is not None
    lane_spec = pl.BlockSpec((1, V7X_LANES), lambda bi, i: (0, 0))
    in_specs = [pl.BlockSpec((1, tm, nq), lambda bi, i: (bi, i, 0))]
    args = [z]
    if has_q:
        in_specs.append(lane_spec)
        args.append(jnp.tile(gq, 2).reshape(1, V7X_LANES))
    in_specs.append(lane_spec)
    args.append(jnp.tile(gk, 2).reshape(1, V7X_LANES))
    if rope:
        in_specs += [pl.BlockSpec((tm, V7X_LANES), lambda bi, i: (i, 0))] * 3
        args += list(rope_tabs)
    out_specs, out_shape = [], []
    if has_q:
        out_specs.append(pl.BlockSpec((1, tm, ATTN_W), lambda bi, i: (bi, i, 0)))
        out_shape.append(jax.ShapeDtypeStruct((b, l, ATTN_W), BF16))
    for _ in range(2):
        out_specs.append(pl.BlockSpec((1, tm, KV_DUP_W), lambda bi, i: (bi, i, 0)))
        out_shape.append(jax.ShapeDtypeStruct((b, l, KV_DUP_W), BF16))
    outs = pl.pallas_call(
        functools.partial(_qkv_body, has_q=has_q, rope=rope),
        grid=(b, l // tm),
        in_specs=in_specs,
        out_specs=out_specs,
        out_shape=out_shape,
        compiler_params=_params("arbitrary", "arbitrary"),
        name="qkv_prep",
    )(*args)
    return outs if has_q else (None, *outs)


def _rope_tables(l):
    rows = l // GRID_W
    row = jnp.repeat(jnp.arange(rows, dtype=F32), GRID_W)
    col = jnp.tile(jnp.arange(GRID_W, dtype=F32), rows)
    inv = ROPE_BASE ** (-jnp.arange(ROPE_FREQS, dtype=F32) / ROPE_FREQS)
    ang = jnp.stack([row[:, None] * inv, col[:, None] * inv], axis=1)
    cos, sin = jnp.cos(ang), jnp.sin(ang)
    zero = jnp.zeros_like(sin)
    cos_h = jnp.stack([cos, cos], axis=2).reshape(l, HEAD_DIM)
    sup_h = jnp.stack([-sin, zero], axis=2).reshape(l, HEAD_DIM)
    sdn_h = jnp.stack([zero, sin], axis=2).reshape(l, HEAD_DIM)
    return tuple(jnp.tile(t, (1, 2)) for t in (cos_h, sup_h, sdn_h))


def _attn_body(*refs, local, tq):
    it = iter(refs)
    sink_ref = next(it)
    q_ref = next(it)
    if local:
        kp_ref, kc_ref, kn_ref, vp_ref, vc_ref, vn_ref = (next(it) for _ in range(6))
    kx_ref, vx_ref = next(it), next(it)
    o_ref = next(it)

    i = pl.program_id(1)
    nb = pl.num_programs(1)
    low = lax.broadcasted_iota(jnp.int32, (1, V7X_LANES), 1) < HEAD_DIM
    rows = GQA_GROUP * tq
    if local:
        qi = lax.broadcasted_iota(jnp.int32, (rows, tq), 0) % tq
        kj = lax.broadcasted_iota(jnp.int32, (rows, tq), 1)
        mask_prev = (kj >= qi) & (i > 0)
        mask_next = (kj <= qi) & (i < nb - 1)
    row_head = lax.broadcasted_iota(jnp.int32, (rows, 1), 0) // tq
    zero = jnp.zeros((), q_ref.dtype)

    for h in range(N_KV_HEADS):
        hs = slice(h * V7X_LANES, (h + 1) * V7X_LANES)
        qa = q_ref[0, :, 2 * h * V7X_LANES:(2 * h + 1) * V7X_LANES]
        qb = q_ref[0, :, (2 * h + 1) * V7X_LANES:(2 * h + 2) * V7X_LANES]
        qs = jnp.concatenate([jnp.where(low, qa, zero), jnp.where(low, zero, qa),
                              jnp.where(low, qb, zero), jnp.where(low, zero, qb)], axis=0)
        kparts, vparts, masks = [kx_ref[0, :, hs]], [vx_ref[0, :, hs]], {}
        if local:
            kparts = [kp_ref[0, :, hs], kc_ref[0, :, hs], kn_ref[0, :, hs]] + kparts
            vparts = [vp_ref[0, :, hs], vc_ref[0, :, hs], vn_ref[0, :, hs]] + vparts
            masks = {0: mask_prev, 2: mask_next}
        k_all = jnp.concatenate(kparts, axis=0)
        v_all = jnp.concatenate(vparts, axis=0)

        sink = jnp.zeros((rows, 1), F32)
        for g in range(GQA_GROUP):
            sink = jnp.where(row_head == g, sink_ref[GQA_GROUP * h + g], sink)
        s_all = lax.dot_general(qs, k_all, (((1,), (1,)), ((), ())), preferred_element_type=F32)
        chunks = []
        for c in range(k_all.shape[0] // tq):
            s = s_all[:, c * tq:(c + 1) * tq]
            chunks.append(jnp.where(masks[c], s, NEG_INF) if c in masks else s)
        m = jnp.maximum(sink, jnp.max(functools.reduce(jnp.maximum, chunks), axis=-1, keepdims=True))
        probs = [jnp.exp(s - m) for s in chunks]
        denom = jnp.exp(sink - m) + jnp.sum(functools.reduce(jnp.add, probs), axis=-1, keepdims=True)
        p_all = jnp.concatenate([p.astype(v_all.dtype) for p in probs], axis=1)
        o = jnp.dot(p_all, v_all, preferred_element_type=F32) / denom
        oa = jnp.where(low, o[0:tq], o[tq:2 * tq])
        ob = jnp.where(low, o[2 * tq:3 * tq], o[3 * tq:4 * tq])
        o_ref[0, :, 2 * h * V7X_LANES:(2 * h + 1) * V7X_LANES] = oa.astype(o_ref.dtype)
        o_ref[0, :, (2 * h + 1) * V7X_LANES:(2 * h + 2) * V7X_LANES] = ob.astype(o_ref.dtype)


def _attn_call(q, k, v, kx, vx, sink, local):
    b, l, _ = q.shape
    lx = kx.shape[1]
    tq = WINDOW
    nb = l // tq
    blk = lambda w: (1, tq, w)
    in_specs = [pl.BlockSpec(memory_space=pltpu.SMEM),
                pl.BlockSpec(blk(ATTN_W), lambda bi, i: (bi, i, 0))]
    args = [sink, q]
    if local:
        maps = [lambda bi, i: (bi, jnp.maximum(i - 1, 0), 0),
                lambda bi, i: (bi, i, 0),
                lambda bi, i: (bi, jnp.minimum(i + 1, nb - 1), 0)]
        for arr in (k, v):
            for mp in maps:
                in_specs.append(pl.BlockSpec(blk(KV_DUP_W), mp))
                args.append(arr)
    for arr in (kx, vx):
        in_specs.append(pl.BlockSpec((1, lx, KV_DUP_W), lambda bi, i: (bi, 0, 0)))
        args.append(arr)
    return pl.pallas_call(
        functools.partial(_attn_body, local=local, tq=tq),
        grid=(b, nb),
        in_specs=in_specs,
        out_specs=pl.BlockSpec(blk(ATTN_W), lambda bi, i: (bi, i, 0)),
        out_shape=jax.ShapeDtypeStruct((b, l, ATTN_W), BF16),
        compiler_params=_params("arbitrary", "arbitrary"),
        name="attention",
    )(*args)


def _filter_body(z_ref, w0_ref, b0_ref, w1_ref, b1_ref, fr_ref, w2_ref, dec_ref, o_ref):
    fr = fr_ref[...]
    dot = functools.partial(jnp.dot, precision=HIGHEST, preferred_element_type=F32)
    h = jnp.sin(fr * (dot(z_ref[...], w0_ref[...]) + b0_ref[...]))
    for i in range(FILTER_INNER):
        h = jnp.sin(fr * (dot(h, w1_ref[i]) + b1_ref[i]))
    dec = dec_ref[...]
    for s in range(2 * HYENA_ORDER):
        sl = slice(s * HYENA_W, (s + 1) * HYENA_W)
        o_ref[:, sl] = dot(h, w2_ref[:, sl]) * dec


def _filter_features(l):
    t = jnp.linspace(0.0, 1.0, l, dtype=F32)[:, None]
    w = 2.0 * math.pi * jnp.arange(l, dtype=F32)[:, None] / l
    bands = jnp.linspace(1e-4, FILTER_BANDS - 1, FILTER_BANDS, dtype=F32)[None, :]
    z = jnp.concatenate([t, jnp.cos(bands * w), -jnp.sin(bands * w)], axis=-1)
    deltas = jnp.linspace(math.log(DECAY_TARGET) / SLOW_DECAY_PCT, math.log(DECAY_TARGET) / FAST_DECAY_PCT,
                          HYENA_W, dtype=F32)
    decay = jnp.exp(-t * jnp.abs(deltas))
    return jnp.pad(z, ((0, 0), (0, V7X_LANES - FILTER_EMB))), decay


def _filter_call(l, w0, b0, w1, b1, freq, w2):
    zfeat, decay = _filter_features(l)
    w0p = jnp.pad(w0, ((0, V7X_LANES - FILTER_EMB), (0, 0)))
    tl = min(512, l)
    nf = 2 * HYENA_ORDER * HYENA_W
    full = lambda shape: pl.BlockSpec(shape, lambda i: (0,) * len(shape))
    return pl.pallas_call(
        _filter_body,
        grid=(l // tl,),
        in_specs=[
            pl.BlockSpec((tl, V7X_LANES), lambda i: (i, 0)),
            full((V7X_LANES, FILTER_HIDDEN)),
            full((1, FILTER_HIDDEN)),
            full((FILTER_INNER, FILTER_HIDDEN, FILTER_HIDDEN)),
            full((FILTER_INNER, 1, FILTER_HIDDEN)),
            full((1, FILTER_HIDDEN)),
            full((FILTER_HIDDEN, nf)),
            pl.BlockSpec((tl, HYENA_W), lambda i: (i, 0)),
        ],
        out_specs=pl.BlockSpec((tl, nf), lambda i: (i, 0)),
        out_shape=jax.ShapeDtypeStruct((l, nf), F32),
        compiler_params=_params("arbitrary"),
        name="hyena_filter",
    )(zfeat, w0p, b0.reshape(1, -1), w1, b1.reshape(FILTER_INNER, 1, -1), freq.reshape(1, -1), w2, decay)


def _hyena_blocks(l):
    return max(1, min(4, l // V7X_LANES))


def _dft_matrices(blk):
    n = 2 * blk
    r = jnp.arange(blk, dtype=jnp.int32)
    ang = ((r[:, None] * r[None, :]) % n).astype(F32) * (2.0 * math.pi / n)
    return jnp.cos(ang).astype(BF16), jnp.sin(ang).astype(BF16)


def _alternating(l):
    row = lax.broadcasted_iota(jnp.int32, (l, 1), 0)
    return row, jnp.where(row % 2 == 0, 1.0, -1.0).astype(F32)


def _spectrum_body(hf_ref, hb_ref, fc_ref, fs_ref, ka_ref, kb_ref, kn_ref, *, nblk):
    l = hf_ref.shape[0]
    b = l // nblk
    n = 2 * b
    row = lax.broadcasted_iota(jnp.int32, (l, 1), 0)
    _, alt = _alternating(b)
    hf = hf_ref[...]
    hbs = jnp.where(row == 0, 0.0, pltpu.roll(hb_ref[...], 1, 0))
    fc, fs = fc_ref[...], fs_ref[...]

    def transforms(h):
        out = []
        for k in range(nblk):
            hk = h[k * b:(k + 1) * b]
            hk16 = hk.astype(BF16)
            out.append(dict(
                c=jnp.dot(fc, hk16, preferred_element_type=F32),
                s=jnp.dot(fs, hk16, preferred_element_type=F32),
                first16=hk16[0:1].astype(F32),
                first=hk[0:1],
                alt=jnp.sum(hk * alt, axis=0, keepdims=True)))
        return out

    tf, tb = transforms(hf), transforms(hbs)
    brow = lax.broadcasted_iota(jnp.int32, (b, 1), 0)
    w_re = jnp.where(brow == 0, 1.0 / n, 2.0 / n)
    for d in range(-(nblk - 1), nblk):
        idx = d + nblk - 1
        if d == 0:
            kre = tf[0]["c"] + tb[0]["c"]
            kim = tb[0]["s"] - tf[0]["s"]
            kn = tf[0]["alt"] + tb[0]["alt"]
        else:
            t, e, sg = (tf, d, -1.0) if d > 0 else (tb, -d, 1.0)
            kre = t[e]["c"] + alt * (t[e - 1]["c"] - t[e - 1]["first16"])
            kim = sg * (t[e]["s"] + alt * t[e - 1]["s"])
            kn = t[e]["alt"] + t[e - 1]["alt"] - t[e - 1]["first"]
        ka_ref[0, idx] = kre * w_re
        kb_ref[0, idx] = kim * (2.0 / n)
        kn_ref[0, idx] = kn * (1.0 / n)


def _spectrum_call(filt, fc, fs, nblk, tc):
    l = filt.shape[0]
    b = l // nblk
    nct = HYENA_W // tc
    nlag = 2 * nblk - 1
    return pl.pallas_call(
        functools.partial(_spectrum_body, nblk=nblk),
        grid=(HYENA_ORDER, nct),
        in_specs=[
            pl.BlockSpec((l, tc), lambda o, c: (0, 2 * nct * o + c)),
            pl.BlockSpec((l, tc), lambda o, c: (0, 2 * nct * o + nct + c)),
            _const_spec((b, b)),
            _const_spec((b, b)),
        ],
        out_specs=[
            pl.BlockSpec((1, nlag, b, tc), lambda o, c: (o, 0, 0, c)),
            pl.BlockSpec((1, nlag, b, tc), lambda o, c: (o, 0, 0, c)),
            pl.BlockSpec((1, nlag, 1, tc), lambda o, c: (o, 0, 0, c)),
        ],
        out_shape=[
            jax.ShapeDtypeStruct((HYENA_ORDER, nlag, b, HYENA_W), F32),
            jax.ShapeDtypeStruct((HYENA_ORDER, nlag, b, HYENA_W), F32),
            jax.ShapeDtypeStruct((HYENA_ORDER, nlag, 1, HYENA_W), F32),
        ],
        compiler_params=_params("arbitrary", "arbitrary"),
        name="hyena_spectrum",
    )(filt, filt, fc, fs)


def _conv3(x, w_ref, b_ref, row):
    l = x.shape[0]
    xm = jnp.where(row == 0, 0.0, pltpu.roll(x, 1, 0))
    xp = jnp.where(row == l - 1, 0.0, pltpu.roll(x, l - 1, 0))
    return xm * w_ref[0:1, :] + x * w_ref[1:2, :] + xp * w_ref[2:3, :] + b_ref[...]


def _fftconv_body(*refs, conv_u, nblk):
    it = iter(refs)
    u_ref = next(it)
    if conv_u:
        uw_ref, ub_ref = next(it), next(it)
    g_ref, gw_ref, gb_ref = next(it), next(it), next(it)
    ka_ref, kb_ref, kn_ref, d_ref, fc_ref, fs_ref, o_ref = (next(it) for _ in range(7))

    l = u_ref.shape[1]
    b = l // nblk
    row = lax.broadcasted_iota(jnp.int32, (l, 1), 0)
    _, alt = _alternating(b)
    u = u_ref[0].astype(F32)
    if conv_u:
        u = _conv3(u, uw_ref, ub_ref, row)
    gate = _conv3(g_ref[0], gw_ref, gb_ref, row)
    fc, fs = fc_ref[...], fs_ref[...]

    ps, qs, ns = [], [], []
    for j in range(nblk):
        uj = u[j * b:(j + 1) * b]
        uj16 = uj.astype(BF16)
        ps.append(jnp.dot(fc, uj16, preferred_element_type=F32))
        qs.append(jnp.dot(fs, uj16, preferred_element_type=F32))
        ns.append(jnp.sum(uj * alt, axis=0, keepdims=True))
    for i in range(nblk):
        r = t = nyq = None
        for j in range(nblk):
            lag = i - j + nblk - 1
            ka, kb = ka_ref[0, lag], kb_ref[0, lag]
            dr = ps[j] * ka + qs[j] * kb
            dt = qs[j] * ka - ps[j] * kb
            dn = ns[j] * kn_ref[0, lag]
            r, t, nyq = (dr, dt, dn) if j == 0 else (r + dr, t + dt, nyq + dn)
        y = (jnp.dot(fc, r.astype(BF16), preferred_element_type=F32)
             + jnp.dot(fs, t.astype(BF16), preferred_element_type=F32))
        rows = slice(i * b, (i + 1) * b)
        y = y + alt * nyq + u[rows] * d_ref[0]
        o_ref[0, rows, :] = (gate[rows] * y).astype(o_ref.dtype)


def _fftconv_call(u, u_col0, z, gate_col0, conv_w, conv_b, spectra, d_skip, order, fc, fs, nblk, tc, out_dtype):
    b, l, _ = z.shape
    conv_u = u is z
    nct = HYENA_W // tc
    ka, kb, kn = spectra
    blk = l // nblk
    nlag = 2 * nblk - 1
    col = lambda c0: (lambda c, bi: (bi, 0, c0 // tc + c))
    wcol = lambda c0: (lambda c, bi: (0, (c0 - HY_OFF) // tc + c))
    in_specs = [pl.BlockSpec((1, l, tc), col(u_col0))]
    args = [u]
    if conv_u:
        in_specs += [pl.BlockSpec((3, tc), wcol(u_col0)), pl.BlockSpec((1, tc), wcol(u_col0))]
        args += [conv_w, conv_b]
    in_specs += [pl.BlockSpec((1, l, tc), col(gate_col0)),
                 pl.BlockSpec((3, tc), wcol(gate_col0)), pl.BlockSpec((1, tc), wcol(gate_col0))]
    args += [z, conv_w, conv_b]
    spec = lambda rows: pl.BlockSpec((1, nlag, rows, tc), lambda c, bi: (order, 0, 0, c),
                                     pipeline_mode=pl.Buffered(1))
    in_specs += [spec(blk), spec(blk), spec(1),
                 pl.BlockSpec((1, 1, tc), lambda c, bi: (order, 0, c), pipeline_mode=pl.Buffered(1)),
                 _const_spec((blk, blk)), _const_spec((blk, blk))]
    args += [ka, kb, kn, d_skip, fc, fs]
    return pl.pallas_call(
        functools.partial(_fftconv_body, conv_u=conv_u, nblk=nblk),
        grid=(nct, b),
        in_specs=in_specs,
        out_specs=pl.BlockSpec((1, l, tc), lambda c, bi: (bi, 0, c)),
        out_shape=jax.ShapeDtypeStruct((b, l, HYENA_W), out_dtype),
        compiler_params=_params("arbitrary", "arbitrary"),
        name="hyena_conv",
    )(*args)


def _pool_body(x_ref, w_ref, s_ref, o_ref):
    l = x_ref.shape[1]
    row = lax.broadcasted_iota(jnp.int32, (l, 1), 0)
    for g, win in enumerate(POOL_WINDOWS):
        half = win // 2
        sl = slice(g * POOL_GROUP, (g + 1) * POOL_GROUP)
        x = x_ref[0, :, sl]
        acc = jnp.zeros_like(x)
        for k in range(-half, half):
            shifted = x if k == 0 else pltpu.roll(x, (-k) % l, 0)
            acc = acc + jnp.where((row + k >= 0) & (row + k < l), shifted, 0.0)
        cnt = (jnp.minimum(row + half, l) - jnp.maximum(row - half, 0)).astype(F32)
        d = acc / cnt - x
        y = jnp.dot(d.astype(BF16), w_ref[g], preferred_element_type=F32)
        o_ref[0, :, sl] = (y * s_ref[:, sl]).astype(o_ref.dtype)


def _pool_call(z, w_grp, scale):
    b, l, _ = z.shape
    ng = len(POOL_WINDOWS)
    return pl.pallas_call(
        _pool_body,
        grid=(b,),
        in_specs=[
            pl.BlockSpec((1, l, POOL_W), lambda bi: (bi, 0, POOL_OFF // POOL_W)),
            pl.BlockSpec((ng, POOL_GROUP, POOL_GROUP), lambda bi: (0, 0, 0)),
            pl.BlockSpec((1, POOL_W), lambda bi: (0, 0)),
        ],
        out_specs=pl.BlockSpec((1, l, POOL_W), lambda bi: (bi, 0, 0)),
        out_shape=jax.ShapeDtypeStruct((b, l, POOL_W), BF16),
        compiler_params=_params("arbitrary"),
        name="pool",
    )(z, w_grp, scale.reshape(1, POOL_W))


def _merge_body(ya_ref, yh_ref, yp_ref, gt_ref, x_ref, ga_ref, g2_ref, sc_ref, sh_ref,
                wa_ref, wh_ref, wp_ref, wo_ref, xn_ref, h2_ref):
    d = x_ref.shape[1]
    cj = 512
    ya, yh, yp = ya_ref[...], yh_ref[...], yp_ref[...]
    acc = jnp.zeros(x_ref.shape, F32)
    for j in range(d // cj):
        sl = slice(j * cj, (j + 1) * cj)
        gate = lambda br: gt_ref[:, br * d + j * cj:br * d + (j + 1) * cj].astype(F32)
        m = (gate(0) * jnp.dot(ya, wa_ref[:, sl], preferred_element_type=F32)
             + gate(1) * jnp.dot(yh, wh_ref[:, sl], preferred_element_type=F32)
             + gate(2) * jnp.dot(yp, wp_ref[:, sl], preferred_element_type=F32))
        acc = acc + jnp.dot(m.astype(BF16), wo_ref[sl, :], preferred_element_type=F32)
    xn = x_ref[...] + ga_ref[0] * acc
    xn_ref[...] = xn
    h2_ref[...] = _norm_mod(xn, g2_ref[...], sc_ref[0], sh_ref[0]).astype(h2_ref.dtype)


def _merge_call(ya, yh, yp, gates, x2, ga1, g2, sc2, sh2, wa, wh, wp, wo, layer, rows_per_batch):
    m, d = x2.shape
    tm = min(512, rows_per_batch)
    rpt = rows_per_batch // tm
    rows = lambda w: pl.BlockSpec((tm, w), lambda i: (i, 0))
    weight = lambda w: pl.BlockSpec((None,) + w.shape[1:], lambda i: (layer, 0, 0), pipeline_mode=pl.Buffered(1))
    return pl.pallas_call(
        _merge_body,
        grid=(m // tm,),
        in_specs=[rows(ATTN_W), rows(HYENA_W), rows(POOL_W), rows(N_BRANCH * d), rows(d),
                  _mod_spec(ga1, rpt), pl.BlockSpec((1, d), lambda i: (0, 0)),
                  _mod_spec(sc2, rpt), _mod_spec(sh2, rpt),
                  weight(wa), weight(wh), weight(wp), weight(wo)],
        out_specs=[rows(d), rows(d)],
        out_shape=[jax.ShapeDtypeStruct((m, d), F32), jax.ShapeDtypeStruct((m, d), BF16)],
        compiler_params=_params("arbitrary"),
        name="merge",
    )(ya, yh, yp, gates, x2, ga1, g2.reshape(1, d), sc2, sh2, wa, wh, wp, wo)


def _mlp_body(*refs, has_next):
    it = iter(refs)
    h_ref, w1_ref, w2_ref, x_ref, ga_ref = (next(it) for _ in range(5))
    if has_next:
        gn_ref, sc_ref, sh_ref = next(it), next(it), next(it)
    o_ref = next(it)
    hn_ref = next(it) if has_next else None

    f = pl.program_id(1)
    d = o_ref.shape[1]
    cn = 512

    @pl.when(f == 0)
    def _():
        o_ref[...] = jnp.zeros_like(o_ref)

    a = jnp.dot(h_ref[...], w1_ref[...], preferred_element_type=F32)
    a = jnp.square(jnp.maximum(a, 0.0)).astype(BF16)
    for n0 in range(0, d, cn):
        o_ref[:, n0:n0 + cn] += jnp.dot(a, w2_ref[:, n0:n0 + cn], preferred_element_type=F32)

    @pl.when(f == pl.num_programs(1) - 1)
    def _():
        xo = x_ref[...] + ga_ref[0] * o_ref[...]
        o_ref[...] = xo
        if has_next:
            hn_ref[...] = _norm_mod(xo, gn_ref[...], sc_ref[0], sh_ref[0]).astype(hn_ref.dtype)


def _mlp_call(h2, w1, w2, layer, xn, ga2, nxt, rows_per_batch):
    m, d = xn.shape
    ff = w1.shape[2]
    tm, tf = min(1024, rows_per_batch if ga2.shape[0] > 1 else m), 512
    rpt = max(rows_per_batch // tm, 1)
    has_next = nxt is not None
    rows = pl.BlockSpec((tm, d), lambda i, f: (i, 0))
    in_specs = [rows, pl.BlockSpec((None, d, tf), lambda i, f: (layer, 0, f)),
                pl.BlockSpec((None, tf, d), lambda i, f: (layer, f, 0)),
                pl.BlockSpec((tm, d), lambda i, f: (i, 0), pipeline_mode=pl.Buffered(1)), _mod_spec(ga2, rpt)]
    args = [h2, w1, w2, xn, ga2]
    out_specs = [rows]
    out_shape = [jax.ShapeDtypeStruct((m, d), F32)]
    if has_next:
        gn, scn, shn = nxt
        in_specs += [pl.BlockSpec((1, d), lambda i, f: (0, 0)), _mod_spec(scn, rpt), _mod_spec(shn, rpt)]
        args += [gn.reshape(1, d), scn, shn]
        out_specs.append(rows)
        out_shape.append(jax.ShapeDtypeStruct((m, d), BF16))
    outs = pl.pallas_call(
        functools.partial(_mlp_body, has_next=has_next),
        grid=(m // tm, ff // tf),
        in_specs=in_specs,
        out_specs=out_specs,
        out_shape=out_shape,
        compiler_params=_params("arbitrary", "arbitrary"),
        name="mlp",
    )(*args)
    return (outs[0], outs[1]) if has_next else (outs[0], None)


def _mixers(z, q, k, v, kx, vx, sink, local, hy, pool_w, pool_scale):
    b, l, _ = z.shape
    y_att = _attn_call(q, k, v, kx, vx, sink, local)
    conv_w, conv_b, filt_params, d_skip, (fc, fs) = hy
    nblk = _hyena_blocks(l)
    tc = 256
    spectra = _spectrum_call(_filter_call(l, *filt_params), fc, fs, nblk, tc)
    conv = functools.partial(_fftconv_call, conv_w=conv_w, conv_b=conv_b, spectra=spectra, d_skip=d_skip,
                             fc=fc, fs=fs, nblk=nblk, tc=tc)
    z1 = conv(z, HY_OFF, z, HY_OFF + HYENA_W, order=0, out_dtype=F32)
    y_hy = conv(z1, 0, z, HY_OFF + 2 * HYENA_W, order=1, out_dtype=BF16)
    y_pool = _pool_call(z, pool_w, pool_scale)
    return (y_att.reshape(b * l, ATTN_W), y_hy.reshape(b * l, HYENA_W), y_pool.reshape(b * l, POOL_W))


def kernel(x, c, ctx, c_ctx, norm1_g, norm2_g, w_mod, b_mod, w_in, q_norm_g, k_norm_g, sink, hy_conv_w, hy_conv_b, filt_w0, filt_b0, filt_w1, filt_b1, filt_freq, filt_w2, hy_bias, pool_w, pool_scale, w_att_o, w_hy_o, w_pool_o, w_out, mlp_w1, mlp_w2):
    b, l, d = x.shape
    lc = ctx.shape[1]
    depth = w_mod.shape[0]

    cc = jnp.concatenate([c, c_ctx[None, :], jnp.zeros((MOD_ROWS - b - 1, d), F32)], axis=0)
    mods = _modulation(cc, w_mod, b_mod)

    def chunks(layer, lo, hi):
        return [mods[layer, lo:hi, i * d:(i + 1) * d].reshape(hi - lo, 1, d) for i in range(6)]

    as_bf16 = lambda w: w.astype(BF16)
    w_in_b, w_att_b, w_hy_b, w_pool_b, w_out_b = map(as_bf16, (w_in, w_att_o, w_hy_o, w_pool_o, w_out))
    w1_b, w2_b, pool_w_b = map(as_bf16, (mlp_w1, mlp_w2, pool_w))

    rope_tabs = _rope_tables(l)
    dft_x = _dft_matrices(l // _hyena_blocks(l))
    dft_c = _dft_matrices(lc // _hyena_blocks(lc))

    x2 = x.reshape(b * l, d)
    c2 = ctx.reshape(b * lc, d)
    sh1, sc1 = chunks(0, 0, b)[:2]
    csh1, csc1 = chunks(0, b, b + 1)[:2]
    hx = _norm_call(x2, norm1_g[0], sc1, sh1, l)
    hc = _norm_call(c2, norm1_g[0], csc1, csh1, lc)

    for layer in range(depth):
        last = layer == depth - 1
        _, _, ga1, sh2, sc2, ga2 = chunks(layer, 0, b)
        _, _, cga1, csh2, csc2, cga2 = chunks(layer, b, b + 1)
        filt_params = (filt_w0[layer], filt_b0[layer], filt_w1[layer], filt_b1[layer], filt_freq[layer],
                       filt_w2[layer])
        conv_b = hy_conv_b[layer].reshape(1, -1)
        d_skip = hy_bias[layer].reshape(HYENA_ORDER, 1, HYENA_W)
        merge_w = (w_att_b, w_hy_b, w_pool_b, w_out_b, layer)

        if last:
            zc = _proj_call(hc, w_in_b, layer, K_OFF, 2 * KV_W, F32).reshape(b, lc, 2 * KV_W)
        else:
            zc = _proj_call(hc, w_in_b, layer, 0, GATE_OFF, F32).reshape(b, lc, GATE_OFF)
        qc, kc, vc = _qkv_call(zc, q_norm_g[layer], k_norm_g[layer], None, has_q=not last)

        zx = _proj_call(hx, w_in_b, layer, 0, GATE_OFF, F32).reshape(b, l, GATE_OFF)
        gx = _proj_call(hx, w_in_b, layer, GATE_OFF, N_BRANCH * d, BF16, sigmoid=True)
        qx, kx, vx = _qkv_call(zx, q_norm_g[layer], k_norm_g[layer], rope_tabs, has_q=True)
        hy = (hy_conv_w[layer], conv_b, filt_params, d_skip, dft_x)
        ya, yh, yp = _mixers(zx, qx, kx, vx, kc, vc, sink[layer], True, hy, pool_w_b[layer], pool_scale[layer])
        xn, h2 = _merge_call(ya, yh, yp, gx, x2, ga1, norm2_g[layer], sc2, sh2, *merge_w, l)
        nxt = None if last else (norm1_g[layer + 1], *reversed(chunks(layer + 1, 0, b)[:2]))
        x2, hx = _mlp_call(h2, w1_b, w2_b, layer, xn, ga2, nxt, l)

        if not last:
            gc = _proj_call(hc, w_in_b, layer, GATE_OFF, N_BRANCH * d, BF16, sigmoid=True)
            hyc = (hy_conv_w[layer], conv_b, filt_params, d_skip, dft_c)
            ya, yh, yp = _mixers(zc, qc, None, None, kc, vc, sink[layer], False, hyc, pool_w_b[layer],
                                 pool_scale[layer])
            cn, h2c = _merge_call(ya, yh, yp, gc, c2, cga1, norm2_g[layer], csc2, csh2, *merge_w, lc)
            nxt = (norm1_g[layer + 1], *reversed(chunks(layer + 1, b, b + 1)[:2]))
            c2, hc = _mlp_call(h2c, w1_b, w2_b, layer, cn, cga2, nxt, lc)

    return x2.reshape(b, l, d)
```

```python
import functools
import math

import jax
import jax.numpy as jnp
from jax import lax
from jax.experimental import pallas as pl
from jax.experimental.pallas import tpu as pltpu

D_MODEL = 2048
DEPTH = 2
GRID_W = 64
EPS = 1e-6
NEG_INF = -1e30

N_HEADS = 16
N_KV_HEADS = 4
GQA_GROUP = N_HEADS // N_KV_HEADS
HEAD_DIM = 64
ATTN_W = N_HEADS * HEAD_DIM
KV_W = N_KV_HEADS * HEAD_DIM
WINDOW = 128
ROPE_FREQS = HEAD_DIM // 4
ROPE_BASE = 10000.0

HYENA_W = D_MODEL // 4
HYENA_ORDER = 2
FILTER_BANDS = 16
FILTER_EMB = 1 + 2 * FILTER_BANDS
FILTER_HIDDEN = 64
FILTER_INNER = 2
DECAY_TARGET = 1e-2
FAST_DECAY_PCT = 0.3
SLOW_DECAY_PCT = 1.5

POOL_W = D_MODEL // 4
POOL_WINDOWS = (2, 4, 8, 16)
POOL_GROUP = POOL_W // len(POOL_WINDOWS)

N_BRANCH = 3
D_FF = 4 * D_MODEL

Q_OFF = 0
K_OFF = Q_OFF + ATTN_W
V_OFF = K_OFF + KV_W
HY_OFF = V_OFF + KV_W
POOL_OFF = HY_OFF + 3 * HYENA_W
GATE_OFF = POOL_OFF + POOL_W
IN_W = GATE_OFF + N_BRANCH * D_MODEL

V7X_LANES = 128
V7X_VMEM_LIMIT = 60 * 1024 * 1024
KV_DUP_W = N_KV_HEADS * V7X_LANES
MOD_ROWS = 24

F32 = jnp.float32
BF16 = jnp.bfloat16
HIGHEST = lax.Precision.HIGHEST


def _params(*semantics):
    return pltpu.CompilerParams(dimension_semantics=semantics, vmem_limit_bytes=V7X_VMEM_LIMIT)


def _const_spec(shape):
    zeros = (0,) * len(shape)
    return pl.BlockSpec(shape, lambda *_: zeros, pipeline_mode=pl.Buffered(1))


def _mod_spec(arr, rows_per_mod_tile):
    d = arr.shape[-1]
    if arr.shape[0] == 1:
        return pl.BlockSpec((1, 1, d), lambda i, *_: (0, 0, 0))
    return pl.BlockSpec((1, 1, d), lambda i, *_: (i // rows_per_mod_tile, 0, 0))


def _norm_mod(xf, g, sc, sh):
    y = xf * lax.rsqrt(jnp.mean(xf * xf, axis=-1, keepdims=True) + EPS)
    return (y * g) * (1.0 + sc) + sh


def _mod_body(c_ref, w_ref, b_ref, o_ref):
    c = c_ref[...]
    s = c * jax.nn.sigmoid(c)
    o_ref[0] = jnp.dot(s.astype(BF16), w_ref[0].astype(BF16), preferred_element_type=F32) + b_ref[0]


def _modulation(cc, w_mod, b_mod):
    depth, d, n = w_mod.shape
    tn = 1024
    return pl.pallas_call(
        _mod_body,
        grid=(depth, n // tn),
        in_specs=[
            pl.BlockSpec((MOD_ROWS, d), lambda l, j: (0, 0)),
            pl.BlockSpec((1, d, tn), lambda l, j: (l, 0, j)),
            pl.BlockSpec((1, 1, tn), lambda l, j: (l, 0, j)),
        ],
        out_specs=pl.BlockSpec((1, MOD_ROWS, tn), lambda l, j: (l, 0, j)),
        out_shape=jax.ShapeDtypeStruct((depth, MOD_ROWS, n), F32),
        compiler_params=_params("arbitrary", "arbitrary"),
        name="modulation",
    )(cc, w_mod, b_mod.reshape(depth, 1, n))


def _norm_body(x_ref, g_ref, sc_ref, sh_ref, o_ref):
    o_ref[...] = _norm_mod(x_ref[...], g_ref[...], sc_ref[0], sh_ref[0]).astype(o_ref.dtype)


def _norm_call(x2, g, sc, sh, rows_per_batch):
    m, d = x2.shape
    tm = min(512, rows_per_batch)
    return pl.pallas_call(
        _norm_body,
        grid=(m // tm,),
        in_specs=[
            pl.BlockSpec((tm, d), lambda i: (i, 0)),
            pl.BlockSpec((1, d), lambda i: (0, 0)),
            _mod_spec(sc, rows_per_batch // tm),
            _mod_spec(sh, rows_per_batch // tm),
        ],
        out_specs=pl.BlockSpec((tm, d), lambda i: (i, 0)),
        out_shape=jax.ShapeDtypeStruct((m, d), BF16),
        compiler_params=_params("arbitrary"),
        name="norm_mod",
    )(x2, g.reshape(1, d), sc, sh)


def _proj_body(a_ref, w_ref, o_ref):
    o_ref[...] = jnp.dot(a_ref[...], w_ref[...], preferred_element_type=F32).astype(o_ref.dtype)


def _proj_call(a, w, layer, col0, n, out_dtype):
    m, k = a.shape
    tm = min(2048, m)
    tn = 512
    c0 = col0 // tn
    return pl.pallas_call(
        _proj_body,
        grid=(m // tm, n // tn),
        in_specs=[
            pl.BlockSpec((tm, k), lambda i, j: (i, 0)),
            pl.BlockSpec((None, k, tn), lambda i, j: (layer, 0, c0 + j)),
        ],
        out_specs=pl.BlockSpec((tm, tn), lambda i, j: (i, j)),
        out_shape=jax.ShapeDtypeStruct((m, n), out_dtype),
        compiler_params=_params("arbitrary", "arbitrary"),
        name="in_proj",
    )(a, w)


Q_SLABS = ATTN_W // V7X_LANES
QK_SLABS = Q_SLABS + KV_W // V7X_LANES
QKV_SLABS = QK_SLABS + KV_W // V7X_LANES


def _gates_qkv_body(a_ref, w_ref, z_ref, gain_ref, cos_ref, sup_ref, sdn_ref, g_ref, q_ref, kv_ref):
    j = pl.program_id(1)
    zg = jnp.dot(a_ref[...], w_ref[...], preferred_element_type=F32)
    g_ref[...] = (0.5 * jnp.tanh(0.5 * zg) + 0.5).astype(g_ref.dtype)

    x = z_ref[...]
    low = lax.broadcasted_iota(jnp.int32, (1, V7X_LANES), 1) < HEAD_DIM
    x2 = x * x
    ss = jnp.where(low, jnp.sum(jnp.where(low, x2, 0.0), axis=-1, keepdims=True),
                   jnp.sum(jnp.where(low, 0.0, x2), axis=-1, keepdims=True))
    inv = jnp.where(j < QK_SLABS, lax.rsqrt(ss * (1.0 / HEAD_DIM) + EPS), 1.0)
    y = _rope((x * inv) * gain_ref[0], cos_ref[...], sup_ref[...], sdn_ref[...])
    da, db = _dup_pair(y, low)

    @pl.when(j < Q_SLABS)
    def _():
        q_ref[...] = y.astype(q_ref.dtype)

    @pl.when(j >= Q_SLABS)
    def _():
        kv_ref[:, 0:V7X_LANES] = da.astype(kv_ref.dtype)
        kv_ref[:, V7X_LANES:2 * V7X_LANES] = db.astype(kv_ref.dtype)


def _gates_qkv_call(a, w, layer, z2, gq, gk, rope_tabs, seq_len):
    m, k = a.shape
    d = D_MODEL
    tm = min(2048, m)
    tn = 512
    assert N_BRANCH * d // tn == QKV_SLABS and tm % seq_len == 0
    c0 = GATE_OFF // tn
    ones = jnp.ones((tm, V7X_LANES), F32)
    zeros = jnp.zeros((tm, V7X_LANES), F32)
    if rope_tabs is None:
        cos, sup, sdn = ones[None], zeros[None], zeros[None]
        tab_map = lambda i, j: (0, 0, 0)
    else:
        rep = lambda t: jnp.tile(t, (tm // seq_len, 1))
        cos, sup, sdn = (jnp.stack([rep(t), ident]) for t, ident in zip(rope_tabs, (ones, zeros, zeros)))
        tab_map = lambda i, j: ((j >= QK_SLABS).astype(jnp.int32), 0, 0)
    gains = jnp.stack([jnp.tile(gq, 2) * HEAD_DIM ** -0.5, jnp.tile(gk, 2), jnp.ones((V7X_LANES,), F32)])
    gain_map = lambda i, j: ((j >= Q_SLABS).astype(jnp.int32) + (j >= QK_SLABS).astype(jnp.int32), 0, 0)
    tab_spec = pl.BlockSpec((None, tm, V7X_LANES), tab_map)
    return pl.pallas_call(
        _gates_qkv_body,
        grid=(m // tm, QKV_SLABS),
        in_specs=[
            pl.BlockSpec((tm, k), lambda i, j: (i, 0)),
            pl.BlockSpec((None, k, tn), lambda i, j: (layer, 0, c0 + j)),
            pl.BlockSpec((tm, V7X_LANES), lambda i, j: (i, j)),
            pl.BlockSpec((None, 1, V7X_LANES), gain_map),
            tab_spec, tab_spec, tab_spec,
        ],
        out_specs=[
            pl.BlockSpec((tm, tn), lambda i, j: (i, j)),
            pl.BlockSpec((tm, V7X_LANES), lambda i, j: (i, jnp.minimum(j, Q_SLABS - 1))),
            pl.BlockSpec((tm, 2 * V7X_LANES), lambda i, j: (i, jnp.maximum(j - Q_SLABS, 0))),
        ],
        out_shape=[
            jax.ShapeDtypeStruct((m, N_BRANCH * d), BF16),
            jax.ShapeDtypeStruct((m, ATTN_W), BF16),
            jax.ShapeDtypeStruct((m, 2 * KV_DUP_W), BF16),
        ],
        compiler_params=_params("arbitrary", "arbitrary"),
        name="gates_qkv",
    )(a, w, z2, gains.reshape(3, 1, V7X_LANES), cos, sup, sdn)


def _pair_block_diag():
    r = lax.broadcasted_iota(jnp.int32, (V7X_LANES, V7X_LANES), 0) // HEAD_DIM
    c = lax.broadcasted_iota(jnp.int32, (V7X_LANES, V7X_LANES), 1) // HEAD_DIM
    return (r == c).astype(F32)


def _head_norm(x, g, bd):
    ss = jnp.dot(x * x, bd, precision=HIGHEST, preferred_element_type=F32)
    return (x * lax.rsqrt(ss * (1.0 / HEAD_DIM) + EPS)) * g


def _rope(x, cos, sin_up, sin_dn):
    up = pltpu.roll(x, V7X_LANES - ROPE_FREQS, 1)
    dn = pltpu.roll(x, ROPE_FREQS, 1)
    return x * cos + up * sin_up + dn * sin_dn


def _dup_pair(x, low):
    r = pltpu.roll(x, HEAD_DIM, 1)
    return jnp.where(low, x, r), jnp.where(low, r, x)


def _kv_body(z_ref, gk_ref, kv_ref):
    bd = _pair_block_diag()
    low = lax.broadcasted_iota(jnp.int32, (1, V7X_LANES), 1) < HEAD_DIM
    for s in range(2 * KV_W // V7X_LANES):
        x = z_ref[:, s * V7X_LANES:(s + 1) * V7X_LANES]
        if s < KV_W // V7X_LANES:
            x = _head_norm(x, gk_ref[...], bd)
        a, b = _dup_pair(x, low)
        base = 2 * s * V7X_LANES
        kv_ref[:, base:base + V7X_LANES] = a.astype(kv_ref.dtype)
        kv_ref[:, base + V7X_LANES:base + 2 * V7X_LANES] = b.astype(kv_ref.dtype)


def _kv_call(z2, gk):
    m, nz = z2.shape
    tm = min(512, m)
    return pl.pallas_call(
        _kv_body,
        grid=(m // tm,),
        in_specs=[pl.BlockSpec((tm, nz), lambda i: (i, 0)),
                  pl.BlockSpec((1, V7X_LANES), lambda i: (0, 0))],
        out_specs=pl.BlockSpec((tm, 2 * KV_DUP_W), lambda i: (i, 0)),
        out_shape=jax.ShapeDtypeStruct((m, 2 * KV_DUP_W), BF16),
        compiler_params=_params("arbitrary"),
        name="kv_prep",
    )(z2, jnp.tile(gk, 2).reshape(1, V7X_LANES))


def _rope_tables(l):
    rows = l // GRID_W
    row = jnp.repeat(jnp.arange(rows, dtype=F32), GRID_W)
    col = jnp.tile(jnp.arange(GRID_W, dtype=F32), rows)
    inv = ROPE_BASE ** (-jnp.arange(ROPE_FREQS, dtype=F32) / ROPE_FREQS)
    ang = jnp.stack([row[:, None] * inv, col[:, None] * inv], axis=1)
    cos, sin = jnp.cos(ang), jnp.sin(ang)
    zero = jnp.zeros_like(sin)
    cos_h = jnp.stack([cos, cos], axis=2).reshape(l, HEAD_DIM)
    sup_h = jnp.stack([-sin, zero], axis=2).reshape(l, HEAD_DIM)
    sdn_h = jnp.stack([zero, sin], axis=2).reshape(l, HEAD_DIM)
    return tuple(jnp.tile(t, (1, 2)) for t in (cos_h, sup_h, sdn_h))


def _attn_body(*refs, local, tq):
    it = iter(refs)
    sink_ref = next(it)
    q_ref = next(it)
    if local:
        kp_ref, kc_ref, kn_ref, vp_ref, vc_ref, vn_ref = (next(it) for _ in range(6))
    kx_ref, vx_ref = next(it), next(it)
    o_ref = next(it)

    i = pl.program_id(1)
    nb = pl.num_programs(1)
    low = lax.broadcasted_iota(jnp.int32, (1, V7X_LANES), 1) < HEAD_DIM
    rows = GQA_GROUP * tq
    if local:
        qi = lax.broadcasted_iota(jnp.int32, (rows, tq), 0) % tq
        kj = lax.broadcasted_iota(jnp.int32, (rows, tq), 1)
        mask_prev = (kj >= qi) & (i > 0)
        mask_next = (kj <= qi) & (i < nb - 1)
    row_head = lax.broadcasted_iota(jnp.int32, (rows, 1), 0) // tq
    zero = jnp.zeros((), q_ref.dtype)

    for h in range(N_KV_HEADS):
        hs = slice(h * V7X_LANES, (h + 1) * V7X_LANES)
        qa = q_ref[0, :, 2 * h * V7X_LANES:(2 * h + 1) * V7X_LANES]
        qb = q_ref[0, :, (2 * h + 1) * V7X_LANES:(2 * h + 2) * V7X_LANES]
        qs = jnp.concatenate([jnp.where(low, qa, zero), jnp.where(low, zero, qa),
                              jnp.where(low, qb, zero), jnp.where(low, zero, qb)], axis=0)
        kparts, vparts, masks = [kx_ref[0, :, hs]], [vx_ref[0, :, hs]], {}
        if local:
            kparts = [kp_ref[0, :, hs], kc_ref[0, :, hs], kn_ref[0, :, hs]] + kparts
            vparts = [vp_ref[0, :, hs], vc_ref[0, :, hs], vn_ref[0, :, hs]] + vparts
            masks = {0: mask_prev, 2: mask_next}
        k_all = jnp.concatenate(kparts, axis=0)
        v_all = jnp.concatenate(vparts, axis=0)

        sink = jnp.zeros((rows, 1), F32)
        for g in range(GQA_GROUP):
            sink = jnp.where(row_head == g, sink_ref[GQA_GROUP * h + g], sink)
        s_all = lax.dot_general(qs, k_all, (((1,), (1,)), ((), ())), preferred_element_type=F32)
        chunks = []
        for c in range(k_all.shape[0] // tq):
            s = s_all[:, c * tq:(c + 1) * tq]
            chunks.append(jnp.where(masks[c], s, NEG_INF) if c in masks else s)
        m = jnp.maximum(sink, jnp.max(functools.reduce(jnp.maximum, chunks), axis=-1, keepdims=True))
        probs = [jnp.exp(s - m) for s in chunks]
        denom = jnp.exp(sink - m) + jnp.sum(functools.reduce(jnp.add, probs), axis=-1, keepdims=True)
        p_all = jnp.concatenate([p.astype(v_all.dtype) for p in probs], axis=1)
        o = jnp.dot(p_all, v_all, preferred_element_type=F32) / denom
        oa = jnp.where(low, o[0:tq], o[tq:2 * tq])
        ob = jnp.where(low, o[2 * tq:3 * tq], o[3 * tq:4 * tq])
        o_ref[0, :, 2 * h * V7X_LANES:(2 * h + 1) * V7X_LANES] = oa.astype(o_ref.dtype)
        o_ref[0, :, (2 * h + 1) * V7X_LANES:(2 * h + 2) * V7X_LANES] = ob.astype(o_ref.dtype)


def _attn_call(q, kv, kvx, sink, local):
    b, l, _ = q.shape
    lx = kvx.shape[1]
    tq = WINDOW
    nb = l // tq
    blk = lambda w: (1, tq, w)
    in_specs = [pl.BlockSpec(memory_space=pltpu.SMEM),
                pl.BlockSpec(blk(ATTN_W), lambda bi, i: (bi, i, 0))]
    args = [sink, q]
    if local:
        for half in (0, 1):
            for mp in (lambda bi, i, half=half: (bi, jnp.maximum(i - 1, 0), half),
                       lambda bi, i, half=half: (bi, i, half),
                       lambda bi, i, half=half: (bi, jnp.minimum(i + 1, nb - 1), half)):
                in_specs.append(pl.BlockSpec(blk(KV_DUP_W), mp))
                args.append(kv)
    for half in (0, 1):
        in_specs.append(pl.BlockSpec((1, lx, KV_DUP_W), lambda bi, i, half=half: (bi, 0, half)))
        args.append(kvx)
    return pl.pallas_call(
        functools.partial(_attn_body, local=local, tq=tq),
        grid=(b, nb),
        in_specs=in_specs,
        out_specs=pl.BlockSpec(blk(ATTN_W), lambda bi, i: (bi, i, 0)),
        out_shape=jax.ShapeDtypeStruct((b, l, ATTN_W), BF16),
        compiler_params=_params("arbitrary", "arbitrary"),
        name="attention",
    )(*args)


def _filter_body(z_ref, w0_ref, b0_ref, w1_ref, b1_ref, fr_ref, w2_ref, dec_ref, o_ref):
    fr = fr_ref[...]
    dot = functools.partial(jnp.dot, precision=HIGHEST, preferred_element_type=F32)
    h = jnp.sin(fr * (dot(z_ref[...], w0_ref[...]) + b0_ref[...]))
    for i in range(FILTER_INNER):
        h = jnp.sin(fr * (dot(h, w1_ref[i]) + b1_ref[i]))
    dec = dec_ref[...]
    for s in range(2 * HYENA_ORDER):
        sl = slice(s * HYENA_W, (s + 1) * HYENA_W)
        o_ref[:, sl] = dot(h, w2_ref[:, sl]) * dec


def _filter_features(l):
    t = jnp.linspace(0.0, 1.0, l, dtype=F32)[:, None]
    w = 2.0 * math.pi * jnp.arange(l, dtype=F32)[:, None] / l
    bands = jnp.linspace(1e-4, FILTER_BANDS - 1, FILTER_BANDS, dtype=F32)[None, :]
    z = jnp.concatenate([t, jnp.cos(bands * w), -jnp.sin(bands * w)], axis=-1)
    deltas = jnp.linspace(math.log(DECAY_TARGET) / SLOW_DECAY_PCT, math.log(DECAY_TARGET) / FAST_DECAY_PCT,
                          HYENA_W, dtype=F32)
    decay = jnp.exp(-t * jnp.abs(deltas))
    return jnp.pad(z, ((0, 0), (0, V7X_LANES - FILTER_EMB))), decay


def _filter_call(l, w0, b0, w1, b1, freq, w2):
    zfeat, decay = _filter_features(l)
    w0p = jnp.pad(w0, ((0, V7X_LANES - FILTER_EMB), (0, 0)))
    tl = min(512, l)
    nf = 2 * HYENA_ORDER * HYENA_W
    full = lambda shape: pl.BlockSpec(shape, lambda i: (0,) * len(shape))
    return pl.pallas_call(
        _filter_body,
        grid=(l // tl,),
        in_specs=[
            pl.BlockSpec((tl, V7X_LANES), lambda i: (i, 0)),
            full((V7X_LANES, FILTER_HIDDEN)),
            full((1, FILTER_HIDDEN)),
            full((FILTER_INNER, FILTER_HIDDEN, FILTER_HIDDEN)),
            full((FILTER_INNER, 1, FILTER_HIDDEN)),
            full((1, FILTER_HIDDEN)),
            full((FILTER_HIDDEN, nf)),
            pl.BlockSpec((tl, HYENA_W), lambda i: (i, 0)),
        ],
        out_specs=pl.BlockSpec((tl, nf), lambda i: (i, 0)),
        out_shape=jax.ShapeDtypeStruct((l, nf), F32),
        compiler_params=_params("arbitrary"),
        name="hyena_filter",
    )(zfeat, w0p, b0.reshape(1, -1), w1, b1.reshape(FILTER_INNER, 1, -1), freq.reshape(1, -1), w2, decay)


def _hyena_blocks(l):
    return max(1, min(4, l // V7X_LANES))


def _dft_matrices(blk):
    n = 2 * blk
    r = jnp.arange(blk, dtype=jnp.int32)
    ang = ((r[:, None] * r[None, :]) % n).astype(F32) * (2.0 * math.pi / n)
    return jnp.cos(ang).astype(BF16), jnp.sin(ang).astype(BF16)


def _alternating(l):
    row = lax.broadcasted_iota(jnp.int32, (l, 1), 0)
    return row, jnp.where(row % 2 == 0, 1.0, -1.0).astype(F32)


def _spectrum_body(hf_ref, hb_ref, fc_ref, fs_ref, ka_ref, kb_ref, kn_ref, *, nblk):
    l = hf_ref.shape[0]
    b = l // nblk
    n = 2 * b
    row = lax.broadcasted_iota(jnp.int32, (l, 1), 0)
    _, alt = _alternating(b)
    hf = hf_ref[...]
    hbs = jnp.where(row == 0, 0.0, pltpu.roll(hb_ref[...], 1, 0))
    fc, fs = fc_ref[...], fs_ref[...]

    def transforms(h):
        out = []
        for k in range(nblk):
            hk = h[k * b:(k + 1) * b]
            hk16 = hk.astype(BF16)
            out.append(dict(
                c=jnp.dot(fc, hk16, preferred_element_type=F32),
                s=jnp.dot(fs, hk16, preferred_element_type=F32),
                first16=hk16[0:1].astype(F32),
                first=hk[0:1],
                alt=jnp.sum(hk * alt, axis=0, keepdims=True)))
        return out

    tf, tb = transforms(hf), transforms(hbs)
    brow = lax.broadcasted_iota(jnp.int32, (b, 1), 0)
    w_re = jnp.where(brow == 0, 1.0 / n, 2.0 / n)
    for d in range(-(nblk - 1), nblk):
        idx = d + nblk - 1
        if d == 0:
            kre = tf[0]["c"] + tb[0]["c"]
            kim = tb[0]["s"] - tf[0]["s"]
            kn = tf[0]["alt"] + tb[0]["alt"]
        else:
            t, e, sg = (tf, d, -1.0) if d > 0 else (tb, -d, 1.0)
            kre = t[e]["c"] + alt * (t[e - 1]["c"] - t[e - 1]["first16"])
            kim = sg * (t[e]["s"] + alt * t[e - 1]["s"])
            kn = t[e]["alt"] + t[e - 1]["alt"] - t[e - 1]["first"]
        ka_ref[0, idx] = kre * w_re
        kb_ref[0, idx] = kim * (2.0 / n)
        kn_ref[0, idx] = kn * (1.0 / n)


def _spectrum_call(filt, fc, fs, nblk, tc):
    l = filt.shape[0]
    b = l // nblk
    nct = HYENA_W // tc
    nlag = 2 * nblk - 1
    return pl.pallas_call(
        functools.partial(_spectrum_body, nblk=nblk),
        grid=(HYENA_ORDER, nct),
        in_specs=[
            pl.BlockSpec((l, tc), lambda o, c: (0, 2 * nct * o + c)),
            pl.BlockSpec((l, tc), lambda o, c: (0, 2 * nct * o + nct + c)),
            _const_spec((b, b)),
            _const_spec((b, b)),
        ],
        out_specs=[
            pl.BlockSpec((1, nlag, b, tc), lambda o, c: (o, 0, 0, c)),
            pl.BlockSpec((1, nlag, b, tc), lambda o, c: (o, 0, 0, c)),
            pl.BlockSpec((1, nlag, 1, tc), lambda o, c: (o, 0, 0, c)),
        ],
        out_shape=[
            jax.ShapeDtypeStruct((HYENA_ORDER, nlag, b, HYENA_W), F32),
            jax.ShapeDtypeStruct((HYENA_ORDER, nlag, b, HYENA_W), F32),
            jax.ShapeDtypeStruct((HYENA_ORDER, nlag, 1, HYENA_W), F32),
        ],
        compiler_params=_params("arbitrary", "arbitrary"),
        name="hyena_spectrum",
    )(filt, filt, fc, fs)


def _conv3(x, w_ref, b_ref, row):
    l = x.shape[0]
    xm = jnp.where(row == 0, 0.0, pltpu.roll(x, 1, 0))
    xp = jnp.where(row == l - 1, 0.0, pltpu.roll(x, l - 1, 0))
    return xm * w_ref[0:1, :] + x * w_ref[1:2, :] + xp * w_ref[2:3, :] + b_ref[...]


def _fftconv_body(*refs, conv_u, nblk):
    it = iter(refs)
    u_ref = next(it)
    if conv_u:
        uw_ref, ub_ref = next(it), next(it)
    g_ref, gw_ref, gb_ref = next(it), next(it), next(it)
    ka_ref, kb_ref, kn_ref, d_ref, fc_ref, fs_ref, o_ref = (next(it) for _ in range(7))

    l = u_ref.shape[1]
    b = l // nblk
    row = lax.broadcasted_iota(jnp.int32, (l, 1), 0)
    _, alt = _alternating(b)
    u = u_ref[0].astype(F32)
    if conv_u:
        u = _conv3(u, uw_ref, ub_ref, row)
    gate = _conv3(g_ref[0], gw_ref, gb_ref, row)
    fc, fs = fc_ref[...], fs_ref[...]

    ps, qs, ns = [], [], []
    for j in range(nblk):
        uj = u[j * b:(j + 1) * b]
        uj16 = uj.astype(BF16)
        ps.append(jnp.dot(fc, uj16, preferred_element_type=F32))
        qs.append(jnp.dot(fs, uj16, preferred_element_type=F32))
        ns.append(jnp.sum(uj * alt, axis=0, keepdims=True))
    for i in range(nblk):
        r = t = nyq = None
        for j in range(nblk):
            lag = i - j + nblk - 1
            ka, kb = ka_ref[0, lag], kb_ref[0, lag]
            dr = ps[j] * ka + qs[j] * kb
            dt = qs[j] * ka - ps[j] * kb
            dn = ns[j] * kn_ref[0, lag]
            r, t, nyq = (dr, dt, dn) if j == 0 else (r + dr, t + dt, nyq + dn)
        y = (jnp.dot(fc, r.astype(BF16), preferred_element_type=F32)
             + jnp.dot(fs, t.astype(BF16), preferred_element_type=F32))
        rows = slice(i * b, (i + 1) * b)
        y = y + alt * nyq + u[rows] * d_ref[0]
        o_ref[0, rows, :] = (gate[rows] * y).astype(o_ref.dtype)


def _fftconv_call(u, u_col0, z, gate_col0, conv_w, conv_b, spectra, d_skip, order, fc, fs, nblk, tc, out_dtype):
    b, l, _ = z.shape
    conv_u = u is z
    nct = HYENA_W // tc
    ka, kb, kn = spectra
    blk = l // nblk
    nlag = 2 * nblk - 1
    col = lambda c0: (lambda c, bi: (bi, 0, c0 // tc + c))
    wcol = lambda c0: (lambda c, bi: (0, (c0 - HY_OFF) // tc + c))
    in_specs = [pl.BlockSpec((1, l, tc), col(u_col0))]
    args = [u]
    if conv_u:
        in_specs += [pl.BlockSpec((3, tc), wcol(u_col0)), pl.BlockSpec((1, tc), wcol(u_col0))]
        args += [conv_w, conv_b]
    in_specs += [pl.BlockSpec((1, l, tc), col(gate_col0)),
                 pl.BlockSpec((3, tc), wcol(gate_col0)), pl.BlockSpec((1, tc), wcol(gate_col0))]
    args += [z, conv_w, conv_b]
    spec = lambda rows: pl.BlockSpec((1, nlag, rows, tc), lambda c, bi: (order, 0, 0, c),
                                     pipeline_mode=pl.Buffered(1))
    in_specs += [spec(blk), spec(blk), spec(1),
                 pl.BlockSpec((1, 1, tc), lambda c, bi: (order, 0, c), pipeline_mode=pl.Buffered(1)),
                 _const_spec((blk, blk)), _const_spec((blk, blk))]
    args += [ka, kb, kn, d_skip, fc, fs]
    return pl.pallas_call(
        functools.partial(_fftconv_body, conv_u=conv_u, nblk=nblk),
        grid=(nct, b),
        in_specs=in_specs,
        out_specs=pl.BlockSpec((1, l, tc), lambda c, bi: (bi, 0, c)),
        out_shape=jax.ShapeDtypeStruct((b, l, HYENA_W), out_dtype),
        compiler_params=_params("arbitrary", "arbitrary"),
        name="hyena_conv",
    )(*args)


def _pool_body(x_ref, w_ref, s_ref, o_ref):
    l = x_ref.shape[1]
    row = lax.broadcasted_iota(jnp.int32, (l, 1), 0)
    for g, win in enumerate(POOL_WINDOWS):
        half = win // 2
        sl = slice(g * POOL_GROUP, (g + 1) * POOL_GROUP)
        x = x_ref[0, :, sl]
        acc = jnp.zeros_like(x)
        for k in range(-half, half):
            shifted = x if k == 0 else pltpu.roll(x, (-k) % l, 0)
            acc = acc + jnp.where((row + k >= 0) & (row + k < l), shifted, 0.0)
        cnt = (jnp.minimum(row + half, l) - jnp.maximum(row - half, 0)).astype(F32)
        d = acc / cnt - x
        y = jnp.dot(d.astype(BF16), w_ref[g], preferred_element_type=F32)
        o_ref[0, :, sl] = (y * s_ref[:, sl]).astype(o_ref.dtype)


def _pool_call(z, w_grp, scale):
    b, l, _ = z.shape
    ng = len(POOL_WINDOWS)
    return pl.pallas_call(
        _pool_body,
        grid=(b,),
        in_specs=[
            pl.BlockSpec((1, l, POOL_W), lambda bi: (bi, 0, POOL_OFF // POOL_W)),
            pl.BlockSpec((ng, POOL_GROUP, POOL_GROUP), lambda bi: (0, 0, 0)),
            pl.BlockSpec((1, POOL_W), lambda bi: (0, 0)),
        ],
        out_specs=pl.BlockSpec((1, l, POOL_W), lambda bi: (bi, 0, 0)),
        out_shape=jax.ShapeDtypeStruct((b, l, POOL_W), BF16),
        compiler_params=_params("arbitrary"),
        name="pool",
    )(z, w_grp, scale.reshape(1, POOL_W))


def _merge_body(ya_ref, yh_ref, yp_ref, gt_ref, x_ref, ga_ref, g2_ref, sc_ref, sh_ref,
                wa_ref, wh_ref, wp_ref, wo_ref, xn_ref, h2_ref):
    d = x_ref.shape[1]
    cj = 512
    ya, yh, yp = ya_ref[...], yh_ref[...], yp_ref[...]
    acc = jnp.zeros(x_ref.shape, F32)
    for j in range(d // cj):
        sl = slice(j * cj, (j + 1) * cj)
        gate = lambda br: gt_ref[:, br * d + j * cj:br * d + (j + 1) * cj].astype(F32)
        m = (gate(0) * jnp.dot(ya, wa_ref[:, sl], preferred_element_type=F32)
             + gate(1) * jnp.dot(yh, wh_ref[:, sl], preferred_element_type=F32)
             + gate(2) * jnp.dot(yp, wp_ref[:, sl], preferred_element_type=F32))
        acc = acc + jnp.dot(m.astype(BF16), wo_ref[sl, :], preferred_element_type=F32)
    xn = x_ref[...] + ga_ref[0] * acc
    xn_ref[...] = xn
    h2_ref[...] = _norm_mod(xn, g2_ref[...], sc_ref[0], sh_ref[0]).astype(h2_ref.dtype)


def _merge_call(ya, yh, yp, gates, x2, ga1, g2, sc2, sh2, wa, wh, wp, wo, layer, rows_per_batch):
    m, d = x2.shape
    tm = min(512, rows_per_batch)
    rpt = rows_per_batch // tm
    rows = lambda w: pl.BlockSpec((tm, w), lambda i: (i, 0))
    weight = lambda w: pl.BlockSpec((None,) + w.shape[1:], lambda i: (layer, 0, 0), pipeline_mode=pl.Buffered(1))
    return pl.pallas_call(
        _merge_body,
        grid=(m // tm,),
        in_specs=[rows(ATTN_W), rows(HYENA_W), rows(POOL_W), rows(N_BRANCH * d), rows(d),
                  _mod_spec(ga1, rpt), pl.BlockSpec((1, d), lambda i: (0, 0)),
                  _mod_spec(sc2, rpt), _mod_spec(sh2, rpt),
                  weight(wa), weight(wh), weight(wp), weight(wo)],
        out_specs=[rows(d), rows(d)],
        out_shape=[jax.ShapeDtypeStruct((m, d), F32), jax.ShapeDtypeStruct((m, d), BF16)],
        compiler_params=_params("arbitrary"),
        name="merge",
    )(ya, yh, yp, gates, x2, ga1, g2.reshape(1, d), sc2, sh2, wa, wh, wp, wo)


def _mlp_body(*refs, has_next):
    it = iter(refs)
    h_ref, w1_ref, w2_ref, x_ref, ga_ref = (next(it) for _ in range(5))
    if has_next:
        gn_ref, sc_ref, sh_ref = next(it), next(it), next(it)
    o_ref = next(it)
    hn_ref = next(it) if has_next else None

    f = pl.program_id(1)
    d = o_ref.shape[1]
    cn = 512

    @pl.when(f == 0)
    def _():
        o_ref[...] = jnp.zeros_like(o_ref)

    a = jnp.dot(h_ref[...], w1_ref[...], preferred_element_type=F32)
    a = jnp.square(jnp.maximum(a, 0.0)).astype(BF16)
    for n0 in range(0, d, cn):
        o_ref[:, n0:n0 + cn] += jnp.dot(a, w2_ref[:, n0:n0 + cn], preferred_element_type=F32)

    @pl.when(f == pl.num_programs(1) - 1)
    def _():
        xo = x_ref[...] + ga_ref[0] * o_ref[...]
        o_ref[...] = xo
        if has_next:
            hn_ref[...] = _norm_mod(xo, gn_ref[...], sc_ref[0], sh_ref[0]).astype(hn_ref.dtype)


def _mlp_call(h2, w1, w2, layer, xn, ga2, nxt, rows_per_batch):
    m, d = xn.shape
    ff = w1.shape[2]
    tm, tf = min(1024, rows_per_batch if ga2.shape[0] > 1 else m), 512
    rpt = max(rows_per_batch // tm, 1)
    has_next = nxt is not None
    rows = pl.BlockSpec((tm, d), lambda i, f: (i, 0))
    in_specs = [rows, pl.BlockSpec((None, d, tf), lambda i, f: (layer, 0, f)),
                pl.BlockSpec((None, tf, d), lambda i, f: (layer, f, 0)),
                pl.BlockSpec((tm, d), lambda i, f: (i, 0), pipeline_mode=pl.Buffered(1)), _mod_spec(ga2, rpt)]
    args = [h2, w1, w2, xn, ga2]
    out_specs = [rows]
    out_shape = [jax.ShapeDtypeStruct((m, d), F32)]
    if has_next:
        gn, scn, shn = nxt
        in_specs += [pl.BlockSpec((1, d), lambda i, f: (0, 0)), _mod_spec(scn, rpt), _mod_spec(shn, rpt)]
        args += [gn.reshape(1, d), scn, shn]
        out_specs.append(rows)
        out_shape.append(jax.ShapeDtypeStruct((m, d), BF16))
    outs = pl.pallas_call(
        functools.partial(_mlp_body, has_next=has_next),
        grid=(m // tm, ff // tf),
        in_specs=in_specs,
        out_specs=out_specs,
        out_shape=out_shape,
        compiler_params=_params("arbitrary", "arbitrary"),
        name="mlp",
    )(*args)
    return (outs[0], outs[1]) if has_next else (outs[0], None)


def _mixers(z, q, kv, kvx, sink, local, hy, pool_w, pool_scale):
    b, l, _ = z.shape
    y_att = _attn_call(q, kv, kvx, sink, local)
    conv_w, conv_b, filt_params, d_skip, (fc, fs) = hy
    nblk = _hyena_blocks(l)
    tc = 256
    spectra = _spectrum_call(_filter_call(l, *filt_params), fc, fs, nblk, tc)
    conv = functools.partial(_fftconv_call, conv_w=conv_w, conv_b=conv_b, spectra=spectra, d_skip=d_skip,
                             fc=fc, fs=fs, nblk=nblk, tc=tc)
    z1 = conv(z, HY_OFF, z, HY_OFF + HYENA_W, order=0, out_dtype=F32)
    y_hy = conv(z1, 0, z, HY_OFF + 2 * HYENA_W, order=1, out_dtype=BF16)
    y_pool = _pool_call(z, pool_w, pool_scale)
    return (y_att.reshape(b * l, ATTN_W), y_hy.reshape(b * l, HYENA_W), y_pool.reshape(b * l, POOL_W))


def kernel(x, c, ctx, c_ctx, norm1_g, norm2_g, w_mod, b_mod, w_in, q_norm_g, k_norm_g, sink, hy_conv_w, hy_conv_b, filt_w0, filt_b0, filt_w1, filt_b1, filt_freq, filt_w2, hy_bias, pool_w, pool_scale, w_att_o, w_hy_o, w_pool_o, w_out, mlp_w1, mlp_w2):
    b, l, d = x.shape
    lc = ctx.shape[1]
    depth = w_mod.shape[0]

    cc = jnp.concatenate([c, c_ctx[None, :], jnp.zeros((MOD_ROWS - b - 1, d), F32)], axis=0)
    mods = _modulation(cc, w_mod, b_mod)

    def chunks(layer, lo, hi):
        return [mods[layer, lo:hi, i * d:(i + 1) * d].reshape(hi - lo, 1, d) for i in range(6)]

    as_bf16 = lambda w: w.astype(BF16)
    w_in_b, w_att_b, w_hy_b, w_pool_b, w_out_b = map(as_bf16, (w_in, w_att_o, w_hy_o, w_pool_o, w_out))
    w1_b, w2_b, pool_w_b = map(as_bf16, (mlp_w1, mlp_w2, pool_w))

    rope_tabs = _rope_tables(l)
    dft_x = _dft_matrices(l // _hyena_blocks(l))
    dft_c = _dft_matrices(lc // _hyena_blocks(lc))

    x2 = x.reshape(b * l, d)
    c2 = ctx.reshape(b * lc, d)
    sh1, sc1 = chunks(0, 0, b)[:2]
    csh1, csc1 = chunks(0, b, b + 1)[:2]
    hx = _norm_call(x2, norm1_g[0], sc1, sh1, l)
    hc = _norm_call(c2, norm1_g[0], csc1, csh1, lc)

    for layer in range(depth):
        last = layer == depth - 1
        _, _, ga1, sh2, sc2, ga2 = chunks(layer, 0, b)
        _, _, cga1, csh2, csc2, cga2 = chunks(layer, b, b + 1)
        filt_params = (filt_w0[layer], filt_b0[layer], filt_w1[layer], filt_b1[layer], filt_freq[layer],
                       filt_w2[layer])
        conv_b = hy_conv_b[layer].reshape(1, -1)
        d_skip = hy_bias[layer].reshape(HYENA_ORDER, 1, HYENA_W)
        merge_w = (w_att_b, w_hy_b, w_pool_b, w_out_b, layer)

        gq, gk = q_norm_g[layer], k_norm_g[layer]
        if last:
            kvc = _kv_call(_proj_call(hc, w_in_b, layer, K_OFF, 2 * KV_W, F32), gk)
        else:
            zc = _proj_call(hc, w_in_b, layer, 0, GATE_OFF, F32)
            gc, qc, kvc = _gates_qkv_call(hc, w_in_b, layer, zc, gq, gk, None, lc)
            zc, qc = zc.reshape(b, lc, GATE_OFF), qc.reshape(b, lc, ATTN_W)
        kvc = kvc.reshape(b, lc, 2 * KV_DUP_W)

        zx = _proj_call(hx, w_in_b, layer, 0, GATE_OFF, F32)
        gx, qx, kvx = _gates_qkv_call(hx, w_in_b, layer, zx, gq, gk, rope_tabs, l)
        zx, qx, kvx = zx.reshape(b, l, GATE_OFF), qx.reshape(b, l, ATTN_W), kvx.reshape(b, l, 2 * KV_DUP_W)
        hy = (hy_conv_w[layer], conv_b, filt_params, d_skip, dft_x)
        ya, yh, yp = _mixers(zx, qx, kvx, kvc, sink[layer], True, hy, pool_w_b[layer], pool_scale[layer])
        xn, h2 = _merge_call(ya, yh, yp, gx, x2, ga1, norm2_g[layer], sc2, sh2, *merge_w, l)
        nxt = None if last else (norm1_g[layer + 1], *reversed(chunks(layer + 1, 0, b)[:2]))
        x2, hx = _mlp_call(h2, w1_b, w2_b, layer, xn, ga2, nxt, l)

        if not last:
            hyc = (hy_conv_w[layer], conv_b, filt_params, d_skip, dft_c)
            ya, yh, yp = _mixers(zc, qc, None, kvc, sink[layer], False, hyc, pool_w_b[layer], pool_scale[layer])
            cn, h2c = _merge_call(ya, yh, yp, gc, c2, cga1, norm2_g[layer], csc2, csh2, *merge_w, lc)
            nxt = (norm1_g[layer + 1], *reversed(chunks(layer + 1, b, b + 1)[:2]))
            c2, hc = _mlp_call(h2c, w1_b, w2_b, layer, cn, cga2, nxt, lc)

    return x2.reshape(b, l, d)
```

```python
import functools
import math

import jax
import jax.numpy as jnp
from jax import lax
from jax.experimental import pallas as pl
from jax.experimental.pallas import tpu as pltpu

D_MODEL = 2048
DEPTH = 2
GRID_W = 64
EPS = 1e-6
NEG_INF = -1e30

N_HEADS = 16
N_KV_HEADS = 4
GQA_GROUP = N_HEADS // N_KV_HEADS
HEAD_DIM = 64
ATTN_W = N_HEADS * HEAD_DIM
KV_W = N_KV_HEADS * HEAD_DIM
WINDOW = 128
ROPE_FREQS = HEAD_DIM // 4
ROPE_BASE = 10000.0

HYENA_W = D_MODEL // 4
HYENA_ORDER = 2
FILTER_BANDS = 16
FILTER_EMB = 1 + 2 * FILTER_BANDS
FILTER_HIDDEN = 64
FILTER_INNER = 2
DECAY_TARGET = 1e-2
FAST_DECAY_PCT = 0.3
SLOW_DECAY_PCT = 1.5

POOL_W = D_MODEL // 4
POOL_WINDOWS = (2, 4, 8, 16)
POOL_GROUP = POOL_W // len(POOL_WINDOWS)

N_BRANCH = 3
D_FF = 4 * D_MODEL

Q_OFF = 0
K_OFF = Q_OFF + ATTN_W
V_OFF = K_OFF + KV_W
HY_OFF = V_OFF + KV_W
POOL_OFF = HY_OFF + 3 * HYENA_W
GATE_OFF = POOL_OFF + POOL_W
IN_W = GATE_OFF + N_BRANCH * D_MODEL

V7X_LANES = 128
V7X_VMEM_LIMIT = 60 * 1024 * 1024
KV_DUP_W = N_KV_HEADS * V7X_LANES
MOD_ROWS = 24

F32 = jnp.float32
BF16 = jnp.bfloat16
HIGHEST = lax.Precision.HIGHEST


def _params(*semantics):
    return pltpu.CompilerParams(dimension_semantics=semantics, vmem_limit_bytes=V7X_VMEM_LIMIT)


def _const_spec(shape):
    zeros = (0,) * len(shape)
    return pl.BlockSpec(shape, lambda *_: zeros, pipeline_mode=pl.Buffered(1))


def _mod_spec(arr, rows_per_mod_tile):
    d = arr.shape[-1]
    if arr.shape[0] == 1:
        return pl.BlockSpec((1, 1, d), lambda i, *_: (0, 0, 0))
    return pl.BlockSpec((1, 1, d), lambda i, *_: (i // rows_per_mod_tile, 0, 0))


def _norm_mod(xf, g, sc, sh):
    y = xf * lax.rsqrt(jnp.mean(xf * xf, axis=-1, keepdims=True) + EPS)
    return (y * g) * (1.0 + sc) + sh


def _mod_body(c_ref, w_ref, b_ref, o_ref):
    c = c_ref[...]
    s = c * jax.nn.sigmoid(c)
    o_ref[0] = jnp.dot(s.astype(BF16), w_ref[0].astype(BF16), preferred_element_type=F32) + b_ref[0]


def _modulation(cc, w_mod, b_mod):
    depth, d, n = w_mod.shape
    tn = 1024
    return pl.pallas_call(
        _mod_body,
        grid=(depth, n // tn),
        in_specs=[
            pl.BlockSpec((MOD_ROWS, d), lambda l, j: (0, 0)),
            pl.BlockSpec((1, d, tn), lambda l, j: (l, 0, j)),
            pl.BlockSpec((1, 1, tn), lambda l, j: (l, 0, j)),
        ],
        out_specs=pl.BlockSpec((1, MOD_ROWS, tn), lambda l, j: (l, 0, j)),
        out_shape=jax.ShapeDtypeStruct((depth, MOD_ROWS, n), F32),
        compiler_params=_params("arbitrary", "arbitrary"),
        name="modulation",
    )(cc, w_mod, b_mod.reshape(depth, 1, n))


def _norm_body(x_ref, g_ref, sc_ref, sh_ref, o_ref):
    o_ref[...] = _norm_mod(x_ref[...], g_ref[...], sc_ref[0], sh_ref[0]).astype(o_ref.dtype)


def _norm_call(x2, g, sc, sh, rows_per_batch):
    m, d = x2.shape
    tm = min(512, rows_per_batch)
    return pl.pallas_call(
        _norm_body,
        grid=(m // tm,),
        in_specs=[
            pl.BlockSpec((tm, d), lambda i: (i, 0)),
            pl.BlockSpec((1, d), lambda i: (0, 0)),
            _mod_spec(sc, rows_per_batch // tm),
            _mod_spec(sh, rows_per_batch // tm),
        ],
        out_specs=pl.BlockSpec((tm, d), lambda i: (i, 0)),
        out_shape=jax.ShapeDtypeStruct((m, d), BF16),
        compiler_params=_params("arbitrary"),
        name="norm_mod",
    )(x2, g.reshape(1, d), sc, sh)


def _proj_body(a_ref, w_ref, o_ref):
    o_ref[...] = jnp.dot(a_ref[...], w_ref[...], preferred_element_type=F32).astype(o_ref.dtype)


def _proj_call(a, w, layer, col0, n, out_dtype):
    m, k = a.shape
    tm = min(2048, m)
    tn = 512
    c0 = col0 // tn
    return pl.pallas_call(
        _proj_body,
        grid=(m // tm, n // tn),
        in_specs=[
            pl.BlockSpec((tm, k), lambda i, j: (i, 0)),
            pl.BlockSpec((None, k, tn), lambda i, j: (layer, 0, c0 + j)),
        ],
        out_specs=pl.BlockSpec((tm, tn), lambda i, j: (i, j)),
        out_shape=jax.ShapeDtypeStruct((m, n), out_dtype),
        compiler_params=_params("arbitrary", "arbitrary"),
        name="in_proj",
    )(a, w)


Q_SLABS = ATTN_W // V7X_LANES
QK_SLABS = Q_SLABS + KV_W // V7X_LANES
QKV_SLABS = QK_SLABS + KV_W // V7X_LANES


def _gates_qkv_body(a_ref, w_ref, z_ref, gain_ref, cos_ref, sup_ref, sdn_ref, g_ref, q_ref, kv_ref):
    j = pl.program_id(1)
    zg = jnp.dot(a_ref[...], w_ref[...], preferred_element_type=F32)
    g_ref[...] = (0.5 * jnp.tanh(0.5 * zg) + 0.5).astype(g_ref.dtype)

    x = z_ref[...]
    low = lax.broadcasted_iota(jnp.int32, (1, V7X_LANES), 1) < HEAD_DIM
    x2 = x * x
    ss = jnp.where(low, jnp.sum(jnp.where(low, x2, 0.0), axis=-1, keepdims=True),
                   jnp.sum(jnp.where(low, 0.0, x2), axis=-1, keepdims=True))
    inv = jnp.where(j < QK_SLABS, lax.rsqrt(ss * (1.0 / HEAD_DIM) + EPS), 1.0)
    y = _rope((x * inv) * gain_ref[0], cos_ref[...], sup_ref[...], sdn_ref[...])
    da, db = _dup_pair(y, low)

    @pl.when(j < Q_SLABS)
    def _():
        q_ref[...] = y.astype(q_ref.dtype)

    @pl.when(j >= Q_SLABS)
    def _():
        kv_ref[:, 0:V7X_LANES] = da.astype(kv_ref.dtype)
        kv_ref[:, V7X_LANES:2 * V7X_LANES] = db.astype(kv_ref.dtype)


def _gates_qkv_call(a, w, layer, z2, gq, gk, rope_tabs, seq_len):
    m, k = a.shape
    d = D_MODEL
    tm = min(2048, m)
    tn = 512
    assert N_BRANCH * d // tn == QKV_SLABS and tm % seq_len == 0
    c0 = GATE_OFF // tn
    ones = jnp.ones((tm, V7X_LANES), F32)
    zeros = jnp.zeros((tm, V7X_LANES), F32)
    if rope_tabs is None:
        cos, sup, sdn = ones[None], zeros[None], zeros[None]
        tab_map = lambda i, j: (0, 0, 0)
    else:
        rep = lambda t: jnp.tile(t, (tm // seq_len, 1))
        cos, sup, sdn = (jnp.stack([rep(t), ident]) for t, ident in zip(rope_tabs, (ones, zeros, zeros)))
        tab_map = lambda i, j: ((j >= QK_SLABS).astype(jnp.int32), 0, 0)
    gains = jnp.stack([jnp.tile(gq, 2) * HEAD_DIM ** -0.5, jnp.tile(gk, 2), jnp.ones((V7X_LANES,), F32)])
    gain_map = lambda i, j: ((j >= Q_SLABS).astype(jnp.int32) + (j >= QK_SLABS).astype(jnp.int32), 0, 0)
    tab_spec = pl.BlockSpec((None, tm, V7X_LANES), tab_map)
    return pl.pallas_call(
        _gates_qkv_body,
        grid=(m // tm, QKV_SLABS),
        in_specs=[
            pl.BlockSpec((tm, k), lambda i, j: (i, 0)),
            pl.BlockSpec((None, k, tn), lambda i, j: (layer, 0, c0 + j)),
            pl.BlockSpec((tm, V7X_LANES), lambda i, j: (i, j)),
            pl.BlockSpec((None, 1, V7X_LANES), gain_map),
            tab_spec, tab_spec, tab_spec,
        ],
        out_specs=[
            pl.BlockSpec((tm, tn), lambda i, j: (i, j)),
            pl.BlockSpec((tm, V7X_LANES), lambda i, j: (i, jnp.minimum(j, Q_SLABS - 1))),
            pl.BlockSpec((tm, 2 * V7X_LANES), lambda i, j: (i, jnp.maximum(j - Q_SLABS, 0))),
        ],
        out_shape=[
            jax.ShapeDtypeStruct((m, N_BRANCH * d), BF16),
            jax.ShapeDtypeStruct((m, ATTN_W), BF16),
            jax.ShapeDtypeStruct((m, 2 * KV_DUP_W), BF16),
        ],
        compiler_params=_params("arbitrary", "arbitrary"),
        name="gates_qkv",
    )(a, w, z2, gains.reshape(3, 1, V7X_LANES), cos, sup, sdn)


def _pair_block_diag():
    r = lax.broadcasted_iota(jnp.int32, (V7X_LANES, V7X_LANES), 0) // HEAD_DIM
    c = lax.broadcasted_iota(jnp.int32, (V7X_LANES, V7X_LANES), 1) // HEAD_DIM
    return (r == c).astype(F32)


def _head_norm(x, g, bd):
    ss = jnp.dot(x * x, bd, precision=HIGHEST, preferred_element_type=F32)
    return (x * lax.rsqrt(ss * (1.0 / HEAD_DIM) + EPS)) * g


def _rope(x, cos, sin_up, sin_dn):
    up = pltpu.roll(x, V7X_LANES - ROPE_FREQS, 1)
    dn = pltpu.roll(x, ROPE_FREQS, 1)
    return x * cos + up * sin_up + dn * sin_dn


def _dup_pair(x, low):
    r = pltpu.roll(x, HEAD_DIM, 1)
    return jnp.where(low, x, r), jnp.where(low, r, x)


def _kv_body(z_ref, gk_ref, kv_ref):
    bd = _pair_block_diag()
    low = lax.broadcasted_iota(jnp.int32, (1, V7X_LANES), 1) < HEAD_DIM
    for s in range(2 * KV_W // V7X_LANES):
        x = z_ref[:, s * V7X_LANES:(s + 1) * V7X_LANES]
        if s < KV_W // V7X_LANES:
            x = _head_norm(x, gk_ref[...], bd)
        a, b = _dup_pair(x, low)
        base = 2 * s * V7X_LANES
        kv_ref[:, base:base + V7X_LANES] = a.astype(kv_ref.dtype)
        kv_ref[:, base + V7X_LANES:base + 2 * V7X_LANES] = b.astype(kv_ref.dtype)


def _kv_call(z2, gk):
    m, nz = z2.shape
    tm = min(512, m)
    return pl.pallas_call(
        _kv_body,
        grid=(m // tm,),
        in_specs=[pl.BlockSpec((tm, nz), lambda i: (i, 0)),
                  pl.BlockSpec((1, V7X_LANES), lambda i: (0, 0))],
        out_specs=pl.BlockSpec((tm, 2 * KV_DUP_W), lambda i: (i, 0)),
        out_shape=jax.ShapeDtypeStruct((m, 2 * KV_DUP_W), BF16),
        compiler_params=_params("arbitrary"),
        name="kv_prep",
    )(z2, jnp.tile(gk, 2).reshape(1, V7X_LANES))


def _rope_tables(l):
    rows = l // GRID_W
    row = jnp.repeat(jnp.arange(rows, dtype=F32), GRID_W)
    col = jnp.tile(jnp.arange(GRID_W, dtype=F32), rows)
    inv = ROPE_BASE ** (-jnp.arange(ROPE_FREQS, dtype=F32) / ROPE_FREQS)
    ang = jnp.stack([row[:, None] * inv, col[:, None] * inv], axis=1)
    cos, sin = jnp.cos(ang), jnp.sin(ang)
    zero = jnp.zeros_like(sin)
    cos_h = jnp.stack([cos, cos], axis=2).reshape(l, HEAD_DIM)
    sup_h = jnp.stack([-sin, zero], axis=2).reshape(l, HEAD_DIM)
    sdn_h = jnp.stack([zero, sin], axis=2).reshape(l, HEAD_DIM)
    return tuple(jnp.tile(t, (1, 2)) for t in (cos_h, sup_h, sdn_h))


def _attn_body(*refs, local, tq):
    it = iter(refs)
    sink_ref = next(it)
    q_ref = next(it)
    if local:
        kp_ref, kc_ref, kn_ref, vp_ref, vc_ref, vn_ref = (next(it) for _ in range(6))
    kx_ref, vx_ref = next(it), next(it)
    o_ref = next(it)

    i = pl.program_id(1)
    nb = pl.num_programs(1)
    low = lax.broadcasted_iota(jnp.int32, (1, V7X_LANES), 1) < HEAD_DIM
    rows = GQA_GROUP * tq
    if local:
        qi = lax.broadcasted_iota(jnp.int32, (rows, tq), 0) % tq
        kj = lax.broadcasted_iota(jnp.int32, (rows, tq), 1)
        mask_prev = (kj >= qi) & (i > 0)
        mask_next = (kj <= qi) & (i < nb - 1)
    row_head = lax.broadcasted_iota(jnp.int32, (rows, 1), 0) // tq
    zero = jnp.zeros((), q_ref.dtype)

    for h in range(N_KV_HEADS):
        hs = slice(h * V7X_LANES, (h + 1) * V7X_LANES)
        qa = q_ref[0, :, 2 * h * V7X_LANES:(2 * h + 1) * V7X_LANES]
        qb = q_ref[0, :, (2 * h + 1) * V7X_LANES:(2 * h + 2) * V7X_LANES]
        qs = jnp.concatenate([jnp.where(low, qa, zero), jnp.where(low, zero, qa),
                              jnp.where(low, qb, zero), jnp.where(low, zero, qb)], axis=0)
        kparts, vparts, masks = [kx_ref[0, :, hs]], [vx_ref[0, :, hs]], {}
        if local:
            kparts = [kp_ref[0, :, hs], kc_ref[0, :, hs], kn_ref[0, :, hs]] + kparts
            vparts = [vp_ref[0, :, hs], vc_ref[0, :, hs], vn_ref[0, :, hs]] + vparts
            masks = {0: mask_prev, 2: mask_next}
        k_all = jnp.concatenate(kparts, axis=0)
        v_all = jnp.concatenate(vparts, axis=0)

        sink = jnp.zeros((rows, 1), F32)
        for g in range(GQA_GROUP):
            sink = jnp.where(row_head == g, sink_ref[GQA_GROUP * h + g], sink)
        s_all = lax.dot_general(qs, k_all, (((1,), (1,)), ((), ())), preferred_element_type=F32)
        chunks = []
        for c in range(k_all.shape[0] // tq):
            s = s_all[:, c * tq:(c + 1) * tq]
            chunks.append(jnp.where(masks[c], s, NEG_INF) if c in masks else s)
        m = jnp.maximum(sink, jnp.max(functools.reduce(jnp.maximum, chunks), axis=-1, keepdims=True))
        probs = [jnp.exp(s - m) for s in chunks]
        denom = jnp.exp(sink - m) + jnp.sum(functools.reduce(jnp.add, probs), axis=-1, keepdims=True)
        p_all = jnp.concatenate([p.astype(v_all.dtype) for p in probs], axis=1)
        o = jnp.dot(p_all, v_all, preferred_element_type=F32) / denom
        oa = jnp.where(low, o[0:tq], o[tq:2 * tq])
        ob = jnp.where(low, o[2 * tq:3 * tq], o[3 * tq:4 * tq])
        o_ref[0, :, 2 * h * V7X_LANES:(2 * h + 1) * V7X_LANES] = oa.astype(o_ref.dtype)
        o_ref[0, :, (2 * h + 1) * V7X_LANES:(2 * h + 2) * V7X_LANES] = ob.astype(o_ref.dtype)


def _attn_call(q, kv, kvx, sink, local):
    b, l, _ = q.shape
    lx = kvx.shape[1]
    tq = WINDOW
    nb = l // tq
    blk = lambda w: (1, tq, w)
    in_specs = [pl.BlockSpec(memory_space=pltpu.SMEM),
                pl.BlockSpec(blk(ATTN_W), lambda bi, i: (bi, i, 0))]
    args = [sink, q]
    if local:
        for half in (0, 1):
            for mp in (lambda bi, i, half=half: (bi, jnp.maximum(i - 1, 0), half),
                       lambda bi, i, half=half: (bi, i, half),
                       lambda bi, i, half=half: (bi, jnp.minimum(i + 1, nb - 1), half)):
                in_specs.append(pl.BlockSpec(blk(KV_DUP_W), mp))
                args.append(kv)
    for half in (0, 1):
        in_specs.append(pl.BlockSpec((1, lx, KV_DUP_W), lambda bi, i, half=half: (bi, 0, half)))
        args.append(kvx)
    return pl.pallas_call(
        functools.partial(_attn_body, local=local, tq=tq),
        grid=(b, nb),
        in_specs=in_specs,
        out_specs=pl.BlockSpec(blk(ATTN_W), lambda bi, i: (bi, i, 0)),
        out_shape=jax.ShapeDtypeStruct((b, l, ATTN_W), BF16),
        compiler_params=_params("arbitrary", "arbitrary"),
        name="attention",
    )(*args)


def _filter_body(z_ref, w0_ref, b0_ref, w1_ref, b1_ref, fr_ref, w2_ref, dec_ref, o_ref):
    fr = fr_ref[...]
    dot = functools.partial(jnp.dot, precision=HIGHEST, preferred_element_type=F32)
    h = jnp.sin(fr * (dot(z_ref[...], w0_ref[...]) + b0_ref[...]))
    for i in range(FILTER_INNER):
        h = jnp.sin(fr * (dot(h, w1_ref[i]) + b1_ref[i]))
    dec = dec_ref[...]
    for s in range(2 * HYENA_ORDER):
        sl = slice(s * HYENA_W, (s + 1) * HYENA_W)
        o_ref[:, sl] = dot(h, w2_ref[:, sl]) * dec


def _filter_features(l):
    t = jnp.linspace(0.0, 1.0, l, dtype=F32)[:, None]
    w = 2.0 * math.pi * jnp.arange(l, dtype=F32)[:, None] / l
    bands = jnp.linspace(1e-4, FILTER_BANDS - 1, FILTER_BANDS, dtype=F32)[None, :]
    z = jnp.concatenate([t, jnp.cos(bands * w), -jnp.sin(bands * w)], axis=-1)
    deltas = jnp.linspace(math.log(DECAY_TARGET) / SLOW_DECAY_PCT, math.log(DECAY_TARGET) / FAST_DECAY_PCT,
                          HYENA_W, dtype=F32)
    decay = jnp.exp(-t * jnp.abs(deltas))
    return jnp.pad(z, ((0, 0), (0, V7X_LANES - FILTER_EMB))), decay


def _filter_call(l, w0, b0, w1, b1, freq, w2):
    zfeat, decay = _filter_features(l)
    w0p = jnp.pad(w0, ((0, V7X_LANES - FILTER_EMB), (0, 0)))
    tl = min(512, l)
    nf = 2 * HYENA_ORDER * HYENA_W
    full = lambda shape: pl.BlockSpec(shape, lambda i: (0,) * len(shape))
    return pl.pallas_call(
        _filter_body,
        grid=(l // tl,),
        in_specs=[
            pl.BlockSpec((tl, V7X_LANES), lambda i: (i, 0)),
            full((V7X_LANES, FILTER_HIDDEN)),
            full((1, FILTER_HIDDEN)),
            full((FILTER_INNER, FILTER_HIDDEN, FILTER_HIDDEN)),
            full((FILTER_INNER, 1, FILTER_HIDDEN)),
            full((1, FILTER_HIDDEN)),
            full((FILTER_HIDDEN, nf)),
            pl.BlockSpec((tl, HYENA_W), lambda i: (i, 0)),
        ],
        out_specs=pl.BlockSpec((tl, nf), lambda i: (i, 0)),
        out_shape=jax.ShapeDtypeStruct((l, nf), F32),
        compiler_params=_params("arbitrary"),
        name="hyena_filter",
    )(zfeat, w0p, b0.reshape(1, -1), w1, b1.reshape(FILTER_INNER, 1, -1), freq.reshape(1, -1), w2, decay)


def _hyena_blocks(l):
    return max(1, min(4, l // V7X_LANES))


def _dft_matrices(blk):
    n = 2 * blk
    r = jnp.arange(blk, dtype=jnp.int32)
    ang = ((r[:, None] * r[None, :]) % n).astype(F32) * (2.0 * math.pi / n)
    return jnp.cos(ang).astype(BF16), jnp.sin(ang).astype(BF16)


def _alternating(l):
    row = lax.broadcasted_iota(jnp.int32, (l, 1), 0)
    return row, jnp.where(row % 2 == 0, 1.0, -1.0).astype(F32)


def _spectrum_body(hf_ref, hb_ref, fc_ref, fs_ref, ka_ref, kb_ref, kn_ref, *, nblk):
    l = hf_ref.shape[0]
    b = l // nblk
    n = 2 * b
    row = lax.broadcasted_iota(jnp.int32, (l, 1), 0)
    _, alt = _alternating(b)
    hf = hf_ref[...]
    hbs = jnp.where(row == 0, 0.0, pltpu.roll(hb_ref[...], 1, 0))
    fc, fs = fc_ref[...], fs_ref[...]

    def transforms(h):
        out = []
        for k in range(nblk):
            hk = h[k * b:(k + 1) * b]
            hk16 = hk.astype(BF16)
            out.append(dict(
                c=jnp.dot(fc, hk16, preferred_element_type=F32),
                s=jnp.dot(fs, hk16, preferred_element_type=F32),
                first16=hk16[0:1].astype(F32),
                first=hk[0:1],
                alt=jnp.sum(hk * alt, axis=0, keepdims=True)))
        return out

    tf, tb = transforms(hf), transforms(hbs)
    brow = lax.broadcasted_iota(jnp.int32, (b, 1), 0)
    w_re = jnp.where(brow == 0, 1.0 / n, 2.0 / n)
    for d in range(-(nblk - 1), nblk):
        idx = d + nblk - 1
        if d == 0:
            kre = tf[0]["c"] + tb[0]["c"]
            kim = tb[0]["s"] - tf[0]["s"]
            kn = tf[0]["alt"] + tb[0]["alt"]
        else:
            t, e, sg = (tf, d, -1.0) if d > 0 else (tb, -d, 1.0)
            kre = t[e]["c"] + alt * (t[e - 1]["c"] - t[e - 1]["first16"])
            kim = sg * (t[e]["s"] + alt * t[e - 1]["s"])
            kn = t[e]["alt"] + t[e - 1]["alt"] - t[e - 1]["first"]
        ka_ref[0, idx] = kre * w_re
        kb_ref[0, idx] = kim * (2.0 / n)
        kn_ref[0, idx] = kn * (1.0 / n)


def _spectrum_call(filt, fc, fs, nblk, tc):
    l = filt.shape[0]
    b = l // nblk
    nct = HYENA_W // tc
    nlag = 2 * nblk - 1
    return pl.pallas_call(
        functools.partial(_spectrum_body, nblk=nblk),
        grid=(HYENA_ORDER, nct),
        in_specs=[
            pl.BlockSpec((l, tc), lambda o, c: (0, 2 * nct * o + c)),
            pl.BlockSpec((l, tc), lambda o, c: (0, 2 * nct * o + nct + c)),
            _const_spec((b, b)),
            _const_spec((b, b)),
        ],
        out_specs=[
            pl.BlockSpec((1, nlag, b, tc), lambda o, c: (o, 0, 0, c)),
            pl.BlockSpec((1, nlag, b, tc), lambda o, c: (o, 0, 0, c)),
            pl.BlockSpec((1, nlag, 1, tc), lambda o, c: (o, 0, 0, c)),
        ],
        out_shape=[
            jax.ShapeDtypeStruct((HYENA_ORDER, nlag, b, HYENA_W), F32),
            jax.ShapeDtypeStruct((HYENA_ORDER, nlag, b, HYENA_W), F32),
            jax.ShapeDtypeStruct((HYENA_ORDER, nlag, 1, HYENA_W), F32),
        ],
        compiler_params=_params("arbitrary", "arbitrary"),
        name="hyena_spectrum",
    )(filt, filt, fc, fs)


def _conv3(x, w_ref, b_ref, row):
    l = x.shape[0]
    xm = jnp.where(row == 0, 0.0, pltpu.roll(x, 1, 0))
    xp = jnp.where(row == l - 1, 0.0, pltpu.roll(x, l - 1, 0))
    return xm * w_ref[0:1, :] + x * w_ref[1:2, :] + xp * w_ref[2:3, :] + b_ref[...]


def _fftconv_body(*refs, conv_u, nblk):
    it = iter(refs)
    u_ref = next(it)
    if conv_u:
        uw_ref, ub_ref = next(it), next(it)
    g_ref, gw_ref, gb_ref = next(it), next(it), next(it)
    ka_ref, kb_ref, kn_ref, d_ref, fc_ref, fs_ref, o_ref = (next(it) for _ in range(7))

    l = u_ref.shape[1]
    b = l // nblk
    row = lax.broadcasted_iota(jnp.int32, (l, 1), 0)
    _, alt = _alternating(b)
    u = u_ref[0].astype(F32)
    if conv_u:
        u = _conv3(u, uw_ref, ub_ref, row)
    gate = _conv3(g_ref[0], gw_ref, gb_ref, row)
    fc, fs = fc_ref[...], fs_ref[...]

    ps, qs, ns = [], [], []
    for j in range(nblk):
        uj = u[j * b:(j + 1) * b]
        uj16 = uj.astype(BF16)
        ps.append(jnp.dot(fc, uj16, preferred_element_type=F32))
        qs.append(jnp.dot(fs, uj16, preferred_element_type=F32))
        ns.append(jnp.sum(uj * alt, axis=0, keepdims=True))
    for i in range(nblk):
        r = t = nyq = None
        for j in range(nblk):
            lag = i - j + nblk - 1
            ka, kb = ka_ref[0, lag], kb_ref[0, lag]
            dr = ps[j] * ka + qs[j] * kb
            dt = qs[j] * ka - ps[j] * kb
            dn = ns[j] * kn_ref[0, lag]
            r, t, nyq = (dr, dt, dn) if j == 0 else (r + dr, t + dt, nyq + dn)
        y = (jnp.dot(fc, r.astype(BF16), preferred_element_type=F32)
             + jnp.dot(fs, t.astype(BF16), preferred_element_type=F32))
        rows = slice(i * b, (i + 1) * b)
        y = y + alt * nyq + u[rows] * d_ref[0]
        o_ref[0, rows, :] = (gate[rows] * y).astype(o_ref.dtype)


def _fftconv_call(u, u_col0, z, gate_col0, conv_w, conv_b, spectra, d_skip, order, fc, fs, nblk, tc, out_dtype):
    b, l, _ = z.shape
    conv_u = u is z
    nct = HYENA_W // tc
    ka, kb, kn = spectra
    blk = l // nblk
    nlag = 2 * nblk - 1
    col = lambda c0: (lambda c, bi: (bi, 0, c0 // tc + c))
    wcol = lambda c0: (lambda c, bi: (0, (c0 - HY_OFF) // tc + c))
    in_specs = [pl.BlockSpec((1, l, tc), col(u_col0))]
    args = [u]
    if conv_u:
        in_specs += [pl.BlockSpec((3, tc), wcol(u_col0)), pl.BlockSpec((1, tc), wcol(u_col0))]
        args += [conv_w, conv_b]
    in_specs += [pl.BlockSpec((1, l, tc), col(gate_col0)),
                 pl.BlockSpec((3, tc), wcol(gate_col0)), pl.BlockSpec((1, tc), wcol(gate_col0))]
    args += [z, conv_w, conv_b]
    spec = lambda rows: pl.BlockSpec((1, nlag, rows, tc), lambda c, bi: (order, 0, 0, c),
                                     pipeline_mode=pl.Buffered(1))
    in_specs += [spec(blk), spec(blk), spec(1),
                 pl.BlockSpec((1, 1, tc), lambda c, bi: (order, 0, c), pipeline_mode=pl.Buffered(1)),
                 _const_spec((blk, blk)), _const_spec((blk, blk))]
    args += [ka, kb, kn, d_skip, fc, fs]
    return pl.pallas_call(
        functools.partial(_fftconv_body, conv_u=conv_u, nblk=nblk),
        grid=(nct, b),
        in_specs=in_specs,
        out_specs=pl.BlockSpec((1, l, tc), lambda c, bi: (bi, 0, c)),
        out_shape=jax.ShapeDtypeStruct((b, l, HYENA_W), out_dtype),
        compiler_params=_params("arbitrary", "arbitrary"),
        name="hyena_conv",
    )(*args)


def _pool_body(x_ref, w_ref, s_ref, o_ref):
    l = x_ref.shape[1]
    row = lax.broadcasted_iota(jnp.int32, (l, 1), 0)
    for g, win in enumerate(POOL_WINDOWS):
        half = win // 2
        sl = slice(g * POOL_GROUP, (g + 1) * POOL_GROUP)
        x = x_ref[0, :, sl]
        acc = jnp.zeros_like(x)
        for k in range(-half, half):
            shifted = x if k == 0 else pltpu.roll(x, (-k) % l, 0)
            acc = acc + jnp.where((row + k >= 0) & (row + k < l), shifted, 0.0)
        cnt = (jnp.minimum(row + half, l) - jnp.maximum(row - half, 0)).astype(F32)
        d = acc / cnt - x
        y = jnp.dot(d.astype(BF16), w_ref[g], preferred_element_type=F32)
        o_ref[0, :, sl] = (y * s_ref[:, sl]).astype(o_ref.dtype)


def _pool_call(z, w_grp, scale):
    b, l, _ = z.shape
    ng = len(POOL_WINDOWS)
    return pl.pallas_call(
        _pool_body,
        grid=(b,),
        in_specs=[
            pl.BlockSpec((1, l, POOL_W), lambda bi: (bi, 0, POOL_OFF // POOL_W)),
            pl.BlockSpec((ng, POOL_GROUP, POOL_GROUP), lambda bi: (0, 0, 0)),
            pl.BlockSpec((1, POOL_W), lambda bi: (0, 0)),
        ],
        out_specs=pl.BlockSpec((1, l, POOL_W), lambda bi: (bi, 0, 0)),
        out_shape=jax.ShapeDtypeStruct((b, l, POOL_W), BF16),
        compiler_params=_params("arbitrary"),
        name="pool",
    )(z, w_grp, scale.reshape(1, POOL_W))


def _merge_body(ya_ref, yh_ref, yp_ref, gt_ref, x_ref, ga_ref, g2_ref, sc_ref, sh_ref,
                wa_ref, wh_ref, wp_ref, wo_ref, xn_ref, h2_ref):
    d = x_ref.shape[1]
    cj = 512
    ya, yh, yp = ya_ref[...], yh_ref[...], yp_ref[...]
    acc = jnp.zeros(x_ref.shape, F32)
    for j in range(d // cj):
        sl = slice(j * cj, (j + 1) * cj)
        gate = lambda br: gt_ref[:, br * d + j * cj:br * d + (j + 1) * cj].astype(F32)
        m = (gate(0) * jnp.dot(ya, wa_ref[:, sl], preferred_element_type=F32)
             + gate(1) * jnp.dot(yh, wh_ref[:, sl], preferred_element_type=F32)
             + gate(2) * jnp.dot(yp, wp_ref[:, sl], preferred_element_type=F32))
        acc = acc + jnp.dot(m.astype(BF16), wo_ref[sl, :], preferred_element_type=F32)
    xn = x_ref[...] + ga_ref[0] * acc
    xn_ref[...] = xn
    h2_ref[...] = _norm_mod(xn, g2_ref[...], sc_ref[0], sh_ref[0]).astype(h2_ref.dtype)


def _merge_call(ya, yh, yp, gates, x2, ga1, g2, sc2, sh2, wa, wh, wp, wo, layer, rows_per_batch):
    m, d = x2.shape
    tm = min(512, rows_per_batch)
    rpt = rows_per_batch // tm
    rows = lambda w: pl.BlockSpec((tm, w), lambda i: (i, 0))
    weight = lambda w: pl.BlockSpec((None,) + w.shape[1:], lambda i: (layer, 0, 0), pipeline_mode=pl.Buffered(1))
    return pl.pallas_call(
        _merge_body,
        grid=(m // tm,),
        in_specs=[rows(ATTN_W), rows(HYENA_W), rows(POOL_W), rows(N_BRANCH * d), rows(d),
                  _mod_spec(ga1, rpt), pl.BlockSpec((1, d), lambda i: (0, 0)),
                  _mod_spec(sc2, rpt), _mod_spec(sh2, rpt),
                  weight(wa), weight(wh), weight(wp), weight(wo)],
        out_specs=[rows(d), rows(d)],
        out_shape=[jax.ShapeDtypeStruct((m, d), F32), jax.ShapeDtypeStruct((m, d), BF16)],
        compiler_params=_params("arbitrary"),
        name="merge",
    )(ya, yh, yp, gates, x2, ga1, g2.reshape(1, d), sc2, sh2, wa, wh, wp, wo)


def _mlp_body(*refs, has_next):
    it = iter(refs)
    h_ref, w1_ref, w2_ref, x_ref, ga_ref = (next(it) for _ in range(5))
    if has_next:
        gn_ref, sc_ref, sh_ref = next(it), next(it), next(it)
    o_ref = next(it)
    hn_ref = next(it) if has_next else None

    f = pl.program_id(1)
    last = pl.num_programs(1) - 1
    tm, d = o_ref.shape
    cn = 512
    rb = min(256, tm)

    def hidden(rows):
        a = jnp.dot(h_ref[rows, :], w1_ref[...], preferred_element_type=F32)
        return jnp.square(jnp.maximum(a, 0.0)).astype(BF16)

    @pl.when(f == 0)
    def _():
        a = hidden(slice(None))
        for n0 in range(0, d, cn):
            o_ref[:, n0:n0 + cn] = jnp.dot(a, w2_ref[:, n0:n0 + cn], preferred_element_type=F32)

    @pl.when((f > 0) & (f < last))
    def _():
        a = hidden(slice(None))
        for n0 in range(0, d, cn):
            o_ref[:, n0:n0 + cn] += jnp.dot(a, w2_ref[:, n0:n0 + cn], preferred_element_type=F32)

    @pl.when(f == last)
    def _():
        for r0 in range(0, tm, rb):
            rows = slice(r0, r0 + rb)
            acc = o_ref[rows, :] + jnp.dot(hidden(rows), w2_ref[...], preferred_element_type=F32)
            xo = x_ref[rows, :] + ga_ref[0] * acc
            o_ref[rows, :] = xo
            if has_next:
                hn_ref[rows, :] = _norm_mod(xo, gn_ref[...], sc_ref[0], sh_ref[0]).astype(hn_ref.dtype)


def _mlp_call(h2, w1, w2, layer, xn, ga2, nxt, rows_per_batch):
    m, d = xn.shape
    ff = w1.shape[2]
    tm, tf = min(1024, rows_per_batch if ga2.shape[0] > 1 else m), 512
    rpt = max(rows_per_batch // tm, 1)
    assert ff // tf >= 2
    has_next = nxt is not None
    rows = pl.BlockSpec((tm, d), lambda i, f: (i, 0))
    in_specs = [rows, pl.BlockSpec((None, d, tf), lambda i, f: (layer, 0, f)),
                pl.BlockSpec((None, tf, d), lambda i, f: (layer, f, 0)),
                pl.BlockSpec((tm, d), lambda i, f: (i, 0), pipeline_mode=pl.Buffered(1)), _mod_spec(ga2, rpt)]
    args = [h2, w1, w2, xn, ga2]
    out_specs = [rows]
    out_shape = [jax.ShapeDtypeStruct((m, d), F32)]
    if has_next:
        gn, scn, shn = nxt
        in_specs += [pl.BlockSpec((1, d), lambda i, f: (0, 0)), _mod_spec(scn, rpt), _mod_spec(shn, rpt)]
        args += [gn.reshape(1, d), scn, shn]
        out_specs.append(rows)
        out_shape.append(jax.ShapeDtypeStruct((m, d), BF16))
    outs = pl.pallas_call(
        functools.partial(_mlp_body, has_next=has_next),
        grid=(m // tm, ff // tf),
        in_specs=in_specs,
        out_specs=out_specs,
        out_shape=out_shape,
        compiler_params=_params("arbitrary", "arbitrary"),
        name="mlp",
    )(*args)
    return (outs[0], outs[1]) if has_next else (outs[0], None)


def _mixers(z, q, kv, kvx, sink, local, hy, pool_w, pool_scale):
    b, l, _ = z.shape
    y_att = _attn_call(q, kv, kvx, sink, local)
    conv_w, conv_b, filt_params, d_skip, (fc, fs) = hy
    nblk = _hyena_blocks(l)
    tc = 256
    spectra = _spectrum_call(_filter_call(l, *filt_params), fc, fs, nblk, tc)
    conv = functools.partial(_fftconv_call, conv_w=conv_w, conv_b=conv_b, spectra=spectra, d_skip=d_skip,
                             fc=fc, fs=fs, nblk=nblk, tc=tc)
    z1 = conv(z, HY_OFF, z, HY_OFF + HYENA_W, order=0, out_dtype=F32)
    y_hy = conv(z1, 0, z, HY_OFF + 2 * HYENA_W, order=1, out_dtype=BF16)
    y_pool = _pool_call(z, pool_w, pool_scale)
    return (y_att.reshape(b * l, ATTN_W), y_hy.reshape(b * l, HYENA_W), y_pool.reshape(b * l, POOL_W))


def kernel(x, c, ctx, c_ctx, norm1_g, norm2_g, w_mod, b_mod, w_in, q_norm_g, k_norm_g, sink, hy_conv_w, hy_conv_b, filt_w0, filt_b0, filt_w1, filt_b1, filt_freq, filt_w2, hy_bias, pool_w, pool_scale, w_att_o, w_hy_o, w_pool_o, w_out, mlp_w1, mlp_w2):
    b, l, d = x.shape
    lc = ctx.shape[1]
    depth = w_mod.shape[0]

    cc = jnp.concatenate([c, c_ctx[None, :], jnp.zeros((MOD_ROWS - b - 1, d), F32)], axis=0)
    mods = _modulation(cc, w_mod, b_mod)

    def chunks(layer, lo, hi):
        return [mods[layer, lo:hi, i * d:(i + 1) * d].reshape(hi - lo, 1, d) for i in range(6)]

    as_bf16 = lambda w: w.astype(BF16)
    w_in_b, w_att_b, w_hy_b, w_pool_b, w_out_b = map(as_bf16, (w_in, w_att_o, w_hy_o, w_pool_o, w_out))
    w1_b, w2_b, pool_w_b = map(as_bf16, (mlp_w1, mlp_w2, pool_w))

    rope_tabs = _rope_tables(l)
    dft_x = _dft_matrices(l // _hyena_blocks(l))
    dft_c = _dft_matrices(lc // _hyena_blocks(lc))

    x2 = x.reshape(b * l, d)
    c2 = ctx.reshape(b * lc, d)
    sh1, sc1 = chunks(0, 0, b)[:2]
    csh1, csc1 = chunks(0, b, b + 1)[:2]
    hx = _norm_call(x2, norm1_g[0], sc1, sh1, l)
    hc = _norm_call(c2, norm1_g[0], csc1, csh1, lc)

    for layer in range(depth):
        last = layer == depth - 1
        _, _, ga1, sh2, sc2, ga2 = chunks(layer, 0, b)
        _, _, cga1, csh2, csc2, cga2 = chunks(layer, b, b + 1)
        filt_params = (filt_w0[layer], filt_b0[layer], filt_w1[layer], filt_b1[layer], filt_freq[layer],
                       filt_w2[layer])
        conv_b = hy_conv_b[layer].reshape(1, -1)
        d_skip = hy_bias[layer].reshape(HYENA_ORDER, 1, HYENA_W)
        merge_w = (w_att_b, w_hy_b, w_pool_b, w_out_b, layer)

        gq, gk = q_norm_g[layer], k_norm_g[layer]
        if last:
            kvc = _kv_call(_proj_call(hc, w_in_b, layer, K_OFF, 2 * KV_W, F32), gk)
        else:
            zc = _proj_call(hc, w_in_b, layer, 0, GATE_OFF, F32)
            gc, qc, kvc = _gates_qkv_call(hc, w_in_b, layer, zc, gq, gk, None, lc)
            zc, qc = zc.reshape(b, lc, GATE_OFF), qc.reshape(b, lc, ATTN_W)
        kvc = kvc.reshape(b, lc, 2 * KV_DUP_W)

        zx = _proj_call(hx, w_in_b, layer, 0, GATE_OFF, F32)
        gx, qx, kvx = _gates_qkv_call(hx, w_in_b, layer, zx, gq, gk, rope_tabs, l)
        zx, qx, kvx = zx.reshape(b, l, GATE_OFF), qx.reshape(b, l, ATTN_W), kvx.reshape(b, l, 2 * KV_DUP_W)
        hy = (hy_conv_w[layer], conv_b, filt_params, d_skip, dft_x)
        ya, yh, yp = _mixers(zx, qx, kvx, kvc, sink[layer], True, hy, pool_w_b[layer], pool_scale[layer])
        xn, h2 = _merge_call(ya, yh, yp, gx, x2, ga1, norm2_g[layer], sc2, sh2, *merge_w, l)
        nxt = None if last else (norm1_g[layer + 1], *reversed(chunks(layer + 1, 0, b)[:2]))
        x2, hx = _mlp_call(h2, w1_b, w2_b, layer, xn, ga2, nxt, l)

        if not last:
            hyc = (hy_conv_w[layer], conv_b, filt_params, d_skip, dft_c)
            ya, yh, yp = _mixers(zc, qc, None, kvc, sink[layer], False, hyc, pool_w_b[layer], pool_scale[layer])
            cn, h2c = _merge_call(ya, yh, yp, gc, c2, cga1, norm2_g[layer], csc2, csh2, *merge_w, lc)
            nxt = (norm1_g[layer + 1], *reversed(chunks(layer + 1, b, b + 1)[:2]))
            c2, hc = _mlp_call(h2c, w1_b, w2_b, layer, cn, cga2, nxt, lc)

    return x2.reshape(b, l, d)
```

```python
import functools
import math

import jax
import jax.numpy as jnp
from jax import lax
from jax.experimental import pallas as pl
from jax.experimental.pallas import tpu as pltpu

D_MODEL = 2048
DEPTH = 2
GRID_W = 64
EPS = 1e-6
NEG_INF = -1e30

N_HEADS = 16
N_KV_HEADS = 4
GQA_GROUP = N_HEADS // N_KV_HEADS
HEAD_DIM = 64
ATTN_W = N_HEADS * HEAD_DIM
KV_W = N_KV_HEADS * HEAD_DIM
WINDOW = 128
ROPE_FREQS = HEAD_DIM // 4
ROPE_BASE = 10000.0

HYENA_W = D_MODEL // 4
HYENA_ORDER = 2
FILTER_BANDS = 16
FILTER_EMB = 1 + 2 * FILTER_BANDS
FILTER_HIDDEN = 64
FILTER_INNER = 2
DECAY_TARGET = 1e-2
FAST_DECAY_PCT = 0.3
SLOW_DECAY_PCT = 1.5

POOL_W = D_MODEL // 4
POOL_WINDOWS = (2, 4, 8, 16)
POOL_GROUP = POOL_W // len(POOL_WINDOWS)

N_BRANCH = 3
D_FF = 4 * D_MODEL

Q_OFF = 0
K_OFF = Q_OFF + ATTN_W
V_OFF = K_OFF + KV_W
HY_OFF = V_OFF + KV_W
POOL_OFF = HY_OFF + 3 * HYENA_W
GATE_OFF = POOL_OFF + POOL_W
IN_W = GATE_OFF + N_BRANCH * D_MODEL

V7X_LANES = 128
V7X_VMEM_LIMIT = 60 * 1024 * 1024
KV_DUP_W = N_KV_HEADS * V7X_LANES
MOD_ROWS = 24

F32 = jnp.float32
BF16 = jnp.bfloat16
HIGHEST = lax.Precision.HIGHEST


def _params(*semantics):
    return pltpu.CompilerParams(dimension_semantics=semantics, vmem_limit_bytes=V7X_VMEM_LIMIT)


def _const_spec(shape):
    zeros = (0,) * len(shape)
    return pl.BlockSpec(shape, lambda *_: zeros, pipeline_mode=pl.Buffered(1))


def _mod_spec(arr, rows_per_mod_tile):
    d = arr.shape[-1]
    if arr.shape[0] == 1:
        return pl.BlockSpec((1, 1, d), lambda i, *_: (0, 0, 0))
    return pl.BlockSpec((1, 1, d), lambda i, *_: (i // rows_per_mod_tile, 0, 0))


def _norm_mod(xf, g, sc, sh):
    y = xf * lax.rsqrt(jnp.mean(xf * xf, axis=-1, keepdims=True) + EPS)
    return (y * g) * (1.0 + sc) + sh


def _mod_body(c_ref, w_ref, b_ref, o_ref):
    c = c_ref[...]
    s = c * jax.nn.sigmoid(c)
    o_ref[0] = jnp.dot(s.astype(BF16), w_ref[0].astype(BF16), preferred_element_type=F32) + b_ref[0]


def _modulation(cc, w_mod, b_mod):
    depth, d, n = w_mod.shape
    tn = 1024
    return pl.pallas_call(
        _mod_body,
        grid=(depth, n // tn),
        in_specs=[
            pl.BlockSpec((MOD_ROWS, d), lambda l, j: (0, 0)),
            pl.BlockSpec((1, d, tn), lambda l, j: (l, 0, j)),
            pl.BlockSpec((1, 1, tn), lambda l, j: (l, 0, j)),
        ],
        out_specs=pl.BlockSpec((1, MOD_ROWS, tn), lambda l, j: (l, 0, j)),
        out_shape=jax.ShapeDtypeStruct((depth, MOD_ROWS, n), F32),
        compiler_params=_params("arbitrary", "arbitrary"),
        name="modulation",
    )(cc, w_mod, b_mod.reshape(depth, 1, n))


def _norm_body(x_ref, g_ref, sc_ref, sh_ref, o_ref):
    o_ref[...] = _norm_mod(x_ref[...], g_ref[...], sc_ref[0], sh_ref[0]).astype(o_ref.dtype)


def _norm_call(x2, g, sc, sh, rows_per_batch):
    m, d = x2.shape
    tm = min(512, rows_per_batch)
    return pl.pallas_call(
        _norm_body,
        grid=(m // tm,),
        in_specs=[
            pl.BlockSpec((tm, d), lambda i: (i, 0)),
            pl.BlockSpec((1, d), lambda i: (0, 0)),
            _mod_spec(sc, rows_per_batch // tm),
            _mod_spec(sh, rows_per_batch // tm),
        ],
        out_specs=pl.BlockSpec((tm, d), lambda i: (i, 0)),
        out_shape=jax.ShapeDtypeStruct((m, d), BF16),
        compiler_params=_params("arbitrary"),
        name="norm_mod",
    )(x2, g.reshape(1, d), sc, sh)


def _proj_body(a_ref, w_ref, o_ref):
    o_ref[...] = jnp.dot(a_ref[...], w_ref[...], preferred_element_type=F32).astype(o_ref.dtype)


def _proj_call(a, w, layer, col0, n, out_dtype):
    m, k = a.shape
    tm = min(2048, m)
    tn = 512
    c0 = col0 // tn
    return pl.pallas_call(
        _proj_body,
        grid=(m // tm, n // tn),
        in_specs=[
            pl.BlockSpec((tm, k), lambda i, j: (i, 0)),
            pl.BlockSpec((None, k, tn), lambda i, j: (layer, 0, c0 + j)),
        ],
        out_specs=pl.BlockSpec((tm, tn), lambda i, j: (i, j)),
        out_shape=jax.ShapeDtypeStruct((m, n), out_dtype),
        compiler_params=_params("arbitrary", "arbitrary"),
        name="in_proj",
    )(a, w)


Q_SLABS = ATTN_W // V7X_LANES
QK_SLABS = Q_SLABS + KV_W // V7X_LANES
QKV_SLABS = QK_SLABS + KV_W // V7X_LANES


def _gates_qkv_body(a_ref, w_ref, z_ref, gain_ref, cos_ref, sup_ref, sdn_ref, g_ref, q_ref, kv_ref):
    j = pl.program_id(1)
    zg = jnp.dot(a_ref[...], w_ref[...], preferred_element_type=F32)
    g_ref[...] = (0.5 * jnp.tanh(0.5 * zg) + 0.5).astype(g_ref.dtype)

    x = z_ref[...]
    low = lax.broadcasted_iota(jnp.int32, (1, V7X_LANES), 1) < HEAD_DIM
    x2 = x * x
    ss = jnp.where(low, jnp.sum(jnp.where(low, x2, 0.0), axis=-1, keepdims=True),
                   jnp.sum(jnp.where(low, 0.0, x2), axis=-1, keepdims=True))
    inv = jnp.where(j < QK_SLABS, lax.rsqrt(ss * (1.0 / HEAD_DIM) + EPS), 1.0)
    y = _rope((x * inv) * gain_ref[0], cos_ref[...], sup_ref[...], sdn_ref[...])
    da, db = _dup_pair(y, low)

    @pl.when(j < Q_SLABS)
    def _():
        q_ref[...] = y.astype(q_ref.dtype)

    @pl.when(j >= Q_SLABS)
    def _():
        kv_ref[:, 0:V7X_LANES] = da.astype(kv_ref.dtype)
        kv_ref[:, V7X_LANES:2 * V7X_LANES] = db.astype(kv_ref.dtype)


def _gates_qkv_call(a, w, layer, z2, gq, gk, rope_tabs, seq_len):
    m, k = a.shape
    d = D_MODEL
    tm = min(2048, m)
    tn = 512
    assert N_BRANCH * d // tn == QKV_SLABS and tm % seq_len == 0
    c0 = GATE_OFF // tn
    ones = jnp.ones((tm, V7X_LANES), F32)
    zeros = jnp.zeros((tm, V7X_LANES), F32)
    if rope_tabs is None:
        cos, sup, sdn = ones[None], zeros[None], zeros[None]
        tab_map = lambda i, j: (0, 0, 0)
    else:
        rep = lambda t: jnp.tile(t, (tm // seq_len, 1))
        cos, sup, sdn = (jnp.stack([rep(t), ident]) for t, ident in zip(rope_tabs, (ones, zeros, zeros)))
        tab_map = lambda i, j: ((j >= QK_SLABS).astype(jnp.int32), 0, 0)
    gains = jnp.stack([jnp.tile(gq, 2) * HEAD_DIM ** -0.5, jnp.tile(gk, 2), jnp.ones((V7X_LANES,), F32)])
    gain_map = lambda i, j: ((j >= Q_SLABS).astype(jnp.int32) + (j >= QK_SLABS).astype(jnp.int32), 0, 0)
    tab_spec = pl.BlockSpec((None, tm, V7X_LANES), tab_map)
    return pl.pallas_call(
        _gates_qkv_body,
        grid=(m // tm, QKV_SLABS),
        in_specs=[
            pl.BlockSpec((tm, k), lambda i, j: (i, 0)),
            pl.BlockSpec((None, k, tn), lambda i, j: (layer, 0, c0 + j)),
            pl.BlockSpec((tm, V7X_LANES), lambda i, j: (i, j)),
            pl.BlockSpec((None, 1, V7X_LANES), gain_map),
            tab_spec, tab_spec, tab_spec,
        ],
        out_specs=[
            pl.BlockSpec((tm, tn), lambda i, j: (i, j)),
            pl.BlockSpec((tm, V7X_LANES), lambda i, j: (i, jnp.minimum(j, Q_SLABS - 1))),
            pl.BlockSpec((tm, 2 * V7X_LANES), lambda i, j: (i, jnp.maximum(j - Q_SLABS, 0))),
        ],
        out_shape=[
            jax.ShapeDtypeStruct((m, N_BRANCH * d), BF16),
            jax.ShapeDtypeStruct((m, ATTN_W), BF16),
            jax.ShapeDtypeStruct((m, 2 * KV_DUP_W), BF16),
        ],
        compiler_params=_params("arbitrary", "arbitrary"),
        name="gates_qkv",
    )(a, w, z2, gains.reshape(3, 1, V7X_LANES), cos, sup, sdn)


def _pair_block_diag():
    r = lax.broadcasted_iota(jnp.int32, (V7X_LANES, V7X_LANES), 0) // HEAD_DIM
    c = lax.broadcasted_iota(jnp.int32, (V7X_LANES, V7X_LANES), 1) // HEAD_DIM
    return (r == c).astype(F32)


def _head_norm(x, g, bd):
    ss = jnp.dot(x * x, bd, precision=HIGHEST, preferred_element_type=F32)
    return (x * lax.rsqrt(ss * (1.0 / HEAD_DIM) + EPS)) * g


def _rope(x, cos, sin_up, sin_dn):
    up = pltpu.roll(x, V7X_LANES - ROPE_FREQS, 1)
    dn = pltpu.roll(x, ROPE_FREQS, 1)
    return x * cos + up * sin_up + dn * sin_dn


def _dup_pair(x, low):
    r = pltpu.roll(x, HEAD_DIM, 1)
    return jnp.where(low, x, r), jnp.where(low, r, x)


def _kv_body(z_ref, gk_ref, kv_ref):
    bd = _pair_block_diag()
    low = lax.broadcasted_iota(jnp.int32, (1, V7X_LANES), 1) < HEAD_DIM
    for s in range(2 * KV_W // V7X_LANES):
        x = z_ref[:, s * V7X_LANES:(s + 1) * V7X_LANES]
        if s < KV_W // V7X_LANES:
            x = _head_norm(x, gk_ref[...], bd)
        a, b = _dup_pair(x, low)
        base = 2 * s * V7X_LANES
        kv_ref[:, base:base + V7X_LANES] = a.astype(kv_ref.dtype)
        kv_ref[:, base + V7X_LANES:base + 2 * V7X_LANES] = b.astype(kv_ref.dtype)


def _kv_call(z2, gk):
    m, nz = z2.shape
    tm = min(512, m)
    return pl.pallas_call(
        _kv_body,
        grid=(m // tm,),
        in_specs=[pl.BlockSpec((tm, nz), lambda i: (i, 0)),
                  pl.BlockSpec((1, V7X_LANES), lambda i: (0, 0))],
        out_specs=pl.BlockSpec((tm, 2 * KV_DUP_W), lambda i: (i, 0)),
        out_shape=jax.ShapeDtypeStruct((m, 2 * KV_DUP_W), BF16),
        compiler_params=_params("arbitrary"),
        name="kv_prep",
    )(z2, jnp.tile(gk, 2).reshape(1, V7X_LANES))


def _rope_tables(l):
    rows = l // GRID_W
    row = jnp.repeat(jnp.arange(rows, dtype=F32), GRID_W)
    col = jnp.tile(jnp.arange(GRID_W, dtype=F32), rows)
    inv = ROPE_BASE ** (-jnp.arange(ROPE_FREQS, dtype=F32) / ROPE_FREQS)
    ang = jnp.stack([row[:, None] * inv, col[:, None] * inv], axis=1)
    cos, sin = jnp.cos(ang), jnp.sin(ang)
    zero = jnp.zeros_like(sin)
    cos_h = jnp.stack([cos, cos], axis=2).reshape(l, HEAD_DIM)
    sup_h = jnp.stack([-sin, zero], axis=2).reshape(l, HEAD_DIM)
    sdn_h = jnp.stack([zero, sin], axis=2).reshape(l, HEAD_DIM)
    return tuple(jnp.tile(t, (1, 2)) for t in (cos_h, sup_h, sdn_h))


def _attn_body(*refs, local, tq):
    it = iter(refs)
    sink_ref = next(it)
    q_ref = next(it)
    if local:
        kp_ref, kc_ref, kn_ref, vp_ref, vc_ref, vn_ref = (next(it) for _ in range(6))
    kx_ref, vx_ref = next(it), next(it)
    o_ref = next(it)

    i = pl.program_id(1)
    nb = pl.num_programs(1)
    low = lax.broadcasted_iota(jnp.int32, (1, V7X_LANES), 1) < HEAD_DIM
    rows = GQA_GROUP * tq
    if local:
        qi = lax.broadcasted_iota(jnp.int32, (rows, tq), 0) % tq
        kj = lax.broadcasted_iota(jnp.int32, (rows, tq), 1)
        mask_prev = (kj >= qi) & (i > 0)
        mask_next = (kj <= qi) & (i < nb - 1)
    row_head = lax.broadcasted_iota(jnp.int32, (rows, 1), 0) // tq
    zero = jnp.zeros((), q_ref.dtype)

    for h in range(N_KV_HEADS):
        hs = slice(h * V7X_LANES, (h + 1) * V7X_LANES)
        qa = q_ref[0, :, 2 * h * V7X_LANES:(2 * h + 1) * V7X_LANES]
        qb = q_ref[0, :, (2 * h + 1) * V7X_LANES:(2 * h + 2) * V7X_LANES]
        qs = jnp.concatenate([jnp.where(low, qa, zero), jnp.where(low, zero, qa),
                              jnp.where(low, qb, zero), jnp.where(low, zero, qb)], axis=0)
        kparts, vparts, masks = [kx_ref[0, :, hs]], [vx_ref[0, :, hs]], {}
        if local:
            kparts = [kp_ref[0, :, hs], kc_ref[0, :, hs], kn_ref[0, :, hs]] + kparts
            vparts = [vp_ref[0, :, hs], vc_ref[0, :, hs], vn_ref[0, :, hs]] + vparts
            masks = {0: mask_prev, 2: mask_next}
        k_all = jnp.concatenate(kparts, axis=0)
        v_all = jnp.concatenate(vparts, axis=0)

        sink = jnp.zeros((rows, 1), F32)
        for g in range(GQA_GROUP):
            sink = jnp.where(row_head == g, sink_ref[GQA_GROUP * h + g], sink)
        s_all = lax.dot_general(qs, k_all, (((1,), (1,)), ((), ())), preferred_element_type=F32)
        chunks = []
        for c in range(k_all.shape[0] // tq):
            s = s_all[:, c * tq:(c + 1) * tq]
            chunks.append(jnp.where(masks[c], s, NEG_INF) if c in masks else s)
        m = jnp.maximum(sink, jnp.max(functools.reduce(jnp.maximum, chunks), axis=-1, keepdims=True))
        probs = [jnp.exp(s - m) for s in chunks]
        denom = jnp.exp(sink - m) + jnp.sum(functools.reduce(jnp.add, probs), axis=-1, keepdims=True)
        p_all = jnp.concatenate([p.astype(v_all.dtype) for p in probs], axis=1)
        o = jnp.dot(p_all, v_all, preferred_element_type=F32) / denom
        oa = jnp.where(low, o[0:tq], o[tq:2 * tq])
        ob = jnp.where(low, o[2 * tq:3 * tq], o[3 * tq:4 * tq])
        o_ref[0, :, 2 * h * V7X_LANES:(2 * h + 1) * V7X_LANES] = oa.astype(o_ref.dtype)
        o_ref[0, :, (2 * h + 1) * V7X_LANES:(2 * h + 2) * V7X_LANES] = ob.astype(o_ref.dtype)


def _attn_call(q, kv, kvx, sink, local):
    b, l, _ = q.shape
    lx = kvx.shape[1]
    tq = WINDOW
    nb = l // tq
    blk = lambda w: (1, tq, w)
    in_specs = [pl.BlockSpec(memory_space=pltpu.SMEM),
                pl.BlockSpec(blk(ATTN_W), lambda bi, i: (bi, i, 0))]
    args = [sink, q]
    if local:
        for half in (0, 1):
            for mp in (lambda bi, i, half=half: (bi, jnp.maximum(i - 1, 0), half),
                       lambda bi, i, half=half: (bi, i, half),
                       lambda bi, i, half=half: (bi, jnp.minimum(i + 1, nb - 1), half)):
                in_specs.append(pl.BlockSpec(blk(KV_DUP_W), mp))
                args.append(kv)
    for half in (0, 1):
        in_specs.append(pl.BlockSpec((1, lx, KV_DUP_W), lambda bi, i, half=half: (bi, 0, half)))
        args.append(kvx)
    return pl.pallas_call(
        functools.partial(_attn_body, local=local, tq=tq),
        grid=(b, nb),
        in_specs=in_specs,
        out_specs=pl.BlockSpec(blk(ATTN_W), lambda bi, i: (bi, i, 0)),
        out_shape=jax.ShapeDtypeStruct((b, l, ATTN_W), BF16),
        compiler_params=_params("arbitrary", "arbitrary"),
        name="attention",
    )(*args)


def _filter_body(z_ref, w0_ref, b0_ref, w1_ref, b1_ref, fr_ref, w2_ref, dec_ref, o_ref):
    fr = fr_ref[...]
    dot = functools.partial(jnp.dot, precision=HIGHEST, preferred_element_type=F32)
    h = jnp.sin(fr * (dot(z_ref[...], w0_ref[...]) + b0_ref[...]))
    for i in range(FILTER_INNER):
        h = jnp.sin(fr * (dot(h, w1_ref[i]) + b1_ref[i]))
    dec = dec_ref[...]
    for s in range(2 * HYENA_ORDER):
        sl = slice(s * HYENA_W, (s + 1) * HYENA_W)
        o_ref[:, sl] = dot(h, w2_ref[:, sl]) * dec


def _filter_features(l):
    t = jnp.linspace(0.0, 1.0, l, dtype=F32)[:, None]
    w = 2.0 * math.pi * jnp.arange(l, dtype=F32)[:, None] / l
    bands = jnp.linspace(1e-4, FILTER_BANDS - 1, FILTER_BANDS, dtype=F32)[None, :]
    z = jnp.concatenate([t, jnp.cos(bands * w), -jnp.sin(bands * w)], axis=-1)
    deltas = jnp.linspace(math.log(DECAY_TARGET) / SLOW_DECAY_PCT, math.log(DECAY_TARGET) / FAST_DECAY_PCT,
                          HYENA_W, dtype=F32)
    decay = jnp.exp(-t * jnp.abs(deltas))
    return jnp.pad(z, ((0, 0), (0, V7X_LANES - FILTER_EMB))), decay


def _filter_call(l, w0, b0, w1, b1, freq, w2):
    zfeat, decay = _filter_features(l)
    w0p = jnp.pad(w0, ((0, V7X_LANES - FILTER_EMB), (0, 0)))
    tl = min(512, l)
    nf = 2 * HYENA_ORDER * HYENA_W
    full = lambda shape: pl.BlockSpec(shape, lambda i: (0,) * len(shape))
    return pl.pallas_call(
        _filter_body,
        grid=(l // tl,),
        in_specs=[
            pl.BlockSpec((tl, V7X_LANES), lambda i: (i, 0)),
            full((V7X_LANES, FILTER_HIDDEN)),
            full((1, FILTER_HIDDEN)),
            full((FILTER_INNER, FILTER_HIDDEN, FILTER_HIDDEN)),
            full((FILTER_INNER, 1, FILTER_HIDDEN)),
            full((1, FILTER_HIDDEN)),
            full((FILTER_HIDDEN, nf)),
            pl.BlockSpec((tl, HYENA_W), lambda i: (i, 0)),
        ],
        out_specs=pl.BlockSpec((tl, nf), lambda i: (i, 0)),
        out_shape=jax.ShapeDtypeStruct((l, nf), F32),
        compiler_params=_params("arbitrary"),
        name="hyena_filter",
    )(zfeat, w0p, b0.reshape(1, -1), w1, b1.reshape(FILTER_INNER, 1, -1), freq.reshape(1, -1), w2, decay)


def _hyena_blocks(l):
    return max(1, min(4, l // V7X_LANES))


def _dft_matrices(blk):
    n = 2 * blk
    r = jnp.arange(blk, dtype=jnp.int32)
    ang = ((r[:, None] * r[None, :]) % n).astype(F32) * (2.0 * math.pi / n)
    return jnp.cos(ang).astype(BF16), jnp.sin(ang).astype(BF16)


def _alternating(l):
    row = lax.broadcasted_iota(jnp.int32, (l, 1), 0)
    return row, jnp.where(row % 2 == 0, 1.0, -1.0).astype(F32)


def _spectrum_body(hf_ref, hb_ref, fc_ref, fs_ref, ka_ref, kb_ref, kn_ref, *, nblk):
    l = hf_ref.shape[0]
    b = l // nblk
    n = 2 * b
    row = lax.broadcasted_iota(jnp.int32, (l, 1), 0)
    _, alt = _alternating(b)
    hf = hf_ref[...]
    hbs = jnp.where(row == 0, 0.0, pltpu.roll(hb_ref[...], 1, 0))
    fc, fs = fc_ref[...], fs_ref[...]

    def transforms(h):
        out = []
        for k in range(nblk):
            hk = h[k * b:(k + 1) * b]
            hk16 = hk.astype(BF16)
            out.append(dict(
                c=jnp.dot(fc, hk16, preferred_element_type=F32),
                s=jnp.dot(fs, hk16, preferred_element_type=F32),
                first16=hk16[0:1].astype(F32),
                first=hk[0:1],
                alt=jnp.sum(hk * alt, axis=0, keepdims=True)))
        return out

    tf, tb = transforms(hf), transforms(hbs)
    brow = lax.broadcasted_iota(jnp.int32, (b, 1), 0)
    w_re = jnp.where(brow == 0, 1.0 / n, 2.0 / n)
    for d in range(-(nblk - 1), nblk):
        idx = d + nblk - 1
        if d == 0:
            kre = tf[0]["c"] + tb[0]["c"]
            kim = tb[0]["s"] - tf[0]["s"]
            kn = tf[0]["alt"] + tb[0]["alt"]
        else:
            t, e, sg = (tf, d, -1.0) if d > 0 else (tb, -d, 1.0)
            kre = t[e]["c"] + alt * (t[e - 1]["c"] - t[e - 1]["first16"])
            kim = sg * (t[e]["s"] + alt * t[e - 1]["s"])
            kn = t[e]["alt"] + t[e - 1]["alt"] - t[e - 1]["first"]
        ka_ref[0, idx] = kre * w_re
        kb_ref[0, idx] = kim * (2.0 / n)
        kn_ref[0, idx] = kn * (1.0 / n)


def _spectrum_call(filt, fc, fs, nblk, tc):
    l = filt.shape[0]
    b = l // nblk
    nct = HYENA_W // tc
    nlag = 2 * nblk - 1
    return pl.pallas_call(
        functools.partial(_spectrum_body, nblk=nblk),
        grid=(HYENA_ORDER, nct),
        in_specs=[
            pl.BlockSpec((l, tc), lambda o, c: (0, 2 * nct * o + c)),
            pl.BlockSpec((l, tc), lambda o, c: (0, 2 * nct * o + nct + c)),
            _const_spec((b, b)),
            _const_spec((b, b)),
        ],
        out_specs=[
            pl.BlockSpec((1, nlag, b, tc), lambda o, c: (o, 0, 0, c)),
            pl.BlockSpec((1, nlag, b, tc), lambda o, c: (o, 0, 0, c)),
            pl.BlockSpec((1, nlag, 1, tc), lambda o, c: (o, 0, 0, c)),
        ],
        out_shape=[
            jax.ShapeDtypeStruct((HYENA_ORDER, nlag, b, HYENA_W), F32),
            jax.ShapeDtypeStruct((HYENA_ORDER, nlag, b, HYENA_W), F32),
            jax.ShapeDtypeStruct((HYENA_ORDER, nlag, 1, HYENA_W), F32),
        ],
        compiler_params=_params("arbitrary", "arbitrary"),
        name="hyena_spectrum",
    )(filt, filt, fc, fs)


def _conv3(x, w_ref, b_ref, row):
    l = x.shape[0]
    xm = jnp.where(row == 0, 0.0, pltpu.roll(x, 1, 0))
    xp = jnp.where(row == l - 1, 0.0, pltpu.roll(x, l - 1, 0))
    return xm * w_ref[0:1, :] + x * w_ref[1:2, :] + xp * w_ref[2:3, :] + b_ref[...]


def _fftconv_body(*refs, conv_u, nblk):
    it = iter(refs)
    u_ref = next(it)
    if conv_u:
        uw_ref, ub_ref = next(it), next(it)
    g_ref, gw_ref, gb_ref = next(it), next(it), next(it)
    ka_ref, kb_ref, kn_ref, d_ref, fc_ref, fs_ref, o_ref = (next(it) for _ in range(7))

    l = u_ref.shape[1]
    b = l // nblk
    row = lax.broadcasted_iota(jnp.int32, (l, 1), 0)
    _, alt = _alternating(b)
    u = u_ref[0].astype(F32)
    if conv_u:
        u = _conv3(u, uw_ref, ub_ref, row)
    gate = _conv3(g_ref[0], gw_ref, gb_ref, row)
    fc, fs = fc_ref[...], fs_ref[...]

    ps, qs, ns = [], [], []
    for j in range(nblk):
        uj = u[j * b:(j + 1) * b]
        uj16 = uj.astype(BF16)
        ps.append(jnp.dot(fc, uj16, preferred_element_type=F32))
        qs.append(jnp.dot(fs, uj16, preferred_element_type=F32))
        ns.append(jnp.sum(uj * alt, axis=0, keepdims=True))
    for i in range(nblk):
        r = t = nyq = None
        for j in range(nblk):
            lag = i - j + nblk - 1
            ka, kb = ka_ref[0, lag], kb_ref[0, lag]
            dr = ps[j] * ka + qs[j] * kb
            dt = qs[j] * ka - ps[j] * kb
            dn = ns[j] * kn_ref[0, lag]
            r, t, nyq = (dr, dt, dn) if j == 0 else (r + dr, t + dt, nyq + dn)
        y = (jnp.dot(fc, r.astype(BF16), preferred_element_type=F32)
             + jnp.dot(fs, t.astype(BF16), preferred_element_type=F32))
        rows = slice(i * b, (i + 1) * b)
        y = y + alt * nyq + u[rows] * d_ref[0]
        o_ref[0, rows, :] = (gate[rows] * y).astype(o_ref.dtype)


def _fftconv_call(u, u_col0, z, gate_col0, conv_w, conv_b, spectra, d_skip, order, fc, fs, nblk, tc, out_dtype):
    b, l, _ = z.shape
    conv_u = u is z
    nct = HYENA_W // tc
    ka, kb, kn = spectra
    blk = l // nblk
    nlag = 2 * nblk - 1
    col = lambda c0: (lambda c, bi: (bi, 0, c0 // tc + c))
    wcol = lambda c0: (lambda c, bi: (0, (c0 - HY_OFF) // tc + c))
    in_specs = [pl.BlockSpec((1, l, tc), col(u_col0))]
    args = [u]
    if conv_u:
        in_specs += [pl.BlockSpec((3, tc), wcol(u_col0)), pl.BlockSpec((1, tc), wcol(u_col0))]
        args += [conv_w, conv_b]
    in_specs += [pl.BlockSpec((1, l, tc), col(gate_col0)),
                 pl.BlockSpec((3, tc), wcol(gate_col0)), pl.BlockSpec((1, tc), wcol(gate_col0))]
    args += [z, conv_w, conv_b]
    spec = lambda rows: pl.BlockSpec((1, nlag, rows, tc), lambda c, bi: (order, 0, 0, c),
                                     pipeline_mode=pl.Buffered(1))
    in_specs += [spec(blk), spec(blk), spec(1),
                 pl.BlockSpec((1, 1, tc), lambda c, bi: (order, 0, c), pipeline_mode=pl.Buffered(1)),
                 _const_spec((blk, blk)), _const_spec((blk, blk))]
    args += [ka, kb, kn, d_skip, fc, fs]
    return pl.pallas_call(
        functools.partial(_fftconv_body, conv_u=conv_u, nblk=nblk),
        grid=(nct, b),
        in_specs=in_specs,
        out_specs=pl.BlockSpec((1, l, tc), lambda c, bi: (bi, 0, c)),
        out_shape=jax.ShapeDtypeStruct((b, l, HYENA_W), out_dtype),
        compiler_params=_params("arbitrary", "arbitrary"),
        name="hyena_conv",
    )(*args)


def _pool_body(x_ref, w_ref, s_ref, o_ref):
    l = x_ref.shape[1]
    row = lax.broadcasted_iota(jnp.int32, (l, 1), 0)
    for g, win in enumerate(POOL_WINDOWS):
        half = win // 2
        sl = slice(g * POOL_GROUP, (g + 1) * POOL_GROUP)
        x = x_ref[0, :, sl]
        acc = jnp.zeros_like(x)
        for k in range(-half, half):
            shifted = x if k == 0 else pltpu.roll(x, (-k) % l, 0)
            acc = acc + jnp.where((row + k >= 0) & (row + k < l), shifted, 0.0)
        cnt = (jnp.minimum(row + half, l) - jnp.maximum(row - half, 0)).astype(F32)
        d = acc / cnt - x
        y = jnp.dot(d.astype(BF16), w_ref[g], preferred_element_type=F32)
        o_ref[0, :, sl] = (y * s_ref[:, sl]).astype(o_ref.dtype)


def _pool_call(z, w_grp, scale):
    b, l, _ = z.shape
    ng = len(POOL_WINDOWS)
    return pl.pallas_call(
        _pool_body,
        grid=(b,),
        in_specs=[
            pl.BlockSpec((1, l, POOL_W), lambda bi: (bi, 0, POOL_OFF // POOL_W)),
            pl.BlockSpec((ng, POOL_GROUP, POOL_GROUP), lambda bi: (0, 0, 0)),
            pl.BlockSpec((1, POOL_W), lambda bi: (0, 0)),
        ],
        out_specs=pl.BlockSpec((1, l, POOL_W), lambda bi: (bi, 0, 0)),
        out_shape=jax.ShapeDtypeStruct((b, l, POOL_W), BF16),
        compiler_params=_params("arbitrary"),
        name="pool",
    )(z, w_grp, scale.reshape(1, POOL_W))


def _merge_body(ya_ref, yh_ref, yp_ref, gt_ref, x_ref, ga_ref, g2_ref, sc_ref, sh_ref,
                wa_ref, wh_ref, wp_ref, wo_ref, xn_ref, h2_ref):
    d = x_ref.shape[1]
    cj = 512
    ya, yh, yp = ya_ref[...], yh_ref[...], yp_ref[...]
    acc = jnp.zeros(x_ref.shape, F32)
    for j in range(d // cj):
        sl = slice(j * cj, (j + 1) * cj)
        gate = lambda br: gt_ref[:, br * d + j * cj:br * d + (j + 1) * cj].astype(F32)
        m = (gate(0) * jnp.dot(ya, wa_ref[:, sl], preferred_element_type=F32)
             + gate(1) * jnp.dot(yh, wh_ref[:, sl], preferred_element_type=F32)
             + gate(2) * jnp.dot(yp, wp_ref[:, sl], preferred_element_type=F32))
        acc = acc + jnp.dot(m.astype(BF16), wo_ref[sl, :], preferred_element_type=F32)
    xn = x_ref[...] + ga_ref[0] * acc
    xn_ref[...] = xn
    h2_ref[...] = _norm_mod(xn, g2_ref[...], sc_ref[0], sh_ref[0]).astype(h2_ref.dtype)


def _merge_call(ya, yh, yp, gates, x2, ga1, g2, sc2, sh2, wa, wh, wp, wo, layer, rows_per_batch):
    m, d = x2.shape
    tm = min(512, rows_per_batch)
    rpt = rows_per_batch // tm
    rows = lambda w: pl.BlockSpec((tm, w), lambda i: (i, 0))
    weight = lambda w: pl.BlockSpec((None,) + w.shape[1:], lambda i: (layer, 0, 0), pipeline_mode=pl.Buffered(1))
    return pl.pallas_call(
        _merge_body,
        grid=(m // tm,),
        in_specs=[rows(ATTN_W), rows(HYENA_W), rows(POOL_W), rows(N_BRANCH * d), rows(d),
                  _mod_spec(ga1, rpt), pl.BlockSpec((1, d), lambda i: (0, 0)),
                  _mod_spec(sc2, rpt), _mod_spec(sh2, rpt),
                  weight(wa), weight(wh), weight(wp), weight(wo)],
        out_specs=[rows(d), rows(d)],
        out_shape=[jax.ShapeDtypeStruct((m, d), F32), jax.ShapeDtypeStruct((m, d), BF16)],
        compiler_params=_params("arbitrary"),
        name="merge",
    )(ya, yh, yp, gates, x2, ga1, g2.reshape(1, d), sc2, sh2, wa, wh, wp, wo)


def _mlp_body(*refs, has_next):
    it = iter(refs)
    h_ref, w1_ref, w2_ref, x_ref, ga_ref = (next(it) for _ in range(5))
    if has_next:
        gn_ref, sc_ref, sh_ref = next(it), next(it), next(it)
    o_ref = next(it)
    hn_ref = next(it) if has_next else None

    f = pl.program_id(1)
    last = pl.num_programs(1) - 1
    tm, d = o_ref.shape
    cn = 512
    rb = min(256, tm)

    def hidden(rows):
        a = jnp.dot(h_ref[rows, :], w1_ref[...], preferred_element_type=F32)
        return jnp.square(jnp.maximum(a, 0.0)).astype(BF16)

    @pl.when(f == 0)
    def _():
        a = hidden(slice(None))
        for n0 in range(0, d, cn):
            o_ref[:, n0:n0 + cn] = jnp.dot(a, w2_ref[:, n0:n0 + cn], preferred_element_type=F32)

    @pl.when((f > 0) & (f < last))
    def _():
        a = hidden(slice(None))
        for n0 in range(0, d, cn):
            o_ref[:, n0:n0 + cn] += jnp.dot(a, w2_ref[:, n0:n0 + cn], preferred_element_type=F32)

    @pl.when(f == last)
    def _():
        for r0 in range(0, tm, rb):
            rows = slice(r0, r0 + rb)
            acc = o_ref[rows, :] + jnp.dot(hidden(rows), w2_ref[...], preferred_element_type=F32)
            xo = x_ref[rows, :] + ga_ref[0] * acc
            o_ref[rows, :] = xo
            if has_next:
                hn_ref[rows, :] = _norm_mod(xo, gn_ref[...], sc_ref[0], sh_ref[0]).astype(hn_ref.dtype)


def _mlp_call(h2, w1, w2, layer, xn, ga2, nxt, rows_per_batch):
    m, d = xn.shape
    ff = w1.shape[2]
    tm, tf = min(512, rows_per_batch if ga2.shape[0] > 1 else m), 2048
    rpt = max(rows_per_batch // tm, 1)
    assert ff // tf >= 2
    has_next = nxt is not None
    rows = pl.BlockSpec((tm, d), lambda i, f: (i, 0))
    in_specs = [rows, pl.BlockSpec((None, d, tf), lambda i, f: (layer, 0, f)),
                pl.BlockSpec((None, tf, d), lambda i, f: (layer, f, 0)),
                pl.BlockSpec((tm, d), lambda i, f: (i, 0), pipeline_mode=pl.Buffered(1)), _mod_spec(ga2, rpt)]
    args = [h2, w1, w2, xn, ga2]
    out_specs = [rows]
    out_shape = [jax.ShapeDtypeStruct((m, d), F32)]
    if has_next:
        gn, scn, shn = nxt
        in_specs += [pl.BlockSpec((1, d), lambda i, f: (0, 0)), _mod_spec(scn, rpt), _mod_spec(shn, rpt)]
        args += [gn.reshape(1, d), scn, shn]
        out_specs.append(rows)
        out_shape.append(jax.ShapeDtypeStruct((m, d), BF16))
    outs = pl.pallas_call(
        functools.partial(_mlp_body, has_next=has_next),
        grid=(m // tm, ff // tf),
        in_specs=in_specs,
        out_specs=out_specs,
        out_shape=out_shape,
        compiler_params=_params("arbitrary", "arbitrary"),
        name="mlp",
    )(*args)
    return (outs[0], outs[1]) if has_next else (outs[0], None)


def _mixers(z, q, kv, kvx, sink, local, hy, pool_w, pool_scale):
    b, l, _ = z.shape
    y_att = _attn_call(q, kv, kvx, sink, local)
    conv_w, conv_b, filt_params, d_skip, (fc, fs) = hy
    nblk = _hyena_blocks(l)
    tc = 256
    spectra = _spectrum_call(_filter_call(l, *filt_params), fc, fs, nblk, tc)
    conv = functools.partial(_fftconv_call, conv_w=conv_w, conv_b=conv_b, spectra=spectra, d_skip=d_skip,
                             fc=fc, fs=fs, nblk=nblk, tc=tc)
    z1 = conv(z, HY_OFF, z, HY_OFF + HYENA_W, order=0, out_dtype=F32)
    y_hy = conv(z1, 0, z, HY_OFF + 2 * HYENA_W, order=1, out_dtype=BF16)
    y_pool = _pool_call(z, pool_w, pool_scale)
    return (y_att.reshape(b * l, ATTN_W), y_hy.reshape(b * l, HYENA_W), y_pool.reshape(b * l, POOL_W))


def kernel(x, c, ctx, c_ctx, norm1_g, norm2_g, w_mod, b_mod, w_in, q_norm_g, k_norm_g, sink, hy_conv_w, hy_conv_b, filt_w0, filt_b0, filt_w1, filt_b1, filt_freq, filt_w2, hy_bias, pool_w, pool_scale, w_att_o, w_hy_o, w_pool_o, w_out, mlp_w1, mlp_w2):
    b, l, d = x.shape
    lc = ctx.shape[1]
    depth = w_mod.shape[0]

    cc = jnp.concatenate([c, c_ctx[None, :], jnp.zeros((MOD_ROWS - b - 1, d), F32)], axis=0)
    mods = _modulation(cc, w_mod, b_mod)

    def chunks(layer, lo, hi):
        return [mods[layer, lo:hi, i * d:(i + 1) * d].reshape(hi - lo, 1, d) for i in range(6)]

    as_bf16 = lambda w: w.astype(BF16)
    w_in_b, w_att_b, w_hy_b, w_pool_b, w_out_b = map(as_bf16, (w_in, w_att_o, w_hy_o, w_pool_o, w_out))
    w1_b, w2_b, pool_w_b = map(as_bf16, (mlp_w1, mlp_w2, pool_w))

    rope_tabs = _rope_tables(l)
    dft_x = _dft_matrices(l // _hyena_blocks(l))
    dft_c = _dft_matrices(lc // _hyena_blocks(lc))

    x2 = x.reshape(b * l, d)
    c2 = ctx.reshape(b * lc, d)
    sh1, sc1 = chunks(0, 0, b)[:2]
    csh1, csc1 = chunks(0, b, b + 1)[:2]
    hx = _norm_call(x2, norm1_g[0], sc1, sh1, l)
    hc = _norm_call(c2, norm1_g[0], csc1, csh1, lc)

    for layer in range(depth):
        last = layer == depth - 1
        _, _, ga1, sh2, sc2, ga2 = chunks(layer, 0, b)
        _, _, cga1, csh2, csc2, cga2 = chunks(layer, b, b + 1)
        filt_params = (filt_w0[layer], filt_b0[layer], filt_w1[layer], filt_b1[layer], filt_freq[layer],
                       filt_w2[layer])
        conv_b = hy_conv_b[layer].reshape(1, -1)
        d_skip = hy_bias[layer].reshape(HYENA_ORDER, 1, HYENA_W)
        merge_w = (w_att_b, w_hy_b, w_pool_b, w_out_b, layer)

        gq, gk = q_norm_g[layer], k_norm_g[layer]
        if last:
            kvc = _kv_call(_proj_call(hc, w_in_b, layer, K_OFF, 2 * KV_W, F32), gk)
        else:
            zc = _proj_call(hc, w_in_b, layer, 0, GATE_OFF, F32)
            gc, qc, kvc = _gates_qkv_call(hc, w_in_b, layer, zc, gq, gk, None, lc)
            zc, qc = zc.reshape(b, lc, GATE_OFF), qc.reshape(b, lc, ATTN_W)
        kvc = kvc.reshape(b, lc, 2 * KV_DUP_W)

        zx = _proj_call(hx, w_in_b, layer, 0, GATE_OFF, F32)
        gx, qx, kvx = _gates_qkv_call(hx, w_in_b, layer, zx, gq, gk, rope_tabs, l)
        zx, qx, kvx = zx.reshape(b, l, GATE_OFF), qx.reshape(b, l, ATTN_W), kvx.reshape(b, l, 2 * KV_DUP_W)
        hy = (hy_conv_w[layer], conv_b, filt_params, d_skip, dft_x)
        ya, yh, yp = _mixers(zx, qx, kvx, kvc, sink[layer], True, hy, pool_w_b[layer], pool_scale[layer])
        xn, h2 = _merge_call(ya, yh, yp, gx, x2, ga1, norm2_g[layer], sc2, sh2, *merge_w, l)
        nxt = None if last else (norm1_g[layer + 1], *reversed(chunks(layer + 1, 0, b)[:2]))
        x2, hx = _mlp_call(h2, w1_b, w2_b, layer, xn, ga2, nxt, l)

        if not last:
            hyc = (hy_conv_w[layer], conv_b, filt_params, d_skip, dft_c)
            ya, yh, yp = _mixers(zc, qc, None, kvc, sink[layer], False, hyc, pool_w_b[layer], pool_scale[layer])
            cn, h2c = _merge_call(ya, yh, yp, gc, c2, cga1, norm2_g[layer], csc2, csh2, *merge_w, lc)
            nxt = (norm1_g[layer + 1], *reversed(chunks(layer + 1, b, b + 1)[:2]))
            c2, hc = _mlp_call(h2c, w1_b, w2_b, layer, cn, cga2, nxt, lc)

    return x2.reshape(b, l, d)
```

```python
import functools
import math

import jax
import jax.numpy as jnp
from jax import lax
from jax.experimental import pallas as pl
from jax.experimental.pallas import tpu as pltpu

D_MODEL = 2048
DEPTH = 2
GRID_W = 64
EPS = 1e-6
NEG_INF = -1e30

N_HEADS = 16
N_KV_HEADS = 4
GQA_GROUP = N_HEADS // N_KV_HEADS
HEAD_DIM = 64
ATTN_W = N_HEADS * HEAD_DIM
KV_W = N_KV_HEADS * HEAD_DIM
WINDOW = 128
ROPE_FREQS = HEAD_DIM // 4
ROPE_BASE = 10000.0

HYENA_W = D_MODEL // 4
HYENA_ORDER = 2
FILTER_BANDS = 16
FILTER_EMB = 1 + 2 * FILTER_BANDS
FILTER_HIDDEN = 64
FILTER_INNER = 2
DECAY_TARGET = 1e-2
FAST_DECAY_PCT = 0.3
SLOW_DECAY_PCT = 1.5

POOL_W = D_MODEL // 4
POOL_WINDOWS = (2, 4, 8, 16)
POOL_GROUP = POOL_W // len(POOL_WINDOWS)

N_BRANCH = 3
D_FF = 4 * D_MODEL

Q_OFF = 0
K_OFF = Q_OFF + ATTN_W
V_OFF = K_OFF + KV_W
HY_OFF = V_OFF + KV_W
POOL_OFF = HY_OFF + 3 * HYENA_W
GATE_OFF = POOL_OFF + POOL_W
IN_W = GATE_OFF + N_BRANCH * D_MODEL

V7X_LANES = 128
V7X_VMEM_LIMIT = 60 * 1024 * 1024
KV_DUP_W = N_KV_HEADS * V7X_LANES
MOD_ROWS = 24

F32 = jnp.float32
BF16 = jnp.bfloat16
HIGHEST = lax.Precision.HIGHEST


def _params(*semantics):
    return pltpu.CompilerParams(dimension_semantics=semantics, vmem_limit_bytes=V7X_VMEM_LIMIT)


def _const_spec(shape):
    zeros = (0,) * len(shape)
    return pl.BlockSpec(shape, lambda *_: zeros, pipeline_mode=pl.Buffered(1))


def _mod_spec(arr, rows_per_mod_tile):
    d = arr.shape[-1]
    if arr.shape[0] == 1:
        return pl.BlockSpec((1, 1, d), lambda i, *_: (0, 0, 0))
    return pl.BlockSpec((1, 1, d), lambda i, *_: (i // rows_per_mod_tile, 0, 0))


def _norm_mod(xf, g, sc, sh):
    y = xf * lax.rsqrt(jnp.mean(xf * xf, axis=-1, keepdims=True) + EPS)
    return (y * g) * (1.0 + sc) + sh


def _mod_body(c_ref, w_ref, b_ref, o_ref):
    c = c_ref[...]
    s = c * jax.nn.sigmoid(c)
    o_ref[0] = jnp.dot(s.astype(BF16), w_ref[0].astype(BF16), preferred_element_type=F32) + b_ref[0]


def _modulation(cc, w_mod, b_mod):
    depth, d, n = w_mod.shape
    tn = 1024
    return pl.pallas_call(
        _mod_body,
        grid=(depth, n // tn),
        in_specs=[
            pl.BlockSpec((MOD_ROWS, d), lambda l, j: (0, 0)),
            pl.BlockSpec((1, d, tn), lambda l, j: (l, 0, j)),
            pl.BlockSpec((1, 1, tn), lambda l, j: (l, 0, j)),
        ],
        out_specs=pl.BlockSpec((1, MOD_ROWS, tn), lambda l, j: (l, 0, j)),
        out_shape=jax.ShapeDtypeStruct((depth, MOD_ROWS, n), F32),
        compiler_params=_params("arbitrary", "arbitrary"),
        name="modulation",
    )(cc, w_mod, b_mod.reshape(depth, 1, n))


def _norm_body(x_ref, g_ref, sc_ref, sh_ref, o_ref):
    o_ref[...] = _norm_mod(x_ref[...], g_ref[...], sc_ref[0], sh_ref[0]).astype(o_ref.dtype)


def _norm_call(x2, g, sc, sh, rows_per_batch):
    m, d = x2.shape
    tm = min(512, rows_per_batch)
    return pl.pallas_call(
        _norm_body,
        grid=(m // tm,),
        in_specs=[
            pl.BlockSpec((tm, d), lambda i: (i, 0)),
            pl.BlockSpec((1, d), lambda i: (0, 0)),
            _mod_spec(sc, rows_per_batch // tm),
            _mod_spec(sh, rows_per_batch // tm),
        ],
        out_specs=pl.BlockSpec((tm, d), lambda i: (i, 0)),
        out_shape=jax.ShapeDtypeStruct((m, d), BF16),
        compiler_params=_params("arbitrary"),
        name="norm_mod",
    )(x2, g.reshape(1, d), sc, sh)


def _proj_body(a_ref, w_ref, o_ref):
    o_ref[...] = jnp.dot(a_ref[...], w_ref[...], preferred_element_type=F32).astype(o_ref.dtype)


def _proj_call(a, w, layer, col0, n, out_dtype):
    m, k = a.shape
    tm = min(2048, m)
    tn = 512
    c0 = col0 // tn
    return pl.pallas_call(
        _proj_body,
        grid=(m // tm, n // tn),
        in_specs=[
            pl.BlockSpec((tm, k), lambda i, j: (i, 0)),
            pl.BlockSpec((None, k, tn), lambda i, j: (layer, 0, c0 + j)),
        ],
        out_specs=pl.BlockSpec((tm, tn), lambda i, j: (i, j)),
        out_shape=jax.ShapeDtypeStruct((m, n), out_dtype),
        compiler_params=_params("arbitrary", "arbitrary"),
        name="in_proj",
    )(a, w)


Q_SLABS = ATTN_W // V7X_LANES
QK_SLABS = Q_SLABS + KV_W // V7X_LANES
QKV_SLABS = QK_SLABS + KV_W // V7X_LANES


def _gates_qkv_body(a_ref, w_ref, z_ref, gain_ref, cos_ref, sup_ref, sdn_ref, g_ref, q_ref, kv_ref):
    j = pl.program_id(1)
    zg = jnp.dot(a_ref[...], w_ref[...], preferred_element_type=F32)
    g_ref[...] = (0.5 * jnp.tanh(0.5 * zg) + 0.5).astype(g_ref.dtype)

    x = z_ref[...]
    low = lax.broadcasted_iota(jnp.int32, (1, V7X_LANES), 1) < HEAD_DIM
    x2 = x * x
    ss = jnp.where(low, jnp.sum(jnp.where(low, x2, 0.0), axis=-1, keepdims=True),
                   jnp.sum(jnp.where(low, 0.0, x2), axis=-1, keepdims=True))
    inv = jnp.where(j < QK_SLABS, lax.rsqrt(ss * (1.0 / HEAD_DIM) + EPS), 1.0)
    y = _rope((x * inv) * gain_ref[0], cos_ref[...], sup_ref[...], sdn_ref[...])
    da, db = _dup_pair(y, low)

    @pl.when(j < Q_SLABS)
    def _():
        q_ref[...] = y.astype(q_ref.dtype)

    @pl.when(j >= Q_SLABS)
    def _():
        kv_ref[:, 0:V7X_LANES] = da.astype(kv_ref.dtype)
        kv_ref[:, V7X_LANES:2 * V7X_LANES] = db.astype(kv_ref.dtype)


def _gates_qkv_call(a, w, layer, z2, gq, gk, rope_tabs, seq_len):
    m, k = a.shape
    d = D_MODEL
    tm = min(2048, m)
    tn = 512
    assert N_BRANCH * d // tn == QKV_SLABS and tm % seq_len == 0
    c0 = GATE_OFF // tn
    ones = jnp.ones((tm, V7X_LANES), F32)
    zeros = jnp.zeros((tm, V7X_LANES), F32)
    if rope_tabs is None:
        cos, sup, sdn = ones[None], zeros[None], zeros[None]
        tab_map = lambda i, j: (0, 0, 0)
    else:
        rep = lambda t: jnp.tile(t, (tm // seq_len, 1))
        cos, sup, sdn = (jnp.stack([rep(t), ident]) for t, ident in zip(rope_tabs, (ones, zeros, zeros)))
        tab_map = lambda i, j: ((j >= QK_SLABS).astype(jnp.int32), 0, 0)
    gains = jnp.stack([jnp.tile(gq, 2) * HEAD_DIM ** -0.5, jnp.tile(gk, 2), jnp.ones((V7X_LANES,), F32)])
    gain_map = lambda i, j: ((j >= Q_SLABS).astype(jnp.int32) + (j >= QK_SLABS).astype(jnp.int32), 0, 0)
    tab_spec = pl.BlockSpec((None, tm, V7X_LANES), tab_map)
    return pl.pallas_call(
        _gates_qkv_body,
        grid=(m // tm, QKV_SLABS),
        in_specs=[
            pl.BlockSpec((tm, k), lambda i, j: (i, 0)),
            pl.BlockSpec((None, k, tn), lambda i, j: (layer, 0, c0 + j)),
            pl.BlockSpec((tm, V7X_LANES), lambda i, j: (i, j)),
            pl.BlockSpec((None, 1, V7X_LANES), gain_map),
            tab_spec, tab_spec, tab_spec,
        ],
        out_specs=[
            pl.BlockSpec((tm, tn), lambda i, j: (i, j)),
            pl.BlockSpec((tm, V7X_LANES), lambda i, j: (i, jnp.minimum(j, Q_SLABS - 1))),
            pl.BlockSpec((tm, 2 * V7X_LANES), lambda i, j: (i, jnp.maximum(j - Q_SLABS, 0))),
        ],
        out_shape=[
            jax.ShapeDtypeStruct((m, N_BRANCH * d), BF16),
            jax.ShapeDtypeStruct((m, ATTN_W), BF16),
            jax.ShapeDtypeStruct((m, 2 * KV_DUP_W), BF16),
        ],
        compiler_params=_params("arbitrary", "arbitrary"),
        name="gates_qkv",
    )(a, w, z2, gains.reshape(3, 1, V7X_LANES), cos, sup, sdn)


def _pair_block_diag():
    r = lax.broadcasted_iota(jnp.int32, (V7X_LANES, V7X_LANES), 0) // HEAD_DIM
    c = lax.broadcasted_iota(jnp.int32, (V7X_LANES, V7X_LANES), 1) // HEAD_DIM
    return (r == c).astype(F32)


def _head_norm(x, g, bd):
    ss = jnp.dot(x * x, bd, precision=HIGHEST, preferred_element_type=F32)
    return (x * lax.rsqrt(ss * (1.0 / HEAD_DIM) + EPS)) * g


def _rope(x, cos, sin_up, sin_dn):
    up = pltpu.roll(x, V7X_LANES - ROPE_FREQS, 1)
    dn = pltpu.roll(x, ROPE_FREQS, 1)
    return x * cos + up * sin_up + dn * sin_dn


def _dup_pair(x, low):
    r = pltpu.roll(x, HEAD_DIM, 1)
    return jnp.where(low, x, r), jnp.where(low, r, x)


def _kv_body(z_ref, gk_ref, kv_ref):
    bd = _pair_block_diag()
    low = lax.broadcasted_iota(jnp.int32, (1, V7X_LANES), 1) < HEAD_DIM
    for s in range(2 * KV_W // V7X_LANES):
        x = z_ref[:, s * V7X_LANES:(s + 1) * V7X_LANES]
        if s < KV_W // V7X_LANES:
            x = _head_norm(x, gk_ref[...], bd)
        a, b = _dup_pair(x, low)
        base = 2 * s * V7X_LANES
        kv_ref[:, base:base + V7X_LANES] = a.astype(kv_ref.dtype)
        kv_ref[:, base + V7X_LANES:base + 2 * V7X_LANES] = b.astype(kv_ref.dtype)


def _kv_call(z2, gk):
    m, nz = z2.shape
    tm = min(512, m)
    return pl.pallas_call(
        _kv_body,
        grid=(m // tm,),
        in_specs=[pl.BlockSpec((tm, nz), lambda i: (i, 0)),
                  pl.BlockSpec((1, V7X_LANES), lambda i: (0, 0))],
        out_specs=pl.BlockSpec((tm, 2 * KV_DUP_W), lambda i: (i, 0)),
        out_shape=jax.ShapeDtypeStruct((m, 2 * KV_DUP_W), BF16),
        compiler_params=_params("arbitrary"),
        name="kv_prep",
    )(z2, jnp.tile(gk, 2).reshape(1, V7X_LANES))


def _rope_tables(l):
    rows = l // GRID_W
    row = jnp.repeat(jnp.arange(rows, dtype=F32), GRID_W)
    col = jnp.tile(jnp.arange(GRID_W, dtype=F32), rows)
    inv = ROPE_BASE ** (-jnp.arange(ROPE_FREQS, dtype=F32) / ROPE_FREQS)
    ang = jnp.stack([row[:, None] * inv, col[:, None] * inv], axis=1)
    cos, sin = jnp.cos(ang), jnp.sin(ang)
    zero = jnp.zeros_like(sin)
    cos_h = jnp.stack([cos, cos], axis=2).reshape(l, HEAD_DIM)
    sup_h = jnp.stack([-sin, zero], axis=2).reshape(l, HEAD_DIM)
    sdn_h = jnp.stack([zero, sin], axis=2).reshape(l, HEAD_DIM)
    return tuple(jnp.tile(t, (1, 2)) for t in (cos_h, sup_h, sdn_h))


def _attn_body(*refs, local, tq):
    it = iter(refs)
    sink_ref = next(it)
    q_ref = next(it)
    if local:
        kp_ref, kc_ref, kn_ref, vp_ref, vc_ref, vn_ref = (next(it) for _ in range(6))
    kx_ref, vx_ref = next(it), next(it)
    o_ref = next(it)

    i = pl.program_id(1)
    nb = pl.num_programs(1)
    low = lax.broadcasted_iota(jnp.int32, (1, V7X_LANES), 1) < HEAD_DIM
    rows = GQA_GROUP * tq
    if local:
        qi = lax.broadcasted_iota(jnp.int32, (rows, tq), 0) % tq
        kj = lax.broadcasted_iota(jnp.int32, (rows, tq), 1)
        mask_prev = (kj >= qi) & (i > 0)
        mask_next = (kj <= qi) & (i < nb - 1)
    row_head = lax.broadcasted_iota(jnp.int32, (rows, 1), 0) // tq
    zero = jnp.zeros((), q_ref.dtype)

    for h in range(N_KV_HEADS):
        hs = slice(h * V7X_LANES, (h + 1) * V7X_LANES)
        qa = q_ref[0, :, 2 * h * V7X_LANES:(2 * h + 1) * V7X_LANES]
        qb = q_ref[0, :, (2 * h + 1) * V7X_LANES:(2 * h + 2) * V7X_LANES]
        qs = jnp.concatenate([jnp.where(low, qa, zero), jnp.where(low, zero, qa),
                              jnp.where(low, qb, zero), jnp.where(low, zero, qb)], axis=0)
        kparts, vparts, masks = [kx_ref[0, :, hs]], [vx_ref[0, :, hs]], {}
        if local:
            kparts = [kp_ref[0, :, hs], kc_ref[0, :, hs], kn_ref[0, :, hs]] + kparts
            vparts = [vp_ref[0, :, hs], vc_ref[0, :, hs], vn_ref[0, :, hs]] + vparts
            masks = {0: mask_prev, 2: mask_next}
        k_all = jnp.concatenate(kparts, axis=0)
        v_all = jnp.concatenate(vparts, axis=0)

        sink = jnp.zeros((rows, 1), F32)
        for g in range(GQA_GROUP):
            sink = jnp.where(row_head == g, sink_ref[GQA_GROUP * h + g], sink)
        s_all = lax.dot_general(qs, k_all, (((1,), (1,)), ((), ())), preferred_element_type=F32)
        chunks = []
        for c in range(k_all.shape[0] // tq):
            s = s_all[:, c * tq:(c + 1) * tq]
            chunks.append(jnp.where(masks[c], s, NEG_INF) if c in masks else s)
        m = jnp.maximum(sink, jnp.max(functools.reduce(jnp.maximum, chunks), axis=-1, keepdims=True))
        probs = [jnp.exp(s - m) for s in chunks]
        denom = jnp.exp(sink - m) + jnp.sum(functools.reduce(jnp.add, probs), axis=-1, keepdims=True)
        p_all = jnp.concatenate([p.astype(v_all.dtype) for p in probs], axis=1)
        o = jnp.dot(p_all, v_all, preferred_element_type=F32) / denom
        oa = jnp.where(low, o[0:tq], o[tq:2 * tq])
        ob = jnp.where(low, o[2 * tq:3 * tq], o[3 * tq:4 * tq])
        o_ref[0, :, 2 * h * V7X_LANES:(2 * h + 1) * V7X_LANES] = oa.astype(o_ref.dtype)
        o_ref[0, :, (2 * h + 1) * V7X_LANES:(2 * h + 2) * V7X_LANES] = ob.astype(o_ref.dtype)


def _attn_call(q, kv, kvx, sink, local):
    b, l, _ = q.shape
    lx = kvx.shape[1]
    tq = WINDOW
    nb = l // tq
    blk = lambda w: (1, tq, w)
    in_specs = [pl.BlockSpec(memory_space=pltpu.SMEM),
                pl.BlockSpec(blk(ATTN_W), lambda bi, i: (bi, i, 0))]
    args = [sink, q]
    if local:
        for half in (0, 1):
            for mp in (lambda bi, i, half=half: (bi, jnp.maximum(i - 1, 0), half),
                       lambda bi, i, half=half: (bi, i, half),
                       lambda bi, i, half=half: (bi, jnp.minimum(i + 1, nb - 1), half)):
                in_specs.append(pl.BlockSpec(blk(KV_DUP_W), mp))
                args.append(kv)
    for half in (0, 1):
        in_specs.append(pl.BlockSpec((1, lx, KV_DUP_W), lambda bi, i, half=half: (bi, 0, half)))
        args.append(kvx)
    return pl.pallas_call(
        functools.partial(_attn_body, local=local, tq=tq),
        grid=(b, nb),
        in_specs=in_specs,
        out_specs=pl.BlockSpec(blk(ATTN_W), lambda bi, i: (bi, i, 0)),
        out_shape=jax.ShapeDtypeStruct((b, l, ATTN_W), BF16),
        compiler_params=_params("arbitrary", "arbitrary"),
        name="attention",
    )(*args)


def _filter_body(z_ref, w0_ref, b0_ref, w1_ref, b1_ref, fr_ref, w2_ref, dec_ref, o_ref):
    fr = fr_ref[...]
    dot = functools.partial(jnp.dot, precision=HIGHEST, preferred_element_type=F32)
    h = jnp.sin(fr * (dot(z_ref[...], w0_ref[...]) + b0_ref[...]))
    for i in range(FILTER_INNER):
        h = jnp.sin(fr * (dot(h, w1_ref[i]) + b1_ref[i]))
    dec = dec_ref[...]
    for s in range(2 * HYENA_ORDER):
        sl = slice(s * HYENA_W, (s + 1) * HYENA_W)
        o_ref[:, sl] = dot(h, w2_ref[:, sl]) * dec


def _filter_features(l):
    t = jnp.linspace(0.0, 1.0, l, dtype=F32)[:, None]
    w = 2.0 * math.pi * jnp.arange(l, dtype=F32)[:, None] / l
    bands = jnp.linspace(1e-4, FILTER_BANDS - 1, FILTER_BANDS, dtype=F32)[None, :]
    z = jnp.concatenate([t, jnp.cos(bands * w), -jnp.sin(bands * w)], axis=-1)
    deltas = jnp.linspace(math.log(DECAY_TARGET) / SLOW_DECAY_PCT, math.log(DECAY_TARGET) / FAST_DECAY_PCT,
                          HYENA_W, dtype=F32)
    decay = jnp.exp(-t * jnp.abs(deltas))
    return jnp.pad(z, ((0, 0), (0, V7X_LANES - FILTER_EMB))), decay


def _filter_call(l, w0, b0, w1, b1, freq, w2):
    zfeat, decay = _filter_features(l)
    w0p = jnp.pad(w0, ((0, V7X_LANES - FILTER_EMB), (0, 0)))
    tl = min(512, l)
    nf = 2 * HYENA_ORDER * HYENA_W
    full = lambda shape: pl.BlockSpec(shape, lambda i: (0,) * len(shape))
    return pl.pallas_call(
        _filter_body,
        grid=(l // tl,),
        in_specs=[
            pl.BlockSpec((tl, V7X_LANES), lambda i: (i, 0)),
            full((V7X_LANES, FILTER_HIDDEN)),
            full((1, FILTER_HIDDEN)),
            full((FILTER_INNER, FILTER_HIDDEN, FILTER_HIDDEN)),
            full((FILTER_INNER, 1, FILTER_HIDDEN)),
            full((1, FILTER_HIDDEN)),
            full((FILTER_HIDDEN, nf)),
            pl.BlockSpec((tl, HYENA_W), lambda i: (i, 0)),
        ],
        out_specs=pl.BlockSpec((tl, nf), lambda i: (i, 0)),
        out_shape=jax.ShapeDtypeStruct((l, nf), F32),
        compiler_params=_params("arbitrary"),
        name="hyena_filter",
    )(zfeat, w0p, b0.reshape(1, -1), w1, b1.reshape(FILTER_INNER, 1, -1), freq.reshape(1, -1), w2, decay)


def _hyena_blocks(l):
    return max(1, min(4, l // V7X_LANES))


def _dft_matrices(blk):
    n = 2 * blk
    r = jnp.arange(blk, dtype=jnp.int32)
    ang = ((r[:, None] * r[None, :]) % n).astype(F32) * (2.0 * math.pi / n)
    return jnp.cos(ang).astype(BF16), jnp.sin(ang).astype(BF16)


def _alternating(l):
    row = lax.broadcasted_iota(jnp.int32, (l, 1), 0)
    return row, jnp.where(row % 2 == 0, 1.0, -1.0).astype(F32)


def _spectrum_body(hf_ref, hb_ref, fc_ref, fs_ref, ka_ref, kb_ref, kn_ref, *, nblk):
    l = hf_ref.shape[0]
    b = l // nblk
    n = 2 * b
    row = lax.broadcasted_iota(jnp.int32, (l, 1), 0)
    _, alt = _alternating(b)
    hf = hf_ref[...]
    hbs = jnp.where(row == 0, 0.0, pltpu.roll(hb_ref[...], 1, 0))
    fc, fs = fc_ref[...], fs_ref[...]

    def transforms(h):
        out = []
        for k in range(nblk):
            hk = h[k * b:(k + 1) * b]
            hk16 = hk.astype(BF16)
            out.append(dict(
                c=jnp.dot(fc, hk16, preferred_element_type=F32),
                s=jnp.dot(fs, hk16, preferred_element_type=F32),
                first16=hk16[0:1].astype(F32),
                first=hk[0:1],
                alt=jnp.sum(hk * alt, axis=0, keepdims=True)))
        return out

    tf, tb = transforms(hf), transforms(hbs)
    brow = lax.broadcasted_iota(jnp.int32, (b, 1), 0)
    w_re = jnp.where(brow == 0, 1.0 / n, 2.0 / n)
    for d in range(-(nblk - 1), nblk):
        idx = d + nblk - 1
        if d == 0:
            kre = tf[0]["c"] + tb[0]["c"]
            kim = tb[0]["s"] - tf[0]["s"]
            kn = tf[0]["alt"] + tb[0]["alt"]
        else:
            t, e, sg = (tf, d, -1.0) if d > 0 else (tb, -d, 1.0)
            kre = t[e]["c"] + alt * (t[e - 1]["c"] - t[e - 1]["first16"])
            kim = sg * (t[e]["s"] + alt * t[e - 1]["s"])
            kn = t[e]["alt"] + t[e - 1]["alt"] - t[e - 1]["first"]
        ka_ref[0, idx] = (kre * w_re).astype(ka_ref.dtype)
        kb_ref[0, idx] = (kim * (2.0 / n)).astype(kb_ref.dtype)
        kn_ref[0, idx] = kn * (1.0 / n)


def _spectrum_call(filt, fc, fs, nblk, tc):
    l = filt.shape[0]
    b = l // nblk
    nct = HYENA_W // tc
    nlag = 2 * nblk - 1
    return pl.pallas_call(
        functools.partial(_spectrum_body, nblk=nblk),
        grid=(HYENA_ORDER, nct),
        in_specs=[
            pl.BlockSpec((l, tc), lambda o, c: (0, 2 * nct * o + c)),
            pl.BlockSpec((l, tc), lambda o, c: (0, 2 * nct * o + nct + c)),
            _const_spec((b, b)),
            _const_spec((b, b)),
        ],
        out_specs=[
            pl.BlockSpec((1, nlag, b, tc), lambda o, c: (o, 0, 0, c)),
            pl.BlockSpec((1, nlag, b, tc), lambda o, c: (o, 0, 0, c)),
            pl.BlockSpec((1, nlag, 1, tc), lambda o, c: (o, 0, 0, c)),
        ],
        out_shape=[
            jax.ShapeDtypeStruct((HYENA_ORDER, nlag, b, HYENA_W), BF16),
            jax.ShapeDtypeStruct((HYENA_ORDER, nlag, b, HYENA_W), BF16),
            jax.ShapeDtypeStruct((HYENA_ORDER, nlag, 1, HYENA_W), F32),
        ],
        compiler_params=_params("arbitrary", "arbitrary"),
        name="hyena_spectrum",
    )(filt, filt, fc, fs)


def _conv3(x, w_ref, b_ref, row):
    l = x.shape[0]
    xm = jnp.where(row == 0, 0.0, pltpu.roll(x, 1, 0))
    xp = jnp.where(row == l - 1, 0.0, pltpu.roll(x, l - 1, 0))
    return xm * w_ref[0:1, :] + x * w_ref[1:2, :] + xp * w_ref[2:3, :] + b_ref[...]


def _fftconv_body(*refs, conv_u, nblk):
    it = iter(refs)
    u_ref = next(it)
    if conv_u:
        uw_ref, ub_ref = next(it), next(it)
    g_ref, gw_ref, gb_ref = next(it), next(it), next(it)
    ka_ref, kb_ref, kn_ref, d_ref, fc_ref, fs_ref, o_ref = (next(it) for _ in range(7))

    l = u_ref.shape[1]
    b = l // nblk
    row = lax.broadcasted_iota(jnp.int32, (l, 1), 0)
    _, alt = _alternating(b)
    u = u_ref[0].astype(F32)
    if conv_u:
        u = _conv3(u, uw_ref, ub_ref, row)
    gate = _conv3(g_ref[0], gw_ref, gb_ref, row)
    fc, fs = fc_ref[...], fs_ref[...]

    ps, qs, ns = [], [], []
    for j in range(nblk):
        uj = u[j * b:(j + 1) * b]
        uj16 = uj.astype(BF16)
        ps.append(jnp.dot(fc, uj16, preferred_element_type=F32).astype(BF16))
        qs.append(jnp.dot(fs, uj16, preferred_element_type=F32).astype(BF16))
        ns.append(jnp.sum(uj * alt, axis=0, keepdims=True))
    for i in range(nblk):
        r = t = nyq = None
        for j in range(nblk):
            lag = i - j + nblk - 1
            ka, kb = ka_ref[0, lag], kb_ref[0, lag]
            dr = ps[j] * ka + qs[j] * kb
            dt = qs[j] * ka - ps[j] * kb
            dn = ns[j] * kn_ref[0, lag]
            r, t, nyq = (dr, dt, dn) if j == 0 else (r + dr, t + dt, nyq + dn)
        y = jnp.dot(fc, r, preferred_element_type=F32) + jnp.dot(fs, t, preferred_element_type=F32)
        rows = slice(i * b, (i + 1) * b)
        y = y + alt * nyq + u[rows] * d_ref[0]
        o_ref[0, rows, :] = (gate[rows] * y).astype(o_ref.dtype)


def _fftconv_call(u, u_col0, z, gate_col0, conv_w, conv_b, spectra, d_skip, order, fc, fs, nblk, tc, out_dtype):
    b, l, _ = z.shape
    conv_u = u is z
    nct = HYENA_W // tc
    ka, kb, kn = spectra
    blk = l // nblk
    nlag = 2 * nblk - 1
    col = lambda c0: (lambda c, bi: (bi, 0, c0 // tc + c))
    wcol = lambda c0: (lambda c, bi: (0, (c0 - HY_OFF) // tc + c))
    in_specs = [pl.BlockSpec((1, l, tc), col(u_col0))]
    args = [u]
    if conv_u:
        in_specs += [pl.BlockSpec((3, tc), wcol(u_col0)), pl.BlockSpec((1, tc), wcol(u_col0))]
        args += [conv_w, conv_b]
    in_specs += [pl.BlockSpec((1, l, tc), col(gate_col0)),
                 pl.BlockSpec((3, tc), wcol(gate_col0)), pl.BlockSpec((1, tc), wcol(gate_col0))]
    args += [z, conv_w, conv_b]
    spec = lambda rows: pl.BlockSpec((1, nlag, rows, tc), lambda c, bi: (order, 0, 0, c),
                                     pipeline_mode=pl.Buffered(1))
    in_specs += [spec(blk), spec(blk), spec(1),
                 pl.BlockSpec((1, 1, tc), lambda c, bi: (order, 0, c), pipeline_mode=pl.Buffered(1)),
                 _const_spec((blk, blk)), _const_spec((blk, blk))]
    args += [ka, kb, kn, d_skip, fc, fs]
    return pl.pallas_call(
        functools.partial(_fftconv_body, conv_u=conv_u, nblk=nblk),
        grid=(nct, b),
        in_specs=in_specs,
        out_specs=pl.BlockSpec((1, l, tc), lambda c, bi: (bi, 0, c)),
        out_shape=jax.ShapeDtypeStruct((b, l, HYENA_W), out_dtype),
        compiler_params=_params("arbitrary", "arbitrary"),
        name="hyena_conv",
    )(*args)


def _pool_body(x_ref, w_ref, s_ref, o_ref):
    l = x_ref.shape[1]
    row = lax.broadcasted_iota(jnp.int32, (l, 1), 0)
    for g, win in enumerate(POOL_WINDOWS):
        half = win // 2
        sl = slice(g * POOL_GROUP, (g + 1) * POOL_GROUP)
        x = x_ref[0, :, sl]
        acc = jnp.zeros_like(x)
        for k in range(-half, half):
            shifted = x if k == 0 else pltpu.roll(x, (-k) % l, 0)
            acc = acc + jnp.where((row + k >= 0) & (row + k < l), shifted, 0.0)
        cnt = (jnp.minimum(row + half, l) - jnp.maximum(row - half, 0)).astype(F32)
        d = acc / cnt - x
        y = jnp.dot(d.astype(BF16), w_ref[g], preferred_element_type=F32)
        o_ref[0, :, sl] = (y * s_ref[:, sl]).astype(o_ref.dtype)


def _pool_call(z, w_grp, scale):
    b, l, _ = z.shape
    ng = len(POOL_WINDOWS)
    return pl.pallas_call(
        _pool_body,
        grid=(b,),
        in_specs=[
            pl.BlockSpec((1, l, POOL_W), lambda bi: (bi, 0, POOL_OFF // POOL_W)),
            pl.BlockSpec((ng, POOL_GROUP, POOL_GROUP), lambda bi: (0, 0, 0)),
            pl.BlockSpec((1, POOL_W), lambda bi: (0, 0)),
        ],
        out_specs=pl.BlockSpec((1, l, POOL_W), lambda bi: (bi, 0, 0)),
        out_shape=jax.ShapeDtypeStruct((b, l, POOL_W), BF16),
        compiler_params=_params("arbitrary"),
        name="pool",
    )(z, w_grp, scale.reshape(1, POOL_W))


def _merge_body(ya_ref, yh_ref, yp_ref, gt_ref, x_ref, ga_ref, g2_ref, sc_ref, sh_ref,
                wa_ref, wh_ref, wp_ref, wo_ref, xn_ref, h2_ref):
    d = x_ref.shape[1]
    cj = 512
    ya, yh, yp = ya_ref[...], yh_ref[...], yp_ref[...]
    acc = jnp.zeros(x_ref.shape, F32)
    for j in range(d // cj):
        sl = slice(j * cj, (j + 1) * cj)
        gate = lambda br: gt_ref[:, br * d + j * cj:br * d + (j + 1) * cj].astype(F32)
        m = (gate(0) * jnp.dot(ya, wa_ref[:, sl], preferred_element_type=F32)
             + gate(1) * jnp.dot(yh, wh_ref[:, sl], preferred_element_type=F32)
             + gate(2) * jnp.dot(yp, wp_ref[:, sl], preferred_element_type=F32))
        acc = acc + jnp.dot(m.astype(BF16), wo_ref[sl, :], preferred_element_type=F32)
    xn = x_ref[...] + ga_ref[0] * acc
    xn_ref[...] = xn
    h2_ref[...] = _norm_mod(xn, g2_ref[...], sc_ref[0], sh_ref[0]).astype(h2_ref.dtype)


def _merge_call(ya, yh, yp, gates, x2, ga1, g2, sc2, sh2, wa, wh, wp, wo, layer, rows_per_batch):
    m, d = x2.shape
    tm = min(512, rows_per_batch)
    rpt = rows_per_batch // tm
    rows = lambda w: pl.BlockSpec((tm, w), lambda i: (i, 0))
    weight = lambda w: pl.BlockSpec((None,) + w.shape[1:], lambda i: (layer, 0, 0), pipeline_mode=pl.Buffered(1))
    return pl.pallas_call(
        _merge_body,
        grid=(m // tm,),
        in_specs=[rows(ATTN_W), rows(HYENA_W), rows(POOL_W), rows(N_BRANCH * d), rows(d),
                  _mod_spec(ga1, rpt), pl.BlockSpec((1, d), lambda i: (0, 0)),
                  _mod_spec(sc2, rpt), _mod_spec(sh2, rpt),
                  weight(wa), weight(wh), weight(wp), weight(wo)],
        out_specs=[rows(d), rows(d)],
        out_shape=[jax.ShapeDtypeStruct((m, d), F32), jax.ShapeDtypeStruct((m, d), BF16)],
        compiler_params=_params("arbitrary"),
        name="merge",
    )(ya, yh, yp, gates, x2, ga1, g2.reshape(1, d), sc2, sh2, wa, wh, wp, wo)


def _mlp_body(*refs, has_next):
    it = iter(refs)
    h_ref, w1_ref, w2_ref, x_ref, ga_ref = (next(it) for _ in range(5))
    if has_next:
        gn_ref, sc_ref, sh_ref = next(it), next(it), next(it)
    o_ref = next(it)
    hn_ref = next(it) if has_next else None

    f = pl.program_id(1)
    last = pl.num_programs(1) - 1
    tm, d = o_ref.shape
    cn = 512
    rb = min(256, tm)

    def hidden(rows):
        a = jnp.dot(h_ref[rows, :], w1_ref[...], preferred_element_type=F32)
        return jnp.square(jnp.maximum(a, 0.0)).astype(BF16)

    @pl.when(f == 0)
    def _():
        a = hidden(slice(None))
        for n0 in range(0, d, cn):
            o_ref[:, n0:n0 + cn] = jnp.dot(a, w2_ref[:, n0:n0 + cn], preferred_element_type=F32)

    @pl.when((f > 0) & (f < last))
    def _():
        a = hidden(slice(None))
        for n0 in range(0, d, cn):
            o_ref[:, n0:n0 + cn] += jnp.dot(a, w2_ref[:, n0:n0 + cn], preferred_element_type=F32)

    @pl.when(f == last)
    def _():
        for r0 in range(0, tm, rb):
            rows = slice(r0, r0 + rb)
            acc = o_ref[rows, :] + jnp.dot(hidden(rows), w2_ref[...], preferred_element_type=F32)
            xo = x_ref[rows, :] + ga_ref[0] * acc
            o_ref[rows, :] = xo
            if has_next:
                hn_ref[rows, :] = _norm_mod(xo, gn_ref[...], sc_ref[0], sh_ref[0]).astype(hn_ref.dtype)


def _mlp_call(h2, w1, w2, layer, xn, ga2, nxt, rows_per_batch):
    m, d = xn.shape
    ff = w1.shape[2]
    tm, tf = min(1024, rows_per_batch if ga2.shape[0] > 1 else m), 512
    rpt = max(rows_per_batch // tm, 1)
    assert ff // tf >= 2
    has_next = nxt is not None
    rows = pl.BlockSpec((tm, d), lambda i, f: (i, 0))
    in_specs = [rows, pl.BlockSpec((None, d, tf), lambda i, f: (layer, 0, f)),
                pl.BlockSpec((None, tf, d), lambda i, f: (layer, f, 0)),
                pl.BlockSpec((tm, d), lambda i, f: (i, 0), pipeline_mode=pl.Buffered(1)), _mod_spec(ga2, rpt)]
    args = [h2, w1, w2, xn, ga2]
    out_specs = [rows]
    out_shape = [jax.ShapeDtypeStruct((m, d), F32)]
    if has_next:
        gn, scn, shn = nxt
        in_specs += [pl.BlockSpec((1, d), lambda i, f: (0, 0)), _mod_spec(scn, rpt), _mod_spec(shn, rpt)]
        args += [gn.reshape(1, d), scn, shn]
        out_specs.append(rows)
        out_shape.append(jax.ShapeDtypeStruct((m, d), BF16))
    outs = pl.pallas_call(
        functools.partial(_mlp_body, has_next=has_next),
        grid=(m // tm, ff // tf),
        in_specs=in_specs,
        out_specs=out_specs,
        out_shape=out_shape,
        compiler_params=_params("arbitrary", "arbitrary"),
        name="mlp",
    )(*args)
    return (outs[0], outs[1]) if has_next else (outs[0], None)


def _mixers(z, q, kv, kvx, sink, local, hy, pool_w, pool_scale):
    b, l, _ = z.shape
    y_att = _attn_call(q, kv, kvx, sink, local)
    conv_w, conv_b, filt_params, d_skip, (fc, fs) = hy
    nblk = _hyena_blocks(l)
    tc = 256
    spectra = _spectrum_call(_filter_call(l, *filt_params), fc, fs, nblk, tc)
    conv = functools.partial(_fftconv_call, conv_w=conv_w, conv_b=conv_b, spectra=spectra, d_skip=d_skip,
                             fc=fc, fs=fs, nblk=nblk, tc=tc)
    z1 = conv(z, HY_OFF, z, HY_OFF + HYENA_W, order=0, out_dtype=F32)
    y_hy = conv(z1, 0, z, HY_OFF + 2 * HYENA_W, order=1, out_dtype=BF16)
    y_pool = _pool_call(z, pool_w, pool_scale)
    return (y_att.reshape(b * l, ATTN_W), y_hy.reshape(b * l, HYENA_W), y_pool.reshape(b * l, POOL_W))


def kernel(x, c, ctx, c_ctx, norm1_g, norm2_g, w_mod, b_mod, w_in, q_norm_g, k_norm_g, sink, hy_conv_w, hy_conv_b, filt_w0, filt_b0, filt_w1, filt_b1, filt_freq, filt_w2, hy_bias, pool_w, pool_scale, w_att_o, w_hy_o, w_pool_o, w_out, mlp_w1, mlp_w2):
    b, l, d = x.shape
    lc = ctx.shape[1]
    depth = w_mod.shape[0]

    cc = jnp.concatenate([c, c_ctx[None, :], jnp.zeros((MOD_ROWS - b - 1, d), F32)], axis=0)
    mods = _modulation(cc, w_mod, b_mod)

    def chunks(layer, lo, hi):
        return [mods[layer, lo:hi, i * d:(i + 1) * d].reshape(hi - lo, 1, d) for i in range(6)]

    as_bf16 = lambda w: w.astype(BF16)
    w_in_b, w_att_b, w_hy_b, w_pool_b, w_out_b = map(as_bf16, (w_in, w_att_o, w_hy_o, w_pool_o, w_out))
    w1_b, w2_b, pool_w_b = map(as_bf16, (mlp_w1, mlp_w2, pool_w))

    rope_tabs = _rope_tables(l)
    dft_x = _dft_matrices(l // _hyena_blocks(l))
    dft_c = _dft_matrices(lc // _hyena_blocks(lc))

    x2 = x.reshape(b * l, d)
    c2 = ctx.reshape(b * lc, d)
    sh1, sc1 = chunks(0, 0, b)[:2]
    csh1, csc1 = chunks(0, b, b + 1)[:2]
    hx = _norm_call(x2, norm1_g[0], sc1, sh1, l)
    hc = _norm_call(c2, norm1_g[0], csc1, csh1, lc)

    for layer in range(depth):
        last = layer == depth - 1
        _, _, ga1, sh2, sc2, ga2 = chunks(layer, 0, b)
        _, _, cga1, csh2, csc2, cga2 = chunks(layer, b, b + 1)
        filt_params = (filt_w0[layer], filt_b0[layer], filt_w1[layer], filt_b1[layer], filt_freq[layer],
                       filt_w2[layer])
        conv_b = hy_conv_b[layer].reshape(1, -1)
        d_skip = hy_bias[layer].reshape(HYENA_ORDER, 1, HYENA_W)
        merge_w = (w_att_b, w_hy_b, w_pool_b, w_out_b, layer)

        gq, gk = q_norm_g[layer], k_norm_g[layer]
        if last:
            kvc = _kv_call(_proj_call(hc, w_in_b, layer, K_OFF, 2 * KV_W, F32), gk)
        else:
            zc = _proj_call(hc, w_in_b, layer, 0, GATE_OFF, F32)
            gc, qc, kvc = _gates_qkv_call(hc, w_in_b, layer, zc, gq, gk, None, lc)
            zc, qc = zc.reshape(b, lc, GATE_OFF), qc.reshape(b, lc, ATTN_W)
        kvc = kvc.reshape(b, lc, 2 * KV_DUP_W)

        zx = _proj_call(hx, w_in_b, layer, 0, GATE_OFF, F32)
        gx, qx, kvx = _gates_qkv_call(hx, w_in_b, layer, zx, gq, gk, rope_tabs, l)
        zx, qx, kvx = zx.reshape(b, l, GATE_OFF), qx.reshape(b, l, ATTN_W), kvx.reshape(b, l, 2 * KV_DUP_W)
        hy = (hy_conv_w[layer], conv_b, filt_params, d_skip, dft_x)
        ya, yh, yp = _mixers(zx, qx, kvx, kvc, sink[layer], True, hy, pool_w_b[layer], pool_scale[layer])
        xn, h2 = _merge_call(ya, yh, yp, gx, x2, ga1, norm2_g[layer], sc2, sh2, *merge_w, l)
        nxt = None if last else (norm1_g[layer + 1], *reversed(chunks(layer + 1, 0, b)[:2]))
        x2, hx = _mlp_call(h2, w1_b, w2_b, layer, xn, ga2, nxt, l)

        if not last:
            hyc = (hy_conv_w[layer], conv_b, filt_params, d_skip, dft_c)
            ya, yh, yp = _mixers(zc, qc, None, kvc, sink[layer], False, hyc, pool_w_b[layer], pool_scale[layer])
            cn, h2c = _merge_call(ya, yh, yp, gc, c2, cga1, norm2_g[layer], csc2, csh2, *merge_w, lc)
            nxt = (norm1_g[layer + 1], *reversed(chunks(layer + 1, b, b + 1)[:2]))
            c2, hc = _mlp_call(h2c, w1_b, w2_b, layer, cn, cga2, nxt, lc)

    return x2.reshape(b, l, d)
```

```python
import functools
import math

import jax
import jax.numpy as jnp
from jax import lax
from jax.experimental import pallas as pl
from jax.experimental.pallas import tpu as pltpu

D_MODEL = 2048
DEPTH = 2
GRID_W = 64
EPS = 1e-6
NEG_INF = -1e30

N_HEADS = 16
N_KV_HEADS = 4
GQA_GROUP = N_HEADS // N_KV_HEADS
HEAD_DIM = 64
ATTN_W = N_HEADS * HEAD_DIM
KV_W = N_KV_HEADS * HEAD_DIM
WINDOW = 128
ROPE_FREQS = HEAD_DIM // 4
ROPE_BASE = 10000.0

HYENA_W = D_MODEL // 4
HYENA_ORDER = 2
FILTER_BANDS = 16
FILTER_EMB = 1 + 2 * FILTER_BANDS
FILTER_HIDDEN = 64
FILTER_INNER = 2
DECAY_TARGET = 1e-2
FAST_DECAY_PCT = 0.3
SLOW_DECAY_PCT = 1.5

POOL_W = D_MODEL // 4
POOL_WINDOWS = (2, 4, 8, 16)
POOL_GROUP = POOL_W // len(POOL_WINDOWS)

N_BRANCH = 3
D_FF = 4 * D_MODEL

Q_OFF = 0
K_OFF = Q_OFF + ATTN_W
V_OFF = K_OFF + KV_W
HY_OFF = V_OFF + KV_W
POOL_OFF = HY_OFF + 3 * HYENA_W
GATE_OFF = POOL_OFF + POOL_W
IN_W = GATE_OFF + N_BRANCH * D_MODEL

V7X_LANES = 128
V7X_VMEM_LIMIT = 60 * 1024 * 1024
KV_DUP_W = N_KV_HEADS * V7X_LANES
MOD_ROWS = 24

F32 = jnp.float32
BF16 = jnp.bfloat16
HIGHEST = lax.Precision.HIGHEST


def _params(*semantics):
    return pltpu.CompilerParams(dimension_semantics=semantics, vmem_limit_bytes=V7X_VMEM_LIMIT)


def _const_spec(shape):
    zeros = (0,) * len(shape)
    return pl.BlockSpec(shape, lambda *_: zeros, pipeline_mode=pl.Buffered(1))


def _mod_spec(arr, rows_per_mod_tile):
    d = arr.shape[-1]
    if arr.shape[0] == 1:
        return pl.BlockSpec((1, 1, d), lambda i, *_: (0, 0, 0))
    return pl.BlockSpec((1, 1, d), lambda i, *_: (i // rows_per_mod_tile, 0, 0))


def _norm_mod(xf, g, sc, sh):
    y = xf * lax.rsqrt(jnp.mean(xf * xf, axis=-1, keepdims=True) + EPS)
    return (y * g) * (1.0 + sc) + sh


def _mod_body(c_ref, w_ref, b_ref, o_ref):
    c = c_ref[...]
    s = c * jax.nn.sigmoid(c)
    o_ref[0] = jnp.dot(s.astype(BF16), w_ref[0].astype(BF16), preferred_element_type=F32) + b_ref[0]


def _modulation(cc, w_mod, b_mod):
    depth, d, n = w_mod.shape
    tn = 1024
    return pl.pallas_call(
        _mod_body,
        grid=(depth, n // tn),
        in_specs=[
            pl.BlockSpec((MOD_ROWS, d), lambda l, j: (0, 0)),
            pl.BlockSpec((1, d, tn), lambda l, j: (l, 0, j)),
            pl.BlockSpec((1, 1, tn), lambda l, j: (l, 0, j)),
        ],
        out_specs=pl.BlockSpec((1, MOD_ROWS, tn), lambda l, j: (l, 0, j)),
        out_shape=jax.ShapeDtypeStruct((depth, MOD_ROWS, n), F32),
        compiler_params=_params("arbitrary", "arbitrary"),
        name="modulation",
    )(cc, w_mod, b_mod.reshape(depth, 1, n))


def _norm_body(x_ref, g_ref, sc_ref, sh_ref, o_ref):
    o_ref[...] = _norm_mod(x_ref[...], g_ref[...], sc_ref[0], sh_ref[0]).astype(o_ref.dtype)


def _norm_call(x2, g, sc, sh, rows_per_batch):
    m, d = x2.shape
    tm = min(1024, rows_per_batch if sc.shape[0] > 1 else m)
    return pl.pallas_call(
        _norm_body,
        grid=(m // tm,),
        in_specs=[
            pl.BlockSpec((tm, d), lambda i: (i, 0)),
            pl.BlockSpec((1, d), lambda i: (0, 0)),
            _mod_spec(sc, rows_per_batch // tm),
            _mod_spec(sh, rows_per_batch // tm),
        ],
        out_specs=pl.BlockSpec((tm, d), lambda i: (i, 0)),
        out_shape=jax.ShapeDtypeStruct((m, d), BF16),
        compiler_params=_params("arbitrary"),
        name="norm_mod",
    )(x2, g.reshape(1, d), sc, sh)


def _proj_body(a_ref, w_ref, o_ref):
    o_ref[...] = jnp.dot(a_ref[...], w_ref[...], preferred_element_type=F32).astype(o_ref.dtype)


def _proj_call(a, w, layer, col0, n, out_dtype):
    m, k = a.shape
    tm = min(2048, m)
    tn = 512
    c0 = col0 // tn
    return pl.pallas_call(
        _proj_body,
        grid=(m // tm, n // tn),
        in_specs=[
            pl.BlockSpec((tm, k), lambda i, j: (i, 0)),
            pl.BlockSpec((None, k, tn), lambda i, j: (layer, 0, c0 + j)),
        ],
        out_specs=pl.BlockSpec((tm, tn), lambda i, j: (i, j)),
        out_shape=jax.ShapeDtypeStruct((m, n), out_dtype),
        compiler_params=_params("arbitrary", "arbitrary"),
        name="in_proj",
    )(a, w)


Q_SLABS = ATTN_W // V7X_LANES
QK_SLABS = Q_SLABS + KV_W // V7X_LANES
QKV_SLABS = QK_SLABS + KV_W // V7X_LANES


def _gates_qkv_body(a_ref, w_ref, z_ref, gain_ref, cos_ref, sup_ref, sdn_ref, g_ref, q_ref, kv_ref):
    j = pl.program_id(1)
    zg = jnp.dot(a_ref[...], w_ref[...], preferred_element_type=F32)
    g_ref[...] = (0.5 * jnp.tanh(0.5 * zg) + 0.5).astype(g_ref.dtype)

    x = z_ref[...]
    low = lax.broadcasted_iota(jnp.int32, (1, V7X_LANES), 1) < HEAD_DIM
    x2 = x * x
    ss = jnp.where(low, jnp.sum(jnp.where(low, x2, 0.0), axis=-1, keepdims=True),
                   jnp.sum(jnp.where(low, 0.0, x2), axis=-1, keepdims=True))
    inv = jnp.where(j < QK_SLABS, lax.rsqrt(ss * (1.0 / HEAD_DIM) + EPS), 1.0)
    y = _rope((x * inv) * gain_ref[0], cos_ref[...], sup_ref[...], sdn_ref[...])
    da, db = _dup_pair(y, low)

    @pl.when(j < Q_SLABS)
    def _():
        q_ref[...] = y.astype(q_ref.dtype)

    @pl.when(j >= Q_SLABS)
    def _():
        kv_ref[:, 0:V7X_LANES] = da.astype(kv_ref.dtype)
        kv_ref[:, V7X_LANES:2 * V7X_LANES] = db.astype(kv_ref.dtype)


def _gates_qkv_call(a, w, layer, z2, gq, gk, rope_tabs, seq_len):
    m, k = a.shape
    d = D_MODEL
    tm = min(2048, m)
    tn = 512
    assert N_BRANCH * d // tn == QKV_SLABS and tm % seq_len == 0
    c0 = GATE_OFF // tn
    ones = jnp.ones((tm, V7X_LANES), F32)
    zeros = jnp.zeros((tm, V7X_LANES), F32)
    if rope_tabs is None:
        cos, sup, sdn = ones[None], zeros[None], zeros[None]
        tab_map = lambda i, j: (0, 0, 0)
    else:
        rep = lambda t: jnp.tile(t, (tm // seq_len, 1))
        cos, sup, sdn = (jnp.stack([rep(t), ident]) for t, ident in zip(rope_tabs, (ones, zeros, zeros)))
        tab_map = lambda i, j: ((j >= QK_SLABS).astype(jnp.int32), 0, 0)
    gains = jnp.stack([jnp.tile(gq, 2) * HEAD_DIM ** -0.5, jnp.tile(gk, 2), jnp.ones((V7X_LANES,), F32)])
    gain_map = lambda i, j: ((j >= Q_SLABS).astype(jnp.int32) + (j >= QK_SLABS).astype(jnp.int32), 0, 0)
    tab_spec = pl.BlockSpec((None, tm, V7X_LANES), tab_map)
    return pl.pallas_call(
        _gates_qkv_body,
        grid=(m // tm, QKV_SLABS),
        in_specs=[
            pl.BlockSpec((tm, k), lambda i, j: (i, 0)),
            pl.BlockSpec((None, k, tn), lambda i, j: (layer, 0, c0 + j)),
            pl.BlockSpec((tm, V7X_LANES), lambda i, j: (i, j)),
            pl.BlockSpec((None, 1, V7X_LANES), gain_map),
            tab_spec, tab_spec, tab_spec,
        ],
        out_specs=[
            pl.BlockSpec((tm, tn), lambda i, j: (i, j)),
            pl.BlockSpec((tm, V7X_LANES), lambda i, j: (i, jnp.minimum(j, Q_SLABS - 1))),
            pl.BlockSpec((tm, 2 * V7X_LANES), lambda i, j: (i, jnp.maximum(j - Q_SLABS, 0))),
        ],
        out_shape=[
            jax.ShapeDtypeStruct((m, N_BRANCH * d), BF16),
            jax.ShapeDtypeStruct((m, ATTN_W), BF16),
            jax.ShapeDtypeStruct((m, 2 * KV_DUP_W), BF16),
        ],
        compiler_params=_params("arbitrary", "arbitrary"),
        name="gates_qkv",
    )(a, w, z2, gains.reshape(3, 1, V7X_LANES), cos, sup, sdn)


def _pair_block_diag():
    r = lax.broadcasted_iota(jnp.int32, (V7X_LANES, V7X_LANES), 0) // HEAD_DIM
    c = lax.broadcasted_iota(jnp.int32, (V7X_LANES, V7X_LANES), 1) // HEAD_DIM
    return (r == c).astype(F32)


def _head_norm(x, g, bd):
    ss = jnp.dot(x * x, bd, precision=HIGHEST, preferred_element_type=F32)
    return (x * lax.rsqrt(ss * (1.0 / HEAD_DIM) + EPS)) * g


def _rope(x, cos, sin_up, sin_dn):
    up = pltpu.roll(x, V7X_LANES - ROPE_FREQS, 1)
    dn = pltpu.roll(x, ROPE_FREQS, 1)
    return x * cos + up * sin_up + dn * sin_dn


def _dup_pair(x, low):
    r = pltpu.roll(x, HEAD_DIM, 1)
    return jnp.where(low, x, r), jnp.where(low, r, x)


def _kv_body(z_ref, gk_ref, kv_ref):
    bd = _pair_block_diag()
    low = lax.broadcasted_iota(jnp.int32, (1, V7X_LANES), 1) < HEAD_DIM
    for s in range(2 * KV_W // V7X_LANES):
        x = z_ref[:, s * V7X_LANES:(s + 1) * V7X_LANES]
        if s < KV_W // V7X_LANES:
            x = _head_norm(x, gk_ref[...], bd)
        a, b = _dup_pair(x, low)
        base = 2 * s * V7X_LANES
        kv_ref[:, base:base + V7X_LANES] = a.astype(kv_ref.dtype)
        kv_ref[:, base + V7X_LANES:base + 2 * V7X_LANES] = b.astype(kv_ref.dtype)


def _kv_call(z2, gk):
    m, nz = z2.shape
    tm = min(512, m)
    return pl.pallas_call(
        _kv_body,
        grid=(m // tm,),
        in_specs=[pl.BlockSpec((tm, nz), lambda i: (i, 0)),
                  pl.BlockSpec((1, V7X_LANES), lambda i: (0, 0))],
        out_specs=pl.BlockSpec((tm, 2 * KV_DUP_W), lambda i: (i, 0)),
        out_shape=jax.ShapeDtypeStruct((m, 2 * KV_DUP_W), BF16),
        compiler_params=_params("arbitrary"),
        name="kv_prep",
    )(z2, jnp.tile(gk, 2).reshape(1, V7X_LANES))


def _rope_tables(l):
    rows = l // GRID_W
    row = jnp.repeat(jnp.arange(rows, dtype=F32), GRID_W)
    col = jnp.tile(jnp.arange(GRID_W, dtype=F32), rows)
    inv = ROPE_BASE ** (-jnp.arange(ROPE_FREQS, dtype=F32) / ROPE_FREQS)
    ang = jnp.stack([row[:, None] * inv, col[:, None] * inv], axis=1)
    cos, sin = jnp.cos(ang), jnp.sin(ang)
    zero = jnp.zeros_like(sin)
    cos_h = jnp.stack([cos, cos], axis=2).reshape(l, HEAD_DIM)
    sup_h = jnp.stack([-sin, zero], axis=2).reshape(l, HEAD_DIM)
    sdn_h = jnp.stack([zero, sin], axis=2).reshape(l, HEAD_DIM)
    return tuple(jnp.tile(t, (1, 2)) for t in (cos_h, sup_h, sdn_h))


def _attn_body(*refs, local, tq):
    it = iter(refs)
    sink_ref = next(it)
    q_ref = next(it)
    window_refs = [next(it) for _ in range(3)] if local else []
    kvx_ref = next(it)
    o_ref = next(it)

    i = pl.program_id(1)
    nb = pl.num_programs(1)
    low = lax.broadcasted_iota(jnp.int32, (1, V7X_LANES), 1) < HEAD_DIM
    rows = GQA_GROUP * tq
    if local:
        qi = lax.broadcasted_iota(jnp.int32, (rows, tq), 0) % tq
        kj = lax.broadcasted_iota(jnp.int32, (rows, tq), 1)
        mask_prev = (kj >= qi) & (i > 0)
        mask_next = (kj <= qi) & (i < nb - 1)
    row_head = lax.broadcasted_iota(jnp.int32, (rows, 1), 0) // tq
    zero = jnp.zeros((), q_ref.dtype)

    for h in range(N_KV_HEADS):
        hs = slice(h * V7X_LANES, (h + 1) * V7X_LANES)
        qa = q_ref[0, :, 2 * h * V7X_LANES:(2 * h + 1) * V7X_LANES]
        qb = q_ref[0, :, (2 * h + 1) * V7X_LANES:(2 * h + 2) * V7X_LANES]
        qs = jnp.concatenate([jnp.where(low, qa, zero), jnp.where(low, zero, qa),
                              jnp.where(low, qb, zero), jnp.where(low, zero, qb)], axis=0)
        vs = slice(KV_DUP_W + h * V7X_LANES, KV_DUP_W + (h + 1) * V7X_LANES)
        masks = {0: mask_prev, 2: mask_next} if local else {}
        k_all = jnp.concatenate([r[0, :, hs] for r in window_refs + [kvx_ref]], axis=0)
        v_all = jnp.concatenate([r[0, :, vs] for r in window_refs + [kvx_ref]], axis=0)

        sink = jnp.zeros((rows, 1), F32)
        for g in range(GQA_GROUP):
            sink = jnp.where(row_head == g, sink_ref[GQA_GROUP * h + g], sink)
        s_all = lax.dot_general(qs, k_all, (((1,), (1,)), ((), ())), preferred_element_type=F32)
        chunks = []
        for c in range(k_all.shape[0] // tq):
            s = s_all[:, c * tq:(c + 1) * tq]
            chunks.append(jnp.where(masks[c], s, NEG_INF) if c in masks else s)
        m = jnp.maximum(sink, jnp.max(functools.reduce(jnp.maximum, chunks), axis=-1, keepdims=True))
        probs = [jnp.exp(s - m) for s in chunks]
        denom = jnp.exp(sink - m) + jnp.sum(functools.reduce(jnp.add, probs), axis=-1, keepdims=True)
        p_all = jnp.concatenate([p.astype(v_all.dtype) for p in probs], axis=1)
        o = jnp.dot(p_all, v_all, preferred_element_type=F32) / denom
        oa = jnp.where(low, o[0:tq], o[tq:2 * tq])
        ob = jnp.where(low, o[2 * tq:3 * tq], o[3 * tq:4 * tq])
        o_ref[0, :, 2 * h * V7X_LANES:(2 * h + 1) * V7X_LANES] = oa.astype(o_ref.dtype)
        o_ref[0, :, (2 * h + 1) * V7X_LANES:(2 * h + 2) * V7X_LANES] = ob.astype(o_ref.dtype)


def _attn_call(q, kv, kvx, sink, local):
    b, l, _ = q.shape
    lx = kvx.shape[1]
    tq = WINDOW
    nb = l // tq
    blk = lambda w: (1, tq, w)
    in_specs = [pl.BlockSpec(memory_space=pltpu.SMEM),
                pl.BlockSpec(blk(ATTN_W), lambda bi, i: (bi, i, 0))]
    args = [sink, q]
    if local:
        for mp in (lambda bi, i: (bi, jnp.maximum(i - 1, 0), 0),
                   lambda bi, i: (bi, i, 0),
                   lambda bi, i: (bi, jnp.minimum(i + 1, nb - 1), 0)):
            in_specs.append(pl.BlockSpec(blk(2 * KV_DUP_W), mp))
            args.append(kv)
    in_specs.append(pl.BlockSpec((1, lx, 2 * KV_DUP_W), lambda bi, i: (bi, 0, 0)))
    args.append(kvx)
    return pl.pallas_call(
        functools.partial(_attn_body, local=local, tq=tq),
        grid=(b, nb),
        in_specs=in_specs,
        out_specs=pl.BlockSpec(blk(ATTN_W), lambda bi, i: (bi, i, 0)),
        out_shape=jax.ShapeDtypeStruct((b, l, ATTN_W), BF16),
        compiler_params=_params("arbitrary", "arbitrary"),
        name="attention",
    )(*args)


def _filter_body(z_ref, w0_ref, b0_ref, w1_ref, b1_ref, fr_ref, w2_ref, dec_ref, o_ref):
    fr = fr_ref[...]
    dot = functools.partial(jnp.dot, precision=HIGHEST, preferred_element_type=F32)
    h = jnp.sin(fr * (dot(z_ref[...], w0_ref[...]) + b0_ref[...]))
    for i in range(FILTER_INNER):
        h = jnp.sin(fr * (dot(h, w1_ref[i]) + b1_ref[i]))
    dec = dec_ref[...]
    for s in range(2 * HYENA_ORDER):
        sl = slice(s * HYENA_W, (s + 1) * HYENA_W)
        o_ref[:, sl] = dot(h, w2_ref[:, sl]) * dec


def _filter_features(l):
    t = jnp.linspace(0.0, 1.0, l, dtype=F32)[:, None]
    w = 2.0 * math.pi * jnp.arange(l, dtype=F32)[:, None] / l
    bands = jnp.linspace(1e-4, FILTER_BANDS - 1, FILTER_BANDS, dtype=F32)[None, :]
    z = jnp.concatenate([t, jnp.cos(bands * w), -jnp.sin(bands * w)], axis=-1)
    deltas = jnp.linspace(math.log(DECAY_TARGET) / SLOW_DECAY_PCT, math.log(DECAY_TARGET) / FAST_DECAY_PCT,
                          HYENA_W, dtype=F32)
    decay = jnp.exp(-t * jnp.abs(deltas))
    return jnp.pad(z, ((0, 0), (0, V7X_LANES - FILTER_EMB))), decay


def _filter_call(l, w0, b0, w1, b1, freq, w2):
    zfeat, decay = _filter_features(l)
    w0p = jnp.pad(w0, ((0, V7X_LANES - FILTER_EMB), (0, 0)))
    tl = min(512, l)
    nf = 2 * HYENA_ORDER * HYENA_W
    full = lambda shape: pl.BlockSpec(shape, lambda i: (0,) * len(shape))
    return pl.pallas_call(
        _filter_body,
        grid=(l // tl,),
        in_specs=[
            pl.BlockSpec((tl, V7X_LANES), lambda i: (i, 0)),
            full((V7X_LANES, FILTER_HIDDEN)),
            full((1, FILTER_HIDDEN)),
            full((FILTER_INNER, FILTER_HIDDEN, FILTER_HIDDEN)),
            full((FILTER_INNER, 1, FILTER_HIDDEN)),
            full((1, FILTER_HIDDEN)),
            full((FILTER_HIDDEN, nf)),
            pl.BlockSpec((tl, HYENA_W), lambda i: (i, 0)),
        ],
        out_specs=pl.BlockSpec((tl, nf), lambda i: (i, 0)),
        out_shape=jax.ShapeDtypeStruct((l, nf), F32),
        compiler_params=_params("arbitrary"),
        name="hyena_filter",
    )(zfeat, w0p, b0.reshape(1, -1), w1, b1.reshape(FILTER_INNER, 1, -1), freq.reshape(1, -1), w2, decay)


def _hyena_blocks(l):
    return max(1, min(4, l // V7X_LANES))


def _dft_matrices(blk):
    n = 2 * blk
    r = jnp.arange(blk, dtype=jnp.int32)
    ang = ((r[:, None] * r[None, :]) % n).astype(F32) * (2.0 * math.pi / n)
    return jnp.cos(ang).astype(BF16), jnp.sin(ang).astype(BF16)


def _alternating(l):
    row = lax.broadcasted_iota(jnp.int32, (l, 1), 0)
    return row, jnp.where(row % 2 == 0, 1.0, -1.0).astype(F32)


def _spectrum_body(hf_ref, hb_ref, fc_ref, fs_ref, ka_ref, kb_ref, kn_ref, *, nblk):
    l = hf_ref.shape[0]
    b = l // nblk
    n = 2 * b
    row = lax.broadcasted_iota(jnp.int32, (l, 1), 0)
    _, alt = _alternating(b)
    hf = hf_ref[...]
    hbs = jnp.where(row == 0, 0.0, pltpu.roll(hb_ref[...], 1, 0))
    fc, fs = fc_ref[...], fs_ref[...]

    def transforms(h):
        out = []
        for k in range(nblk):
            hk = h[k * b:(k + 1) * b]
            hk16 = hk.astype(BF16)
            out.append(dict(
                c=jnp.dot(fc, hk16, preferred_element_type=F32),
                s=jnp.dot(fs, hk16, preferred_element_type=F32),
                first16=hk16[0:1].astype(F32),
                first=hk[0:1],
                alt=jnp.sum(hk * alt, axis=0, keepdims=True)))
        return out

    tf, tb = transforms(hf), transforms(hbs)
    brow = lax.broadcasted_iota(jnp.int32, (b, 1), 0)
    w_re = jnp.where(brow == 0, 1.0 / n, 2.0 / n)
    for d in range(-(nblk - 1), nblk):
        idx = d + nblk - 1
        if d == 0:
            kre = tf[0]["c"] + tb[0]["c"]
            kim = tb[0]["s"] - tf[0]["s"]
            kn = tf[0]["alt"] + tb[0]["alt"]
        else:
            t, e, sg = (tf, d, -1.0) if d > 0 else (tb, -d, 1.0)
            kre = t[e]["c"] + alt * (t[e - 1]["c"] - t[e - 1]["first16"])
            kim = sg * (t[e]["s"] + alt * t[e - 1]["s"])
            kn = t[e]["alt"] + t[e - 1]["alt"] - t[e - 1]["first"]
        ka_ref[0, idx] = (kre * w_re).astype(ka_ref.dtype)
        kb_ref[0, idx] = (kim * (2.0 / n)).astype(kb_ref.dtype)
        kn_ref[0, idx] = kn * (1.0 / n)


def _spectrum_call(filt, fc, fs, nblk, tc):
    l = filt.shape[0]
    b = l // nblk
    nct = HYENA_W // tc
    nlag = 2 * nblk - 1
    return pl.pallas_call(
        functools.partial(_spectrum_body, nblk=nblk),
        grid=(HYENA_ORDER, nct),
        in_specs=[
            pl.BlockSpec((l, tc), lambda o, c: (0, 2 * nct * o + c)),
            pl.BlockSpec((l, tc), lambda o, c: (0, 2 * nct * o + nct + c)),
            _const_spec((b, b)),
            _const_spec((b, b)),
        ],
        out_specs=[
            pl.BlockSpec((1, nlag, b, tc), lambda o, c: (o, 0, 0, c)),
            pl.BlockSpec((1, nlag, b, tc), lambda o, c: (o, 0, 0, c)),
            pl.BlockSpec((1, nlag, 1, tc), lambda o, c: (o, 0, 0, c)),
        ],
        out_shape=[
            jax.ShapeDtypeStruct((HYENA_ORDER, nlag, b, HYENA_W), BF16),
            jax.ShapeDtypeStruct((HYENA_ORDER, nlag, b, HYENA_W), BF16),
            jax.ShapeDtypeStruct((HYENA_ORDER, nlag, 1, HYENA_W), F32),
        ],
        compiler_params=_params("arbitrary", "arbitrary"),
        name="hyena_spectrum",
    )(filt, filt, fc, fs)


def _conv3(x, w_ref, b_ref, row):
    l = x.shape[0]
    xm = jnp.where(row == 0, 0.0, pltpu.roll(x, 1, 0))
    xp = jnp.where(row == l - 1, 0.0, pltpu.roll(x, l - 1, 0))
    return xm * w_ref[0:1, :] + x * w_ref[1:2, :] + xp * w_ref[2:3, :] + b_ref[...]


def _fftconv_body(*refs, conv_u, nblk):
    it = iter(refs)
    u_ref = next(it)
    if conv_u:
        uw_ref, ub_ref = next(it), next(it)
    g_ref, gw_ref, gb_ref = next(it), next(it), next(it)
    ka_ref, kb_ref, kn_ref, d_ref, fc_ref, fs_ref, o_ref = (next(it) for _ in range(7))

    l = u_ref.shape[1]
    b = l // nblk
    row = lax.broadcasted_iota(jnp.int32, (l, 1), 0)
    _, alt = _alternating(b)
    u = u_ref[0].astype(F32)
    if conv_u:
        u = _conv3(u, uw_ref, ub_ref, row)
    gate = _conv3(g_ref[0], gw_ref, gb_ref, row)
    fc, fs = fc_ref[...], fs_ref[...]

    ps, qs, ns = [], [], []
    for j in range(nblk):
        uj = u[j * b:(j + 1) * b]
        uj16 = uj.astype(BF16)
        ps.append(jnp.dot(fc, uj16, preferred_element_type=F32).astype(BF16))
        qs.append(jnp.dot(fs, uj16, preferred_element_type=F32).astype(BF16))
        ns.append(jnp.sum(uj * alt, axis=0, keepdims=True))
    for i in range(nblk):
        r = t = nyq = None
        for j in range(nblk):
            lag = i - j + nblk - 1
            ka, kb = ka_ref[0, lag], kb_ref[0, lag]
            dr = ps[j] * ka + qs[j] * kb
            dt = qs[j] * ka - ps[j] * kb
            dn = ns[j] * kn_ref[0, lag]
            r, t, nyq = (dr, dt, dn) if j == 0 else (r + dr, t + dt, nyq + dn)
        y = jnp.dot(fc, r, preferred_element_type=F32) + jnp.dot(fs, t, preferred_element_type=F32)
        rows = slice(i * b, (i + 1) * b)
        y = y + alt * nyq + u[rows] * d_ref[0]
        o_ref[0, rows, :] = (gate[rows] * y).astype(o_ref.dtype)


def _fftconv_call(u, u_col0, z, gate_col0, conv_w, conv_b, spectra, d_skip, order, fc, fs, nblk, tc, out_dtype):
    b, l, _ = z.shape
    conv_u = u is z
    nct = HYENA_W // tc
    ka, kb, kn = spectra
    blk = l // nblk
    nlag = 2 * nblk - 1
    col = lambda c0: (lambda c, bi: (bi, 0, c0 // tc + c))
    wcol = lambda c0: (lambda c, bi: (0, (c0 - HY_OFF) // tc + c))
    in_specs = [pl.BlockSpec((1, l, tc), col(u_col0))]
    args = [u]
    if conv_u:
        in_specs += [pl.BlockSpec((3, tc), wcol(u_col0)), pl.BlockSpec((1, tc), wcol(u_col0))]
        args += [conv_w, conv_b]
    in_specs += [pl.BlockSpec((1, l, tc), col(gate_col0)),
                 pl.BlockSpec((3, tc), wcol(gate_col0)), pl.BlockSpec((1, tc), wcol(gate_col0))]
    args += [z, conv_w, conv_b]
    spec = lambda rows: pl.BlockSpec((1, nlag, rows, tc), lambda c, bi: (order, 0, 0, c),
                                     pipeline_mode=pl.Buffered(1))
    in_specs += [spec(blk), spec(blk), spec(1),
                 pl.BlockSpec((1, 1, tc), lambda c, bi: (order, 0, c), pipeline_mode=pl.Buffered(1)),
                 _const_spec((blk, blk)), _const_spec((blk, blk))]
    args += [ka, kb, kn, d_skip, fc, fs]
    return pl.pallas_call(
        functools.partial(_fftconv_body, conv_u=conv_u, nblk=nblk),
        grid=(nct, b),
        in_specs=in_specs,
        out_specs=pl.BlockSpec((1, l, tc), lambda c, bi: (bi, 0, c)),
        out_shape=jax.ShapeDtypeStruct((b, l, HYENA_W), out_dtype),
        compiler_params=_params("arbitrary", "arbitrary"),
        name="hyena_conv",
    )(*args)


def _pool_body(x_ref, w_ref, s_ref, o_ref):
    l = x_ref.shape[1]
    row = lax.broadcasted_iota(jnp.int32, (l, 1), 0)
    for g, win in enumerate(POOL_WINDOWS):
        half = win // 2
        sl = slice(g * POOL_GROUP, (g + 1) * POOL_GROUP)
        x = x_ref[0, :, sl]

        def shifted(a, k):
            return jnp.where((row >= k) & (row < l + k), pltpu.roll(a, k % l, 0), 0.0)

        back = fwd = x
        span = 1
        while span < half:
            back = back + shifted(back, span)
            fwd = fwd + shifted(fwd, -span)
            span *= 2
        acc = shifted(back, 1) + fwd
        cnt = (jnp.minimum(row + half, l) - jnp.maximum(row - half, 0)).astype(F32)
        d = acc / cnt - x
        y = jnp.dot(d.astype(BF16), w_ref[g], preferred_element_type=F32)
        o_ref[0, :, sl] = (y * s_ref[:, sl]).astype(o_ref.dtype)


def _pool_call(z, w_grp, scale):
    b, l, _ = z.shape
    ng = len(POOL_WINDOWS)
    return pl.pallas_call(
        _pool_body,
        grid=(b,),
        in_specs=[
            pl.BlockSpec((1, l, POOL_W), lambda bi: (bi, 0, POOL_OFF // POOL_W)),
            pl.BlockSpec((ng, POOL_GROUP, POOL_GROUP), lambda bi: (0, 0, 0)),
            pl.BlockSpec((1, POOL_W), lambda bi: (0, 0)),
        ],
        out_specs=pl.BlockSpec((1, l, POOL_W), lambda bi: (bi, 0, 0)),
        out_shape=jax.ShapeDtypeStruct((b, l, POOL_W), BF16),
        compiler_params=_params("arbitrary"),
        name="pool",
    )(z, w_grp, scale.reshape(1, POOL_W))


def _merge_body(ya_ref, yh_ref, yp_ref, gt_ref, x_ref, ga_ref, g2_ref, sc_ref, sh_ref,
                wa_ref, wh_ref, wp_ref, wo_ref, xn_ref, h2_ref):
    d = x_ref.shape[1]
    cj = 512
    ya, yh, yp = ya_ref[...], yh_ref[...], yp_ref[...]
    acc = jnp.zeros(x_ref.shape, F32)
    for j in range(d // cj):
        sl = slice(j * cj, (j + 1) * cj)
        gate = lambda br: gt_ref[:, br * d + j * cj:br * d + (j + 1) * cj].astype(F32)
        m = (gate(0) * jnp.dot(ya, wa_ref[:, sl], preferred_element_type=F32)
             + gate(1) * jnp.dot(yh, wh_ref[:, sl], preferred_element_type=F32)
             + gate(2) * jnp.dot(yp, wp_ref[:, sl], preferred_element_type=F32))
        acc = acc + jnp.dot(m.astype(BF16), wo_ref[sl, :], preferred_element_type=F32)
    xn = x_ref[...] + ga_ref[0] * acc
    xn_ref[...] = xn
    h2_ref[...] = _norm_mod(xn, g2_ref[...], sc_ref[0], sh_ref[0]).astype(h2_ref.dtype)


def _merge_call(ya, yh, yp, gates, x2, ga1, g2, sc2, sh2, wa, wh, wp, wo, layer, rows_per_batch):
    m, d = x2.shape
    tm = min(512, rows_per_batch)
    rpt = rows_per_batch // tm
    rows = lambda w: pl.BlockSpec((tm, w), lambda i: (i, 0))
    weight = lambda w: pl.BlockSpec((None,) + w.shape[1:], lambda i: (layer, 0, 0), pipeline_mode=pl.Buffered(1))
    return pl.pallas_call(
        _merge_body,
        grid=(m // tm,),
        in_specs=[rows(ATTN_W), rows(HYENA_W), rows(POOL_W), rows(N_BRANCH * d), rows(d),
                  _mod_spec(ga1, rpt), pl.BlockSpec((1, d), lambda i: (0, 0)),
                  _mod_spec(sc2, rpt), _mod_spec(sh2, rpt),
                  weight(wa), weight(wh), weight(wp), weight(wo)],
        out_specs=[rows(d), rows(d)],
        out_shape=[jax.ShapeDtypeStruct((m, d), F32), jax.ShapeDtypeStruct((m, d), BF16)],
        compiler_params=_params("arbitrary"),
        name="merge",
    )(ya, yh, yp, gates, x2, ga1, g2.reshape(1, d), sc2, sh2, wa, wh, wp, wo)


def _mlp_body(*refs, has_next):
    it = iter(refs)
    h_ref, w1_ref, w2_ref, x_ref, ga_ref = (next(it) for _ in range(5))
    if has_next:
        gn_ref, sc_ref, sh_ref = next(it), next(it), next(it)
    o_ref = next(it)
    hn_ref = next(it) if has_next else None

    f = pl.program_id(1)
    last = pl.num_programs(1) - 1
    tm, d = o_ref.shape
    cn = 512
    rb = min(256, tm)

    def hidden(rows):
        a = jnp.dot(h_ref[rows, :], w1_ref[...], preferred_element_type=F32)
        return jnp.square(jnp.maximum(a, 0.0)).astype(BF16)

    @pl.when(f == 0)
    def _():
        a = hidden(slice(None))
        for n0 in range(0, d, cn):
            o_ref[:, n0:n0 + cn] = jnp.dot(a, w2_ref[:, n0:n0 + cn], preferred_element_type=F32)

    @pl.when((f > 0) & (f < last))
    def _():
        a = hidden(slice(None))
        for n0 in range(0, d, cn):
            o_ref[:, n0:n0 + cn] += jnp.dot(a, w2_ref[:, n0:n0 + cn], preferred_element_type=F32)

    @pl.when(f == last)
    def _():
        for r0 in range(0, tm, rb):
            rows = slice(r0, r0 + rb)
            acc = o_ref[rows, :] + jnp.dot(hidden(rows), w2_ref[...], preferred_element_type=F32)
            xo = x_ref[rows, :] + ga_ref[0] * acc
            o_ref[rows, :] = xo
            if has_next:
                hn_ref[rows, :] = _norm_mod(xo, gn_ref[...], sc_ref[0], sh_ref[0]).astype(hn_ref.dtype)


def _mlp_call(h2, w1, w2, layer, xn, ga2, nxt, rows_per_batch):
    m, d = xn.shape
    ff = w1.shape[2]
    tm, tf = min(1024, rows_per_batch if ga2.shape[0] > 1 else m), 512
    rpt = max(rows_per_batch // tm, 1)
    assert ff // tf >= 2
    has_next = nxt is not None
    rows = pl.BlockSpec((tm, d), lambda i, f: (i, 0))
    in_specs = [rows, pl.BlockSpec((None, d, tf), lambda i, f: (layer, 0, f)),
                pl.BlockSpec((None, tf, d), lambda i, f: (layer, f, 0)),
                pl.BlockSpec((tm, d), lambda i, f: (i, 0), pipeline_mode=pl.Buffered(1)), _mod_spec(ga2, rpt)]
    args = [h2, w1, w2, xn, ga2]
    out_specs = [rows]
    out_shape = [jax.ShapeDtypeStruct((m, d), F32)]
    if has_next:
        gn, scn, shn = nxt
        in_specs += [pl.BlockSpec((1, d), lambda i, f: (0, 0)), _mod_spec(scn, rpt), _mod_spec(shn, rpt)]
        args += [gn.reshape(1, d), scn, shn]
        out_specs.append(rows)
        out_shape.append(jax.ShapeDtypeStruct((m, d), BF16))
    outs = pl.pallas_call(
        functools.partial(_mlp_body, has_next=has_next),
        grid=(m // tm, ff // tf),
        in_specs=in_specs,
        out_specs=out_specs,
        out_shape=out_shape,
        compiler_params=_params("arbitrary", "arbitrary"),
        name="mlp",
    )(*args)
    return (outs[0], outs[1]) if has_next else (outs[0], None)


def _mixers(z, q, kv, kvx, sink, local, hy, pool_w, pool_scale):
    b, l, _ = z.shape
    y_att = _attn_call(q, kv, kvx, sink, local)
    conv_w, conv_b, filt_params, d_skip, (fc, fs) = hy
    nblk = _hyena_blocks(l)
    tc = 256
    spectra = _spectrum_call(_filter_call(l, *filt_params), fc, fs, nblk, tc)
    conv = functools.partial(_fftconv_call, conv_w=conv_w, conv_b=conv_b, spectra=spectra, d_skip=d_skip,
                             fc=fc, fs=fs, nblk=nblk, tc=tc)
    z1 = conv(z, HY_OFF, z, HY_OFF + HYENA_W, order=0, out_dtype=F32)
    y_hy = conv(z1, 0, z, HY_OFF + 2 * HYENA_W, order=1, out_dtype=BF16)
    y_pool = _pool_call(z, pool_w, pool_scale)
    return (y_att.reshape(b * l, ATTN_W), y_hy.reshape(b * l, HYENA_W), y_pool.reshape(b * l, POOL_W))


def kernel(x, c, ctx, c_ctx, norm1_g, norm2_g, w_mod, b_mod, w_in, q_norm_g, k_norm_g, sink, hy_conv_w, hy_conv_b, filt_w0, filt_b0, filt_w1, filt_b1, filt_freq, filt_w2, hy_bias, pool_w, pool_scale, w_att_o, w_hy_o, w_pool_o, w_out, mlp_w1, mlp_w2):
    b, l, d = x.shape
    lc = ctx.shape[1]
    depth = w_mod.shape[0]

    cc = jnp.concatenate([c, c_ctx[None, :], jnp.zeros((MOD_ROWS - b - 1, d), F32)], axis=0)
    mods = _modulation(cc, w_mod, b_mod)

    def chunks(layer, lo, hi):
        return [mods[layer, lo:hi, i * d:(i + 1) * d].reshape(hi - lo, 1, d) for i in range(6)]

    as_bf16 = lambda w: w.astype(BF16)
    w_in_b, w_att_b, w_hy_b, w_pool_b, w_out_b = map(as_bf16, (w_in, w_att_o, w_hy_o, w_pool_o, w_out))
    w1_b, w2_b, pool_w_b = map(as_bf16, (mlp_w1, mlp_w2, pool_w))

    rope_tabs = _rope_tables(l)
    dft_x = _dft_matrices(l // _hyena_blocks(l))
    dft_c = _dft_matrices(lc // _hyena_blocks(lc))

    x2 = x.reshape(b * l, d)
    c2 = ctx.reshape(b * lc, d)
    sh1, sc1 = chunks(0, 0, b)[:2]
    csh1, csc1 = chunks(0, b, b + 1)[:2]
    hx = _norm_call(x2, norm1_g[0], sc1, sh1, l)
    hc = _norm_call(c2, norm1_g[0], csc1, csh1, lc)

    for layer in range(depth):
        last = layer == depth - 1
        _, _, ga1, sh2, sc2, ga2 = chunks(layer, 0, b)
        _, _, cga1, csh2, csc2, cga2 = chunks(layer, b, b + 1)
        filt_params = (filt_w0[layer], filt_b0[layer], filt_w1[layer], filt_b1[layer], filt_freq[layer],
                       filt_w2[layer])
        conv_b = hy_conv_b[layer].reshape(1, -1)
        d_skip = hy_bias[layer].reshape(HYENA_ORDER, 1, HYENA_W)
        merge_w = (w_att_b, w_hy_b, w_pool_b, w_out_b, layer)

        gq, gk = q_norm_g[layer], k_norm_g[layer]
        if last:
            kvc = _kv_call(_proj_call(hc, w_in_b, layer, K_OFF, 2 * KV_W, F32), gk)
        else:
            zc = _proj_call(hc, w_in_b, layer, 0, GATE_OFF, F32)
            gc, qc, kvc = _gates_qkv_call(hc, w_in_b, layer, zc, gq, gk, None, lc)
            zc, qc = zc.reshape(b, lc, GATE_OFF), qc.reshape(b, lc, ATTN_W)
        kvc = kvc.reshape(b, lc, 2 * KV_DUP_W)

        zx = _proj_call(hx, w_in_b, layer, 0, GATE_OFF, F32)
        gx, qx, kvx = _gates_qkv_call(hx, w_in_b, layer, zx, gq, gk, rope_tabs, l)
        zx, qx, kvx = zx.reshape(b, l, GATE_OFF), qx.reshape(b, l, ATTN_W), kvx.reshape(b, l, 2 * KV_DUP_W)
        hy = (hy_conv_w[layer], conv_b, filt_params, d_skip, dft_x)
        ya, yh, yp = _mixers(zx, qx, kvx, kvc, sink[layer], True, hy, pool_w_b[layer], pool_scale[layer])
        xn, h2 = _merge_call(ya, yh, yp, gx, x2, ga1, norm2_g[layer], sc2, sh2, *merge_w, l)
        nxt = None if last else (norm1_g[layer + 1], *reversed(chunks(layer + 1, 0, b)[:2]))
        x2, hx = _mlp_call(h2, w1_b, w2_b, layer, xn, ga2, nxt, l)

        if not last:
            hyc = (hy_conv_w[layer], conv_b, filt_params, d_skip, dft_c)
            ya, yh, yp = _mixers(zc, qc, None, kvc, sink[layer], False, hyc, pool_w_b[layer], pool_scale[layer])
            cn, h2c = _merge_call(ya, yh, yp, gc, c2, cga1, norm2_g[layer], csc2, csh2, *merge_w, lc)
            nxt = (norm1_g[layer + 1], *reversed(chunks(layer + 1, b, b + 1)[:2]))
            c2, hc = _mlp_call(h2c, w1_b, w2_b, layer, cn, cga2, nxt, lc)

    return x2.reshape(b, l, d)
```

```python
import functools
import math

import jax
import jax.numpy as jnp
from jax import lax
from jax.experimental import pallas as pl
from jax.experimental.pallas import tpu as pltpu

D_MODEL = 2048
DEPTH = 2
GRID_W = 64
EPS = 1e-6
NEG_INF = -1e30

N_HEADS = 16
N_KV_HEADS = 4
GQA_GROUP = N_HEADS // N_KV_HEADS
HEAD_DIM = 64
ATTN_W = N_HEADS * HEAD_DIM
KV_W = N_KV_HEADS * HEAD_DIM
WINDOW = 128
ROPE_FREQS = HEAD_DIM // 4
ROPE_BASE = 10000.0

HYENA_W = D_MODEL // 4
HYENA_ORDER = 2
FILTER_BANDS = 16
FILTER_EMB = 1 + 2 * FILTER_BANDS
FILTER_HIDDEN = 64
FILTER_INNER = 2
DECAY_TARGET = 1e-2
FAST_DECAY_PCT = 0.3
SLOW_DECAY_PCT = 1.5

POOL_W = D_MODEL // 4
POOL_WINDOWS = (2, 4, 8, 16)
POOL_GROUP = POOL_W // len(POOL_WINDOWS)

N_BRANCH = 3
D_FF = 4 * D_MODEL

Q_OFF = 0
K_OFF = Q_OFF + ATTN_W
V_OFF = K_OFF + KV_W
HY_OFF = V_OFF + KV_W
POOL_OFF = HY_OFF + 3 * HYENA_W
GATE_OFF = POOL_OFF + POOL_W
IN_W = GATE_OFF + N_BRANCH * D_MODEL

V7X_LANES = 128
V7X_VMEM_LIMIT = 60 * 1024 * 1024
KV_DUP_W = N_KV_HEADS * V7X_LANES
MOD_ROWS = 24

F32 = jnp.float32
BF16 = jnp.bfloat16
HIGHEST = lax.Precision.HIGHEST


def _params(*semantics):
    return pltpu.CompilerParams(dimension_semantics=semantics, vmem_limit_bytes=V7X_VMEM_LIMIT)


def _const_spec(shape):
    zeros = (0,) * len(shape)
    return pl.BlockSpec(shape, lambda *_: zeros, pipeline_mode=pl.Buffered(1))


def _mod_spec(arr, rows_per_mod_tile):
    d = arr.shape[-1]
    if arr.shape[0] == 1:
        return pl.BlockSpec((1, 1, d), lambda i, *_: (0, 0, 0))
    return pl.BlockSpec((1, 1, d), lambda i, *_: (i // rows_per_mod_tile, 0, 0))


def _norm_mod(xf, g, sc, sh):
    y = xf * lax.rsqrt(jnp.mean(xf * xf, axis=-1, keepdims=True) + EPS)
    return (y * g) * (1.0 + sc) + sh


def _mod_body(c_ref, w_ref, b_ref, o_ref):
    c = c_ref[...]
    s = c * jax.nn.sigmoid(c)
    o_ref[0] = jnp.dot(s.astype(BF16), w_ref[0].astype(BF16), preferred_element_type=F32) + b_ref[0]


def _modulation(cc, w_mod, b_mod):
    depth, d, n = w_mod.shape
    tn = 1024
    return pl.pallas_call(
        _mod_body,
        grid=(depth, n // tn),
        in_specs=[
            pl.BlockSpec((MOD_ROWS, d), lambda l, j: (0, 0)),
            pl.BlockSpec((1, d, tn), lambda l, j: (l, 0, j)),
            pl.BlockSpec((1, 1, tn), lambda l, j: (l, 0, j)),
        ],
        out_specs=pl.BlockSpec((1, MOD_ROWS, tn), lambda l, j: (l, 0, j)),
        out_shape=jax.ShapeDtypeStruct((depth, MOD_ROWS, n), F32),
        compiler_params=_params("arbitrary", "arbitrary"),
        name="modulation",
    )(cc, w_mod, b_mod.reshape(depth, 1, n))


def _norm_body(x_ref, g_ref, sc_ref, sh_ref, o_ref):
    o_ref[...] = _norm_mod(x_ref[...], g_ref[...], sc_ref[0], sh_ref[0]).astype(o_ref.dtype)


def _norm_call(x2, g, sc, sh, rows_per_batch):
    m, d = x2.shape
    tm = min(1024, rows_per_batch if sc.shape[0] > 1 else m)
    return pl.pallas_call(
        _norm_body,
        grid=(m // tm,),
        in_specs=[
            pl.BlockSpec((tm, d), lambda i: (i, 0)),
            pl.BlockSpec((1, d), lambda i: (0, 0)),
            _mod_spec(sc, rows_per_batch // tm),
            _mod_spec(sh, rows_per_batch // tm),
        ],
        out_specs=pl.BlockSpec((tm, d), lambda i: (i, 0)),
        out_shape=jax.ShapeDtypeStruct((m, d), BF16),
        compiler_params=_params("arbitrary"),
        name="norm_mod",
    )(x2, g.reshape(1, d), sc, sh)


def _proj_body(a_ref, w_ref, o_ref):
    o_ref[...] = jnp.dot(a_ref[...], w_ref[...], preferred_element_type=F32).astype(o_ref.dtype)


def _proj_call(a, w, layer, col0, n, out_dtype):
    m, k = a.shape
    tm = min(2048, m)
    tn = 512
    c0 = col0 // tn
    return pl.pallas_call(
        _proj_body,
        grid=(m // tm, n // tn),
        in_specs=[
            pl.BlockSpec((tm, k), lambda i, j: (i, 0)),
            pl.BlockSpec((None, k, tn), lambda i, j: (layer, 0, c0 + j)),
        ],
        out_specs=pl.BlockSpec((tm, tn), lambda i, j: (i, j)),
        out_shape=jax.ShapeDtypeStruct((m, n), out_dtype),
        compiler_params=_params("arbitrary", "arbitrary"),
        name="in_proj",
    )(a, w)


Q_SLABS = ATTN_W // V7X_LANES
QK_SLABS = Q_SLABS + KV_W // V7X_LANES
QKV_SLABS = QK_SLABS + KV_W // V7X_LANES


def _gates_qkv_body(a_ref, w_ref, z_ref, gain_ref, cos_ref, sup_ref, sdn_ref, g_ref, q_ref, kv_ref):
    j = pl.program_id(1)
    zg = jnp.dot(a_ref[...], w_ref[...], preferred_element_type=F32)
    g_ref[...] = (0.5 * jnp.tanh(0.5 * zg) + 0.5).astype(g_ref.dtype)

    x = z_ref[...]
    low = lax.broadcasted_iota(jnp.int32, (1, V7X_LANES), 1) < HEAD_DIM
    x2 = x * x
    ss = jnp.where(low, jnp.sum(jnp.where(low, x2, 0.0), axis=-1, keepdims=True),
                   jnp.sum(jnp.where(low, 0.0, x2), axis=-1, keepdims=True))
    inv = jnp.where(j < QK_SLABS, lax.rsqrt(ss * (1.0 / HEAD_DIM) + EPS), 1.0)
    y = _rope((x * inv) * gain_ref[0], cos_ref[...], sup_ref[...], sdn_ref[...])
    da, db = _dup_pair(y, low)

    @pl.when(j < Q_SLABS)
    def _():
        q_ref[...] = y.astype(q_ref.dtype)

    @pl.when(j >= Q_SLABS)
    def _():
        kv_ref[:, 0:V7X_LANES] = da.astype(kv_ref.dtype)
        kv_ref[:, V7X_LANES:2 * V7X_LANES] = db.astype(kv_ref.dtype)


def _gates_qkv_call(a, w, layer, z2, gq, gk, rope_tabs, seq_len):
    m, k = a.shape
    d = D_MODEL
    tm = min(2048, m)
    tn = 512
    assert N_BRANCH * d // tn == QKV_SLABS and tm % seq_len == 0
    c0 = GATE_OFF // tn
    ones = jnp.ones((tm, V7X_LANES), F32)
    zeros = jnp.zeros((tm, V7X_LANES), F32)
    if rope_tabs is None:
        cos, sup, sdn = ones[None], zeros[None], zeros[None]
        tab_map = lambda i, j: (0, 0, 0)
    else:
        rep = lambda t: jnp.tile(t, (tm // seq_len, 1))
        cos, sup, sdn = (jnp.stack([rep(t), ident]) for t, ident in zip(rope_tabs, (ones, zeros, zeros)))
        tab_map = lambda i, j: ((j >= QK_SLABS).astype(jnp.int32), 0, 0)
    gains = jnp.stack([jnp.tile(gq, 2) * HEAD_DIM ** -0.5, jnp.tile(gk, 2), jnp.ones((V7X_LANES,), F32)])
    gain_map = lambda i, j: ((j >= Q_SLABS).astype(jnp.int32) + (j >= QK_SLABS).astype(jnp.int32), 0, 0)
    tab_spec = pl.BlockSpec((None, tm, V7X_LANES), tab_map)
    return pl.pallas_call(
        _gates_qkv_body,
        grid=(m // tm, QKV_SLABS),
        in_specs=[
            pl.BlockSpec((tm, k), lambda i, j: (i, 0)),
            pl.BlockSpec((None, k, tn), lambda i, j: (layer, 0, c0 + j)),
            pl.BlockSpec((tm, V7X_LANES), lambda i, j: (i, j)),
            pl.BlockSpec((None, 1, V7X_LANES), gain_map),
            tab_spec, tab_spec, tab_spec,
        ],
        out_specs=[
            pl.BlockSpec((tm, tn), lambda i, j: (i, j)),
            pl.BlockSpec((tm, V7X_LANES), lambda i, j: (i, jnp.minimum(j, Q_SLABS - 1))),
            pl.BlockSpec((tm, 2 * V7X_LANES), lambda i, j: (i, jnp.maximum(j - Q_SLABS, 0))),
        ],
        out_shape=[
            jax.ShapeDtypeStruct((m, N_BRANCH * d), BF16),
            jax.ShapeDtypeStruct((m, ATTN_W), BF16),
            jax.ShapeDtypeStruct((m, 2 * KV_DUP_W), BF16),
        ],
        compiler_params=_params("arbitrary", "arbitrary"),
        name="gates_qkv",
    )(a, w, z2, gains.reshape(3, 1, V7X_LANES), cos, sup, sdn)


def _pair_block_diag():
    r = lax.broadcasted_iota(jnp.int32, (V7X_LANES, V7X_LANES), 0) // HEAD_DIM
    c = lax.broadcasted_iota(jnp.int32, (V7X_LANES, V7X_LANES), 1) // HEAD_DIM
    return (r == c).astype(F32)


def _head_norm(x, g, bd):
    ss = jnp.dot(x * x, bd, precision=HIGHEST, preferred_element_type=F32)
    return (x * lax.rsqrt(ss * (1.0 / HEAD_DIM) + EPS)) * g


def _rope(x, cos, sin_up, sin_dn):
    up = pltpu.roll(x, V7X_LANES - ROPE_FREQS, 1)
    dn = pltpu.roll(x, ROPE_FREQS, 1)
    return x * cos + up * sin_up + dn * sin_dn


def _dup_pair(x, low):
    r = pltpu.roll(x, HEAD_DIM, 1)
    return jnp.where(low, x, r), jnp.where(low, r, x)


def _kv_body(z_ref, gk_ref, kv_ref):
    bd = _pair_block_diag()
    low = lax.broadcasted_iota(jnp.int32, (1, V7X_LANES), 1) < HEAD_DIM
    for s in range(2 * KV_W // V7X_LANES):
        x = z_ref[:, s * V7X_LANES:(s + 1) * V7X_LANES]
        if s < KV_W // V7X_LANES:
            x = _head_norm(x, gk_ref[...], bd)
        a, b = _dup_pair(x, low)
        base = 2 * s * V7X_LANES
        kv_ref[:, base:base + V7X_LANES] = a.astype(kv_ref.dtype)
        kv_ref[:, base + V7X_LANES:base + 2 * V7X_LANES] = b.astype(kv_ref.dtype)


def _kv_call(z2, gk):
    m, nz = z2.shape
    tm = min(512, m)
    return pl.pallas_call(
        _kv_body,
        grid=(m // tm,),
        in_specs=[pl.BlockSpec((tm, nz), lambda i: (i, 0)),
                  pl.BlockSpec((1, V7X_LANES), lambda i: (0, 0))],
        out_specs=pl.BlockSpec((tm, 2 * KV_DUP_W), lambda i: (i, 0)),
        out_shape=jax.ShapeDtypeStruct((m, 2 * KV_DUP_W), BF16),
        compiler_params=_params("arbitrary"),
        name="kv_prep",
    )(z2, jnp.tile(gk, 2).reshape(1, V7X_LANES))


def _rope_tables(l):
    rows = l // GRID_W
    row = jnp.repeat(jnp.arange(rows, dtype=F32), GRID_W)
    col = jnp.tile(jnp.arange(GRID_W, dtype=F32), rows)
    inv = ROPE_BASE ** (-jnp.arange(ROPE_FREQS, dtype=F32) / ROPE_FREQS)
    ang = jnp.stack([row[:, None] * inv, col[:, None] * inv], axis=1)
    cos, sin = jnp.cos(ang), jnp.sin(ang)
    zero = jnp.zeros_like(sin)
    cos_h = jnp.stack([cos, cos], axis=2).reshape(l, HEAD_DIM)
    sup_h = jnp.stack([-sin, zero], axis=2).reshape(l, HEAD_DIM)
    sdn_h = jnp.stack([zero, sin], axis=2).reshape(l, HEAD_DIM)
    return tuple(jnp.tile(t, (1, 2)) for t in (cos_h, sup_h, sdn_h))


def _attn_body(*refs, local, tq):
    it = iter(refs)
    sink_ref = next(it)
    q_ref = next(it)
    if local:
        kp_ref, kc_ref, kn_ref, vp_ref, vc_ref, vn_ref = (next(it) for _ in range(6))
    kx_ref, vx_ref = next(it), next(it)
    o_ref = next(it)

    i = pl.program_id(1)
    nb = pl.num_programs(1)
    low = lax.broadcasted_iota(jnp.int32, (1, V7X_LANES), 1) < HEAD_DIM
    rows = GQA_GROUP * tq
    if local:
        qi = lax.broadcasted_iota(jnp.int32, (rows, tq), 0) % tq
        kj = lax.broadcasted_iota(jnp.int32, (rows, tq), 1)
        mask_prev = (kj >= qi) & (i > 0)
        mask_next = (kj <= qi) & (i < nb - 1)
    row_head = lax.broadcasted_iota(jnp.int32, (rows, 1), 0) // tq
    zero = jnp.zeros((), q_ref.dtype)

    for e, h in [(e, h) for e in range(q_ref.shape[0]) for h in range(N_KV_HEADS)]:
        hs = slice(h * V7X_LANES, (h + 1) * V7X_LANES)
        qa = q_ref[e, :, 2 * h * V7X_LANES:(2 * h + 1) * V7X_LANES]
        qb = q_ref[e, :, (2 * h + 1) * V7X_LANES:(2 * h + 2) * V7X_LANES]
        qs = jnp.concatenate([jnp.where(low, qa, zero), jnp.where(low, zero, qa),
                              jnp.where(low, qb, zero), jnp.where(low, zero, qb)], axis=0)
        kparts, vparts, masks = [kx_ref[e, :, hs]], [vx_ref[e, :, hs]], {}
        if local:
            kparts = [kp_ref[e, :, hs], kc_ref[e, :, hs], kn_ref[e, :, hs]] + kparts
            vparts = [vp_ref[e, :, hs], vc_ref[e, :, hs], vn_ref[e, :, hs]] + vparts
            masks = {0: mask_prev, 2: mask_next}
        k_all = jnp.concatenate(kparts, axis=0)
        v_all = jnp.concatenate(vparts, axis=0)

        sink = jnp.zeros((rows, 1), F32)
        for g in range(GQA_GROUP):
            sink = jnp.where(row_head == g, sink_ref[GQA_GROUP * h + g], sink)
        s_all = lax.dot_general(qs, k_all, (((1,), (1,)), ((), ())), preferred_element_type=F32)
        chunks = []
        for c in range(k_all.shape[0] // tq):
            s = s_all[:, c * tq:(c + 1) * tq]
            chunks.append(jnp.where(masks[c], s, NEG_INF) if c in masks else s)
        m = jnp.maximum(sink, jnp.max(functools.reduce(jnp.maximum, chunks), axis=-1, keepdims=True))
        probs = [jnp.exp(s - m) for s in chunks]
        denom = jnp.exp(sink - m) + jnp.sum(functools.reduce(jnp.add, probs), axis=-1, keepdims=True)
        p_all = jnp.concatenate([p.astype(v_all.dtype) for p in probs], axis=1)
        o = jnp.dot(p_all, v_all, preferred_element_type=F32) / denom
        oa = jnp.where(low, o[0:tq], o[tq:2 * tq])
        ob = jnp.where(low, o[2 * tq:3 * tq], o[3 * tq:4 * tq])
        o_ref[e, :, 2 * h * V7X_LANES:(2 * h + 1) * V7X_LANES] = oa.astype(o_ref.dtype)
        o_ref[e, :, (2 * h + 1) * V7X_LANES:(2 * h + 2) * V7X_LANES] = ob.astype(o_ref.dtype)


def _attn_call(q, kv, kvx, sink, local):
    b, l, _ = q.shape
    lx = kvx.shape[1]
    tq = WINDOW
    nb = l // tq
    ne = math.gcd(b, 4)
    blk = lambda w: (ne, tq, w)
    in_specs = [pl.BlockSpec(memory_space=pltpu.SMEM),
                pl.BlockSpec(blk(ATTN_W), lambda bi, i: (bi, i, 0))]
    args = [sink, q]
    if local:
        for half in (0, 1):
            for mp in (lambda bi, i, half=half: (bi, jnp.maximum(i - 1, 0), half),
                       lambda bi, i, half=half: (bi, i, half),
                       lambda bi, i, half=half: (bi, jnp.minimum(i + 1, nb - 1), half)):
                in_specs.append(pl.BlockSpec(blk(KV_DUP_W), mp))
                args.append(kv)
    for half in (0, 1):
        in_specs.append(pl.BlockSpec((ne, lx, KV_DUP_W), lambda bi, i, half=half: (bi, 0, half)))
        args.append(kvx)
    return pl.pallas_call(
        functools.partial(_attn_body, local=local, tq=tq),
        grid=(b // ne, nb),
        in_specs=in_specs,
        out_specs=pl.BlockSpec(blk(ATTN_W), lambda bi, i: (bi, i, 0)),
        out_shape=jax.ShapeDtypeStruct((b, l, ATTN_W), BF16),
        compiler_params=_params("arbitrary", "arbitrary"),
        name="attention",
    )(*args)


def _filter_body(z_ref, w0_ref, b0_ref, w1_ref, b1_ref, fr_ref, w2_ref, dec_ref, o_ref):
    fr = fr_ref[...]
    dot = functools.partial(jnp.dot, precision=HIGHEST, preferred_element_type=F32)
    h = jnp.sin(fr * (dot(z_ref[...], w0_ref[...]) + b0_ref[...]))
    for i in range(FILTER_INNER):
        h = jnp.sin(fr * (dot(h, w1_ref[i]) + b1_ref[i]))
    dec = dec_ref[...]
    for s in range(2 * HYENA_ORDER):
        sl = slice(s * HYENA_W, (s + 1) * HYENA_W)
        o_ref[:, sl] = dot(h, w2_ref[:, sl]) * dec


def _filter_features(l):
    t = jnp.linspace(0.0, 1.0, l, dtype=F32)[:, None]
    w = 2.0 * math.pi * jnp.arange(l, dtype=F32)[:, None] / l
    bands = jnp.linspace(1e-4, FILTER_BANDS - 1, FILTER_BANDS, dtype=F32)[None, :]
    z = jnp.concatenate([t, jnp.cos(bands * w), -jnp.sin(bands * w)], axis=-1)
    deltas = jnp.linspace(math.log(DECAY_TARGET) / SLOW_DECAY_PCT, math.log(DECAY_TARGET) / FAST_DECAY_PCT,
                          HYENA_W, dtype=F32)
    decay = jnp.exp(-t * jnp.abs(deltas))
    return jnp.pad(z, ((0, 0), (0, V7X_LANES - FILTER_EMB))), decay


def _filter_call(l, w0, b0, w1, b1, freq, w2):
    zfeat, decay = _filter_features(l)
    w0p = jnp.pad(w0, ((0, V7X_LANES - FILTER_EMB), (0, 0)))
    tl = min(512, l)
    nf = 2 * HYENA_ORDER * HYENA_W
    full = lambda shape: pl.BlockSpec(shape, lambda i: (0,) * len(shape))
    return pl.pallas_call(
        _filter_body,
        grid=(l // tl,),
        in_specs=[
            pl.BlockSpec((tl, V7X_LANES), lambda i: (i, 0)),
            full((V7X_LANES, FILTER_HIDDEN)),
            full((1, FILTER_HIDDEN)),
            full((FILTER_INNER, FILTER_HIDDEN, FILTER_HIDDEN)),
            full((FILTER_INNER, 1, FILTER_HIDDEN)),
            full((1, FILTER_HIDDEN)),
            full((FILTER_HIDDEN, nf)),
            pl.BlockSpec((tl, HYENA_W), lambda i: (i, 0)),
        ],
        out_specs=pl.BlockSpec((tl, nf), lambda i: (i, 0)),
        out_shape=jax.ShapeDtypeStruct((l, nf), F32),
        compiler_params=_params("arbitrary"),
        name="hyena_filter",
    )(zfeat, w0p, b0.reshape(1, -1), w1, b1.reshape(FILTER_INNER, 1, -1), freq.reshape(1, -1), w2, decay)


def _hyena_blocks(l):
    return max(1, min(4, l // V7X_LANES))


def _dft_matrices(blk):
    n = 2 * blk
    r = jnp.arange(blk, dtype=jnp.int32)
    ang = ((r[:, None] * r[None, :]) % n).astype(F32) * (2.0 * math.pi / n)
    return jnp.cos(ang).astype(BF16), jnp.sin(ang).astype(BF16)


def _alternating(l):
    row = lax.broadcasted_iota(jnp.int32, (l, 1), 0)
    return row, jnp.where(row % 2 == 0, 1.0, -1.0).astype(F32)


def _spectrum_body(hf_ref, hb_ref, fc_ref, fs_ref, ka_ref, kb_ref, kn_ref, *, nblk):
    l = hf_ref.shape[0]
    b = l // nblk
    n = 2 * b
    row = lax.broadcasted_iota(jnp.int32, (l, 1), 0)
    _, alt = _alternating(b)
    hf = hf_ref[...]
    hbs = jnp.where(row == 0, 0.0, pltpu.roll(hb_ref[...], 1, 0))
    fc, fs = fc_ref[...], fs_ref[...]

    def transforms(h):
        out = []
        for k in range(nblk):
            hk = h[k * b:(k + 1) * b]
            hk16 = hk.astype(BF16)
            out.append(dict(
                c=jnp.dot(fc, hk16, preferred_element_type=F32),
                s=jnp.dot(fs, hk16, preferred_element_type=F32),
                first16=hk16[0:1].astype(F32),
                first=hk[0:1],
                alt=jnp.sum(hk * alt, axis=0, keepdims=True)))
        return out

    tf, tb = transforms(hf), transforms(hbs)
    brow = lax.broadcasted_iota(jnp.int32, (b, 1), 0)
    w_re = jnp.where(brow == 0, 1.0 / n, 2.0 / n)
    for d in range(-(nblk - 1), nblk):
        idx = d + nblk - 1
        if d == 0:
            kre = tf[0]["c"] + tb[0]["c"]
            kim = tb[0]["s"] - tf[0]["s"]
            kn = tf[0]["alt"] + tb[0]["alt"]
        else:
            t, e, sg = (tf, d, -1.0) if d > 0 else (tb, -d, 1.0)
            kre = t[e]["c"] + alt * (t[e - 1]["c"] - t[e - 1]["first16"])
            kim = sg * (t[e]["s"] + alt * t[e - 1]["s"])
            kn = t[e]["alt"] + t[e - 1]["alt"] - t[e - 1]["first"]
        ka_ref[0, idx] = (kre * w_re).astype(ka_ref.dtype)
        kb_ref[0, idx] = (kim * (2.0 / n)).astype(kb_ref.dtype)
        kn_ref[0, idx] = kn * (1.0 / n)


def _spectrum_call(filt, fc, fs, nblk, tc):
    l = filt.shape[0]
    b = l // nblk
    nct = HYENA_W // tc
    nlag = 2 * nblk - 1
    return pl.pallas_call(
        functools.partial(_spectrum_body, nblk=nblk),
        grid=(HYENA_ORDER, nct),
        in_specs=[
            pl.BlockSpec((l, tc), lambda o, c: (0, 2 * nct * o + c)),
            pl.BlockSpec((l, tc), lambda o, c: (0, 2 * nct * o + nct + c)),
            _const_spec((b, b)),
            _const_spec((b, b)),
        ],
        out_specs=[
            pl.BlockSpec((1, nlag, b, tc), lambda o, c: (o, 0, 0, c)),
            pl.BlockSpec((1, nlag, b, tc), lambda o, c: (o, 0, 0, c)),
            pl.BlockSpec((1, nlag, 1, tc), lambda o, c: (o, 0, 0, c)),
        ],
        out_shape=[
            jax.ShapeDtypeStruct((HYENA_ORDER, nlag, b, HYENA_W), BF16),
            jax.ShapeDtypeStruct((HYENA_ORDER, nlag, b, HYENA_W), BF16),
            jax.ShapeDtypeStruct((HYENA_ORDER, nlag, 1, HYENA_W), F32),
        ],
        compiler_params=_params("arbitrary", "arbitrary"),
        name="hyena_spectrum",
    )(filt, filt, fc, fs)


def _conv3(x, w_ref, b_ref, row):
    l = x.shape[0]
    xm = jnp.where(row == 0, 0.0, pltpu.roll(x, 1, 0))
    xp = jnp.where(row == l - 1, 0.0, pltpu.roll(x, l - 1, 0))
    return xm * w_ref[0:1, :] + x * w_ref[1:2, :] + xp * w_ref[2:3, :] + b_ref[...]


def _fftconv_body(*refs, conv_u, nblk):
    it = iter(refs)
    u_ref = next(it)
    if conv_u:
        uw_ref, ub_ref = next(it), next(it)
    g_ref, gw_ref, gb_ref = next(it), next(it), next(it)
    ka_ref, kb_ref, kn_ref, d_ref, fc_ref, fs_ref, o_ref = (next(it) for _ in range(7))

    l = u_ref.shape[1]
    b = l // nblk
    row = lax.broadcasted_iota(jnp.int32, (l, 1), 0)
    _, alt = _alternating(b)
    u = u_ref[0].astype(F32)
    if conv_u:
        u = _conv3(u, uw_ref, ub_ref, row)
    gate = _conv3(g_ref[0], gw_ref, gb_ref, row)
    fc, fs = fc_ref[...], fs_ref[...]

    ps, qs, ns = [], [], []
    for j in range(nblk):
        uj = u[j * b:(j + 1) * b]
        uj16 = uj.astype(BF16)
        ps.append(jnp.dot(fc, uj16, preferred_element_type=F32).astype(BF16))
        qs.append(jnp.dot(fs, uj16, preferred_element_type=F32).astype(BF16))
        ns.append(jnp.sum(uj * alt, axis=0, keepdims=True))
    for i in range(nblk):
        r = t = nyq = None
        for j in range(nblk):
            lag = i - j + nblk - 1
            ka, kb = ka_ref[0, lag], kb_ref[0, lag]
            dr = ps[j] * ka + qs[j] * kb
            dt = qs[j] * ka - ps[j] * kb
            dn = ns[j] * kn_ref[0, lag]
            r, t, nyq = (dr, dt, dn) if j == 0 else (r + dr, t + dt, nyq + dn)
        y = jnp.dot(fc, r, preferred_element_type=F32) + jnp.dot(fs, t, preferred_element_type=F32)
        rows = slice(i * b, (i + 1) * b)
        y = y + alt * nyq + u[rows] * d_ref[0]
        o_ref[0, rows, :] = (gate[rows] * y).astype(o_ref.dtype)


def _fftconv_call(u, u_col0, z, gate_col0, conv_w, conv_b, spectra, d_skip, order, fc, fs, nblk, tc, out_dtype):
    b, l, _ = z.shape
    conv_u = u is z
    nct = HYENA_W // tc
    ka, kb, kn = spectra
    blk = l // nblk
    nlag = 2 * nblk - 1
    col = lambda c0: (lambda c, bi: (bi, 0, c0 // tc + c))
    wcol = lambda c0: (lambda c, bi: (0, (c0 - HY_OFF) // tc + c))
    in_specs = [pl.BlockSpec((1, l, tc), col(u_col0))]
    args = [u]
    if conv_u:
        in_specs += [pl.BlockSpec((3, tc), wcol(u_col0)), pl.BlockSpec((1, tc), wcol(u_col0))]
        args += [conv_w, conv_b]
    in_specs += [pl.BlockSpec((1, l, tc), col(gate_col0)),
                 pl.BlockSpec((3, tc), wcol(gate_col0)), pl.BlockSpec((1, tc), wcol(gate_col0))]
    args += [z, conv_w, conv_b]
    spec = lambda rows: pl.BlockSpec((1, nlag, rows, tc), lambda c, bi: (order, 0, 0, c),
                                     pipeline_mode=pl.Buffered(1))
    in_specs += [spec(blk), spec(blk), spec(1),
                 pl.BlockSpec((1, 1, tc), lambda c, bi: (order, 0, c), pipeline_mode=pl.Buffered(1)),
                 _const_spec((blk, blk)), _const_spec((blk, blk))]
    args += [ka, kb, kn, d_skip, fc, fs]
    return pl.pallas_call(
        functools.partial(_fftconv_body, conv_u=conv_u, nblk=nblk),
        grid=(nct, b),
        in_specs=in_specs,
        out_specs=pl.BlockSpec((1, l, tc), lambda c, bi: (bi, 0, c)),
        out_shape=jax.ShapeDtypeStruct((b, l, HYENA_W), out_dtype),
        compiler_params=_params("arbitrary", "arbitrary"),
        name="hyena_conv",
    )(*args)


def _pool_body(x_ref, w_ref, s_ref, o_ref):
    l = x_ref.shape[1]
    row = lax.broadcasted_iota(jnp.int32, (l, 1), 0)
    for g, win in enumerate(POOL_WINDOWS):
        half = win // 2
        sl = slice(g * POOL_GROUP, (g + 1) * POOL_GROUP)
        x = x_ref[0, :, sl]

        def shifted(a, k):
            return jnp.where((row >= k) & (row < l + k), pltpu.roll(a, k % l, 0), 0.0)

        back = fwd = x
        span = 1
        while span < half:
            back = back + shifted(back, span)
            fwd = fwd + shifted(fwd, -span)
            span *= 2
        acc = shifted(back, 1) + fwd
        cnt = (jnp.minimum(row + half, l) - jnp.maximum(row - half, 0)).astype(F32)
        d = acc / cnt - x
        y = jnp.dot(d.astype(BF16), w_ref[g], preferred_element_type=F32)
        o_ref[0, :, sl] = (y * s_ref[:, sl]).astype(o_ref.dtype)


def _pool_call(z, w_grp, scale):
    b, l, _ = z.shape
    ng = len(POOL_WINDOWS)
    return pl.pallas_call(
        _pool_body,
        grid=(b,),
        in_specs=[
            pl.BlockSpec((1, l, POOL_W), lambda bi: (bi, 0, POOL_OFF // POOL_W)),
            pl.BlockSpec((ng, POOL_GROUP, POOL_GROUP), lambda bi: (0, 0, 0)),
            pl.BlockSpec((1, POOL_W), lambda bi: (0, 0)),
        ],
        out_specs=pl.BlockSpec((1, l, POOL_W), lambda bi: (bi, 0, 0)),
        out_shape=jax.ShapeDtypeStruct((b, l, POOL_W), BF16),
        compiler_params=_params("arbitrary"),
        name="pool",
    )(z, w_grp, scale.reshape(1, POOL_W))


def _merge_body(ya_ref, yh_ref, yp_ref, gt_ref, x_ref, ga_ref, g2_ref, sc_ref, sh_ref,
                wa_ref, wh_ref, wp_ref, wo_ref, xn_ref, h2_ref):
    d = x_ref.shape[1]
    cj = 512
    ya, yh, yp = ya_ref[...], yh_ref[...], yp_ref[...]
    acc = jnp.zeros(x_ref.shape, F32)
    for j in range(d // cj):
        sl = slice(j * cj, (j + 1) * cj)
        gate = lambda br: gt_ref[:, br * d + j * cj:br * d + (j + 1) * cj].astype(F32)
        m = (gate(0) * jnp.dot(ya, wa_ref[:, sl], preferred_element_type=F32)
             + gate(1) * jnp.dot(yh, wh_ref[:, sl], preferred_element_type=F32)
             + gate(2) * jnp.dot(yp, wp_ref[:, sl], preferred_element_type=F32))
        acc = acc + jnp.dot(m.astype(BF16), wo_ref[sl, :], preferred_element_type=F32)
    xn = x_ref[...] + ga_ref[0] * acc
    xn_ref[...] = xn
    h2_ref[...] = _norm_mod(xn, g2_ref[...], sc_ref[0], sh_ref[0]).astype(h2_ref.dtype)


def _merge_call(ya, yh, yp, gates, x2, ga1, g2, sc2, sh2, wa, wh, wp, wo, layer, rows_per_batch):
    m, d = x2.shape
    tm = min(512, rows_per_batch)
    rpt = rows_per_batch // tm
    rows = lambda w: pl.BlockSpec((tm, w), lambda i: (i, 0))
    weight = lambda w: pl.BlockSpec((None,) + w.shape[1:], lambda i: (layer, 0, 0), pipeline_mode=pl.Buffered(1))
    return pl.pallas_call(
        _merge_body,
        grid=(m // tm,),
        in_specs=[rows(ATTN_W), rows(HYENA_W), rows(POOL_W), rows(N_BRANCH * d), rows(d),
                  _mod_spec(ga1, rpt), pl.BlockSpec((1, d), lambda i: (0, 0)),
                  _mod_spec(sc2, rpt), _mod_spec(sh2, rpt),
                  weight(wa), weight(wh), weight(wp), weight(wo)],
        out_specs=[rows(d), rows(d)],
        out_shape=[jax.ShapeDtypeStruct((m, d), F32), jax.ShapeDtypeStruct((m, d), BF16)],
        compiler_params=_params("arbitrary"),
        name="merge",
    )(ya, yh, yp, gates, x2, ga1, g2.reshape(1, d), sc2, sh2, wa, wh, wp, wo)


def _mlp_body(*refs, has_next):
    it = iter(refs)
    h_ref, w1_ref, w2_ref, x_ref, ga_ref = (next(it) for _ in range(5))
    if has_next:
        gn_ref, sc_ref, sh_ref = next(it), next(it), next(it)
    o_ref = next(it)
    hn_ref = next(it) if has_next else None

    f = pl.program_id(1)
    last = pl.num_programs(1) - 1
    tm, d = o_ref.shape
    cn = 512
    rb = min(256, tm)

    def hidden(rows):
        a = jnp.dot(h_ref[rows, :], w1_ref[...], preferred_element_type=F32)
        return jnp.square(jnp.maximum(a, 0.0)).astype(BF16)

    @pl.when(f == 0)
    def _():
        a = hidden(slice(None))
        for n0 in range(0, d, cn):
            o_ref[:, n0:n0 + cn] = jnp.dot(a, w2_ref[:, n0:n0 + cn], preferred_element_type=F32)

    @pl.when((f > 0) & (f < last))
    def _():
        a = hidden(slice(None))
        for n0 in range(0, d, cn):
            o_ref[:, n0:n0 + cn] += jnp.dot(a, w2_ref[:, n0:n0 + cn], preferred_element_type=F32)

    @pl.when(f == last)
    def _():
        for r0 in range(0, tm, rb):
            rows = slice(r0, r0 + rb)
            acc = o_ref[rows, :] + jnp.dot(hidden(rows), w2_ref[...], preferred_element_type=F32)
            xo = x_ref[rows, :] + ga_ref[0] * acc
            o_ref[rows, :] = xo
            if has_next:
                hn_ref[rows, :] = _norm_mod(xo, gn_ref[...], sc_ref[0], sh_ref[0]).astype(hn_ref.dtype)


def _mlp_call(h2, w1, w2, layer, xn, ga2, nxt, rows_per_batch):
    m, d = xn.shape
    ff = w1.shape[2]
    tm, tf = min(1024, rows_per_batch if ga2.shape[0] > 1 else m), 512
    rpt = max(rows_per_batch // tm, 1)
    assert ff // tf >= 2
    has_next = nxt is not None
    rows = pl.BlockSpec((tm, d), lambda i, f: (i, 0))
    in_specs = [rows, pl.BlockSpec((None, d, tf), lambda i, f: (layer, 0, f)),
                pl.BlockSpec((None, tf, d), lambda i, f: (layer, f, 0)),
                pl.BlockSpec((tm, d), lambda i, f: (i, 0), pipeline_mode=pl.Buffered(1)), _mod_spec(ga2, rpt)]
    args = [h2, w1, w2, xn, ga2]
    out_specs = [rows]
    out_shape = [jax.ShapeDtypeStruct((m, d), F32)]
    if has_next:
        gn, scn, shn = nxt
        in_specs += [pl.BlockSpec((1, d), lambda i, f: (0, 0)), _mod_spec(scn, rpt), _mod_spec(shn, rpt)]
        args += [gn.reshape(1, d), scn, shn]
        out_specs.append(rows)
        out_shape.append(jax.ShapeDtypeStruct((m, d), BF16))
    outs = pl.pallas_call(
        functools.partial(_mlp_body, has_next=has_next),
        grid=(m // tm, ff // tf),
        in_specs=in_specs,
        out_specs=out_specs,
        out_shape=out_shape,
        compiler_params=_params("arbitrary", "arbitrary"),
        name="mlp",
    )(*args)
    return (outs[0], outs[1]) if has_next else (outs[0], None)


def _mixers(z, q, kv, kvx, sink, local, hy, pool_w, pool_scale):
    b, l, _ = z.shape
    y_att = _attn_call(q, kv, kvx, sink, local)
    conv_w, conv_b, filt_params, d_skip, (fc, fs) = hy
    nblk = _hyena_blocks(l)
    tc = 256
    spectra = _spectrum_call(_filter_call(l, *filt_params), fc, fs, nblk, tc)
    conv = functools.partial(_fftconv_call, conv_w=conv_w, conv_b=conv_b, spectra=spectra, d_skip=d_skip,
                             fc=fc, fs=fs, nblk=nblk, tc=tc)
    z1 = conv(z, HY_OFF, z, HY_OFF + HYENA_W, order=0, out_dtype=F32)
    y_hy = conv(z1, 0, z, HY_OFF + 2 * HYENA_W, order=1, out_dtype=BF16)
    y_pool = _pool_call(z, pool_w, pool_scale)
    return (y_att.reshape(b * l, ATTN_W), y_hy.reshape(b * l, HYENA_W), y_pool.reshape(b * l, POOL_W))


def kernel(x, c, ctx, c_ctx, norm1_g, norm2_g, w_mod, b_mod, w_in, q_norm_g, k_norm_g, sink, hy_conv_w, hy_conv_b, filt_w0, filt_b0, filt_w1, filt_b1, filt_freq, filt_w2, hy_bias, pool_w, pool_scale, w_att_o, w_hy_o, w_pool_o, w_out, mlp_w1, mlp_w2):
    b, l, d = x.shape
    lc = ctx.shape[1]
    depth = w_mod.shape[0]

    cc = jnp.concatenate([c, c_ctx[None, :], jnp.zeros((MOD_ROWS - b - 1, d), F32)], axis=0)
    mods = _modulation(cc, w_mod, b_mod)

    def chunks(layer, lo, hi):
        return [mods[layer, lo:hi, i * d:(i + 1) * d].reshape(hi - lo, 1, d) for i in range(6)]

    as_bf16 = lambda w: w.astype(BF16)
    w_in_b, w_att_b, w_hy_b, w_pool_b, w_out_b = map(as_bf16, (w_in, w_att_o, w_hy_o, w_pool_o, w_out))
    w1_b, w2_b, pool_w_b = map(as_bf16, (mlp_w1, mlp_w2, pool_w))

    rope_tabs = _rope_tables(l)
    dft_x = _dft_matrices(l // _hyena_blocks(l))
    dft_c = _dft_matrices(lc // _hyena_blocks(lc))

    x2 = x.reshape(b * l, d)
    c2 = ctx.reshape(b * lc, d)
    sh1, sc1 = chunks(0, 0, b)[:2]
    csh1, csc1 = chunks(0, b, b + 1)[:2]
    hx = _norm_call(x2, norm1_g[0], sc1, sh1, l)
    hc = _norm_call(c2, norm1_g[0], csc1, csh1, lc)

    for layer in range(depth):
        last = layer == depth - 1
        _, _, ga1, sh2, sc2, ga2 = chunks(layer, 0, b)
        _, _, cga1, csh2, csc2, cga2 = chunks(layer, b, b + 1)
        filt_params = (filt_w0[layer], filt_b0[layer], filt_w1[layer], filt_b1[layer], filt_freq[layer],
                       filt_w2[layer])
        conv_b = hy_conv_b[layer].reshape(1, -1)
        d_skip = hy_bias[layer].reshape(HYENA_ORDER, 1, HYENA_W)
        merge_w = (w_att_b, w_hy_b, w_pool_b, w_out_b, layer)

        gq, gk = q_norm_g[layer], k_norm_g[layer]
        if last:
            kvc = _kv_call(_proj_call(hc, w_in_b, layer, K_OFF, 2 * KV_W, F32), gk)
        else:
            zc = _proj_call(hc, w_in_b, layer, 0, GATE_OFF, F32)
            gc, qc, kvc = _gates_qkv_call(hc, w_in_b, layer, zc, gq, gk, None, lc)
            zc, qc = zc.reshape(b, lc, GATE_OFF), qc.reshape(b, lc, ATTN_W)
        kvc = kvc.reshape(b, lc, 2 * KV_DUP_W)

        zx = _proj_call(hx, w_in_b, layer, 0, GATE_OFF, F32)
        gx, qx, kvx = _gates_qkv_call(hx, w_in_b, layer, zx, gq, gk, rope_tabs, l)
        zx, qx, kvx = zx.reshape(b, l, GATE_OFF), qx.reshape(b, l, ATTN_W), kvx.reshape(b, l, 2 * KV_DUP_W)
        hy = (hy_conv_w[layer], conv_b, filt_params, d_skip, dft_x)
        ya, yh, yp = _mixers(zx, qx, kvx, kvc, sink[layer], True, hy, pool_w_b[layer], pool_scale[layer])
        xn, h2 = _merge_call(ya, yh, yp, gx, x2, ga1, norm2_g[layer], sc2, sh2, *merge_w, l)
        nxt = None if last else (norm1_g[layer + 1], *reversed(chunks(layer + 1, 0, b)[:2]))
        x2, hx = _mlp_call(h2, w1_b, w2_b, layer, xn, ga2, nxt, l)

        if not last:
            hyc = (hy_conv_w[layer], conv_b, filt_params, d_skip, dft_c)
            ya, yh, yp = _mixers(zc, qc, None, kvc, sink[layer], False, hyc, pool_w_b[layer], pool_scale[layer])
            cn, h2c = _merge_call(ya, yh, yp, gc, c2, cga1, norm2_g[layer], csc2, csh2, *merge_w, lc)
            nxt = (norm1_g[layer + 1], *reversed(chunks(layer + 1, b, b + 1)[:2]))
            c2, hc = _mlp_call(h2c, w1_b, w2_b, layer, cn, cga2, nxt, lc)

    return x2.reshape(b, l, d)
```

```python
import functools
import math

import jax
import jax.numpy as jnp
from jax import lax
from jax.experimental import pallas as pl
from jax.experimental.pallas import tpu as pltpu

D_MODEL = 2048
DEPTH = 2
GRID_W = 64
EPS = 1e-6
NEG_INF = -1e30

N_HEADS = 16
N_KV_HEADS = 4
GQA_GROUP = N_HEADS // N_KV_HEADS
HEAD_DIM = 64
ATTN_W = N_HEADS * HEAD_DIM
KV_W = N_KV_HEADS * HEAD_DIM
WINDOW = 128
ROPE_FREQS = HEAD_DIM // 4
ROPE_BASE = 10000.0

HYENA_W = D_MODEL // 4
HYENA_ORDER = 2
FILTER_BANDS = 16
FILTER_EMB = 1 + 2 * FILTER_BANDS
FILTER_HIDDEN = 64
FILTER_INNER = 2
DECAY_TARGET = 1e-2
FAST_DECAY_PCT = 0.3
SLOW_DECAY_PCT = 1.5

POOL_W = D_MODEL // 4
POOL_WINDOWS = (2, 4, 8, 16)
POOL_GROUP = POOL_W // len(POOL_WINDOWS)

N_BRANCH = 3
D_FF = 4 * D_MODEL

Q_OFF = 0
K_OFF = Q_OFF + ATTN_W
V_OFF = K_OFF + KV_W
HY_OFF = V_OFF + KV_W
POOL_OFF = HY_OFF + 3 * HYENA_W
GATE_OFF = POOL_OFF + POOL_W
IN_W = GATE_OFF + N_BRANCH * D_MODEL

V7X_LANES = 128
V7X_VMEM_LIMIT = 60 * 1024 * 1024
KV_DUP_W = N_KV_HEADS * V7X_LANES
MOD_ROWS = 24

F32 = jnp.float32
BF16 = jnp.bfloat16
HIGHEST = lax.Precision.HIGHEST


def _params(*semantics):
    return pltpu.CompilerParams(dimension_semantics=semantics, vmem_limit_bytes=V7X_VMEM_LIMIT)


def _const_spec(shape):
    zeros = (0,) * len(shape)
    return pl.BlockSpec(shape, lambda *_: zeros, pipeline_mode=pl.Buffered(1))


def _mod_spec(arr, rows_per_mod_tile):
    d = arr.shape[-1]
    if arr.shape[0] == 1:
        return pl.BlockSpec((1, 1, d), lambda i, *_: (0, 0, 0))
    return pl.BlockSpec((1, 1, d), lambda i, *_: (i // rows_per_mod_tile, 0, 0))


def _norm_mod(xf, g, sc, sh):
    y = xf * lax.rsqrt(jnp.mean(xf * xf, axis=-1, keepdims=True) + EPS)
    return (y * g) * (1.0 + sc) + sh


def _mod_body(c_ref, w_ref, b_ref, o_ref):
    c = c_ref[...]
    s = c * jax.nn.sigmoid(c)
    o_ref[0] = jnp.dot(s.astype(BF16), w_ref[0].astype(BF16), preferred_element_type=F32) + b_ref[0]


def _modulation(cc, w_mod, b_mod):
    depth, d, n = w_mod.shape
    tn = 1024
    return pl.pallas_call(
        _mod_body,
        grid=(depth, n // tn),
        in_specs=[
            pl.BlockSpec((MOD_ROWS, d), lambda l, j: (0, 0)),
            pl.BlockSpec((1, d, tn), lambda l, j: (l, 0, j)),
            pl.BlockSpec((1, 1, tn), lambda l, j: (l, 0, j)),
        ],
        out_specs=pl.BlockSpec((1, MOD_ROWS, tn), lambda l, j: (l, 0, j)),
        out_shape=jax.ShapeDtypeStruct((depth, MOD_ROWS, n), F32),
        compiler_params=_params("arbitrary", "arbitrary"),
        name="modulation",
    )(cc, w_mod, b_mod.reshape(depth, 1, n))


def _norm_body(x_ref, g_ref, sc_ref, sh_ref, o_ref):
    o_ref[...] = _norm_mod(x_ref[...], g_ref[...], sc_ref[0], sh_ref[0]).astype(o_ref.dtype)


def _norm_call(x2, g, sc, sh, rows_per_batch):
    m, d = x2.shape
    tm = min(1024, rows_per_batch if sc.shape[0] > 1 else m)
    return pl.pallas_call(
        _norm_body,
        grid=(m // tm,),
        in_specs=[
            pl.BlockSpec((tm, d), lambda i: (i, 0)),
            pl.BlockSpec((1, d), lambda i: (0, 0)),
            _mod_spec(sc, rows_per_batch // tm),
            _mod_spec(sh, rows_per_batch // tm),
        ],
        out_specs=pl.BlockSpec((tm, d), lambda i: (i, 0)),
        out_shape=jax.ShapeDtypeStruct((m, d), BF16),
        compiler_params=_params("arbitrary"),
        name="norm_mod",
    )(x2, g.reshape(1, d), sc, sh)


def _proj_body(a_ref, w_ref, o_ref):
    o_ref[...] = jnp.dot(a_ref[...], w_ref[...], preferred_element_type=F32).astype(o_ref.dtype)


def _proj_call(a, w, layer, col0, n, out_dtype):
    m, k = a.shape
    tm = min(4096, m)
    tn = 512
    c0 = col0 // tn
    return pl.pallas_call(
        _proj_body,
        grid=(m // tm, n // tn),
        in_specs=[
            pl.BlockSpec((tm, k), lambda i, j: (i, 0)),
            pl.BlockSpec((None, k, tn), lambda i, j: (layer, 0, c0 + j)),
        ],
        out_specs=pl.BlockSpec((tm, tn), lambda i, j: (i, j)),
        out_shape=jax.ShapeDtypeStruct((m, n), out_dtype),
        compiler_params=_params("arbitrary", "arbitrary"),
        name="in_proj",
    )(a, w)


Q_SLABS = ATTN_W // V7X_LANES
QK_SLABS = Q_SLABS + KV_W // V7X_LANES
QKV_SLABS = QK_SLABS + KV_W // V7X_LANES


def _gates_qkv_body(a_ref, w_ref, z_ref, gain_ref, cos_ref, sup_ref, sdn_ref, g_ref, q_ref, kv_ref):
    j = pl.program_id(1)
    zg = jnp.dot(a_ref[...], w_ref[...], preferred_element_type=F32)
    g_ref[...] = (0.5 * jnp.tanh(0.5 * zg) + 0.5).astype(g_ref.dtype)

    x = z_ref[...]
    low = lax.broadcasted_iota(jnp.int32, (1, V7X_LANES), 1) < HEAD_DIM
    x2 = x * x
    ss = jnp.where(low, jnp.sum(jnp.where(low, x2, 0.0), axis=-1, keepdims=True),
                   jnp.sum(jnp.where(low, 0.0, x2), axis=-1, keepdims=True))
    inv = jnp.where(j < QK_SLABS, lax.rsqrt(ss * (1.0 / HEAD_DIM) + EPS), 1.0)
    y = _rope((x * inv) * gain_ref[0], cos_ref[...], sup_ref[...], sdn_ref[...])
    da, db = _dup_pair(y, low)

    @pl.when(j < Q_SLABS)
    def _():
        q_ref[...] = y.astype(q_ref.dtype)

    @pl.when(j >= Q_SLABS)
    def _():
        kv_ref[:, 0:V7X_LANES] = da.astype(kv_ref.dtype)
        kv_ref[:, V7X_LANES:2 * V7X_LANES] = db.astype(kv_ref.dtype)


def _gates_qkv_call(a, w, layer, z2, gq, gk, rope_tabs, seq_len):
    m, k = a.shape
    d = D_MODEL
    tm = min(2048, m)
    tn = 512
    assert N_BRANCH * d // tn == QKV_SLABS and tm % seq_len == 0
    c0 = GATE_OFF // tn
    ones = jnp.ones((tm, V7X_LANES), F32)
    zeros = jnp.zeros((tm, V7X_LANES), F32)
    if rope_tabs is None:
        cos, sup, sdn = ones[None], zeros[None], zeros[None]
        tab_map = lambda i, j: (0, 0, 0)
    else:
        rep = lambda t: jnp.tile(t, (tm // seq_len, 1))
        cos, sup, sdn = (jnp.stack([rep(t), ident]) for t, ident in zip(rope_tabs, (ones, zeros, zeros)))
        tab_map = lambda i, j: ((j >= QK_SLABS).astype(jnp.int32), 0, 0)
    gains = jnp.stack([jnp.tile(gq, 2) * HEAD_DIM ** -0.5, jnp.tile(gk, 2), jnp.ones((V7X_LANES,), F32)])
    gain_map = lambda i, j: ((j >= Q_SLABS).astype(jnp.int32) + (j >= QK_SLABS).astype(jnp.int32), 0, 0)
    tab_spec = pl.BlockSpec((None, tm, V7X_LANES), tab_map)
    return pl.pallas_call(
        _gates_qkv_body,
        grid=(m // tm, QKV_SLABS),
        in_specs=[
            pl.BlockSpec((tm, k), lambda i, j: (i, 0)),
            pl.BlockSpec((None, k, tn), lambda i, j: (layer, 0, c0 + j)),
            pl.BlockSpec((tm, V7X_LANES), lambda i, j: (i, j)),
            pl.BlockSpec((None, 1, V7X_LANES), gain_map),
            tab_spec, tab_spec, tab_spec,
        ],
        out_specs=[
            pl.BlockSpec((tm, tn), lambda i, j: (i, j)),
            pl.BlockSpec((tm, V7X_LANES), lambda i, j: (i, jnp.minimum(j, Q_SLABS - 1))),
            pl.BlockSpec((tm, 2 * V7X_LANES), lambda i, j: (i, jnp.maximum(j - Q_SLABS, 0))),
        ],
        out_shape=[
            jax.ShapeDtypeStruct((m, N_BRANCH * d), BF16),
            jax.ShapeDtypeStruct((m, ATTN_W), BF16),
            jax.ShapeDtypeStruct((m, 2 * KV_DUP_W), BF16),
        ],
        compiler_params=_params("arbitrary", "arbitrary"),
        name="gates_qkv",
    )(a, w, z2, gains.reshape(3, 1, V7X_LANES), cos, sup, sdn)


def _pair_block_diag():
    r = lax.broadcasted_iota(jnp.int32, (V7X_LANES, V7X_LANES), 0) // HEAD_DIM
    c = lax.broadcasted_iota(jnp.int32, (V7X_LANES, V7X_LANES), 1) // HEAD_DIM
    return (r == c).astype(F32)


def _head_norm(x, g, bd):
    ss = jnp.dot(x * x, bd, precision=HIGHEST, preferred_element_type=F32)
    return (x * lax.rsqrt(ss * (1.0 / HEAD_DIM) + EPS)) * g


def _rope(x, cos, sin_up, sin_dn):
    up = pltpu.roll(x, V7X_LANES - ROPE_FREQS, 1)
    dn = pltpu.roll(x, ROPE_FREQS, 1)
    return x * cos + up * sin_up + dn * sin_dn


def _dup_pair(x, low):
    r = pltpu.roll(x, HEAD_DIM, 1)
    return jnp.where(low, x, r), jnp.where(low, r, x)


def _kv_body(z_ref, gk_ref, kv_ref):
    bd = _pair_block_diag()
    low = lax.broadcasted_iota(jnp.int32, (1, V7X_LANES), 1) < HEAD_DIM
    for s in range(2 * KV_W // V7X_LANES):
        x = z_ref[:, s * V7X_LANES:(s + 1) * V7X_LANES]
        if s < KV_W // V7X_LANES:
            x = _head_norm(x, gk_ref[...], bd)
        a, b = _dup_pair(x, low)
        base = 2 * s * V7X_LANES
        kv_ref[:, base:base + V7X_LANES] = a.astype(kv_ref.dtype)
        kv_ref[:, base + V7X_LANES:base + 2 * V7X_LANES] = b.astype(kv_ref.dtype)


def _kv_call(z2, gk):
    m, nz = z2.shape
    tm = min(512, m)
    return pl.pallas_call(
        _kv_body,
        grid=(m // tm,),
        in_specs=[pl.BlockSpec((tm, nz), lambda i: (i, 0)),
                  pl.BlockSpec((1, V7X_LANES), lambda i: (0, 0))],
        out_specs=pl.BlockSpec((tm, 2 * KV_DUP_W), lambda i: (i, 0)),
        out_shape=jax.ShapeDtypeStruct((m, 2 * KV_DUP_W), BF16),
        compiler_params=_params("arbitrary"),
        name="kv_prep",
    )(z2, jnp.tile(gk, 2).reshape(1, V7X_LANES))


def _rope_tables(l):
    rows = l // GRID_W
    row = jnp.repeat(jnp.arange(rows, dtype=F32), GRID_W)
    col = jnp.tile(jnp.arange(GRID_W, dtype=F32), rows)
    inv = ROPE_BASE ** (-jnp.arange(ROPE_FREQS, dtype=F32) / ROPE_FREQS)
    ang = jnp.stack([row[:, None] * inv, col[:, None] * inv], axis=1)
    cos, sin = jnp.cos(ang), jnp.sin(ang)
    zero = jnp.zeros_like(sin)
    cos_h = jnp.stack([cos, cos], axis=2).reshape(l, HEAD_DIM)
    sup_h = jnp.stack([-sin, zero], axis=2).reshape(l, HEAD_DIM)
    sdn_h = jnp.stack([zero, sin], axis=2).reshape(l, HEAD_DIM)
    return tuple(jnp.tile(t, (1, 2)) for t in (cos_h, sup_h, sdn_h))


def _attn_body(*refs, local, tq):
    it = iter(refs)
    sink_ref = next(it)
    q_ref = next(it)
    if local:
        kp_ref, kc_ref, kn_ref, vp_ref, vc_ref, vn_ref = (next(it) for _ in range(6))
    kx_ref, vx_ref = next(it), next(it)
    o_ref = next(it)

    i = pl.program_id(1)
    nb = pl.num_programs(1)
    low = lax.broadcasted_iota(jnp.int32, (1, V7X_LANES), 1) < HEAD_DIM
    rows = GQA_GROUP * tq
    if local:
        qi = lax.broadcasted_iota(jnp.int32, (rows, tq), 0) % tq
        kj = lax.broadcasted_iota(jnp.int32, (rows, tq), 1)
        mask_prev = (kj >= qi) & (i > 0)
        mask_next = (kj <= qi) & (i < nb - 1)
    row_head = lax.broadcasted_iota(jnp.int32, (rows, 1), 0) // tq
    zero = jnp.zeros((), q_ref.dtype)

    for e, h in [(e, h) for e in range(q_ref.shape[0]) for h in range(N_KV_HEADS)]:
        hs = slice(h * V7X_LANES, (h + 1) * V7X_LANES)
        qa = q_ref[e, :, 2 * h * V7X_LANES:(2 * h + 1) * V7X_LANES]
        qb = q_ref[e, :, (2 * h + 1) * V7X_LANES:(2 * h + 2) * V7X_LANES]
        qs = jnp.concatenate([jnp.where(low, qa, zero), jnp.where(low, zero, qa),
                              jnp.where(low, qb, zero), jnp.where(low, zero, qb)], axis=0)
        kparts, vparts, masks = [kx_ref[e, :, hs]], [vx_ref[e, :, hs]], {}
        if local:
            kparts = [kp_ref[e, :, hs], kc_ref[e, :, hs], kn_ref[e, :, hs]] + kparts
            vparts = [vp_ref[e, :, hs], vc_ref[e, :, hs], vn_ref[e, :, hs]] + vparts
            masks = {0: mask_prev, 2: mask_next}
        k_all = jnp.concatenate(kparts, axis=0)
        v_all = jnp.concatenate(vparts, axis=0)

        sink = jnp.zeros((rows, 1), F32)
        for g in range(GQA_GROUP):
            sink = jnp.where(row_head == g, sink_ref[GQA_GROUP * h + g], sink)
        s_all = lax.dot_general(qs, k_all, (((1,), (1,)), ((), ())), preferred_element_type=F32)
        chunks = []
        for c in range(k_all.shape[0] // tq):
            s = s_all[:, c * tq:(c + 1) * tq]
            chunks.append(jnp.where(masks[c], s, NEG_INF) if c in masks else s)
        m = jnp.maximum(sink, jnp.max(functools.reduce(jnp.maximum, chunks), axis=-1, keepdims=True))
        probs = [jnp.exp(s - m) for s in chunks]
        denom = jnp.exp(sink - m) + jnp.sum(functools.reduce(jnp.add, probs), axis=-1, keepdims=True)
        p_all = jnp.concatenate([p.astype(v_all.dtype) for p in probs], axis=1)
        o = jnp.dot(p_all, v_all, preferred_element_type=F32) / denom
        oa = jnp.where(low, o[0:tq], o[tq:2 * tq])
        ob = jnp.where(low, o[2 * tq:3 * tq], o[3 * tq:4 * tq])
        o_ref[e, :, 2 * h * V7X_LANES:(2 * h + 1) * V7X_LANES] = oa.astype(o_ref.dtype)
        o_ref[e, :, (2 * h + 1) * V7X_LANES:(2 * h + 2) * V7X_LANES] = ob.astype(o_ref.dtype)


def _attn_call(q, kv, kvx, sink, local):
    b, l, _ = q.shape
    lx = kvx.shape[1]
    tq = WINDOW
    nb = l // tq
    ne = math.gcd(b, 4)
    blk = lambda w: (ne, tq, w)
    in_specs = [pl.BlockSpec(memory_space=pltpu.SMEM),
                pl.BlockSpec(blk(ATTN_W), lambda bi, i: (bi, i, 0))]
    args = [sink, q]
    if local:
        for half in (0, 1):
            for mp in (lambda bi, i, half=half: (bi, jnp.maximum(i - 1, 0), half),
                       lambda bi, i, half=half: (bi, i, half),
                       lambda bi, i, half=half: (bi, jnp.minimum(i + 1, nb - 1), half)):
                in_specs.append(pl.BlockSpec(blk(KV_DUP_W), mp))
                args.append(kv)
    for half in (0, 1):
        in_specs.append(pl.BlockSpec((ne, lx, KV_DUP_W), lambda bi, i, half=half: (bi, 0, half)))
        args.append(kvx)
    return pl.pallas_call(
        functools.partial(_attn_body, local=local, tq=tq),
        grid=(b // ne, nb),
        in_specs=in_specs,
        out_specs=pl.BlockSpec(blk(ATTN_W), lambda bi, i: (bi, i, 0)),
        out_shape=jax.ShapeDtypeStruct((b, l, ATTN_W), BF16),
        compiler_params=_params("arbitrary", "arbitrary"),
        name="attention",
    )(*args)


def _filter_body(z_ref, w0_ref, b0_ref, w1_ref, b1_ref, fr_ref, w2_ref, dec_ref, o_ref):
    fr = fr_ref[...]
    dot = functools.partial(jnp.dot, precision=HIGHEST, preferred_element_type=F32)
    h = jnp.sin(fr * (dot(z_ref[...], w0_ref[...]) + b0_ref[...]))
    for i in range(FILTER_INNER):
        h = jnp.sin(fr * (dot(h, w1_ref[i]) + b1_ref[i]))
    dec = dec_ref[...]
    for s in range(2 * HYENA_ORDER):
        sl = slice(s * HYENA_W, (s + 1) * HYENA_W)
        o_ref[:, sl] = dot(h, w2_ref[:, sl]) * dec


def _filter_features(l):
    t = jnp.linspace(0.0, 1.0, l, dtype=F32)[:, None]
    w = 2.0 * math.pi * jnp.arange(l, dtype=F32)[:, None] / l
    bands = jnp.linspace(1e-4, FILTER_BANDS - 1, FILTER_BANDS, dtype=F32)[None, :]
    z = jnp.concatenate([t, jnp.cos(bands * w), -jnp.sin(bands * w)], axis=-1)
    deltas = jnp.linspace(math.log(DECAY_TARGET) / SLOW_DECAY_PCT, math.log(DECAY_TARGET) / FAST_DECAY_PCT,
                          HYENA_W, dtype=F32)
    decay = jnp.exp(-t * jnp.abs(deltas))
    return jnp.pad(z, ((0, 0), (0, V7X_LANES - FILTER_EMB))), decay


def _filter_call(l, w0, b0, w1, b1, freq, w2):
    zfeat, decay = _filter_features(l)
    w0p = jnp.pad(w0, ((0, V7X_LANES - FILTER_EMB), (0, 0)))
    tl = min(512, l)
    nf = 2 * HYENA_ORDER * HYENA_W
    full = lambda shape: pl.BlockSpec(shape, lambda i: (0,) * len(shape))
    return pl.pallas_call(
        _filter_body,
        grid=(l // tl,),
        in_specs=[
            pl.BlockSpec((tl, V7X_LANES), lambda i: (i, 0)),
            full((V7X_LANES, FILTER_HIDDEN)),
            full((1, FILTER_HIDDEN)),
            full((FILTER_INNER, FILTER_HIDDEN, FILTER_HIDDEN)),
            full((FILTER_INNER, 1, FILTER_HIDDEN)),
            full((1, FILTER_HIDDEN)),
            full((FILTER_HIDDEN, nf)),
            pl.BlockSpec((tl, HYENA_W), lambda i: (i, 0)),
        ],
        out_specs=pl.BlockSpec((tl, nf), lambda i: (i, 0)),
        out_shape=jax.ShapeDtypeStruct((l, nf), F32),
        compiler_params=_params("arbitrary"),
        name="hyena_filter",
    )(zfeat, w0p, b0.reshape(1, -1), w1, b1.reshape(FILTER_INNER, 1, -1), freq.reshape(1, -1), w2, decay)


def _hyena_blocks(l):
    return max(1, min(4, l // V7X_LANES))


def _dft_matrices(blk):
    n = 2 * blk
    r = jnp.arange(blk, dtype=jnp.int32)
    ang = ((r[:, None] * r[None, :]) % n).astype(F32) * (2.0 * math.pi / n)
    return jnp.cos(ang).astype(BF16), jnp.sin(ang).astype(BF16)


def _alternating(l):
    row = lax.broadcasted_iota(jnp.int32, (l, 1), 0)
    return row, jnp.where(row % 2 == 0, 1.0, -1.0).astype(F32)


def _spectrum_body(hf_ref, hb_ref, fc_ref, fs_ref, ka_ref, kb_ref, kn_ref, *, nblk):
    l = hf_ref.shape[0]
    b = l // nblk
    n = 2 * b
    row = lax.broadcasted_iota(jnp.int32, (l, 1), 0)
    _, alt = _alternating(b)
    hf = hf_ref[...]
    hbs = jnp.where(row == 0, 0.0, pltpu.roll(hb_ref[...], 1, 0))
    fc, fs = fc_ref[...], fs_ref[...]

    def transforms(h):
        out = []
        for k in range(nblk):
            hk = h[k * b:(k + 1) * b]
            hk16 = hk.astype(BF16)
            out.append(dict(
                c=jnp.dot(fc, hk16, preferred_element_type=F32),
                s=jnp.dot(fs, hk16, preferred_element_type=F32),
                first16=hk16[0:1].astype(F32),
                first=hk[0:1],
                alt=jnp.sum(hk * alt, axis=0, keepdims=True)))
        return out

    tf, tb = transforms(hf), transforms(hbs)
    brow = lax.broadcasted_iota(jnp.int32, (b, 1), 0)
    w_re = jnp.where(brow == 0, 1.0 / n, 2.0 / n)
    for d in range(-(nblk - 1), nblk):
        idx = d + nblk - 1
        if d == 0:
            kre = tf[0]["c"] + tb[0]["c"]
            kim = tb[0]["s"] - tf[0]["s"]
            kn = tf[0]["alt"] + tb[0]["alt"]
        else:
            t, e, sg = (tf, d, -1.0) if d > 0 else (tb, -d, 1.0)
            kre = t[e]["c"] + alt * (t[e - 1]["c"] - t[e - 1]["first16"])
            kim = sg * (t[e]["s"] + alt * t[e - 1]["s"])
            kn = t[e]["alt"] + t[e - 1]["alt"] - t[e - 1]["first"]
        ka_ref[0, idx] = (kre * w_re).astype(ka_ref.dtype)
        kb_ref[0, idx] = (kim * (2.0 / n)).astype(kb_ref.dtype)
        kn_ref[0, idx] = kn * (1.0 / n)


def _spectrum_call(filt, fc, fs, nblk, tc):
    l = filt.shape[0]
    b = l // nblk
    nct = HYENA_W // tc
    nlag = 2 * nblk - 1
    return pl.pallas_call(
        functools.partial(_spectrum_body, nblk=nblk),
        grid=(HYENA_ORDER, nct),
        in_specs=[
            pl.BlockSpec((l, tc), lambda o, c: (0, 2 * nct * o + c)),
            pl.BlockSpec((l, tc), lambda o, c: (0, 2 * nct * o + nct + c)),
            _const_spec((b, b)),
            _const_spec((b, b)),
        ],
        out_specs=[
            pl.BlockSpec((1, nlag, b, tc), lambda o, c: (o, 0, 0, c)),
            pl.BlockSpec((1, nlag, b, tc), lambda o, c: (o, 0, 0, c)),
            pl.BlockSpec((1, nlag, 1, tc), lambda o, c: (o, 0, 0, c)),
        ],
        out_shape=[
            jax.ShapeDtypeStruct((HYENA_ORDER, nlag, b, HYENA_W), BF16),
            jax.ShapeDtypeStruct((HYENA_ORDER, nlag, b, HYENA_W), BF16),
            jax.ShapeDtypeStruct((HYENA_ORDER, nlag, 1, HYENA_W), F32),
        ],
        compiler_params=_params("arbitrary", "arbitrary"),
        name="hyena_spectrum",
    )(filt, filt, fc, fs)


def _conv3(x, w_ref, b_ref, row):
    l = x.shape[0]
    xm = jnp.where(row == 0, 0.0, pltpu.roll(x, 1, 0))
    xp = jnp.where(row == l - 1, 0.0, pltpu.roll(x, l - 1, 0))
    return xm * w_ref[0:1, :] + x * w_ref[1:2, :] + xp * w_ref[2:3, :] + b_ref[...]


def _fftconv_body(*refs, conv_u, nblk):
    it = iter(refs)
    u_ref = next(it)
    if conv_u:
        uw_ref, ub_ref = next(it), next(it)
    g_ref, gw_ref, gb_ref = next(it), next(it), next(it)
    ka_ref, kb_ref, kn_ref, d_ref, fc_ref, fs_ref, o_ref = (next(it) for _ in range(7))

    l = u_ref.shape[1]
    b = l // nblk
    row = lax.broadcasted_iota(jnp.int32, (l, 1), 0)
    _, alt = _alternating(b)
    u = u_ref[0].astype(F32)
    if conv_u:
        u = _conv3(u, uw_ref, ub_ref, row)
    gate = _conv3(g_ref[0], gw_ref, gb_ref, row)
    fc, fs = fc_ref[...], fs_ref[...]

    ps, qs, ns = [], [], []
    for j in range(nblk):
        uj = u[j * b:(j + 1) * b]
        uj16 = uj.astype(BF16)
        ps.append(jnp.dot(fc, uj16, preferred_element_type=F32).astype(BF16))
        qs.append(jnp.dot(fs, uj16, preferred_element_type=F32).astype(BF16))
        ns.append(jnp.sum(uj * alt, axis=0, keepdims=True))
    for i in range(nblk):
        r = t = nyq = None
        for j in range(nblk):
            lag = i - j + nblk - 1
            ka, kb = ka_ref[0, lag], kb_ref[0, lag]
            dr = ps[j] * ka + qs[j] * kb
            dt = qs[j] * ka - ps[j] * kb
            dn = ns[j] * kn_ref[0, lag]
            r, t, nyq = (dr, dt, dn) if j == 0 else (r + dr, t + dt, nyq + dn)
        y = jnp.dot(fc, r, preferred_element_type=F32) + jnp.dot(fs, t, preferred_element_type=F32)
        rows = slice(i * b, (i + 1) * b)
        y = y + alt * nyq + u[rows] * d_ref[0]
        o_ref[0, rows, :] = (gate[rows] * y).astype(o_ref.dtype)


def _fftconv_call(u, u_col0, z, gate_col0, conv_w, conv_b, spectra, d_skip, order, fc, fs, nblk, tc, out_dtype):
    b, l, _ = z.shape
    conv_u = u is z
    nct = HYENA_W // tc
    ka, kb, kn = spectra
    blk = l // nblk
    nlag = 2 * nblk - 1
    col = lambda c0: (lambda c, bi: (bi, 0, c0 // tc + c))
    wcol = lambda c0: (lambda c, bi: (0, (c0 - HY_OFF) // tc + c))
    in_specs = [pl.BlockSpec((1, l, tc), col(u_col0))]
    args = [u]
    if conv_u:
        in_specs += [pl.BlockSpec((3, tc), wcol(u_col0)), pl.BlockSpec((1, tc), wcol(u_col0))]
        args += [conv_w, conv_b]
    in_specs += [pl.BlockSpec((1, l, tc), col(gate_col0)),
                 pl.BlockSpec((3, tc), wcol(gate_col0)), pl.BlockSpec((1, tc), wcol(gate_col0))]
    args += [z, conv_w, conv_b]
    spec = lambda rows: pl.BlockSpec((1, nlag, rows, tc), lambda c, bi: (order, 0, 0, c),
                                     pipeline_mode=pl.Buffered(1))
    in_specs += [spec(blk), spec(blk), spec(1),
                 pl.BlockSpec((1, 1, tc), lambda c, bi: (order, 0, c), pipeline_mode=pl.Buffered(1)),
                 _const_spec((blk, blk)), _const_spec((blk, blk))]
    args += [ka, kb, kn, d_skip, fc, fs]
    return pl.pallas_call(
        functools.partial(_fftconv_body, conv_u=conv_u, nblk=nblk),
        grid=(nct, b),
        in_specs=in_specs,
        out_specs=pl.BlockSpec((1, l, tc), lambda c, bi: (bi, 0, c)),
        out_shape=jax.ShapeDtypeStruct((b, l, HYENA_W), out_dtype),
        compiler_params=_params("arbitrary", "arbitrary"),
        name="hyena_conv",
    )(*args)


def _pool_body(x_ref, w_ref, s_ref, o_ref):
    l = x_ref.shape[1]
    row = lax.broadcasted_iota(jnp.int32, (l, 1), 0)
    for g, win in enumerate(POOL_WINDOWS):
        half = win // 2
        sl = slice(g * POOL_GROUP, (g + 1) * POOL_GROUP)
        x = x_ref[0, :, sl]

        def shifted(a, k):
            return jnp.where((row >= k) & (row < l + k), pltpu.roll(a, k % l, 0), 0.0)

        back = fwd = x
        span = 1
        while span < half:
            back = back + shifted(back, span)
            fwd = fwd + shifted(fwd, -span)
            span *= 2
        acc = shifted(back, 1) + fwd
        cnt = (jnp.minimum(row + half, l) - jnp.maximum(row - half, 0)).astype(F32)
        d = acc / cnt - x
        y = jnp.dot(d.astype(BF16), w_ref[g], preferred_element_type=F32)
        o_ref[0, :, sl] = (y * s_ref[:, sl]).astype(o_ref.dtype)


def _pool_call(z, w_grp, scale):
    b, l, _ = z.shape
    ng = len(POOL_WINDOWS)
    return pl.pallas_call(
        _pool_body,
        grid=(b,),
        in_specs=[
            pl.BlockSpec((1, l, POOL_W), lambda bi: (bi, 0, POOL_OFF // POOL_W)),
            pl.BlockSpec((ng, POOL_GROUP, POOL_GROUP), lambda bi: (0, 0, 0)),
            pl.BlockSpec((1, POOL_W), lambda bi: (0, 0)),
        ],
        out_specs=pl.BlockSpec((1, l, POOL_W), lambda bi: (bi, 0, 0)),
        out_shape=jax.ShapeDtypeStruct((b, l, POOL_W), BF16),
        compiler_params=_params("arbitrary"),
        name="pool",
    )(z, w_grp, scale.reshape(1, POOL_W))


MERGE_CHUNK = 512


def _merge_body(ya_ref, yh_ref, yp_ref, gate_a_ref, gate_h_ref, gate_p_ref, x_ref, ga_ref, g2_ref, sc_ref, sh_ref,
                wa_ref, wh_ref, wp_ref, wo_ref, xn_ref, h2_ref):
    j = pl.program_id(1)
    last = pl.num_programs(1) - 1
    tm, d = xn_ref.shape
    cn = 512
    rb = min(256, tm)

    def merged(rows):
        branch = lambda g_ref, y_ref, w_ref: g_ref[rows, :].astype(F32) * jnp.dot(
            y_ref[rows, :], w_ref[...], preferred_element_type=F32)
        m = branch(gate_a_ref, ya_ref, wa_ref) + branch(gate_h_ref, yh_ref, wh_ref) + branch(gate_p_ref, yp_ref, wp_ref)
        return m.astype(BF16)

    @pl.when(j == 0)
    def _():
        m = merged(slice(None))
        for n0 in range(0, d, cn):
            xn_ref[:, n0:n0 + cn] = jnp.dot(m, wo_ref[:, n0:n0 + cn], preferred_element_type=F32)

    @pl.when((j > 0) & (j < last))
    def _():
        m = merged(slice(None))
        for n0 in range(0, d, cn):
            xn_ref[:, n0:n0 + cn] += jnp.dot(m, wo_ref[:, n0:n0 + cn], preferred_element_type=F32)

    @pl.when(j == last)
    def _():
        for r0 in range(0, tm, rb):
            rows = slice(r0, r0 + rb)
            acc = xn_ref[rows, :] + jnp.dot(merged(rows), wo_ref[...], preferred_element_type=F32)
            xn = x_ref[rows, :] + ga_ref[0] * acc
            xn_ref[rows, :] = xn
            h2_ref[rows, :] = _norm_mod(xn, g2_ref[...], sc_ref[0], sh_ref[0]).astype(h2_ref.dtype)


def _merge_call(ya, yh, yp, gates, x2, ga1, g2, sc2, sh2, wa, wh, wp, wo, layer, rows_per_batch):
    m, d = x2.shape
    tm = min(1024, rows_per_batch if ga1.shape[0] > 1 else m)
    rpt = max(rows_per_batch // tm, 1)
    cj = MERGE_CHUNK
    nj = d // cj
    assert nj >= 2
    rows = lambda w: pl.BlockSpec((tm, w), lambda i, j: (i, 0))
    gate = lambda br: pl.BlockSpec((tm, cj), lambda i, j: (i, br * nj + j))
    wcols = lambda w: pl.BlockSpec((None, w.shape[1], cj), lambda i, j: (layer, 0, j))
    return pl.pallas_call(
        _merge_body,
        grid=(m // tm, nj),
        in_specs=[rows(ATTN_W), rows(HYENA_W), rows(POOL_W), gate(0), gate(1), gate(2),
                  pl.BlockSpec((tm, d), lambda i, j: (i, 0), pipeline_mode=pl.Buffered(1)),
                  _mod_spec(ga1, rpt), pl.BlockSpec((1, d), lambda i, j: (0, 0)),
                  _mod_spec(sc2, rpt), _mod_spec(sh2, rpt),
                  wcols(wa), wcols(wh), wcols(wp),
                  pl.BlockSpec((None, cj, d), lambda i, j: (layer, j, 0))],
        out_specs=[rows(d), rows(d)],
        out_shape=[jax.ShapeDtypeStruct((m, d), F32), jax.ShapeDtypeStruct((m, d), BF16)],
        compiler_params=_params("arbitrary", "arbitrary"),
        name="merge",
    )(ya, yh, yp, gates, gates, gates, x2, ga1, g2.reshape(1, d), sc2, sh2, wa, wh, wp, wo)


def _mlp_body(*refs, has_next):
    it = iter(refs)
    h_ref, w1_ref, w2_ref, x_ref, ga_ref = (next(it) for _ in range(5))
    if has_next:
        gn_ref, sc_ref, sh_ref = next(it), next(it), next(it)
    o_ref = next(it)
    hn_ref = next(it) if has_next else None

    f = pl.program_id(1)
    last = pl.num_programs(1) - 1
    tm, d = o_ref.shape
    cn = 512
    rb = min(256, tm)

    def hidden(rows):
        a = jnp.dot(h_ref[rows, :], w1_ref[...], preferred_element_type=F32)
        return jnp.square(jnp.maximum(a, 0.0)).astype(BF16)

    @pl.when(f == 0)
    def _():
        a = hidden(slice(None))
        for n0 in range(0, d, cn):
            o_ref[:, n0:n0 + cn] = jnp.dot(a, w2_ref[:, n0:n0 + cn], preferred_element_type=F32)

    @pl.when((f > 0) & (f < last))
    def _():
        a = hidden(slice(None))
        for n0 in range(0, d, cn):
            o_ref[:, n0:n0 + cn] += jnp.dot(a, w2_ref[:, n0:n0 + cn], preferred_element_type=F32)

    @pl.when(f == last)
    def _():
        for r0 in range(0, tm, rb):
            rows = slice(r0, r0 + rb)
            acc = o_ref[rows, :] + jnp.dot(hidden(rows), w2_ref[...], preferred_element_type=F32)
            xo = x_ref[rows, :] + ga_ref[0] * acc
            o_ref[rows, :] = xo
            if has_next:
                hn_ref[rows, :] = _norm_mod(xo, gn_ref[...], sc_ref[0], sh_ref[0]).astype(hn_ref.dtype)


def _mlp_call(h2, w1, w2, layer, xn, ga2, nxt, rows_per_batch):
    m, d = xn.shape
    ff = w1.shape[2]
    tm, tf = min(1024, rows_per_batch if ga2.shape[0] > 1 else m), 512
    rpt = max(rows_per_batch // tm, 1)
    assert ff // tf >= 2
    has_next = nxt is not None
    rows = pl.BlockSpec((tm, d), lambda i, f: (i, 0))
    in_specs = [rows, pl.BlockSpec((None, d, tf), lambda i, f: (layer, 0, f)),
                pl.BlockSpec((None, tf, d), lambda i, f: (layer, f, 0)),
                pl.BlockSpec((tm, d), lambda i, f: (i, 0), pipeline_mode=pl.Buffered(1)), _mod_spec(ga2, rpt)]
    args = [h2, w1, w2, xn, ga2]
    out_specs = [rows]
    out_shape = [jax.ShapeDtypeStruct((m, d), F32)]
    if has_next:
        gn, scn, shn = nxt
        in_specs += [pl.BlockSpec((1, d), lambda i, f: (0, 0)), _mod_spec(scn, rpt), _mod_spec(shn, rpt)]
        args += [gn.reshape(1, d), scn, shn]
        out_specs.append(rows)
        out_shape.append(jax.ShapeDtypeStruct((m, d), BF16))
    outs = pl.pallas_call(
        functools.partial(_mlp_body, has_next=has_next),
        grid=(m // tm, ff // tf),
        in_specs=in_specs,
        out_specs=out_specs,
        out_shape=out_shape,
        compiler_params=_params("arbitrary", "arbitrary"),
        name="mlp",
    )(*args)
    return (outs[0], outs[1]) if has_next else (outs[0], None)


def _mixers(z, q, kv, kvx, sink, local, hy, pool_w, pool_scale):
    b, l, _ = z.shape
    y_att = _attn_call(q, kv, kvx, sink, local)
    conv_w, conv_b, filt_params, d_skip, (fc, fs) = hy
    nblk = _hyena_blocks(l)
    tc = 256
    spectra = _spectrum_call(_filter_call(l, *filt_params), fc, fs, nblk, tc)
    conv = functools.partial(_fftconv_call, conv_w=conv_w, conv_b=conv_b, spectra=spectra, d_skip=d_skip,
                             fc=fc, fs=fs, nblk=nblk, tc=tc)
    z1 = conv(z, HY_OFF, z, HY_OFF + HYENA_W, order=0, out_dtype=F32)
    y_hy = conv(z1, 0, z, HY_OFF + 2 * HYENA_W, order=1, out_dtype=BF16)
    y_pool = _pool_call(z, pool_w, pool_scale)
    return (y_att.reshape(b * l, ATTN_W), y_hy.reshape(b * l, HYENA_W), y_pool.reshape(b * l, POOL_W))


def kernel(x, c, ctx, c_ctx, norm1_g, norm2_g, w_mod, b_mod, w_in, q_norm_g, k_norm_g, sink, hy_conv_w, hy_conv_b, filt_w0, filt_b0, filt_w1, filt_b1, filt_freq, filt_w2, hy_bias, pool_w, pool_scale, w_att_o, w_hy_o, w_pool_o, w_out, mlp_w1, mlp_w2):
    b, l, d = x.shape
    lc = ctx.shape[1]
    depth = w_mod.shape[0]

    cc = jnp.concatenate([c, c_ctx[None, :], jnp.zeros((MOD_ROWS - b - 1, d), F32)], axis=0)
    mods = _modulation(cc, w_mod, b_mod)

    def chunks(layer, lo, hi):
        return [mods[layer, lo:hi, i * d:(i + 1) * d].reshape(hi - lo, 1, d) for i in range(6)]

    as_bf16 = lambda w: w.astype(BF16)
    w_in_b, w_att_b, w_hy_b, w_pool_b, w_out_b = map(as_bf16, (w_in, w_att_o, w_hy_o, w_pool_o, w_out))
    w1_b, w2_b, pool_w_b = map(as_bf16, (mlp_w1, mlp_w2, pool_w))

    rope_tabs = _rope_tables(l)
    dft_x = _dft_matrices(l // _hyena_blocks(l))
    dft_c = _dft_matrices(lc // _hyena_blocks(lc))

    x2 = x.reshape(b * l, d)
    c2 = ctx.reshape(b * lc, d)
    sh1, sc1 = chunks(0, 0, b)[:2]
    csh1, csc1 = chunks(0, b, b + 1)[:2]
    hx = _norm_call(x2, norm1_g[0], sc1, sh1, l)
    hc = _norm_call(c2, norm1_g[0], csc1, csh1, lc)

    for layer in range(depth):
        last = layer == depth - 1
        _, _, ga1, sh2, sc2, ga2 = chunks(layer, 0, b)
        _, _, cga1, csh2, csc2, cga2 = chunks(layer, b, b + 1)
        filt_params = (filt_w0[layer], filt_b0[layer], filt_w1[layer], filt_b1[layer], filt_freq[layer],
                       filt_w2[layer])
        conv_b = hy_conv_b[layer].reshape(1, -1)
        d_skip = hy_bias[layer].reshape(HYENA_ORDER, 1, HYENA_W)
        merge_w = (w_att_b, w_hy_b, w_pool_b, w_out_b, layer)

        gq, gk = q_norm_g[layer], k_norm_g[layer]
        if last:
            kvc = _kv_call(_proj_call(hc, w_in_b, layer, K_OFF, 2 * KV_W, F32), gk)
        else:
            zc = _proj_call(hc, w_in_b, layer, 0, GATE_OFF, F32)
            gc, qc, kvc = _gates_qkv_call(hc, w_in_b, layer, zc, gq, gk, None, lc)
            zc, qc = zc.reshape(b, lc, GATE_OFF), qc.reshape(b, lc, ATTN_W)
        kvc = kvc.reshape(b, lc, 2 * KV_DUP_W)

        zx = _proj_call(hx, w_in_b, layer, 0, GATE_OFF, F32)
        gx, qx, kvx = _gates_qkv_call(hx, w_in_b, layer, zx, gq, gk, rope_tabs, l)
        zx, qx, kvx = zx.reshape(b, l, GATE_OFF), qx.reshape(b, l, ATTN_W), kvx.reshape(b, l, 2 * KV_DUP_W)
        hy = (hy_conv_w[layer], conv_b, filt_params, d_skip, dft_x)
        ya, yh, yp = _mixers(zx, qx, kvx, kvc, sink[layer], True, hy, pool_w_b[layer], pool_scale[layer])
        xn, h2 = _merge_call(ya, yh, yp, gx, x2, ga1, norm2_g[layer], sc2, sh2, *merge_w, l)
        nxt = None if last else (norm1_g[layer + 1], *reversed(chunks(layer + 1, 0, b)[:2]))
        x2, hx = _mlp_call(h2, w1_b, w2_b, layer, xn, ga2, nxt, l)

        if not last:
            hyc = (hy_conv_w[layer], conv_b, filt_params, d_skip, dft_c)
            ya, yh, yp = _mixers(zc, qc, None, kvc, sink[layer], False, hyc, pool_w_b[layer], pool_scale[layer])
            cn, h2c = _merge_call(ya, yh, yp, gc, c2, cga1, norm2_g[layer], csc2, csh2, *merge_w, lc)
            nxt = (norm1_g[layer + 1], *reversed(chunks(layer + 1, b, b + 1)[:2]))
            c2, hc = _mlp_call(h2c, w1_b, w2_b, layer, cn, cga2, nxt, lc)

    return x2.reshape(b, l, d)
```

```python
import functools
import math

import jax
import jax.numpy as jnp
from jax import lax
from jax.experimental import pallas as pl
from jax.experimental.pallas import tpu as pltpu

D_MODEL = 2048
DEPTH = 2
GRID_W = 64
EPS = 1e-6
NEG_INF = -1e30

N_HEADS = 16
N_KV_HEADS = 4
GQA_GROUP = N_HEADS // N_KV_HEADS
HEAD_DIM = 64
ATTN_W = N_HEADS * HEAD_DIM
KV_W = N_KV_HEADS * HEAD_DIM
WINDOW = 128
ROPE_FREQS = HEAD_DIM // 4
ROPE_BASE = 10000.0

HYENA_W = D_MODEL // 4
HYENA_ORDER = 2
FILTER_BANDS = 16
FILTER_EMB = 1 + 2 * FILTER_BANDS
FILTER_HIDDEN = 64
FILTER_INNER = 2
DECAY_TARGET = 1e-2
FAST_DECAY_PCT = 0.3
SLOW_DECAY_PCT = 1.5

POOL_W = D_MODEL // 4
POOL_WINDOWS = (2, 4, 8, 16)
POOL_GROUP = POOL_W // len(POOL_WINDOWS)

N_BRANCH = 3
D_FF = 4 * D_MODEL

Q_OFF = 0
K_OFF = Q_OFF + ATTN_W
V_OFF = K_OFF + KV_W
HY_OFF = V_OFF + KV_W
POOL_OFF = HY_OFF + 3 * HYENA_W
GATE_OFF = POOL_OFF + POOL_W
IN_W = GATE_OFF + N_BRANCH * D_MODEL

V7X_LANES = 128
V7X_VMEM_LIMIT = 60 * 1024 * 1024
KV_DUP_W = N_KV_HEADS * V7X_LANES
MOD_ROWS = 24

F32 = jnp.float32
BF16 = jnp.bfloat16
HIGHEST = lax.Precision.HIGHEST


def _params(*semantics):
    return pltpu.CompilerParams(dimension_semantics=semantics, vmem_limit_bytes=V7X_VMEM_LIMIT)


def _const_spec(shape):
    zeros = (0,) * len(shape)
    return pl.BlockSpec(shape, lambda *_: zeros, pipeline_mode=pl.Buffered(1))


def _mod_spec(arr, rows_per_mod_tile):
    d = arr.shape[-1]
    if arr.shape[0] == 1:
        return pl.BlockSpec((1, 1, d), lambda i, *_: (0, 0, 0))
    return pl.BlockSpec((1, 1, d), lambda i, *_: (i // rows_per_mod_tile, 0, 0))


def _norm_mod(xf, g, sc, sh):
    y = xf * lax.rsqrt(jnp.mean(xf * xf, axis=-1, keepdims=True) + EPS)
    return (y * g) * (1.0 + sc) + sh


def _mod_body(c_ref, w_ref, b_ref, o_ref):
    c = c_ref[...]
    s = c * jax.nn.sigmoid(c)
    o_ref[0] = jnp.dot(s.astype(BF16), w_ref[0].astype(BF16), preferred_element_type=F32) + b_ref[0]


def _modulation(cc, w_mod, b_mod):
    depth, d, n = w_mod.shape
    tn = 1024
    return pl.pallas_call(
        _mod_body,
        grid=(depth, n // tn),
        in_specs=[
            pl.BlockSpec((MOD_ROWS, d), lambda l, j: (0, 0)),
            pl.BlockSpec((1, d, tn), lambda l, j: (l, 0, j)),
            pl.BlockSpec((1, 1, tn), lambda l, j: (l, 0, j)),
        ],
        out_specs=pl.BlockSpec((1, MOD_ROWS, tn), lambda l, j: (l, 0, j)),
        out_shape=jax.ShapeDtypeStruct((depth, MOD_ROWS, n), F32),
        compiler_params=_params("arbitrary", "arbitrary"),
        name="modulation",
    )(cc, w_mod, b_mod.reshape(depth, 1, n))


def _norm_body(x_ref, g_ref, sc_ref, sh_ref, o_ref):
    o_ref[...] = _norm_mod(x_ref[...], g_ref[...], sc_ref[0], sh_ref[0]).astype(o_ref.dtype)


def _norm_call(x2, g, sc, sh, rows_per_batch):
    m, d = x2.shape
    tm = min(1024, rows_per_batch if sc.shape[0] > 1 else m)
    return pl.pallas_call(
        _norm_body,
        grid=(m // tm,),
        in_specs=[
            pl.BlockSpec((tm, d), lambda i: (i, 0)),
            pl.BlockSpec((1, d), lambda i: (0, 0)),
            _mod_spec(sc, rows_per_batch // tm),
            _mod_spec(sh, rows_per_batch // tm),
        ],
        out_specs=pl.BlockSpec((tm, d), lambda i: (i, 0)),
        out_shape=jax.ShapeDtypeStruct((m, d), BF16),
        compiler_params=_params("arbitrary"),
        name="norm_mod",
    )(x2, g.reshape(1, d), sc, sh)


def _proj_body(a_ref, w_ref, o_ref):
    o_ref[...] = jnp.dot(a_ref[...], w_ref[...], preferred_element_type=F32).astype(o_ref.dtype)


W_TILE = 512


def _col_tiled(w):
    depth, k, n = w.shape
    return w.astype(BF16).reshape(depth, k, n // W_TILE, W_TILE).transpose(0, 2, 1, 3)


def _proj_call(a, w, layer, col0, n, out_dtype):
    m, k = a.shape
    tm = min(4096, m)
    tn = W_TILE
    c0 = col0 // tn
    return pl.pallas_call(
        _proj_body,
        grid=(m // tm, n // tn),
        in_specs=[
            pl.BlockSpec((tm, k), lambda i, j: (i, 0)),
            pl.BlockSpec((None, None, k, tn), lambda i, j: (layer, c0 + j, 0, 0)),
        ],
        out_specs=pl.BlockSpec((tm, tn), lambda i, j: (i, j)),
        out_shape=jax.ShapeDtypeStruct((m, n), out_dtype),
        compiler_params=_params("arbitrary", "arbitrary"),
        name="in_proj",
    )(a, w)


Q_SLABS = ATTN_W // V7X_LANES
QK_SLABS = Q_SLABS + KV_W // V7X_LANES
QKV_SLABS = QK_SLABS + KV_W // V7X_LANES


def _gates_qkv_body(a_ref, w_ref, z_ref, gain_ref, cos_ref, sup_ref, sdn_ref, g_ref, q_ref, kv_ref):
    j = pl.program_id(1)
    zg = jnp.dot(a_ref[...], w_ref[...], preferred_element_type=F32)
    g_ref[...] = (0.5 * jnp.tanh(0.5 * zg) + 0.5).astype(g_ref.dtype)

    x = z_ref[...]
    low = lax.broadcasted_iota(jnp.int32, (1, V7X_LANES), 1) < HEAD_DIM
    x2 = x * x
    ss = jnp.where(low, jnp.sum(jnp.where(low, x2, 0.0), axis=-1, keepdims=True),
                   jnp.sum(jnp.where(low, 0.0, x2), axis=-1, keepdims=True))
    inv = jnp.where(j < QK_SLABS, lax.rsqrt(ss * (1.0 / HEAD_DIM) + EPS), 1.0)
    y = _rope((x * inv) * gain_ref[0], cos_ref[...], sup_ref[...], sdn_ref[...])
    da, db = _dup_pair(y, low)

    @pl.when(j < Q_SLABS)
    def _():
        q_ref[...] = y.astype(q_ref.dtype)

    @pl.when(j >= Q_SLABS)
    def _():
        kv_ref[:, 0:V7X_LANES] = da.astype(kv_ref.dtype)
        kv_ref[:, V7X_LANES:2 * V7X_LANES] = db.astype(kv_ref.dtype)


def _gates_qkv_call(a, w, layer, z2, gq, gk, rope_tabs, seq_len):
    m, k = a.shape
    d = D_MODEL
    tm = min(2048, m)
    tn = W_TILE
    assert N_BRANCH * d // tn == QKV_SLABS and tm % seq_len == 0
    c0 = GATE_OFF // tn
    ones = jnp.ones((tm, V7X_LANES), F32)
    zeros = jnp.zeros((tm, V7X_LANES), F32)
    if rope_tabs is None:
        cos, sup, sdn = ones[None], zeros[None], zeros[None]
        tab_map = lambda i, j: (0, 0, 0)
    else:
        rep = lambda t: jnp.tile(t, (tm // seq_len, 1))
        cos, sup, sdn = (jnp.stack([rep(t), ident]) for t, ident in zip(rope_tabs, (ones, zeros, zeros)))
        tab_map = lambda i, j: ((j >= QK_SLABS).astype(jnp.int32), 0, 0)
    gains = jnp.stack([jnp.tile(gq, 2) * HEAD_DIM ** -0.5, jnp.tile(gk, 2), jnp.ones((V7X_LANES,), F32)])
    gain_map = lambda i, j: ((j >= Q_SLABS).astype(jnp.int32) + (j >= QK_SLABS).astype(jnp.int32), 0, 0)
    tab_spec = pl.BlockSpec((None, tm, V7X_LANES), tab_map)
    return pl.pallas_call(
        _gates_qkv_body,
        grid=(m // tm, QKV_SLABS),
        in_specs=[
            pl.BlockSpec((tm, k), lambda i, j: (i, 0)),
            pl.BlockSpec((None, None, k, tn), lambda i, j: (layer, c0 + j, 0, 0)),
            pl.BlockSpec((tm, V7X_LANES), lambda i, j: (i, j)),
            pl.BlockSpec((None, 1, V7X_LANES), gain_map),
            tab_spec, tab_spec, tab_spec,
        ],
        out_specs=[
            pl.BlockSpec((tm, tn), lambda i, j: (i, j)),
            pl.BlockSpec((tm, V7X_LANES), lambda i, j: (i, jnp.minimum(j, Q_SLABS - 1))),
            pl.BlockSpec((tm, 2 * V7X_LANES), lambda i, j: (i, jnp.maximum(j - Q_SLABS, 0))),
        ],
        out_shape=[
            jax.ShapeDtypeStruct((m, N_BRANCH * d), BF16),
            jax.ShapeDtypeStruct((m, ATTN_W), BF16),
            jax.ShapeDtypeStruct((m, 2 * KV_DUP_W), BF16),
        ],
        compiler_params=_params("arbitrary", "arbitrary"),
        name="gates_qkv",
    )(a, w, z2, gains.reshape(3, 1, V7X_LANES), cos, sup, sdn)


def _pair_block_diag():
    r = lax.broadcasted_iota(jnp.int32, (V7X_LANES, V7X_LANES), 0) // HEAD_DIM
    c = lax.broadcasted_iota(jnp.int32, (V7X_LANES, V7X_LANES), 1) // HEAD_DIM
    return (r == c).astype(F32)


def _head_norm(x, g, bd):
    ss = jnp.dot(x * x, bd, precision=HIGHEST, preferred_element_type=F32)
    return (x * lax.rsqrt(ss * (1.0 / HEAD_DIM) + EPS)) * g


def _rope(x, cos, sin_up, sin_dn):
    up = pltpu.roll(x, V7X_LANES - ROPE_FREQS, 1)
    dn = pltpu.roll(x, ROPE_FREQS, 1)
    return x * cos + up * sin_up + dn * sin_dn


def _dup_pair(x, low):
    r = pltpu.roll(x, HEAD_DIM, 1)
    return jnp.where(low, x, r), jnp.where(low, r, x)


def _kv_body(z_ref, gk_ref, kv_ref):
    bd = _pair_block_diag()
    low = lax.broadcasted_iota(jnp.int32, (1, V7X_LANES), 1) < HEAD_DIM
    for s in range(2 * KV_W // V7X_LANES):
        x = z_ref[:, s * V7X_LANES:(s + 1) * V7X_LANES]
        if s < KV_W // V7X_LANES:
            x = _head_norm(x, gk_ref[...], bd)
        a, b = _dup_pair(x, low)
        base = 2 * s * V7X_LANES
        kv_ref[:, base:base + V7X_LANES] = a.astype(kv_ref.dtype)
        kv_ref[:, base + V7X_LANES:base + 2 * V7X_LANES] = b.astype(kv_ref.dtype)


def _kv_call(z2, gk):
    m, nz = z2.shape
    tm = min(512, m)
    return pl.pallas_call(
        _kv_body,
        grid=(m // tm,),
        in_specs=[pl.BlockSpec((tm, nz), lambda i: (i, 0)),
                  pl.BlockSpec((1, V7X_LANES), lambda i: (0, 0))],
        out_specs=pl.BlockSpec((tm, 2 * KV_DUP_W), lambda i: (i, 0)),
        out_shape=jax.ShapeDtypeStruct((m, 2 * KV_DUP_W), BF16),
        compiler_params=_params("arbitrary"),
        name="kv_prep",
    )(z2, jnp.tile(gk, 2).reshape(1, V7X_LANES))


def _rope_tables(l):
    rows = l // GRID_W
    row = jnp.repeat(jnp.arange(rows, dtype=F32), GRID_W)
    col = jnp.tile(jnp.arange(GRID_W, dtype=F32), rows)
    inv = ROPE_BASE ** (-jnp.arange(ROPE_FREQS, dtype=F32) / ROPE_FREQS)
    ang = jnp.stack([row[:, None] * inv, col[:, None] * inv], axis=1)
    cos, sin = jnp.cos(ang), jnp.sin(ang)
    zero = jnp.zeros_like(sin)
    cos_h = jnp.stack([cos, cos], axis=2).reshape(l, HEAD_DIM)
    sup_h = jnp.stack([-sin, zero], axis=2).reshape(l, HEAD_DIM)
    sdn_h = jnp.stack([zero, sin], axis=2).reshape(l, HEAD_DIM)
    return tuple(jnp.tile(t, (1, 2)) for t in (cos_h, sup_h, sdn_h))


def _attn_body(*refs, local, tq):
    it = iter(refs)
    sink_ref = next(it)
    q_ref = next(it)
    if local:
        kp_ref, kc_ref, kn_ref, vp_ref, vc_ref, vn_ref = (next(it) for _ in range(6))
    kx_ref, vx_ref = next(it), next(it)
    o_ref = next(it)

    i = pl.program_id(1)
    nb = pl.num_programs(1)
    low = lax.broadcasted_iota(jnp.int32, (1, V7X_LANES), 1) < HEAD_DIM
    rows = GQA_GROUP * tq
    if local:
        qi = lax.broadcasted_iota(jnp.int32, (rows, tq), 0) % tq
        kj = lax.broadcasted_iota(jnp.int32, (rows, tq), 1)
        mask_prev = (kj >= qi) & (i > 0)
        mask_next = (kj <= qi) & (i < nb - 1)
    row_head = lax.broadcasted_iota(jnp.int32, (rows, 1), 0) // tq
    zero = jnp.zeros((), q_ref.dtype)

    for e, h in [(e, h) for e in range(q_ref.shape[0]) for h in range(N_KV_HEADS)]:
        hs = slice(h * V7X_LANES, (h + 1) * V7X_LANES)
        qa = q_ref[e, :, 2 * h * V7X_LANES:(2 * h + 1) * V7X_LANES]
        qb = q_ref[e, :, (2 * h + 1) * V7X_LANES:(2 * h + 2) * V7X_LANES]
        qs = jnp.concatenate([jnp.where(low, qa, zero), jnp.where(low, zero, qa),
                              jnp.where(low, qb, zero), jnp.where(low, zero, qb)], axis=0)
        kparts, vparts, masks = [kx_ref[e, :, hs]], [vx_ref[e, :, hs]], {}
        if local:
            kparts = [kp_ref[e, :, hs], kc_ref[e, :, hs], kn_ref[e, :, hs]] + kparts
            vparts = [vp_ref[e, :, hs], vc_ref[e, :, hs], vn_ref[e, :, hs]] + vparts
            masks = {0: mask_prev, 2: mask_next}
        k_all = jnp.concatenate(kparts, axis=0)
        v_all = jnp.concatenate(vparts, axis=0)

        sink = jnp.zeros((rows, 1), F32)
        for g in range(GQA_GROUP):
            sink = jnp.where(row_head == g, sink_ref[GQA_GROUP * h + g], sink)
        s_all = lax.dot_general(qs, k_all, (((1,), (1,)), ((), ())), preferred_element_type=F32)
        chunks = []
        for c in range(k_all.shape[0] // tq):
            s = s_all[:, c * tq:(c + 1) * tq]
            chunks.append(jnp.where(masks[c], s, NEG_INF) if c in masks else s)
        m = jnp.maximum(sink, jnp.max(functools.reduce(jnp.maximum, chunks), axis=-1, keepdims=True))
        probs = [jnp.exp(s - m) for s in chunks]
        denom = jnp.exp(sink - m) + jnp.sum(functools.reduce(jnp.add, probs), axis=-1, keepdims=True)
        p_all = jnp.concatenate([p.astype(v_all.dtype) for p in probs], axis=1)
        o = jnp.dot(p_all, v_all, preferred_element_type=F32) / denom
        oa = jnp.where(low, o[0:tq], o[tq:2 * tq])
        ob = jnp.where(low, o[2 * tq:3 * tq], o[3 * tq:4 * tq])
        o_ref[e, :, 2 * h * V7X_LANES:(2 * h + 1) * V7X_LANES] = oa.astype(o_ref.dtype)
        o_ref[e, :, (2 * h + 1) * V7X_LANES:(2 * h + 2) * V7X_LANES] = ob.astype(o_ref.dtype)


def _attn_call(q, kv, kvx, sink, local):
    b, l, _ = q.shape
    lx = kvx.shape[1]
    tq = WINDOW
    nb = l // tq
    ne = math.gcd(b, 4)
    blk = lambda w: (ne, tq, w)
    in_specs = [pl.BlockSpec(memory_space=pltpu.SMEM),
                pl.BlockSpec(blk(ATTN_W), lambda bi, i: (bi, i, 0))]
    args = [sink, q]
    if local:
        for half in (0, 1):
            for mp in (lambda bi, i, half=half: (bi, jnp.maximum(i - 1, 0), half),
                       lambda bi, i, half=half: (bi, i, half),
                       lambda bi, i, half=half: (bi, jnp.minimum(i + 1, nb - 1), half)):
                in_specs.append(pl.BlockSpec(blk(KV_DUP_W), mp))
                args.append(kv)
    for half in (0, 1):
        in_specs.append(pl.BlockSpec((ne, lx, KV_DUP_W), lambda bi, i, half=half: (bi, 0, half)))
        args.append(kvx)
    return pl.pallas_call(
        functools.partial(_attn_body, local=local, tq=tq),
        grid=(b // ne, nb),
        in_specs=in_specs,
        out_specs=pl.BlockSpec(blk(ATTN_W), lambda bi, i: (bi, i, 0)),
        out_shape=jax.ShapeDtypeStruct((b, l, ATTN_W), BF16),
        compiler_params=_params("arbitrary", "arbitrary"),
        name="attention",
    )(*args)


def _filter_body(z_ref, w0_ref, b0_ref, w1_ref, b1_ref, fr_ref, w2_ref, dec_ref, o_ref):
    fr = fr_ref[...]
    dot = functools.partial(jnp.dot, precision=HIGHEST, preferred_element_type=F32)
    h = jnp.sin(fr * (dot(z_ref[...], w0_ref[...]) + b0_ref[...]))
    for i in range(FILTER_INNER):
        h = jnp.sin(fr * (dot(h, w1_ref[i]) + b1_ref[i]))
    dec = dec_ref[...]
    for s in range(2 * HYENA_ORDER):
        sl = slice(s * HYENA_W, (s + 1) * HYENA_W)
        o_ref[:, sl] = dot(h, w2_ref[:, sl]) * dec


def _filter_features(l):
    t = jnp.linspace(0.0, 1.0, l, dtype=F32)[:, None]
    w = 2.0 * math.pi * jnp.arange(l, dtype=F32)[:, None] / l
    bands = jnp.linspace(1e-4, FILTER_BANDS - 1, FILTER_BANDS, dtype=F32)[None, :]
    z = jnp.concatenate([t, jnp.cos(bands * w), -jnp.sin(bands * w)], axis=-1)
    deltas = jnp.linspace(math.log(DECAY_TARGET) / SLOW_DECAY_PCT, math.log(DECAY_TARGET) / FAST_DECAY_PCT,
                          HYENA_W, dtype=F32)
    decay = jnp.exp(-t * jnp.abs(deltas))
    return jnp.pad(z, ((0, 0), (0, V7X_LANES - FILTER_EMB))), decay


def _filter_call(l, w0, b0, w1, b1, freq, w2):
    zfeat, decay = _filter_features(l)
    w0p = jnp.pad(w0, ((0, V7X_LANES - FILTER_EMB), (0, 0)))
    tl = min(512, l)
    nf = 2 * HYENA_ORDER * HYENA_W
    full = lambda shape: pl.BlockSpec(shape, lambda i: (0,) * len(shape))
    return pl.pallas_call(
        _filter_body,
        grid=(l // tl,),
        in_specs=[
            pl.BlockSpec((tl, V7X_LANES), lambda i: (i, 0)),
            full((V7X_LANES, FILTER_HIDDEN)),
            full((1, FILTER_HIDDEN)),
            full((FILTER_INNER, FILTER_HIDDEN, FILTER_HIDDEN)),
            full((FILTER_INNER, 1, FILTER_HIDDEN)),
            full((1, FILTER_HIDDEN)),
            full((FILTER_HIDDEN, nf)),
            pl.BlockSpec((tl, HYENA_W), lambda i: (i, 0)),
        ],
        out_specs=pl.BlockSpec((tl, nf), lambda i: (i, 0)),
        out_shape=jax.ShapeDtypeStruct((l, nf), F32),
        compiler_params=_params("arbitrary"),
        name="hyena_filter",
    )(zfeat, w0p, b0.reshape(1, -1), w1, b1.reshape(FILTER_INNER, 1, -1), freq.reshape(1, -1), w2, decay)


def _hyena_blocks(l):
    return max(1, min(4, l // V7X_LANES))


def _dft_matrices(blk):
    n = 2 * blk
    r = jnp.arange(blk, dtype=jnp.int32)
    ang = ((r[:, None] * r[None, :]) % n).astype(F32) * (2.0 * math.pi / n)
    return jnp.cos(ang).astype(BF16), jnp.sin(ang).astype(BF16)


def _alternating(l):
    row = lax.broadcasted_iota(jnp.int32, (l, 1), 0)
    return row, jnp.where(row % 2 == 0, 1.0, -1.0).astype(F32)


def _spectrum_body(hf_ref, hb_ref, fc_ref, fs_ref, ka_ref, kb_ref, kn_ref, *, nblk):
    l = hf_ref.shape[0]
    b = l // nblk
    n = 2 * b
    row = lax.broadcasted_iota(jnp.int32, (l, 1), 0)
    _, alt = _alternating(b)
    hf = hf_ref[...]
    hbs = jnp.where(row == 0, 0.0, pltpu.roll(hb_ref[...], 1, 0))
    fc, fs = fc_ref[...], fs_ref[...]

    def transforms(h):
        out = []
        for k in range(nblk):
            hk = h[k * b:(k + 1) * b]
            hk16 = hk.astype(BF16)
            out.append(dict(
                c=jnp.dot(fc, hk16, preferred_element_type=F32),
                s=jnp.dot(fs, hk16, preferred_element_type=F32),
                first16=hk16[0:1].astype(F32),
                first=hk[0:1],
                alt=jnp.sum(hk * alt, axis=0, keepdims=True)))
        return out

    tf, tb = transforms(hf), transforms(hbs)
    brow = lax.broadcasted_iota(jnp.int32, (b, 1), 0)
    w_re = jnp.where(brow == 0, 1.0 / n, 2.0 / n)
    for d in range(-(nblk - 1), nblk):
        idx = d + nblk - 1
        if d == 0:
            kre = tf[0]["c"] + tb[0]["c"]
            kim = tb[0]["s"] - tf[0]["s"]
            kn = tf[0]["alt"] + tb[0]["alt"]
        else:
            t, e, sg = (tf, d, -1.0) if d > 0 else (tb, -d, 1.0)
            kre = t[e]["c"] + alt * (t[e - 1]["c"] - t[e - 1]["first16"])
            kim = sg * (t[e]["s"] + alt * t[e - 1]["s"])
            kn = t[e]["alt"] + t[e - 1]["alt"] - t[e - 1]["first"]
        ka_ref[0, idx] = (kre * w_re).astype(ka_ref.dtype)
        kb_ref[0, idx] = (kim * (2.0 / n)).astype(kb_ref.dtype)
        kn_ref[0, idx] = kn * (1.0 / n)


def _spectrum_call(filt, fc, fs, nblk, tc):
    l = filt.shape[0]
    b = l // nblk
    nct = HYENA_W // tc
    nlag = 2 * nblk - 1
    return pl.pallas_call(
        functools.partial(_spectrum_body, nblk=nblk),
        grid=(HYENA_ORDER, nct),
        in_specs=[
            pl.BlockSpec((l, tc), lambda o, c: (0, 2 * nct * o + c)),
            pl.BlockSpec((l, tc), lambda o, c: (0, 2 * nct * o + nct + c)),
            _const_spec((b, b)),
            _const_spec((b, b)),
        ],
        out_specs=[
            pl.BlockSpec((1, nlag, b, tc), lambda o, c: (o, 0, 0, c)),
            pl.BlockSpec((1, nlag, b, tc), lambda o, c: (o, 0, 0, c)),
            pl.BlockSpec((1, nlag, 1, tc), lambda o, c: (o, 0, 0, c)),
        ],
        out_shape=[
            jax.ShapeDtypeStruct((HYENA_ORDER, nlag, b, HYENA_W), BF16),
            jax.ShapeDtypeStruct((HYENA_ORDER, nlag, b, HYENA_W), BF16),
            jax.ShapeDtypeStruct((HYENA_ORDER, nlag, 1, HYENA_W), F32),
        ],
        compiler_params=_params("arbitrary", "arbitrary"),
        name="hyena_spectrum",
    )(filt, filt, fc, fs)


def _conv3(x, w_ref, b_ref, row):
    l = x.shape[0]
    xm = jnp.where(row == 0, 0.0, pltpu.roll(x, 1, 0))
    xp = jnp.where(row == l - 1, 0.0, pltpu.roll(x, l - 1, 0))
    return xm * w_ref[0:1, :] + x * w_ref[1:2, :] + xp * w_ref[2:3, :] + b_ref[...]


def _fftconv_body(*refs, conv_u, nblk):
    it = iter(refs)
    u_ref = next(it)
    if conv_u:
        uw_ref, ub_ref = next(it), next(it)
    g_ref, gw_ref, gb_ref = next(it), next(it), next(it)
    ka_ref, kb_ref, kn_ref, d_ref, fc_ref, fs_ref, o_ref = (next(it) for _ in range(7))

    l = u_ref.shape[1]
    b = l // nblk
    row = lax.broadcasted_iota(jnp.int32, (l, 1), 0)
    _, alt = _alternating(b)
    u = u_ref[0].astype(F32)
    if conv_u:
        u = _conv3(u, uw_ref, ub_ref, row)
    gate = _conv3(g_ref[0], gw_ref, gb_ref, row)
    fc, fs = fc_ref[...], fs_ref[...]

    ps, qs, ns = [], [], []
    for j in range(nblk):
        uj = u[j * b:(j + 1) * b]
        uj16 = uj.astype(BF16)
        ps.append(jnp.dot(fc, uj16, preferred_element_type=F32).astype(BF16))
        qs.append(jnp.dot(fs, uj16, preferred_element_type=F32).astype(BF16))
        ns.append(jnp.sum(uj * alt, axis=0, keepdims=True))
    for i in range(nblk):
        r = t = nyq = None
        for j in range(nblk):
            lag = i - j + nblk - 1
            ka, kb = ka_ref[0, lag], kb_ref[0, lag]
            dr = ps[j] * ka + qs[j] * kb
            dt = qs[j] * ka - ps[j] * kb
            dn = ns[j] * kn_ref[0, lag]
            r, t, nyq = (dr, dt, dn) if j == 0 else (r + dr, t + dt, nyq + dn)
        y = jnp.dot(fc, r, preferred_element_type=F32) + jnp.dot(fs, t, preferred_element_type=F32)
        rows = slice(i * b, (i + 1) * b)
        y = y + alt * nyq + u[rows] * d_ref[0]
        o_ref[0, rows, :] = (gate[rows] * y).astype(o_ref.dtype)


def _fftconv_call(u, u_col0, z, gate_col0, conv_w, conv_b, spectra, d_skip, order, fc, fs, nblk, tc, out_dtype):
    b, l, _ = z.shape
    conv_u = u is z
    nct = HYENA_W // tc
    ka, kb, kn = spectra
    blk = l // nblk
    nlag = 2 * nblk - 1
    col = lambda c0: (lambda c, bi: (bi, 0, c0 // tc + c))
    wcol = lambda c0: (lambda c, bi: (0, (c0 - HY_OFF) // tc + c))
    in_specs = [pl.BlockSpec((1, l, tc), col(u_col0))]
    args = [u]
    if conv_u:
        in_specs += [pl.BlockSpec((3, tc), wcol(u_col0)), pl.BlockSpec((1, tc), wcol(u_col0))]
        args += [conv_w, conv_b]
    in_specs += [pl.BlockSpec((1, l, tc), col(gate_col0)),
                 pl.BlockSpec((3, tc), wcol(gate_col0)), pl.BlockSpec((1, tc), wcol(gate_col0))]
    args += [z, conv_w, conv_b]
    spec = lambda rows: pl.BlockSpec((1, nlag, rows, tc), lambda c, bi: (order, 0, 0, c),
                                     pipeline_mode=pl.Buffered(1))
    in_specs += [spec(blk), spec(blk), spec(1),
                 pl.BlockSpec((1, 1, tc), lambda c, bi: (order, 0, c), pipeline_mode=pl.Buffered(1)),
                 _const_spec((blk, blk)), _const_spec((blk, blk))]
    args += [ka, kb, kn, d_skip, fc, fs]
    return pl.pallas_call(
        functools.partial(_fftconv_body, conv_u=conv_u, nblk=nblk),
        grid=(nct, b),
        in_specs=in_specs,
        out_specs=pl.BlockSpec((1, l, tc), lambda c, bi: (bi, 0, c)),
        out_shape=jax.ShapeDtypeStruct((b, l, HYENA_W), out_dtype),
        compiler_params=_params("arbitrary", "arbitrary"),
        name="hyena_conv",
    )(*args)


def _pool_body(x_ref, w_ref, s_ref, o_ref):
    l = x_ref.shape[1]
    row = lax.broadcasted_iota(jnp.int32, (l, 1), 0)
    for g, win in enumerate(POOL_WINDOWS):
        half = win // 2
        sl = slice(g * POOL_GROUP, (g + 1) * POOL_GROUP)
        x = x_ref[0, :, sl]

        def shifted(a, k):
            return jnp.where((row >= k) & (row < l + k), pltpu.roll(a, k % l, 0), 0.0)

        back = fwd = x
        span = 1
        while span < half:
            back = back + shifted(back, span)
            fwd = fwd + shifted(fwd, -span)
            span *= 2
        acc = shifted(back, 1) + fwd
        cnt = (jnp.minimum(row + half, l) - jnp.maximum(row - half, 0)).astype(F32)
        d = acc / cnt - x
        y = jnp.dot(d.astype(BF16), w_ref[g], preferred_element_type=F32)
        o_ref[0, :, sl] = (y * s_ref[:, sl]).astype(o_ref.dtype)


def _pool_call(z, w_grp, scale):
    b, l, _ = z.shape
    ng = len(POOL_WINDOWS)
    return pl.pallas_call(
        _pool_body,
        grid=(b,),
        in_specs=[
            pl.BlockSpec((1, l, POOL_W), lambda bi: (bi, 0, POOL_OFF // POOL_W)),
            pl.BlockSpec((ng, POOL_GROUP, POOL_GROUP), lambda bi: (0, 0, 0)),
            pl.BlockSpec((1, POOL_W), lambda bi: (0, 0)),
        ],
        out_specs=pl.BlockSpec((1, l, POOL_W), lambda bi: (bi, 0, 0)),
        out_shape=jax.ShapeDtypeStruct((b, l, POOL_W), BF16),
        compiler_params=_params("arbitrary"),
        name="pool",
    )(z, w_grp, scale.reshape(1, POOL_W))


def _merge_body(ya_ref, yh_ref, yp_ref, gt_ref, x_ref, ga_ref, g2_ref, sc_ref, sh_ref,
                wa_ref, wh_ref, wp_ref, wo_ref, xn_ref, h2_ref):
    d = x_ref.shape[1]
    cj = 512
    ya, yh, yp = ya_ref[...], yh_ref[...], yp_ref[...]
    acc = jnp.zeros(x_ref.shape, F32)
    for j in range(d // cj):
        sl = slice(j * cj, (j + 1) * cj)
        gate = lambda br: gt_ref[:, br * d + j * cj:br * d + (j + 1) * cj].astype(F32)
        m = (gate(0) * jnp.dot(ya, wa_ref[:, sl], preferred_element_type=F32)
             + gate(1) * jnp.dot(yh, wh_ref[:, sl], preferred_element_type=F32)
             + gate(2) * jnp.dot(yp, wp_ref[:, sl], preferred_element_type=F32))
        acc = acc + jnp.dot(m.astype(BF16), wo_ref[sl, :], preferred_element_type=F32)
    xn = x_ref[...] + ga_ref[0] * acc
    xn_ref[...] = xn
    h2_ref[...] = _norm_mod(xn, g2_ref[...], sc_ref[0], sh_ref[0]).astype(h2_ref.dtype)


def _merge_call(ya, yh, yp, gates, x2, ga1, g2, sc2, sh2, wa, wh, wp, wo, layer, rows_per_batch):
    m, d = x2.shape
    tm = min(512, rows_per_batch)
    rpt = rows_per_batch // tm
    rows = lambda w: pl.BlockSpec((tm, w), lambda i: (i, 0))
    weight = lambda w: pl.BlockSpec((None,) + w.shape[1:], lambda i: (layer, 0, 0), pipeline_mode=pl.Buffered(1))
    return pl.pallas_call(
        _merge_body,
        grid=(m // tm,),
        in_specs=[rows(ATTN_W), rows(HYENA_W), rows(POOL_W), rows(N_BRANCH * d), rows(d),
                  _mod_spec(ga1, rpt), pl.BlockSpec((1, d), lambda i: (0, 0)),
                  _mod_spec(sc2, rpt), _mod_spec(sh2, rpt),
                  weight(wa), weight(wh), weight(wp), weight(wo)],
        out_specs=[rows(d), rows(d)],
        out_shape=[jax.ShapeDtypeStruct((m, d), F32), jax.ShapeDtypeStruct((m, d), BF16)],
        compiler_params=_params("arbitrary"),
        name="merge",
    )(ya, yh, yp, gates, x2, ga1, g2.reshape(1, d), sc2, sh2, wa, wh, wp, wo)


def _mlp_body(*refs, has_next):
    it = iter(refs)
    h_ref, w1_ref, w2_ref, x_ref, ga_ref = (next(it) for _ in range(5))
    if has_next:
        gn_ref, sc_ref, sh_ref = next(it), next(it), next(it)
    o_ref = next(it)
    hn_ref = next(it) if has_next else None

    f = pl.program_id(1)
    last = pl.num_programs(1) - 1
    tm, d = o_ref.shape
    cn = 512
    rb = min(256, tm)

    def hidden(rows):
        a = jnp.dot(h_ref[rows, :], w1_ref[...], preferred_element_type=F32)
        return jnp.square(jnp.maximum(a, 0.0)).astype(BF16)

    @pl.when(f == 0)
    def _():
        a = hidden(slice(None))
        for n0 in range(0, d, cn):
            o_ref[:, n0:n0 + cn] = jnp.dot(a, w2_ref[:, n0:n0 + cn], preferred_element_type=F32)

    @pl.when((f > 0) & (f < last))
    def _():
        a = hidden(slice(None))
        for n0 in range(0, d, cn):
            o_ref[:, n0:n0 + cn] += jnp.dot(a, w2_ref[:, n0:n0 + cn], preferred_element_type=F32)

    @pl.when(f == last)
    def _():
        for r0 in range(0, tm, rb):
            rows = slice(r0, r0 + rb)
            acc = o_ref[rows, :] + jnp.dot(hidden(rows), w2_ref[...], preferred_element_type=F32)
            xo = x_ref[rows, :] + ga_ref[0] * acc
            o_ref[rows, :] = xo
            if has_next:
                hn_ref[rows, :] = _norm_mod(xo, gn_ref[...], sc_ref[0], sh_ref[0]).astype(hn_ref.dtype)


def _mlp_call(h2, w1, w2, layer, xn, ga2, nxt, rows_per_batch):
    m, d = xn.shape
    ff = w2.shape[1]
    tm, tf = min(1024, rows_per_batch if ga2.shape[0] > 1 else m), 512
    rpt = max(rows_per_batch // tm, 1)
    assert ff // tf >= 2
    assert tf == W_TILE
    has_next = nxt is not None
    rows = pl.BlockSpec((tm, d), lambda i, f: (i, 0))
    in_specs = [rows, pl.BlockSpec((None, None, d, tf), lambda i, f: (layer, f, 0, 0)),
                pl.BlockSpec((None, tf, d), lambda i, f: (layer, f, 0)),
                pl.BlockSpec((tm, d), lambda i, f: (i, 0), pipeline_mode=pl.Buffered(1)), _mod_spec(ga2, rpt)]
    args = [h2, w1, w2, xn, ga2]
    out_specs = [rows]
    out_shape = [jax.ShapeDtypeStruct((m, d), F32)]
    if has_next:
        gn, scn, shn = nxt
        in_specs += [pl.BlockSpec((1, d), lambda i, f: (0, 0)), _mod_spec(scn, rpt), _mod_spec(shn, rpt)]
        args += [gn.reshape(1, d), scn, shn]
        out_specs.append(rows)
        out_shape.append(jax.ShapeDtypeStruct((m, d), BF16))
    outs = pl.pallas_call(
        functools.partial(_mlp_body, has_next=has_next),
        grid=(m // tm, ff // tf),
        in_specs=in_specs,
        out_specs=out_specs,
        out_shape=out_shape,
        compiler_params=_params("arbitrary", "arbitrary"),
        name="mlp",
    )(*args)
    return (outs[0], outs[1]) if has_next else (outs[0], None)


def _mixers(z, q, kv, kvx, sink, local, hy, pool_w, pool_scale):
    b, l, _ = z.shape
    y_att = _attn_call(q, kv, kvx, sink, local)
    conv_w, conv_b, filt_params, d_skip, (fc, fs) = hy
    nblk = _hyena_blocks(l)
    tc = 256
    spectra = _spectrum_call(_filter_call(l, *filt_params), fc, fs, nblk, tc)
    conv = functools.partial(_fftconv_call, conv_w=conv_w, conv_b=conv_b, spectra=spectra, d_skip=d_skip,
                             fc=fc, fs=fs, nblk=nblk, tc=tc)
    z1 = conv(z, HY_OFF, z, HY_OFF + HYENA_W, order=0, out_dtype=F32)
    y_hy = conv(z1, 0, z, HY_OFF + 2 * HYENA_W, order=1, out_dtype=BF16)
    y_pool = _pool_call(z, pool_w, pool_scale)
    return (y_att.reshape(b * l, ATTN_W), y_hy.reshape(b * l, HYENA_W), y_pool.reshape(b * l, POOL_W))


def kernel(x, c, ctx, c_ctx, norm1_g, norm2_g, w_mod, b_mod, w_in, q_norm_g, k_norm_g, sink, hy_conv_w, hy_conv_b, filt_w0, filt_b0, filt_w1, filt_b1, filt_freq, filt_w2, hy_bias, pool_w, pool_scale, w_att_o, w_hy_o, w_pool_o, w_out, mlp_w1, mlp_w2):
    b, l, d = x.shape
    lc = ctx.shape[1]
    depth = w_mod.shape[0]

    cc = jnp.concatenate([c, c_ctx[None, :], jnp.zeros((MOD_ROWS - b - 1, d), F32)], axis=0)
    mods = _modulation(cc, w_mod, b_mod)

    def chunks(layer, lo, hi):
        return [mods[layer, lo:hi, i * d:(i + 1) * d].reshape(hi - lo, 1, d) for i in range(6)]

    as_bf16 = lambda w: w.astype(BF16)
    w_att_b, w_hy_b, w_pool_b, w_out_b = map(as_bf16, (w_att_o, w_hy_o, w_pool_o, w_out))
    w_in_b, w1_b = _col_tiled(w_in), _col_tiled(mlp_w1)
    w2_b, pool_w_b = as_bf16(mlp_w2), as_bf16(pool_w)

    rope_tabs = _rope_tables(l)
    dft_x = _dft_matrices(l // _hyena_blocks(l))
    dft_c = _dft_matrices(lc // _hyena_blocks(lc))

    x2 = x.reshape(b * l, d)
    c2 = ctx.reshape(b * lc, d)
    sh1, sc1 = chunks(0, 0, b)[:2]
    csh1, csc1 = chunks(0, b, b + 1)[:2]
    hx = _norm_call(x2, norm1_g[0], sc1, sh1, l)
    hc = _norm_call(c2, norm1_g[0], csc1, csh1, lc)

    for layer in range(depth):
        last = layer == depth - 1
        _, _, ga1, sh2, sc2, ga2 = chunks(layer, 0, b)
        _, _, cga1, csh2, csc2, cga2 = chunks(layer, b, b + 1)
        filt_params = (filt_w0[layer], filt_b0[layer], filt_w1[layer], filt_b1[layer], filt_freq[layer],
                       filt_w2[layer])
        conv_b = hy_conv_b[layer].reshape(1, -1)
        d_skip = hy_bias[layer].reshape(HYENA_ORDER, 1, HYENA_W)
        merge_w = (w_att_b, w_hy_b, w_pool_b, w_out_b, layer)

        gq, gk = q_norm_g[layer], k_norm_g[layer]
        if last:
            kvc = _kv_call(_proj_call(hc, w_in_b, layer, K_OFF, 2 * KV_W, F32), gk)
        else:
            zc = _proj_call(hc, w_in_b, layer, 0, GATE_OFF, F32)
            gc, qc, kvc = _gates_qkv_call(hc, w_in_b, layer, zc, gq, gk, None, lc)
            zc, qc = zc.reshape(b, lc, GATE_OFF), qc.reshape(b, lc, ATTN_W)
        kvc = kvc.reshape(b, lc, 2 * KV_DUP_W)

        zx = _proj_call(hx, w_in_b, layer, 0, GATE_OFF, F32)
        gx, qx, kvx = _gates_qkv_call(hx, w_in_b, layer, zx, gq, gk, rope_tabs, l)
        zx, qx, kvx = zx.reshape(b, l, GATE_OFF), qx.reshape(b, l, ATTN_W), kvx.reshape(b, l, 2 * KV_DUP_W)
        hy = (hy_conv_w[layer], conv_b, filt_params, d_skip, dft_x)
        ya, yh, yp = _mixers(zx, qx, kvx, kvc, sink[layer], True, hy, pool_w_b[layer], pool_scale[layer])
        xn, h2 = _merge_call(ya, yh, yp, gx, x2, ga1, norm2_g[layer], sc2, sh2, *merge_w, l)
        nxt = None if last else (norm1_g[layer + 1], *reversed(chunks(layer + 1, 0, b)[:2]))
        x2, hx = _mlp_call(h2, w1_b, w2_b, layer, xn, ga2, nxt, l)

        if not last:
            hyc = (hy_conv_w[layer], conv_b, filt_params, d_skip, dft_c)
            ya, yh, yp = _mixers(zc, qc, None, kvc, sink[layer], False, hyc, pool_w_b[layer], pool_scale[layer])
            cn, h2c = _merge_call(ya, yh, yp, gc, c2, cga1, norm2_g[layer], csc2, csh2, *merge_w, lc)
            nxt = (norm1_g[layer + 1], *reversed(chunks(layer + 1, b, b + 1)[:2]))
            c2, hc = _mlp_call(h2c, w1_b, w2_b, layer, cn, cga2, nxt, lc)

    return x2.reshape(b, l, d)
```

```python
import functools
import math

import jax
import jax.numpy as jnp
from jax import lax
from jax.experimental import pallas as pl
from jax.experimental.pallas import tpu as pltpu

D_MODEL = 2048
DEPTH = 2
GRID_W = 64
EPS = 1e-6
NEG_INF = -1e30

N_HEADS = 16
N_KV_HEADS = 4
GQA_GROUP = N_HEADS // N_KV_HEADS
HEAD_DIM = 64
ATTN_W = N_HEADS * HEAD_DIM
KV_W = N_KV_HEADS * HEAD_DIM
WINDOW = 128
ROPE_FREQS = HEAD_DIM // 4
ROPE_BASE = 10000.0

HYENA_W = D_MODEL // 4
HYENA_ORDER = 2
FILTER_BANDS = 16
FILTER_EMB = 1 + 2 * FILTER_BANDS
FILTER_HIDDEN = 64
FILTER_INNER = 2
DECAY_TARGET = 1e-2
FAST_DECAY_PCT = 0.3
SLOW_DECAY_PCT = 1.5

POOL_W = D_MODEL // 4
POOL_WINDOWS = (2, 4, 8, 16)
POOL_GROUP = POOL_W // len(POOL_WINDOWS)

N_BRANCH = 3
D_FF = 4 * D_MODEL

Q_OFF = 0
K_OFF = Q_OFF + ATTN_W
V_OFF = K_OFF + KV_W
HY_OFF = V_OFF + KV_W
POOL_OFF = HY_OFF + 3 * HYENA_W
GATE_OFF = POOL_OFF + POOL_W
IN_W = GATE_OFF + N_BRANCH * D_MODEL

V7X_LANES = 128
V7X_VMEM_LIMIT = 60 * 1024 * 1024
KV_DUP_W = N_KV_HEADS * V7X_LANES
MOD_ROWS = 24

F32 = jnp.float32
BF16 = jnp.bfloat16
HIGHEST = lax.Precision.HIGHEST


def _params(*semantics):
    return pltpu.CompilerParams(dimension_semantics=semantics, vmem_limit_bytes=V7X_VMEM_LIMIT)


def _const_spec(shape):
    zeros = (0,) * len(shape)
    return pl.BlockSpec(shape, lambda *_: zeros, pipeline_mode=pl.Buffered(1))


def _mod_spec(arr, rows_per_mod_tile):
    d = arr.shape[-1]
    if arr.shape[0] == 1:
        return pl.BlockSpec((1, 1, d), lambda i, *_: (0, 0, 0))
    return pl.BlockSpec((1, 1, d), lambda i, *_: (i // rows_per_mod_tile, 0, 0))


def _norm_mod(xf, g, sc, sh):
    y = xf * lax.rsqrt(jnp.mean(xf * xf, axis=-1, keepdims=True) + EPS)
    return (y * g) * (1.0 + sc) + sh


def _mod_body(c_ref, w_ref, b_ref, o_ref):
    c = c_ref[...]
    s = c * jax.nn.sigmoid(c)
    o_ref[0] = jnp.dot(s.astype(BF16), w_ref[0].astype(BF16), preferred_element_type=F32) + b_ref[0]


def _modulation(cc, w_mod, b_mod):
    depth, d, n = w_mod.shape
    tn = 1024
    return pl.pallas_call(
        _mod_body,
        grid=(depth, n // tn),
        in_specs=[
            pl.BlockSpec((MOD_ROWS, d), lambda l, j: (0, 0)),
            pl.BlockSpec((1, d, tn), lambda l, j: (l, 0, j)),
            pl.BlockSpec((1, 1, tn), lambda l, j: (l, 0, j)),
        ],
        out_specs=pl.BlockSpec((1, MOD_ROWS, tn), lambda l, j: (l, 0, j)),
        out_shape=jax.ShapeDtypeStruct((depth, MOD_ROWS, n), F32),
        compiler_params=_params("arbitrary", "arbitrary"),
        name="modulation",
    )(cc, w_mod, b_mod.reshape(depth, 1, n))


def _norm_body(x_ref, g_ref, sc_ref, sh_ref, o_ref):
    o_ref[...] = _norm_mod(x_ref[...], g_ref[...], sc_ref[0], sh_ref[0]).astype(o_ref.dtype)


def _norm_call(x2, g, sc, sh, rows_per_batch):
    m, d = x2.shape
    tm = min(1024, rows_per_batch if sc.shape[0] > 1 else m)
    return pl.pallas_call(
        _norm_body,
        grid=(m // tm,),
        in_specs=[
            pl.BlockSpec((tm, d), lambda i: (i, 0)),
            pl.BlockSpec((1, d), lambda i: (0, 0)),
            _mod_spec(sc, rows_per_batch // tm),
            _mod_spec(sh, rows_per_batch // tm),
        ],
        out_specs=pl.BlockSpec((tm, d), lambda i: (i, 0)),
        out_shape=jax.ShapeDtypeStruct((m, d), BF16),
        compiler_params=_params("arbitrary"),
        name="norm_mod",
    )(x2, g.reshape(1, d), sc, sh)


def _proj_body(a_ref, w_ref, o_ref):
    o_ref[...] = jnp.dot(a_ref[...], w_ref[...], preferred_element_type=F32).astype(o_ref.dtype)


def _proj_call(a, w, layer, col0, n, out_dtype):
    m, k = a.shape
    tm = min(4096, m)
    tn = 512
    c0 = col0 // tn
    return pl.pallas_call(
        _proj_body,
        grid=(m // tm, n // tn),
        in_specs=[
            pl.BlockSpec((tm, k), lambda i, j: (i, 0)),
            pl.BlockSpec((None, k, tn), lambda i, j: (layer, 0, c0 + j)),
        ],
        out_specs=pl.BlockSpec((tm, tn), lambda i, j: (i, j)),
        out_shape=jax.ShapeDtypeStruct((m, n), out_dtype),
        compiler_params=_params("arbitrary", "arbitrary"),
        name="in_proj",
    )(a, w)


Q_SLABS = ATTN_W // V7X_LANES
QK_SLABS = Q_SLABS + KV_W // V7X_LANES
QKV_SLABS = QK_SLABS + KV_W // V7X_LANES


def _gates_qkv_body(a_ref, w_ref, z_ref, gain_ref, cos_ref, sup_ref, sdn_ref, g_ref, q_ref, kv_ref):
    j = pl.program_id(1)
    zg = jnp.dot(a_ref[...], w_ref[...], preferred_element_type=F32)
    g_ref[...] = (0.5 * jnp.tanh(0.5 * zg) + 0.5).astype(g_ref.dtype)

    x = z_ref[...]
    low = lax.broadcasted_iota(jnp.int32, (1, V7X_LANES), 1) < HEAD_DIM
    x2 = x * x
    ss = jnp.where(low, jnp.sum(jnp.where(low, x2, 0.0), axis=-1, keepdims=True),
                   jnp.sum(jnp.where(low, 0.0, x2), axis=-1, keepdims=True))
    inv = jnp.where(j < QK_SLABS, lax.rsqrt(ss * (1.0 / HEAD_DIM) + EPS), 1.0)
    y = _rope((x * inv) * gain_ref[0], cos_ref[...], sup_ref[...], sdn_ref[...])
    da, db = _dup_pair(y, low)

    @pl.when(j < Q_SLABS)
    def _():
        q_ref[...] = y.astype(q_ref.dtype)

    @pl.when(j >= Q_SLABS)
    def _():
        kv_ref[:, 0:V7X_LANES] = da.astype(kv_ref.dtype)
        kv_ref[:, V7X_LANES:2 * V7X_LANES] = db.astype(kv_ref.dtype)


def _gates_qkv_call(a, w, layer, z2, gq, gk, rope_tabs, seq_len):
    m, k = a.shape
    d = D_MODEL
    tm = min(2048, m)
    tn = 512
    assert N_BRANCH * d // tn == QKV_SLABS and tm % seq_len == 0
    c0 = GATE_OFF // tn
    ones = jnp.ones((tm, V7X_LANES), F32)
    zeros = jnp.zeros((tm, V7X_LANES), F32)
    if rope_tabs is None:
        cos, sup, sdn = ones[None], zeros[None], zeros[None]
        tab_map = lambda i, j: (0, 0, 0)
    else:
        rep = lambda t: jnp.tile(t, (tm // seq_len, 1))
        cos, sup, sdn = (jnp.stack([rep(t), ident]) for t, ident in zip(rope_tabs, (ones, zeros, zeros)))
        tab_map = lambda i, j: ((j >= QK_SLABS).astype(jnp.int32), 0, 0)
    gains = jnp.stack([jnp.tile(gq, 2) * HEAD_DIM ** -0.5, jnp.tile(gk, 2), jnp.ones((V7X_LANES,), F32)])
    gain_map = lambda i, j: ((j >= Q_SLABS).astype(jnp.int32) + (j >= QK_SLABS).astype(jnp.int32), 0, 0)
    tab_spec = pl.BlockSpec((None, tm, V7X_LANES), tab_map)
    return pl.pallas_call(
        _gates_qkv_body,
        grid=(m // tm, QKV_SLABS),
        in_specs=[
            pl.BlockSpec((tm, k), lambda i, j: (i, 0)),
            pl.BlockSpec((None, k, tn), lambda i, j: (layer, 0, c0 + j)),
            pl.BlockSpec((tm, V7X_LANES), lambda i, j: (i, j)),
            pl.BlockSpec((None, 1, V7X_LANES), gain_map),
            tab_spec, tab_spec, tab_spec,
        ],
        out_specs=[
            pl.BlockSpec((tm, tn), lambda i, j: (i, j)),
            pl.BlockSpec((tm, V7X_LANES), lambda i, j: (i, jnp.minimum(j, Q_SLABS - 1))),
            pl.BlockSpec((tm, 2 * V7X_LANES), lambda i, j: (i, jnp.maximum(j - Q_SLABS, 0))),
        ],
        out_shape=[
            jax.ShapeDtypeStruct((m, N_BRANCH * d), BF16),
            jax.ShapeDtypeStruct((m, ATTN_W), BF16),
            jax.ShapeDtypeStruct((m, 2 * KV_DUP_W), BF16),
        ],
        compiler_params=_params("arbitrary", "arbitrary"),
        name="gates_qkv",
    )(a, w, z2, gains.reshape(3, 1, V7X_LANES), cos, sup, sdn)


def _pair_block_diag():
    r = lax.broadcasted_iota(jnp.int32, (V7X_LANES, V7X_LANES), 0) // HEAD_DIM
    c = lax.broadcasted_iota(jnp.int32, (V7X_LANES, V7X_LANES), 1) // HEAD_DIM
    return (r == c).astype(F32)


def _head_norm(x, g, bd):
    ss = jnp.dot(x * x, bd, precision=HIGHEST, preferred_element_type=F32)
    return (x * lax.rsqrt(ss * (1.0 / HEAD_DIM) + EPS)) * g


def _rope(x, cos, sin_up, sin_dn):
    up = pltpu.roll(x, V7X_LANES - ROPE_FREQS, 1)
    dn = pltpu.roll(x, ROPE_FREQS, 1)
    return x * cos + up * sin_up + dn * sin_dn


def _dup_pair(x, low):
    r = pltpu.roll(x, HEAD_DIM, 1)
    return jnp.where(low, x, r), jnp.where(low, r, x)


def _kv_body(z_ref, gk_ref, kv_ref):
    bd = _pair_block_diag()
    low = lax.broadcasted_iota(jnp.int32, (1, V7X_LANES), 1) < HEAD_DIM
    for s in range(2 * KV_W // V7X_LANES):
        x = z_ref[:, s * V7X_LANES:(s + 1) * V7X_LANES]
        if s < KV_W // V7X_LANES:
            x = _head_norm(x, gk_ref[...], bd)
        a, b = _dup_pair(x, low)
        base = 2 * s * V7X_LANES
        kv_ref[:, base:base + V7X_LANES] = a.astype(kv_ref.dtype)
        kv_ref[:, base + V7X_LANES:base + 2 * V7X_LANES] = b.astype(kv_ref.dtype)


def _kv_call(z2, gk):
    m, nz = z2.shape
    tm = min(512, m)
    return pl.pallas_call(
        _kv_body,
        grid=(m // tm,),
        in_specs=[pl.BlockSpec((tm, nz), lambda i: (i, 0)),
                  pl.BlockSpec((1, V7X_LANES), lambda i: (0, 0))],
        out_specs=pl.BlockSpec((tm, 2 * KV_DUP_W), lambda i: (i, 0)),
        out_shape=jax.ShapeDtypeStruct((m, 2 * KV_DUP_W), BF16),
        compiler_params=_params("arbitrary"),
        name="kv_prep",
    )(z2, jnp.tile(gk, 2).reshape(1, V7X_LANES))


def _rope_tables(l):
    rows = l // GRID_W
    row = jnp.repeat(jnp.arange(rows, dtype=F32), GRID_W)
    col = jnp.tile(jnp.arange(GRID_W, dtype=F32), rows)
    inv = ROPE_BASE ** (-jnp.arange(ROPE_FREQS, dtype=F32) / ROPE_FREQS)
    ang = jnp.stack([row[:, None] * inv, col[:, None] * inv], axis=1)
    cos, sin = jnp.cos(ang), jnp.sin(ang)
    zero = jnp.zeros_like(sin)
    cos_h = jnp.stack([cos, cos], axis=2).reshape(l, HEAD_DIM)
    sup_h = jnp.stack([-sin, zero], axis=2).reshape(l, HEAD_DIM)
    sdn_h = jnp.stack([zero, sin], axis=2).reshape(l, HEAD_DIM)
    return tuple(jnp.tile(t, (1, 2)) for t in (cos_h, sup_h, sdn_h))


def _attn_body(*refs, local, tq):
    it = iter(refs)
    sink_ref = next(it)
    q_ref = next(it)
    if local:
        kp_ref, kc_ref, kn_ref, vp_ref, vc_ref, vn_ref = (next(it) for _ in range(6))
    kx_ref, vx_ref = next(it), next(it)
    o_ref = next(it)

    i = pl.program_id(1)
    nb = pl.num_programs(1)
    low = lax.broadcasted_iota(jnp.int32, (1, V7X_LANES), 1) < HEAD_DIM
    rows = GQA_GROUP * tq
    if local:
        qi = lax.broadcasted_iota(jnp.int32, (rows, tq), 0) % tq
        kj = lax.broadcasted_iota(jnp.int32, (rows, tq), 1)
        mask_prev = (kj >= qi) & (i > 0)
        mask_next = (kj <= qi) & (i < nb - 1)
    row_head = lax.broadcasted_iota(jnp.int32, (rows, 1), 0) // tq
    zero = jnp.zeros((), q_ref.dtype)

    for e, h in [(e, h) for e in range(q_ref.shape[0]) for h in range(N_KV_HEADS)]:
        hs = slice(h * V7X_LANES, (h + 1) * V7X_LANES)
        qa = q_ref[e, :, 2 * h * V7X_LANES:(2 * h + 1) * V7X_LANES]
        qb = q_ref[e, :, (2 * h + 1) * V7X_LANES:(2 * h + 2) * V7X_LANES]
        qs = jnp.concatenate([jnp.where(low, qa, zero), jnp.where(low, zero, qa),
                              jnp.where(low, qb, zero), jnp.where(low, zero, qb)], axis=0)
        kparts, vparts, masks = [kx_ref[e, :, hs]], [vx_ref[e, :, hs]], {}
        if local:
            kparts = [kp_ref[e, :, hs], kc_ref[e, :, hs], kn_ref[e, :, hs]] + kparts
            vparts = [vp_ref[e, :, hs], vc_ref[e, :, hs], vn_ref[e, :, hs]] + vparts
            masks = {0: mask_prev, 2: mask_next}
        k_all = jnp.concatenate(kparts, axis=0)
        v_all = jnp.concatenate(vparts, axis=0)

        sink = jnp.zeros((rows, 1), F32)
        for g in range(GQA_GROUP):
            sink = jnp.where(row_head == g, sink_ref[GQA_GROUP * h + g], sink)
        s_all = lax.dot_general(qs, k_all, (((1,), (1,)), ((), ())), preferred_element_type=F32)
        chunks = []
        for c in range(k_all.shape[0] // tq):
            s = s_all[:, c * tq:(c + 1) * tq]
            chunks.append(jnp.where(masks[c], s, NEG_INF) if c in masks else s)
        m = jnp.maximum(sink, jnp.max(functools.reduce(jnp.maximum, chunks), axis=-1, keepdims=True))
        probs = [jnp.exp(s - m) for s in chunks]
        denom = jnp.exp(sink - m) + jnp.sum(functools.reduce(jnp.add, probs), axis=-1, keepdims=True)
        p_all = jnp.concatenate([p.astype(v_all.dtype) for p in probs], axis=1)
        o = jnp.dot(p_all, v_all, preferred_element_type=F32) / denom
        oa = jnp.where(low, o[0:tq], o[tq:2 * tq])
        ob = jnp.where(low, o[2 * tq:3 * tq], o[3 * tq:4 * tq])
        o_ref[e, :, 2 * h * V7X_LANES:(2 * h + 1) * V7X_LANES] = oa.astype(o_ref.dtype)
        o_ref[e, :, (2 * h + 1) * V7X_LANES:(2 * h + 2) * V7X_LANES] = ob.astype(o_ref.dtype)


def _attn_call(q, kv, kvx, sink, local):
    b, l, _ = q.shape
    lx = kvx.shape[1]
    tq = WINDOW
    nb = l // tq
    ne = math.gcd(b, 4)
    blk = lambda w: (ne, tq, w)
    in_specs = [pl.BlockSpec(memory_space=pltpu.SMEM),
                pl.BlockSpec(blk(ATTN_W), lambda bi, i: (bi, i, 0))]
    args = [sink, q]
    if local:
        for half in (0, 1):
            for mp in (lambda bi, i, half=half: (bi, jnp.maximum(i - 1, 0), half),
                       lambda bi, i, half=half: (bi, i, half),
                       lambda bi, i, half=half: (bi, jnp.minimum(i + 1, nb - 1), half)):
                in_specs.append(pl.BlockSpec(blk(KV_DUP_W), mp))
                args.append(kv)
    for half in (0, 1):
        in_specs.append(pl.BlockSpec((ne, lx, KV_DUP_W), lambda bi, i, half=half: (bi, 0, half)))
        args.append(kvx)
    return pl.pallas_call(
        functools.partial(_attn_body, local=local, tq=tq),
        grid=(b // ne, nb),
        in_specs=in_specs,
        out_specs=pl.BlockSpec(blk(ATTN_W), lambda bi, i: (bi, i, 0)),
        out_shape=jax.ShapeDtypeStruct((b, l, ATTN_W), BF16),
        compiler_params=_params("arbitrary", "arbitrary"),
        name="attention",
    )(*args)


def _filter_body(z_ref, w0_ref, b0_ref, w1_ref, b1_ref, fr_ref, w2_ref, dec_ref, o_ref):
    fr = fr_ref[...]
    dot = functools.partial(jnp.dot, precision=HIGHEST, preferred_element_type=F32)
    h = jnp.sin(fr * (dot(z_ref[...], w0_ref[...]) + b0_ref[...]))
    for i in range(FILTER_INNER):
        h = jnp.sin(fr * (dot(h, w1_ref[i]) + b1_ref[i]))
    dec = dec_ref[...]
    for s in range(2 * HYENA_ORDER):
        sl = slice(s * HYENA_W, (s + 1) * HYENA_W)
        o_ref[:, sl] = dot(h, w2_ref[:, sl]) * dec


def _filter_features(l):
    t = jnp.linspace(0.0, 1.0, l, dtype=F32)[:, None]
    w = 2.0 * math.pi * jnp.arange(l, dtype=F32)[:, None] / l
    bands = jnp.linspace(1e-4, FILTER_BANDS - 1, FILTER_BANDS, dtype=F32)[None, :]
    z = jnp.concatenate([t, jnp.cos(bands * w), -jnp.sin(bands * w)], axis=-1)
    deltas = jnp.linspace(math.log(DECAY_TARGET) / SLOW_DECAY_PCT, math.log(DECAY_TARGET) / FAST_DECAY_PCT,
                          HYENA_W, dtype=F32)
    decay = jnp.exp(-t * jnp.abs(deltas))
    return jnp.pad(z, ((0, 0), (0, V7X_LANES - FILTER_EMB))), decay


def _filter_call(l, w0, b0, w1, b1, freq, w2):
    zfeat, decay = _filter_features(l)
    w0p = jnp.pad(w0, ((0, V7X_LANES - FILTER_EMB), (0, 0)))
    tl = min(512, l)
    nf = 2 * HYENA_ORDER * HYENA_W
    full = lambda shape: pl.BlockSpec(shape, lambda i: (0,) * len(shape))
    return pl.pallas_call(
        _filter_body,
        grid=(l // tl,),
        in_specs=[
            pl.BlockSpec((tl, V7X_LANES), lambda i: (i, 0)),
            full((V7X_LANES, FILTER_HIDDEN)),
            full((1, FILTER_HIDDEN)),
            full((FILTER_INNER, FILTER_HIDDEN, FILTER_HIDDEN)),
            full((FILTER_INNER, 1, FILTER_HIDDEN)),
            full((1, FILTER_HIDDEN)),
            full((FILTER_HIDDEN, nf)),
            pl.BlockSpec((tl, HYENA_W), lambda i: (i, 0)),
        ],
        out_specs=pl.BlockSpec((tl, nf), lambda i: (i, 0)),
        out_shape=jax.ShapeDtypeStruct((l, nf), F32),
        compiler_params=_params("arbitrary"),
        name="hyena_filter",
    )(zfeat, w0p, b0.reshape(1, -1), w1, b1.reshape(FILTER_INNER, 1, -1), freq.reshape(1, -1), w2, decay)


def _hyena_blocks(l):
    return max(1, min(4, l // V7X_LANES))


def _dft_matrices(blk):
    n = 2 * blk
    r = jnp.arange(blk, dtype=jnp.int32)
    ang = ((r[:, None] * r[None, :]) % n).astype(F32) * (2.0 * math.pi / n)
    return jnp.cos(ang).astype(BF16), jnp.sin(ang).astype(BF16)


def _alternating(l):
    row = lax.broadcasted_iota(jnp.int32, (l, 1), 0)
    return row, jnp.where(row % 2 == 0, 1.0, -1.0).astype(F32)


def _spectrum_body(hf_ref, hb_ref, fc_ref, fs_ref, ka_ref, kb_ref, kn_ref, *, nblk):
    l = hf_ref.shape[0]
    b = l // nblk
    n = 2 * b
    row = lax.broadcasted_iota(jnp.int32, (l, 1), 0)
    _, alt = _alternating(b)
    hf = hf_ref[...]
    hbs = jnp.where(row == 0, 0.0, pltpu.roll(hb_ref[...], 1, 0))
    fc, fs = fc_ref[...], fs_ref[...]

    def transforms(h):
        out = []
        for k in range(nblk):
            hk = h[k * b:(k + 1) * b]
            hk16 = hk.astype(BF16)
            out.append(dict(
                c=jnp.dot(fc, hk16, preferred_element_type=F32),
                s=jnp.dot(fs, hk16, preferred_element_type=F32),
                first16=hk16[0:1].astype(F32),
                first=hk[0:1],
                alt=jnp.sum(hk * alt, axis=0, keepdims=True)))
        return out

    tf, tb = transforms(hf), transforms(hbs)
    brow = lax.broadcasted_iota(jnp.int32, (b, 1), 0)
    w_re = jnp.where(brow == 0, 1.0 / n, 2.0 / n)
    for d in range(-(nblk - 1), nblk):
        idx = d + nblk - 1
        if d == 0:
            kre = tf[0]["c"] + tb[0]["c"]
            kim = tb[0]["s"] - tf[0]["s"]
            kn = tf[0]["alt"] + tb[0]["alt"]
        else:
            t, e, sg = (tf, d, -1.0) if d > 0 else (tb, -d, 1.0)
            kre = t[e]["c"] + alt * (t[e - 1]["c"] - t[e - 1]["first16"])
            kim = sg * (t[e]["s"] + alt * t[e - 1]["s"])
            kn = t[e]["alt"] + t[e - 1]["alt"] - t[e - 1]["first"]
        ka_ref[0, idx] = (kre * w_re).astype(ka_ref.dtype)
        kb_ref[0, idx] = (kim * (2.0 / n)).astype(kb_ref.dtype)
        kn_ref[0, idx] = kn * (1.0 / n)


def _spectrum_call(filt, fc, fs, nblk, tc):
    l = filt.shape[0]
    b = l // nblk
    nct = HYENA_W // tc
    nlag = 2 * nblk - 1
    return pl.pallas_call(
        functools.partial(_spectrum_body, nblk=nblk),
        grid=(HYENA_ORDER, nct),
        in_specs=[
            pl.BlockSpec((l, tc), lambda o, c: (0, 2 * nct * o + c)),
            pl.BlockSpec((l, tc), lambda o, c: (0, 2 * nct * o + nct + c)),
            _const_spec((b, b)),
            _const_spec((b, b)),
        ],
        out_specs=[
            pl.BlockSpec((1, nlag, b, tc), lambda o, c: (o, 0, 0, c)),
            pl.BlockSpec((1, nlag, b, tc), lambda o, c: (o, 0, 0, c)),
            pl.BlockSpec((1, nlag, 1, tc), lambda o, c: (o, 0, 0, c)),
        ],
        out_shape=[
            jax.ShapeDtypeStruct((HYENA_ORDER, nlag, b, HYENA_W), BF16),
            jax.ShapeDtypeStruct((HYENA_ORDER, nlag, b, HYENA_W), BF16),
            jax.ShapeDtypeStruct((HYENA_ORDER, nlag, 1, HYENA_W), F32),
        ],
        compiler_params=_params("arbitrary", "arbitrary"),
        name="hyena_spectrum",
    )(filt, filt, fc, fs)


def _conv3(x, w_ref, b_ref, row):
    l = x.shape[0]
    xm = jnp.where(row == 0, 0.0, pltpu.roll(x, 1, 0))
    xp = jnp.where(row == l - 1, 0.0, pltpu.roll(x, l - 1, 0))
    return xm * w_ref[0:1, :] + x * w_ref[1:2, :] + xp * w_ref[2:3, :] + b_ref[...]


def _fftconv_body(*refs, conv_u, nblk):
    it = iter(refs)
    u_ref = next(it)
    if conv_u:
        uw_ref, ub_ref = next(it), next(it)
    g_ref, gw_ref, gb_ref = next(it), next(it), next(it)
    ka_ref, kb_ref, kn_ref, d_ref, fc_ref, fs_ref, o_ref = (next(it) for _ in range(7))

    l = u_ref.shape[1]
    b = l // nblk
    row = lax.broadcasted_iota(jnp.int32, (l, 1), 0)
    _, alt = _alternating(b)
    u = u_ref[0].astype(F32)
    if conv_u:
        u = _conv3(u, uw_ref, ub_ref, row)
    gate = _conv3(g_ref[0], gw_ref, gb_ref, row)
    fc, fs = fc_ref[...], fs_ref[...]

    ps, qs, ns = [], [], []
    for j in range(nblk):
        uj = u[j * b:(j + 1) * b]
        uj16 = uj.astype(BF16)
        ps.append(jnp.dot(fc, uj16, preferred_element_type=F32).astype(BF16))
        qs.append(jnp.dot(fs, uj16, preferred_element_type=F32).astype(BF16))
        ns.append(jnp.sum(uj * alt, axis=0, keepdims=True))
    for i in range(nblk):
        r = t = nyq = None
        for j in range(nblk):
            lag = i - j + nblk - 1
            ka, kb = ka_ref[0, lag], kb_ref[0, lag]
            dr = ps[j] * ka + qs[j] * kb
            dt = qs[j] * ka - ps[j] * kb
            dn = ns[j] * kn_ref[0, lag]
            r, t, nyq = (dr, dt, dn) if j == 0 else (r + dr, t + dt, nyq + dn)
        y = jnp.dot(fc, r, preferred_element_type=F32) + jnp.dot(fs, t, preferred_element_type=F32)
        rows = slice(i * b, (i + 1) * b)
        y = y + alt * nyq + u[rows] * d_ref[0]
        o_ref[0, rows, :] = (gate[rows] * y).astype(o_ref.dtype)


def _fftconv_call(u, u_col0, z, gate_col0, conv_w, conv_b, spectra, d_skip, order, fc, fs, nblk, tc, out_dtype):
    b, l, _ = z.shape
    conv_u = u is z
    nct = HYENA_W // tc
    ka, kb, kn = spectra
    blk = l // nblk
    nlag = 2 * nblk - 1
    col = lambda c0: (lambda c, bi: (bi, 0, c0 // tc + c))
    wcol = lambda c0: (lambda c, bi: (0, (c0 - HY_OFF) // tc + c))
    in_specs = [pl.BlockSpec((1, l, tc), col(u_col0))]
    args = [u]
    if conv_u:
        in_specs += [pl.BlockSpec((3, tc), wcol(u_col0)), pl.BlockSpec((1, tc), wcol(u_col0))]
        args += [conv_w, conv_b]
    in_specs += [pl.BlockSpec((1, l, tc), col(gate_col0)),
                 pl.BlockSpec((3, tc), wcol(gate_col0)), pl.BlockSpec((1, tc), wcol(gate_col0))]
    args += [z, conv_w, conv_b]
    spec = lambda rows: pl.BlockSpec((1, nlag, rows, tc), lambda c, bi: (order, 0, 0, c),
                                     pipeline_mode=pl.Buffered(1))
    in_specs += [spec(blk), spec(blk), spec(1),
                 pl.BlockSpec((1, 1, tc), lambda c, bi: (order, 0, c), pipeline_mode=pl.Buffered(1)),
                 _const_spec((blk, blk)), _const_spec((blk, blk))]
    args += [ka, kb, kn, d_skip, fc, fs]
    return pl.pallas_call(
        functools.partial(_fftconv_body, conv_u=conv_u, nblk=nblk),
        grid=(nct, b),
        in_specs=in_specs,
        out_specs=pl.BlockSpec((1, l, tc), lambda c, bi: (bi, 0, c)),
        out_shape=jax.ShapeDtypeStruct((b, l, HYENA_W), out_dtype),
        compiler_params=_params("arbitrary", "arbitrary"),
        name="hyena_conv",
    )(*args)


def _pool_body(x_ref, w_ref, s_ref, o_ref):
    l = x_ref.shape[1]
    row = lax.broadcasted_iota(jnp.int32, (l, 1), 0)
    for g, win in enumerate(POOL_WINDOWS):
        half = win // 2
        sl = slice(g * POOL_GROUP, (g + 1) * POOL_GROUP)
        x = x_ref[0, :, sl]

        def shifted(a, k):
            return jnp.where((row >= k) & (row < l + k), pltpu.roll(a, k % l, 0), 0.0)

        back = fwd = x
        span = 1
        while span < half:
            back = back + shifted(back, span)
            fwd = fwd + shifted(fwd, -span)
            span *= 2
        acc = shifted(back, 1) + fwd
        cnt = (jnp.minimum(row + half, l) - jnp.maximum(row - half, 0)).astype(F32)
        d = acc / cnt - x
        y = jnp.dot(d.astype(BF16), w_ref[g], preferred_element_type=F32)
        o_ref[0, :, sl] = (y * s_ref[:, sl]).astype(o_ref.dtype)


def _pool_call(z, w_grp, scale):
    b, l, _ = z.shape
    ng = len(POOL_WINDOWS)
    return pl.pallas_call(
        _pool_body,
        grid=(b,),
        in_specs=[
            pl.BlockSpec((1, l, POOL_W), lambda bi: (bi, 0, POOL_OFF // POOL_W)),
            pl.BlockSpec((ng, POOL_GROUP, POOL_GROUP), lambda bi: (0, 0, 0)),
            pl.BlockSpec((1, POOL_W), lambda bi: (0, 0)),
        ],
        out_specs=pl.BlockSpec((1, l, POOL_W), lambda bi: (bi, 0, 0)),
        out_shape=jax.ShapeDtypeStruct((b, l, POOL_W), BF16),
        compiler_params=_params("arbitrary"),
        name="pool",
    )(z, w_grp, scale.reshape(1, POOL_W))


def _merge_body(ya_ref, yh_ref, yp_ref, gt_ref, x_ref, ga_ref, g2_ref, sc_ref, sh_ref,
                wa_ref, wh_ref, wp_ref, wo_ref, xn_ref, h2_ref):
    d = x_ref.shape[1]
    cj = 512
    ya, yh, yp = ya_ref[...], yh_ref[...], yp_ref[...]
    acc = jnp.zeros(x_ref.shape, F32)
    for j in range(d // cj):
        sl = slice(j * cj, (j + 1) * cj)
        gate = lambda br: gt_ref[:, br * d + j * cj:br * d + (j + 1) * cj].astype(F32)
        m = (gate(0) * jnp.dot(ya, wa_ref[:, sl], preferred_element_type=F32)
             + gate(1) * jnp.dot(yh, wh_ref[:, sl], preferred_element_type=F32)
             + gate(2) * jnp.dot(yp, wp_ref[:, sl], preferred_element_type=F32))
        acc = acc + jnp.dot(m.astype(BF16), wo_ref[sl, :], preferred_element_type=F32)
    xn = x_ref[...] + ga_ref[0] * acc
    xn_ref[...] = xn
    h2_ref[...] = _norm_mod(xn, g2_ref[...], sc_ref[0], sh_ref[0]).astype(h2_ref.dtype)


def _merge_call(ya, yh, yp, gates, x2, ga1, g2, sc2, sh2, wa, wh, wp, wo, layer, rows_per_batch):
    m, d = x2.shape
    tm = min(512, rows_per_batch)
    rpt = rows_per_batch // tm
    rows = lambda w: pl.BlockSpec((tm, w), lambda i: (i, 0))
    weight = lambda w: pl.BlockSpec((None,) + w.shape[1:], lambda i: (layer, 0, 0), pipeline_mode=pl.Buffered(1))
    return pl.pallas_call(
        _merge_body,
        grid=(m // tm,),
        in_specs=[rows(ATTN_W), rows(HYENA_W), rows(POOL_W), rows(N_BRANCH * d), rows(d),
                  _mod_spec(ga1, rpt), pl.BlockSpec((1, d), lambda i: (0, 0)),
                  _mod_spec(sc2, rpt), _mod_spec(sh2, rpt),
                  weight(wa), weight(wh), weight(wp), weight(wo)],
        out_specs=[rows(d), rows(d)],
        out_shape=[jax.ShapeDtypeStruct((m, d), F32), jax.ShapeDtypeStruct((m, d), BF16)],
        compiler_params=_params("arbitrary"),
        name="merge",
    )(ya, yh, yp, gates, x2, ga1, g2.reshape(1, d), sc2, sh2, wa, wh, wp, wo)


def _mlp_body(*refs, has_next):
    it = iter(refs)
    h_ref, w1_ref, w2_ref, x_ref, ga_ref = (next(it) for _ in range(5))
    if has_next:
        gn_ref, sc_ref, sh_ref = next(it), next(it), next(it)
    o_ref = next(it)
    hn_ref = next(it) if has_next else None
    xs_ref = next(it)

    f = pl.program_id(1)
    last = pl.num_programs(1) - 1
    tm, d = o_ref.shape
    cn = 512
    rb = min(256, tm)

    xs_ref[f] = x_ref[...]

    def hidden(rows):
        a = jnp.dot(h_ref[rows, :], w1_ref[...], preferred_element_type=F32)
        return jnp.square(jnp.maximum(a, 0.0)).astype(BF16)

    @pl.when(f == 0)
    def _():
        a = hidden(slice(None))
        for n0 in range(0, d, cn):
            o_ref[:, n0:n0 + cn] = jnp.dot(a, w2_ref[:, n0:n0 + cn], preferred_element_type=F32)

    @pl.when((f > 0) & (f < last))
    def _():
        a = hidden(slice(None))
        for n0 in range(0, d, cn):
            o_ref[:, n0:n0 + cn] += jnp.dot(a, w2_ref[:, n0:n0 + cn], preferred_element_type=F32)

    @pl.when(f == last)
    def _():
        for r0 in range(0, tm, rb):
            rows = slice(r0, r0 + rb)
            acc = o_ref[rows, :] + jnp.dot(hidden(rows), w2_ref[...], preferred_element_type=F32)
            x_rows = jnp.concatenate([xs_ref[c, rows, :] for c in range(xs_ref.shape[0])], axis=1)
            xo = x_rows + ga_ref[0] * acc
            o_ref[rows, :] = xo
            if has_next:
                hn_ref[rows, :] = _norm_mod(xo, gn_ref[...], sc_ref[0], sh_ref[0]).astype(hn_ref.dtype)


def _mlp_call(h2, w1, w2, layer, xn, ga2, nxt, rows_per_batch):
    m, d = xn.shape
    ff = w1.shape[2]
    tm, tf = min(1024, rows_per_batch if ga2.shape[0] > 1 else m), 512
    rpt = max(rows_per_batch // tm, 1)
    nf = ff // tf
    assert nf >= 2
    xw = d // nf
    assert xw % V7X_LANES == 0
    has_next = nxt is not None
    rows = pl.BlockSpec((tm, d), lambda i, f: (i, 0))
    in_specs = [rows, pl.BlockSpec((None, d, tf), lambda i, f: (layer, 0, f)),
                pl.BlockSpec((None, tf, d), lambda i, f: (layer, f, 0)),
                pl.BlockSpec((tm, xw), lambda i, f: (i, f)), _mod_spec(ga2, rpt)]
    args = [h2, w1, w2, xn, ga2]
    out_specs = [rows]
    out_shape = [jax.ShapeDtypeStruct((m, d), F32)]
    if has_next:
        gn, scn, shn = nxt
        in_specs += [pl.BlockSpec((1, d), lambda i, f: (0, 0)), _mod_spec(scn, rpt), _mod_spec(shn, rpt)]
        args += [gn.reshape(1, d), scn, shn]
        out_specs.append(rows)
        out_shape.append(jax.ShapeDtypeStruct((m, d), BF16))
    outs = pl.pallas_call(
        functools.partial(_mlp_body, has_next=has_next),
        grid=(m // tm, nf),
        in_specs=in_specs,
        out_specs=out_specs,
        out_shape=out_shape,
        scratch_shapes=[pltpu.VMEM((nf, tm, xw), F32)],
        compiler_params=_params("arbitrary", "arbitrary"),
        name="mlp",
    )(*args)
    return (outs[0], outs[1]) if has_next else (outs[0], None)


def _mixers(z, q, kv, kvx, sink, local, hy, pool_w, pool_scale):
    b, l, _ = z.shape
    y_att = _attn_call(q, kv, kvx, sink, local)
    conv_w, conv_b, filt_params, d_skip, (fc, fs) = hy
    nblk = _hyena_blocks(l)
    tc = 256
    spectra = _spectrum_call(_filter_call(l, *filt_params), fc, fs, nblk, tc)
    conv = functools.partial(_fftconv_call, conv_w=conv_w, conv_b=conv_b, spectra=spectra, d_skip=d_skip,
                             fc=fc, fs=fs, nblk=nblk, tc=tc)
    z1 = conv(z, HY_OFF, z, HY_OFF + HYENA_W, order=0, out_dtype=F32)
    y_hy = conv(z1, 0, z, HY_OFF + 2 * HYENA_W, order=1, out_dtype=BF16)
    y_pool = _pool_call(z, pool_w, pool_scale)
    return (y_att.reshape(b * l, ATTN_W), y_hy.reshape(b * l, HYENA_W), y_pool.reshape(b * l, POOL_W))


def kernel(x, c, ctx, c_ctx, norm1_g, norm2_g, w_mod, b_mod, w_in, q_norm_g, k_norm_g, sink, hy_conv_w, hy_conv_b, filt_w0, filt_b0, filt_w1, filt_b1, filt_freq, filt_w2, hy_bias, pool_w, pool_scale, w_att_o, w_hy_o, w_pool_o, w_out, mlp_w1, mlp_w2):
    b, l, d = x.shape
    lc = ctx.shape[1]
    depth = w_mod.shape[0]

    cc = jnp.concatenate([c, c_ctx[None, :], jnp.zeros((MOD_ROWS - b - 1, d), F32)], axis=0)
    mods = _modulation(cc, w_mod, b_mod)

    def chunks(layer, lo, hi):
        return [mods[layer, lo:hi, i * d:(i + 1) * d].reshape(hi - lo, 1, d) for i in range(6)]

    as_bf16 = lambda w: w.astype(BF16)
    w_in_b, w_att_b, w_hy_b, w_pool_b, w_out_b = map(as_bf16, (w_in, w_att_o, w_hy_o, w_pool_o, w_out))
    w1_b, w2_b, pool_w_b = map(as_bf16, (mlp_w1, mlp_w2, pool_w))

    rope_tabs = _rope_tables(l)
    dft_x = _dft_matrices(l // _hyena_blocks(l))
    dft_c = _dft_matrices(lc // _hyena_blocks(lc))

    x2 = x.reshape(b * l, d)
    c2 = ctx.reshape(b * lc, d)
    sh1, sc1 = chunks(0, 0, b)[:2]
    csh1, csc1 = chunks(0, b, b + 1)[:2]
    hx = _norm_call(x2, norm1_g[0], sc1, sh1, l)
    hc = _norm_call(c2, norm1_g[0], csc1, csh1, lc)

    for layer in range(depth):
        last = layer == depth - 1
        _, _, ga1, sh2, sc2, ga2 = chunks(layer, 0, b)
        _, _, cga1, csh2, csc2, cga2 = chunks(layer, b, b + 1)
        filt_params = (filt_w0[layer], filt_b0[layer], filt_w1[layer], filt_b1[layer], filt_freq[layer],
                       filt_w2[layer])
        conv_b = hy_conv_b[layer].reshape(1, -1)
        d_skip = hy_bias[layer].reshape(HYENA_ORDER, 1, HYENA_W)
        merge_w = (w_att_b, w_hy_b, w_pool_b, w_out_b, layer)

        gq, gk = q_norm_g[layer], k_norm_g[layer]
        if last:
            kvc = _kv_call(_proj_call(hc, w_in_b, layer, K_OFF, 2 * KV_W, F32), gk)
        else:
            zc = _proj_call(hc, w_in_b, layer, 0, GATE_OFF, F32)
            gc, qc, kvc = _gates_qkv_call(hc, w_in_b, layer, zc, gq, gk, None, lc)
            zc, qc = zc.reshape(b, lc, GATE_OFF), qc.reshape(b, lc, ATTN_W)
        kvc = kvc.reshape(b, lc, 2 * KV_DUP_W)

        zx = _proj_call(hx, w_in_b, layer, 0, GATE_OFF, F32)
        gx, qx, kvx = _gates_qkv_call(hx, w_in_b, layer, zx, gq, gk, rope_tabs, l)
        zx, qx, kvx = zx.reshape(b, l, GATE_OFF), qx.reshape(b, l, ATTN_W), kvx.reshape(b, l, 2 * KV_DUP_W)
        hy = (hy_conv_w[layer], conv_b, filt_params, d_skip, dft_x)
        ya, yh, yp = _mixers(zx, qx, kvx, kvc, sink[layer], True, hy, pool_w_b[layer], pool_scale[layer])
        xn, h2 = _merge_call(ya, yh, yp, gx, x2, ga1, norm2_g[layer], sc2, sh2, *merge_w, l)
        nxt = None if last else (norm1_g[layer + 1], *reversed(chunks(layer + 1, 0, b)[:2]))
        x2, hx = _mlp_call(h2, w1_b, w2_b, layer, xn, ga2, nxt, l)

        if not last:
            hyc = (hy_conv_w[layer], conv_b, filt_params, d_skip, dft_c)
            ya, yh, yp = _mixers(zc, qc, None, kvc, sink[layer], False, hyc, pool_w_b[layer], pool_scale[layer])
            cn, h2c = _merge_call(ya, yh, yp, gc, c2, cga1, norm2_g[layer], csc2, csh2, *merge_w, lc)
            nxt = (norm1_g[layer + 1], *reversed(chunks(layer + 1, b, b + 1)[:2]))
            c2, hc = _mlp_call(h2c, w1_b, w2_b, layer, cn, cga2, nxt, lc)

    return x2.reshape(b, l, d)
```

```python
import functools
import math

import jax
import jax.numpy as jnp
from jax import lax
from jax.experimental import pallas as pl
from jax.experimental.pallas import tpu as pltpu

D_MODEL = 2048
DEPTH = 2
GRID_W = 64
EPS = 1e-6
NEG_INF = -1e30

N_HEADS = 16
N_KV_HEADS = 4
GQA_GROUP = N_HEADS // N_KV_HEADS
HEAD_DIM = 64
ATTN_W = N_HEADS * HEAD_DIM
KV_W = N_KV_HEADS * HEAD_DIM
WINDOW = 128
ROPE_FREQS = HEAD_DIM // 4
ROPE_BASE = 10000.0

HYENA_W = D_MODEL // 4
HYENA_ORDER = 2
FILTER_BANDS = 16
FILTER_EMB = 1 + 2 * FILTER_BANDS
FILTER_HIDDEN = 64
FILTER_INNER = 2
DECAY_TARGET = 1e-2
FAST_DECAY_PCT = 0.3
SLOW_DECAY_PCT = 1.5

POOL_W = D_MODEL // 4
POOL_WINDOWS = (2, 4, 8, 16)
POOL_GROUP = POOL_W // len(POOL_WINDOWS)

N_BRANCH = 3
D_FF = 4 * D_MODEL

Q_OFF = 0
K_OFF = Q_OFF + ATTN_W
V_OFF = K_OFF + KV_W
HY_OFF = V_OFF + KV_W
POOL_OFF = HY_OFF + 3 * HYENA_W
GATE_OFF = POOL_OFF + POOL_W
IN_W = GATE_OFF + N_BRANCH * D_MODEL

V7X_LANES = 128
V7X_VMEM_LIMIT = 60 * 1024 * 1024
KV_DUP_W = N_KV_HEADS * V7X_LANES
MOD_ROWS = 24

F32 = jnp.float32
BF16 = jnp.bfloat16
HIGHEST = lax.Precision.HIGHEST


def _params(*semantics):
    return pltpu.CompilerParams(dimension_semantics=semantics, vmem_limit_bytes=V7X_VMEM_LIMIT)


def _const_spec(shape):
    zeros = (0,) * len(shape)
    return pl.BlockSpec(shape, lambda *_: zeros, pipeline_mode=pl.Buffered(1))


def _mod_spec(arr, rows_per_mod_tile):
    d = arr.shape[-1]
    if arr.shape[0] == 1:
        return pl.BlockSpec((1, 1, d), lambda i, *_: (0, 0, 0))
    return pl.BlockSpec((1, 1, d), lambda i, *_: (i // rows_per_mod_tile, 0, 0))


def _norm_mod(xf, g, sc, sh):
    y = xf * lax.rsqrt(jnp.mean(xf * xf, axis=-1, keepdims=True) + EPS)
    return (y * g) * (1.0 + sc) + sh


def _mod_body(c_ref, w_ref, b_ref, o_ref):
    c = c_ref[...]
    s = c * jax.nn.sigmoid(c)
    o_ref[0] = jnp.dot(s.astype(BF16), w_ref[0].astype(BF16), preferred_element_type=F32) + b_ref[0]


def _modulation(cc, w_mod, b_mod):
    depth, d, n = w_mod.shape
    tn = 1024
    return pl.pallas_call(
        _mod_body,
        grid=(depth, n // tn),
        in_specs=[
            pl.BlockSpec((MOD_ROWS, d), lambda l, j: (0, 0)),
            pl.BlockSpec((1, d, tn), lambda l, j: (l, 0, j)),
            pl.BlockSpec((1, 1, tn), lambda l, j: (l, 0, j)),
        ],
        out_specs=pl.BlockSpec((1, MOD_ROWS, tn), lambda l, j: (l, 0, j)),
        out_shape=jax.ShapeDtypeStruct((depth, MOD_ROWS, n), F32),
        compiler_params=_params("arbitrary", "arbitrary"),
        name="modulation",
    )(cc, w_mod, b_mod.reshape(depth, 1, n))


def _norm_body(x_ref, g_ref, sc_ref, sh_ref, o_ref):
    o_ref[...] = _norm_mod(x_ref[...], g_ref[...], sc_ref[0], sh_ref[0]).astype(o_ref.dtype)


def _norm_call(x2, g, sc, sh, rows_per_batch):
    m, d = x2.shape
    tm = min(1024, rows_per_batch if sc.shape[0] > 1 else m)
    return pl.pallas_call(
        _norm_body,
        grid=(m // tm,),
        in_specs=[
            pl.BlockSpec((tm, d), lambda i: (i, 0)),
            pl.BlockSpec((1, d), lambda i: (0, 0)),
            _mod_spec(sc, rows_per_batch // tm),
            _mod_spec(sh, rows_per_batch // tm),
        ],
        out_specs=pl.BlockSpec((tm, d), lambda i: (i, 0)),
        out_shape=jax.ShapeDtypeStruct((m, d), BF16),
        compiler_params=_params("arbitrary"),
        name="norm_mod",
    )(x2, g.reshape(1, d), sc, sh)


def _proj_body(a_ref, w_ref, o_ref):
    o_ref[...] = jnp.dot(a_ref[...], w_ref[...], preferred_element_type=F32).astype(o_ref.dtype)


def _proj_call(a, w, layer, col0, n, out_dtype):
    m, k = a.shape
    tm = min(4096, m)
    tn = 512
    c0 = col0 // tn
    return pl.pallas_call(
        _proj_body,
        grid=(m // tm, n // tn),
        in_specs=[
            pl.BlockSpec((tm, k), lambda i, j: (i, 0)),
            pl.BlockSpec((None, k, tn), lambda i, j: (layer, 0, c0 + j)),
        ],
        out_specs=pl.BlockSpec((tm, tn), lambda i, j: (i, j)),
        out_shape=jax.ShapeDtypeStruct((m, n), out_dtype),
        compiler_params=_params("arbitrary", "arbitrary"),
        name="in_proj",
    )(a, w)


Q_SLABS = ATTN_W // V7X_LANES
QK_SLABS = Q_SLABS + KV_W // V7X_LANES
QKV_SLABS = QK_SLABS + KV_W // V7X_LANES


def _gates_qkv_body(a_ref, w_ref, z_ref, gain_ref, cos_ref, sup_ref, sdn_ref, g_ref, q_ref, kv_ref):
    j = pl.program_id(1)
    zg = jnp.dot(a_ref[...], w_ref[...], preferred_element_type=F32)
    g_ref[...] = (0.5 * jnp.tanh(0.5 * zg) + 0.5).astype(g_ref.dtype)

    x = z_ref[...].astype(F32)
    low = lax.broadcasted_iota(jnp.int32, (1, V7X_LANES), 1) < HEAD_DIM
    x2 = x * x
    ss = jnp.where(low, jnp.sum(jnp.where(low, x2, 0.0), axis=-1, keepdims=True),
                   jnp.sum(jnp.where(low, 0.0, x2), axis=-1, keepdims=True))
    inv = jnp.where(j < QK_SLABS, lax.rsqrt(ss * (1.0 / HEAD_DIM) + EPS), 1.0)
    y = _rope((x * inv) * gain_ref[0], cos_ref[...], sup_ref[...], sdn_ref[...])
    da, db = _dup_pair(y, low)

    @pl.when(j < Q_SLABS)
    def _():
        q_ref[...] = y.astype(q_ref.dtype)

    @pl.when(j >= Q_SLABS)
    def _():
        kv_ref[:, 0:V7X_LANES] = da.astype(kv_ref.dtype)
        kv_ref[:, V7X_LANES:2 * V7X_LANES] = db.astype(kv_ref.dtype)


def _gates_qkv_call(a, w, layer, z2, gq, gk, rope_tabs, seq_len):
    m, k = a.shape
    d = D_MODEL
    tm = min(2048, m)
    tn = 512
    assert N_BRANCH * d // tn == QKV_SLABS and tm % seq_len == 0
    c0 = GATE_OFF // tn
    ones = jnp.ones((tm, V7X_LANES), F32)
    zeros = jnp.zeros((tm, V7X_LANES), F32)
    if rope_tabs is None:
        cos, sup, sdn = ones[None], zeros[None], zeros[None]
        tab_map = lambda i, j: (0, 0, 0)
    else:
        rep = lambda t: jnp.tile(t, (tm // seq_len, 1))
        cos, sup, sdn = (jnp.stack([rep(t), ident]) for t, ident in zip(rope_tabs, (ones, zeros, zeros)))
        tab_map = lambda i, j: ((j >= QK_SLABS).astype(jnp.int32), 0, 0)
    gains = jnp.stack([jnp.tile(gq, 2) * HEAD_DIM ** -0.5, jnp.tile(gk, 2), jnp.ones((V7X_LANES,), F32)])
    gain_map = lambda i, j: ((j >= Q_SLABS).astype(jnp.int32) + (j >= QK_SLABS).astype(jnp.int32), 0, 0)
    tab_spec = pl.BlockSpec((None, tm, V7X_LANES), tab_map)
    return pl.pallas_call(
        _gates_qkv_body,
        grid=(m // tm, QKV_SLABS),
        in_specs=[
            pl.BlockSpec((tm, k), lambda i, j: (i, 0)),
            pl.BlockSpec((None, k, tn), lambda i, j: (layer, 0, c0 + j)),
            pl.BlockSpec((tm, V7X_LANES), lambda i, j: (i, j)),
            pl.BlockSpec((None, 1, V7X_LANES), gain_map),
            tab_spec, tab_spec, tab_spec,
        ],
        out_specs=[
            pl.BlockSpec((tm, tn), lambda i, j: (i, j)),
            pl.BlockSpec((tm, V7X_LANES), lambda i, j: (i, jnp.minimum(j, Q_SLABS - 1))),
            pl.BlockSpec((tm, 2 * V7X_LANES), lambda i, j: (i, jnp.maximum(j - Q_SLABS, 0))),
        ],
        out_shape=[
            jax.ShapeDtypeStruct((m, N_BRANCH * d), BF16),
            jax.ShapeDtypeStruct((m, ATTN_W), BF16),
            jax.ShapeDtypeStruct((m, 2 * KV_DUP_W), BF16),
        ],
        compiler_params=_params("arbitrary", "arbitrary"),
        name="gates_qkv",
    )(a, w, z2, gains.reshape(3, 1, V7X_LANES), cos, sup, sdn)


def _pair_block_diag():
    r = lax.broadcasted_iota(jnp.int32, (V7X_LANES, V7X_LANES), 0) // HEAD_DIM
    c = lax.broadcasted_iota(jnp.int32, (V7X_LANES, V7X_LANES), 1) // HEAD_DIM
    return (r == c).astype(F32)


def _head_norm(x, g, bd):
    ss = jnp.dot(x * x, bd, precision=HIGHEST, preferred_element_type=F32)
    return (x * lax.rsqrt(ss * (1.0 / HEAD_DIM) + EPS)) * g


def _rope(x, cos, sin_up, sin_dn):
    up = pltpu.roll(x, V7X_LANES - ROPE_FREQS, 1)
    dn = pltpu.roll(x, ROPE_FREQS, 1)
    return x * cos + up * sin_up + dn * sin_dn


def _dup_pair(x, low):
    r = pltpu.roll(x, HEAD_DIM, 1)
    return jnp.where(low, x, r), jnp.where(low, r, x)


def _kv_body(z_ref, gk_ref, kv_ref):
    bd = _pair_block_diag()
    low = lax.broadcasted_iota(jnp.int32, (1, V7X_LANES), 1) < HEAD_DIM
    for s in range(2 * KV_W // V7X_LANES):
        x = z_ref[:, s * V7X_LANES:(s + 1) * V7X_LANES].astype(F32)
        if s < KV_W // V7X_LANES:
            x = _head_norm(x, gk_ref[...], bd)
        a, b = _dup_pair(x, low)
        base = 2 * s * V7X_LANES
        kv_ref[:, base:base + V7X_LANES] = a.astype(kv_ref.dtype)
        kv_ref[:, base + V7X_LANES:base + 2 * V7X_LANES] = b.astype(kv_ref.dtype)


def _kv_call(z2, gk):
    m, nz = z2.shape
    tm = min(512, m)
    return pl.pallas_call(
        _kv_body,
        grid=(m // tm,),
        in_specs=[pl.BlockSpec((tm, nz), lambda i: (i, 0)),
                  pl.BlockSpec((1, V7X_LANES), lambda i: (0, 0))],
        out_specs=pl.BlockSpec((tm, 2 * KV_DUP_W), lambda i: (i, 0)),
        out_shape=jax.ShapeDtypeStruct((m, 2 * KV_DUP_W), BF16),
        compiler_params=_params("arbitrary"),
        name="kv_prep",
    )(z2, jnp.tile(gk, 2).reshape(1, V7X_LANES))


def _rope_tables(l):
    rows = l // GRID_W
    row = jnp.repeat(jnp.arange(rows, dtype=F32), GRID_W)
    col = jnp.tile(jnp.arange(GRID_W, dtype=F32), rows)
    inv = ROPE_BASE ** (-jnp.arange(ROPE_FREQS, dtype=F32) / ROPE_FREQS)
    ang = jnp.stack([row[:, None] * inv, col[:, None] * inv], axis=1)
    cos, sin = jnp.cos(ang), jnp.sin(ang)
    zero = jnp.zeros_like(sin)
    cos_h = jnp.stack([cos, cos], axis=2).reshape(l, HEAD_DIM)
    sup_h = jnp.stack([-sin, zero], axis=2).reshape(l, HEAD_DIM)
    sdn_h = jnp.stack([zero, sin], axis=2).reshape(l, HEAD_DIM)
    return tuple(jnp.tile(t, (1, 2)) for t in (cos_h, sup_h, sdn_h))


def _attn_body(*refs, local, tq):
    it = iter(refs)
    sink_ref = next(it)
    q_ref = next(it)
    if local:
        kp_ref, kc_ref, kn_ref, vp_ref, vc_ref, vn_ref = (next(it) for _ in range(6))
    kx_ref, vx_ref = next(it), next(it)
    o_ref = next(it)

    i = pl.program_id(1)
    nb = pl.num_programs(1)
    low = lax.broadcasted_iota(jnp.int32, (1, V7X_LANES), 1) < HEAD_DIM
    rows = GQA_GROUP * tq
    if local:
        qi = lax.broadcasted_iota(jnp.int32, (rows, tq), 0) % tq
        kj = lax.broadcasted_iota(jnp.int32, (rows, tq), 1)
        mask_prev = (kj >= qi) & (i > 0)
        mask_next = (kj <= qi) & (i < nb - 1)
    row_head = lax.broadcasted_iota(jnp.int32, (rows, 1), 0) // tq
    zero = jnp.zeros((), q_ref.dtype)

    for e, h in [(e, h) for e in range(q_ref.shape[0]) for h in range(N_KV_HEADS)]:
        hs = slice(h * V7X_LANES, (h + 1) * V7X_LANES)
        qa = q_ref[e, :, 2 * h * V7X_LANES:(2 * h + 1) * V7X_LANES]
        qb = q_ref[e, :, (2 * h + 1) * V7X_LANES:(2 * h + 2) * V7X_LANES]
        qs = jnp.concatenate([jnp.where(low, qa, zero), jnp.where(low, zero, qa),
                              jnp.where(low, qb, zero), jnp.where(low, zero, qb)], axis=0)
        kparts, vparts, masks = [kx_ref[e, :, hs]], [vx_ref[e, :, hs]], {}
        if local:
            kparts = [kp_ref[e, :, hs], kc_ref[e, :, hs], kn_ref[e, :, hs]] + kparts
            vparts = [vp_ref[e, :, hs], vc_ref[e, :, hs], vn_ref[e, :, hs]] + vparts
            masks = {0: mask_prev, 2: mask_next}
        k_all = jnp.concatenate(kparts, axis=0)
        v_all = jnp.concatenate(vparts, axis=0)

        sink = jnp.zeros((rows, 1), F32)
        for g in range(GQA_GROUP):
            sink = jnp.where(row_head == g, sink_ref[GQA_GROUP * h + g], sink)
        s_all = lax.dot_general(qs, k_all, (((1,), (1,)), ((), ())), preferred_element_type=F32)
        chunks = []
        for c in range(k_all.shape[0] // tq):
            s = s_all[:, c * tq:(c + 1) * tq]
            chunks.append(jnp.where(masks[c], s, NEG_INF) if c in masks else s)
        m = jnp.maximum(sink, jnp.max(functools.reduce(jnp.maximum, chunks), axis=-1, keepdims=True))
        probs = [jnp.exp(s - m) for s in chunks]
        denom = jnp.exp(sink - m) + jnp.sum(functools.reduce(jnp.add, probs), axis=-1, keepdims=True)
        p_all = jnp.concatenate([p.astype(v_all.dtype) for p in probs], axis=1)
        o = jnp.dot(p_all, v_all, preferred_element_type=F32) / denom
        oa = jnp.where(low, o[0:tq], o[tq:2 * tq])
        ob = jnp.where(low, o[2 * tq:3 * tq], o[3 * tq:4 * tq])
        o_ref[e, :, 2 * h * V7X_LANES:(2 * h + 1) * V7X_LANES] = oa.astype(o_ref.dtype)
        o_ref[e, :, (2 * h + 1) * V7X_LANES:(2 * h + 2) * V7X_LANES] = ob.astype(o_ref.dtype)


def _attn_call(q, kv, kvx, sink, local):
    b, l, _ = q.shape
    lx = kvx.shape[1]
    tq = WINDOW
    nb = l // tq
    ne = math.gcd(b, 4)
    blk = lambda w: (ne, tq, w)
    in_specs = [pl.BlockSpec(memory_space=pltpu.SMEM),
                pl.BlockSpec(blk(ATTN_W), lambda bi, i: (bi, i, 0))]
    args = [sink, q]
    if local:
        for half in (0, 1):
            for mp in (lambda bi, i, half=half: (bi, jnp.maximum(i - 1, 0), half),
                       lambda bi, i, half=half: (bi, i, half),
                       lambda bi, i, half=half: (bi, jnp.minimum(i + 1, nb - 1), half)):
                in_specs.append(pl.BlockSpec(blk(KV_DUP_W), mp))
                args.append(kv)
    for half in (0, 1):
        in_specs.append(pl.BlockSpec((ne, lx, KV_DUP_W), lambda bi, i, half=half: (bi, 0, half)))
        args.append(kvx)
    return pl.pallas_call(
        functools.partial(_attn_body, local=local, tq=tq),
        grid=(b // ne, nb),
        in_specs=in_specs,
        out_specs=pl.BlockSpec(blk(ATTN_W), lambda bi, i: (bi, i, 0)),
        out_shape=jax.ShapeDtypeStruct((b, l, ATTN_W), BF16),
        compiler_params=_params("arbitrary", "arbitrary"),
        name="attention",
    )(*args)


def _filter_body(z_ref, w0_ref, b0_ref, w1_ref, b1_ref, fr_ref, w2_ref, dec_ref, o_ref):
    fr = fr_ref[...]
    dot = functools.partial(jnp.dot, precision=HIGHEST, preferred_element_type=F32)
    h = jnp.sin(fr * (dot(z_ref[...], w0_ref[...]) + b0_ref[...]))
    for i in range(FILTER_INNER):
        h = jnp.sin(fr * (dot(h, w1_ref[i]) + b1_ref[i]))
    dec = dec_ref[...]
    for s in range(2 * HYENA_ORDER):
        sl = slice(s * HYENA_W, (s + 1) * HYENA_W)
        o_ref[:, sl] = dot(h, w2_ref[:, sl]) * dec


def _filter_features(l):
    t = jnp.linspace(0.0, 1.0, l, dtype=F32)[:, None]
    w = 2.0 * math.pi * jnp.arange(l, dtype=F32)[:, None] / l
    bands = jnp.linspace(1e-4, FILTER_BANDS - 1, FILTER_BANDS, dtype=F32)[None, :]
    z = jnp.concatenate([t, jnp.cos(bands * w), -jnp.sin(bands * w)], axis=-1)
    deltas = jnp.linspace(math.log(DECAY_TARGET) / SLOW_DECAY_PCT, math.log(DECAY_TARGET) / FAST_DECAY_PCT,
                          HYENA_W, dtype=F32)
    decay = jnp.exp(-t * jnp.abs(deltas))
    return jnp.pad(z, ((0, 0), (0, V7X_LANES - FILTER_EMB))), decay


def _filter_call(l, w0, b0, w1, b1, freq, w2):
    zfeat, decay = _filter_features(l)
    w0p = jnp.pad(w0, ((0, V7X_LANES - FILTER_EMB), (0, 0)))
    tl = min(512, l)
    nf = 2 * HYENA_ORDER * HYENA_W
    full = lambda shape: pl.BlockSpec(shape, lambda i: (0,) * len(shape))
    return pl.pallas_call(
        _filter_body,
        grid=(l // tl,),
        in_specs=[
            pl.BlockSpec((tl, V7X_LANES), lambda i: (i, 0)),
            full((V7X_LANES, FILTER_HIDDEN)),
            full((1, FILTER_HIDDEN)),
            full((FILTER_INNER, FILTER_HIDDEN, FILTER_HIDDEN)),
            full((FILTER_INNER, 1, FILTER_HIDDEN)),
            full((1, FILTER_HIDDEN)),
            full((FILTER_HIDDEN, nf)),
            pl.BlockSpec((tl, HYENA_W), lambda i: (i, 0)),
        ],
        out_specs=pl.BlockSpec((tl, nf), lambda i: (i, 0)),
        out_shape=jax.ShapeDtypeStruct((l, nf), F32),
        compiler_params=_params("arbitrary"),
        name="hyena_filter",
    )(zfeat, w0p, b0.reshape(1, -1), w1, b1.reshape(FILTER_INNER, 1, -1), freq.reshape(1, -1), w2, decay)


def _hyena_blocks(l):
    return max(1, min(4, l // V7X_LANES))


def _dft_matrices(blk):
    n = 2 * blk
    r = jnp.arange(blk, dtype=jnp.int32)
    ang = ((r[:, None] * r[None, :]) % n).astype(F32) * (2.0 * math.pi / n)
    return jnp.cos(ang).astype(BF16), jnp.sin(ang).astype(BF16)


def _alternating(l):
    row = lax.broadcasted_iota(jnp.int32, (l, 1), 0)
    return row, jnp.where(row % 2 == 0, 1.0, -1.0).astype(F32)


def _spectrum_body(hf_ref, hb_ref, fc_ref, fs_ref, ka_ref, kb_ref, kn_ref, *, nblk):
    l = hf_ref.shape[0]
    b = l // nblk
    n = 2 * b
    row = lax.broadcasted_iota(jnp.int32, (l, 1), 0)
    _, alt = _alternating(b)
    hf = hf_ref[...]
    hbs = jnp.where(row == 0, 0.0, pltpu.roll(hb_ref[...], 1, 0))
    fc, fs = fc_ref[...], fs_ref[...]

    def transforms(h):
        out = []
        for k in range(nblk):
            hk = h[k * b:(k + 1) * b]
            hk16 = hk.astype(BF16)
            out.append(dict(
                c=jnp.dot(fc, hk16, preferred_element_type=F32),
                s=jnp.dot(fs, hk16, preferred_element_type=F32),
                first16=hk16[0:1].astype(F32),
                first=hk[0:1],
                alt=jnp.sum(hk * alt, axis=0, keepdims=True)))
        return out

    tf, tb = transforms(hf), transforms(hbs)
    brow = lax.broadcasted_iota(jnp.int32, (b, 1), 0)
    w_re = jnp.where(brow == 0, 1.0 / n, 2.0 / n)
    for d in range(-(nblk - 1), nblk):
        idx = d + nblk - 1
        if d == 0:
            kre = tf[0]["c"] + tb[0]["c"]
            kim = tb[0]["s"] - tf[0]["s"]
            kn = tf[0]["alt"] + tb[0]["alt"]
        else:
            t, e, sg = (tf, d, -1.0) if d > 0 else (tb, -d, 1.0)
            kre = t[e]["c"] + alt * (t[e - 1]["c"] - t[e - 1]["first16"])
            kim = sg * (t[e]["s"] + alt * t[e - 1]["s"])
            kn = t[e]["alt"] + t[e - 1]["alt"] - t[e - 1]["first"]
        ka_ref[0, idx] = (kre * w_re).astype(ka_ref.dtype)
        kb_ref[0, idx] = (kim * (2.0 / n)).astype(kb_ref.dtype)
        kn_ref[0, idx] = kn * (1.0 / n)


def _spectrum_call(filt, fc, fs, nblk, tc):
    l = filt.shape[0]
    b = l // nblk
    nct = HYENA_W // tc
    nlag = 2 * nblk - 1
    return pl.pallas_call(
        functools.partial(_spectrum_body, nblk=nblk),
        grid=(HYENA_ORDER, nct),
        in_specs=[
            pl.BlockSpec((l, tc), lambda o, c: (0, 2 * nct * o + c)),
            pl.BlockSpec((l, tc), lambda o, c: (0, 2 * nct * o + nct + c)),
            _const_spec((b, b)),
            _const_spec((b, b)),
        ],
        out_specs=[
            pl.BlockSpec((1, nlag, b, tc), lambda o, c: (o, 0, 0, c)),
            pl.BlockSpec((1, nlag, b, tc), lambda o, c: (o, 0, 0, c)),
            pl.BlockSpec((1, nlag, 1, tc), lambda o, c: (o, 0, 0, c)),
        ],
        out_shape=[
            jax.ShapeDtypeStruct((HYENA_ORDER, nlag, b, HYENA_W), BF16),
            jax.ShapeDtypeStruct((HYENA_ORDER, nlag, b, HYENA_W), BF16),
            jax.ShapeDtypeStruct((HYENA_ORDER, nlag, 1, HYENA_W), F32),
        ],
        compiler_params=_params("arbitrary", "arbitrary"),
        name="hyena_spectrum",
    )(filt, filt, fc, fs)


def _conv3(x, w_ref, b_ref, row):
    l = x.shape[0]
    xm = jnp.where(row == 0, 0.0, pltpu.roll(x, 1, 0))
    xp = jnp.where(row == l - 1, 0.0, pltpu.roll(x, l - 1, 0))
    return xm * w_ref[0:1, :] + x * w_ref[1:2, :] + xp * w_ref[2:3, :] + b_ref[...]


def _fftconv_body(*refs, conv_u, nblk):
    it = iter(refs)
    u_ref = next(it)
    if conv_u:
        uw_ref, ub_ref = next(it), next(it)
    g_ref, gw_ref, gb_ref = next(it), next(it), next(it)
    ka_ref, kb_ref, kn_ref, d_ref, fc_ref, fs_ref, o_ref = (next(it) for _ in range(7))

    l = u_ref.shape[1]
    b = l // nblk
    row = lax.broadcasted_iota(jnp.int32, (l, 1), 0)
    _, alt = _alternating(b)
    u = u_ref[0].astype(F32)
    if conv_u:
        u = _conv3(u, uw_ref, ub_ref, row)
    gate = _conv3(g_ref[0].astype(F32), gw_ref, gb_ref, row)
    fc, fs = fc_ref[...], fs_ref[...]

    ps, qs, ns = [], [], []
    for j in range(nblk):
        uj = u[j * b:(j + 1) * b]
        uj16 = uj.astype(BF16)
        ps.append(jnp.dot(fc, uj16, preferred_element_type=F32).astype(BF16))
        qs.append(jnp.dot(fs, uj16, preferred_element_type=F32).astype(BF16))
        ns.append(jnp.sum(uj * alt, axis=0, keepdims=True))
    for i in range(nblk):
        r = t = nyq = None
        for j in range(nblk):
            lag = i - j + nblk - 1
            ka, kb = ka_ref[0, lag], kb_ref[0, lag]
            dr = ps[j] * ka + qs[j] * kb
            dt = qs[j] * ka - ps[j] * kb
            dn = ns[j] * kn_ref[0, lag]
            r, t, nyq = (dr, dt, dn) if j == 0 else (r + dr, t + dt, nyq + dn)
        y = jnp.dot(fc, r, preferred_element_type=F32) + jnp.dot(fs, t, preferred_element_type=F32)
        rows = slice(i * b, (i + 1) * b)
        y = y + alt * nyq + u[rows] * d_ref[0]
        o_ref[0, rows, :] = (gate[rows] * y).astype(o_ref.dtype)


def _fftconv_call(u, u_col0, z, gate_col0, conv_w, conv_b, spectra, d_skip, order, fc, fs, nblk, tc, out_dtype):
    b, l, _ = z.shape
    conv_u = u is z
    nct = HYENA_W // tc
    ka, kb, kn = spectra
    blk = l // nblk
    nlag = 2 * nblk - 1
    col = lambda c0: (lambda c, bi: (bi, 0, c0 // tc + c))
    wcol = lambda c0: (lambda c, bi: (0, (c0 - HY_OFF) // tc + c))
    in_specs = [pl.BlockSpec((1, l, tc), col(u_col0))]
    args = [u]
    if conv_u:
        in_specs += [pl.BlockSpec((3, tc), wcol(u_col0)), pl.BlockSpec((1, tc), wcol(u_col0))]
        args += [conv_w, conv_b]
    in_specs += [pl.BlockSpec((1, l, tc), col(gate_col0)),
                 pl.BlockSpec((3, tc), wcol(gate_col0)), pl.BlockSpec((1, tc), wcol(gate_col0))]
    args += [z, conv_w, conv_b]
    spec = lambda rows: pl.BlockSpec((1, nlag, rows, tc), lambda c, bi: (order, 0, 0, c),
                                     pipeline_mode=pl.Buffered(1))
    in_specs += [spec(blk), spec(blk), spec(1),
                 pl.BlockSpec((1, 1, tc), lambda c, bi: (order, 0, c), pipeline_mode=pl.Buffered(1)),
                 _const_spec((blk, blk)), _const_spec((blk, blk))]
    args += [ka, kb, kn, d_skip, fc, fs]
    return pl.pallas_call(
        functools.partial(_fftconv_body, conv_u=conv_u, nblk=nblk),
        grid=(nct, b),
        in_specs=in_specs,
        out_specs=pl.BlockSpec((1, l, tc), lambda c, bi: (bi, 0, c)),
        out_shape=jax.ShapeDtypeStruct((b, l, HYENA_W), out_dtype),
        compiler_params=_params("arbitrary", "arbitrary"),
        name="hyena_conv",
    )(*args)


def _pool_body(x_ref, w_ref, s_ref, o_ref):
    l = x_ref.shape[1]
    row = lax.broadcasted_iota(jnp.int32, (l, 1), 0)
    for g, win in enumerate(POOL_WINDOWS):
        half = win // 2
        sl = slice(g * POOL_GROUP, (g + 1) * POOL_GROUP)
        x = x_ref[0, :, sl].astype(F32)

        def shifted(a, k):
            return jnp.where((row >= k) & (row < l + k), pltpu.roll(a, k % l, 0), 0.0)

        back = fwd = x
        span = 1
        while span < half:
            back = back + shifted(back, span)
            fwd = fwd + shifted(fwd, -span)
            span *= 2
        acc = shifted(back, 1) + fwd
        cnt = (jnp.minimum(row + half, l) - jnp.maximum(row - half, 0)).astype(F32)
        d = acc / cnt - x
        y = jnp.dot(d.astype(BF16), w_ref[g], preferred_element_type=F32)
        o_ref[0, :, sl] = (y * s_ref[:, sl]).astype(o_ref.dtype)


def _pool_call(z, w_grp, scale):
    b, l, _ = z.shape
    ng = len(POOL_WINDOWS)
    return pl.pallas_call(
        _pool_body,
        grid=(b,),
        in_specs=[
            pl.BlockSpec((1, l, POOL_W), lambda bi: (bi, 0, POOL_OFF // POOL_W)),
            pl.BlockSpec((ng, POOL_GROUP, POOL_GROUP), lambda bi: (0, 0, 0)),
            pl.BlockSpec((1, POOL_W), lambda bi: (0, 0)),
        ],
        out_specs=pl.BlockSpec((1, l, POOL_W), lambda bi: (bi, 0, 0)),
        out_shape=jax.ShapeDtypeStruct((b, l, POOL_W), BF16),
        compiler_params=_params("arbitrary"),
        name="pool",
    )(z, w_grp, scale.reshape(1, POOL_W))


def _merge_body(ya_ref, yh_ref, yp_ref, gt_ref, x_ref, ga_ref, g2_ref, sc_ref, sh_ref,
                wa_ref, wh_ref, wp_ref, wo_ref, xn_ref, h2_ref):
    d = x_ref.shape[1]
    cj = 512
    ya, yh, yp = ya_ref[...], yh_ref[...], yp_ref[...]
    acc = jnp.zeros(x_ref.shape, F32)
    for j in range(d // cj):
        sl = slice(j * cj, (j + 1) * cj)
        gate = lambda br: gt_ref[:, br * d + j * cj:br * d + (j + 1) * cj].astype(F32)
        m = (gate(0) * jnp.dot(ya, wa_ref[:, sl], preferred_element_type=F32)
             + gate(1) * jnp.dot(yh, wh_ref[:, sl], preferred_element_type=F32)
             + gate(2) * jnp.dot(yp, wp_ref[:, sl], preferred_element_type=F32))
        acc = acc + jnp.dot(m.astype(BF16), wo_ref[sl, :], preferred_element_type=F32)
    xn = x_ref[...] + ga_ref[0] * acc
    xn_ref[...] = xn
    h2_ref[...] = _norm_mod(xn, g2_ref[...], sc_ref[0], sh_ref[0]).astype(h2_ref.dtype)


def _merge_call(ya, yh, yp, gates, x2, ga1, g2, sc2, sh2, wa, wh, wp, wo, layer, rows_per_batch):
    m, d = x2.shape
    tm = min(512, rows_per_batch)
    rpt = rows_per_batch // tm
    rows = lambda w: pl.BlockSpec((tm, w), lambda i: (i, 0))
    weight = lambda w: pl.BlockSpec((None,) + w.shape[1:], lambda i: (layer, 0, 0), pipeline_mode=pl.Buffered(1))
    return pl.pallas_call(
        _merge_body,
        grid=(m // tm,),
        in_specs=[rows(ATTN_W), rows(HYENA_W), rows(POOL_W), rows(N_BRANCH * d), rows(d),
                  _mod_spec(ga1, rpt), pl.BlockSpec((1, d), lambda i: (0, 0)),
                  _mod_spec(sc2, rpt), _mod_spec(sh2, rpt),
                  weight(wa), weight(wh), weight(wp), weight(wo)],
        out_specs=[rows(d), rows(d)],
        out_shape=[jax.ShapeDtypeStruct((m, d), F32), jax.ShapeDtypeStruct((m, d), BF16)],
        compiler_params=_params("arbitrary"),
        name="merge",
    )(ya, yh, yp, gates, x2, ga1, g2.reshape(1, d), sc2, sh2, wa, wh, wp, wo)


def _mlp_body(*refs, has_next):
    it = iter(refs)
    h_ref, w1_ref, w2_ref, x_ref, ga_ref = (next(it) for _ in range(5))
    if has_next:
        gn_ref, sc_ref, sh_ref = next(it), next(it), next(it)
    o_ref = next(it)
    hn_ref = next(it) if has_next else None
    xs_ref = next(it)

    f = pl.program_id(1)
    last = pl.num_programs(1) - 1
    tm, d = o_ref.shape
    cn = 512
    rb = min(256, tm)

    xs_ref[f] = x_ref[...]

    def hidden(rows):
        a = jnp.dot(h_ref[rows, :], w1_ref[...], preferred_element_type=F32)
        return jnp.square(jnp.maximum(a, 0.0)).astype(BF16)

    @pl.when(f == 0)
    def _():
        a = hidden(slice(None))
        for n0 in range(0, d, cn):
            o_ref[:, n0:n0 + cn] = jnp.dot(a, w2_ref[:, n0:n0 + cn], preferred_element_type=F32)

    @pl.when((f > 0) & (f < last))
    def _():
        a = hidden(slice(None))
        for n0 in range(0, d, cn):
            o_ref[:, n0:n0 + cn] += jnp.dot(a, w2_ref[:, n0:n0 + cn], preferred_element_type=F32)

    @pl.when(f == last)
    def _():
        for r0 in range(0, tm, rb):
            rows = slice(r0, r0 + rb)
            acc = o_ref[rows, :] + jnp.dot(hidden(rows), w2_ref[...], preferred_element_type=F32)
            x_rows = jnp.concatenate([xs_ref[c, rows, :] for c in range(xs_ref.shape[0])], axis=1)
            xo = x_rows + ga_ref[0] * acc
            o_ref[rows, :] = xo
            if has_next:
                hn_ref[rows, :] = _norm_mod(xo, gn_ref[...], sc_ref[0], sh_ref[0]).astype(hn_ref.dtype)


def _mlp_call(h2, w1, w2, layer, xn, ga2, nxt, rows_per_batch):
    m, d = xn.shape
    ff = w1.shape[2]
    tm, tf = min(1024, rows_per_batch if ga2.shape[0] > 1 else m), 512
    rpt = max(rows_per_batch // tm, 1)
    nf = ff // tf
    assert nf >= 2
    xw = d // nf
    assert xw % V7X_LANES == 0
    has_next = nxt is not None
    rows = pl.BlockSpec((tm, d), lambda i, f: (i, 0))
    in_specs = [rows, pl.BlockSpec((None, d, tf), lambda i, f: (layer, 0, f)),
                pl.BlockSpec((None, tf, d), lambda i, f: (layer, f, 0)),
                pl.BlockSpec((tm, xw), lambda i, f: (i, f)), _mod_spec(ga2, rpt)]
    args = [h2, w1, w2, xn, ga2]
    out_specs = [rows]
    out_shape = [jax.ShapeDtypeStruct((m, d), F32)]
    if has_next:
        gn, scn, shn = nxt
        in_specs += [pl.BlockSpec((1, d), lambda i, f: (0, 0)), _mod_spec(scn, rpt), _mod_spec(shn, rpt)]
        args += [gn.reshape(1, d), scn, shn]
        out_specs.append(rows)
        out_shape.append(jax.ShapeDtypeStruct((m, d), BF16))
    outs = pl.pallas_call(
        functools.partial(_mlp_body, has_next=has_next),
        grid=(m // tm, nf),
        in_specs=in_specs,
        out_specs=out_specs,
        out_shape=out_shape,
        scratch_shapes=[pltpu.VMEM((nf, tm, xw), F32)],
        compiler_params=_params("arbitrary", "arbitrary"),
        name="mlp",
    )(*args)
    return (outs[0], outs[1]) if has_next else (outs[0], None)


def _mixers(z, q, kv, kvx, sink, local, hy, pool_w, pool_scale):
    b, l, _ = z.shape
    y_att = _attn_call(q, kv, kvx, sink, local)
    conv_w, conv_b, filt_params, d_skip, (fc, fs) = hy
    nblk = _hyena_blocks(l)
    tc = 256
    spectra = _spectrum_call(_filter_call(l, *filt_params), fc, fs, nblk, tc)
    conv = functools.partial(_fftconv_call, conv_w=conv_w, conv_b=conv_b, spectra=spectra, d_skip=d_skip,
                             fc=fc, fs=fs, nblk=nblk, tc=tc)
    z1 = conv(z, HY_OFF, z, HY_OFF + HYENA_W, order=0, out_dtype=F32)
    y_hy = conv(z1, 0, z, HY_OFF + 2 * HYENA_W, order=1, out_dtype=BF16)
    y_pool = _pool_call(z, pool_w, pool_scale)
    return (y_att.reshape(b * l, ATTN_W), y_hy.reshape(b * l, HYENA_W), y_pool.reshape(b * l, POOL_W))


def kernel(x, c, ctx, c_ctx, norm1_g, norm2_g, w_mod, b_mod, w_in, q_norm_g, k_norm_g, sink, hy_conv_w, hy_conv_b, filt_w0, filt_b0, filt_w1, filt_b1, filt_freq, filt_w2, hy_bias, pool_w, pool_scale, w_att_o, w_hy_o, w_pool_o, w_out, mlp_w1, mlp_w2):
    b, l, d = x.shape
    lc = ctx.shape[1]
    depth = w_mod.shape[0]

    cc = jnp.concatenate([c, c_ctx[None, :], jnp.zeros((MOD_ROWS - b - 1, d), F32)], axis=0)
    mods = _modulation(cc, w_mod, b_mod)

    def chunks(layer, lo, hi):
        return [mods[layer, lo:hi, i * d:(i + 1) * d].reshape(hi - lo, 1, d) for i in range(6)]

    as_bf16 = lambda w: w.astype(BF16)
    w_in_b, w_att_b, w_hy_b, w_pool_b, w_out_b = map(as_bf16, (w_in, w_att_o, w_hy_o, w_pool_o, w_out))
    w1_b, w2_b, pool_w_b = map(as_bf16, (mlp_w1, mlp_w2, pool_w))

    rope_tabs = _rope_tables(l)
    dft_x = _dft_matrices(l // _hyena_blocks(l))
    dft_c = _dft_matrices(lc // _hyena_blocks(lc))

    x2 = x.reshape(b * l, d)
    c2 = ctx.reshape(b * lc, d)
    sh1, sc1 = chunks(0, 0, b)[:2]
    csh1, csc1 = chunks(0, b, b + 1)[:2]
    hx = _norm_call(x2, norm1_g[0], sc1, sh1, l)
    hc = _norm_call(c2, norm1_g[0], csc1, csh1, lc)

    for layer in range(depth):
        last = layer == depth - 1
        _, _, ga1, sh2, sc2, ga2 = chunks(layer, 0, b)
        _, _, cga1, csh2, csc2, cga2 = chunks(layer, b, b + 1)
        filt_params = (filt_w0[layer], filt_b0[layer], filt_w1[layer], filt_b1[layer], filt_freq[layer],
                       filt_w2[layer])
        conv_b = hy_conv_b[layer].reshape(1, -1)
        d_skip = hy_bias[layer].reshape(HYENA_ORDER, 1, HYENA_W)
        merge_w = (w_att_b, w_hy_b, w_pool_b, w_out_b, layer)

        gq, gk = q_norm_g[layer], k_norm_g[layer]
        if last:
            kvc = _kv_call(_proj_call(hc, w_in_b, layer, K_OFF, 2 * KV_W, BF16), gk)
        else:
            zc = _proj_call(hc, w_in_b, layer, 0, GATE_OFF, BF16)
            gc, qc, kvc = _gates_qkv_call(hc, w_in_b, layer, zc, gq, gk, None, lc)
            zc, qc = zc.reshape(b, lc, GATE_OFF), qc.reshape(b, lc, ATTN_W)
        kvc = kvc.reshape(b, lc, 2 * KV_DUP_W)

        zx = _proj_call(hx, w_in_b, layer, 0, GATE_OFF, BF16)
        gx, qx, kvx = _gates_qkv_call(hx, w_in_b, layer, zx, gq, gk, rope_tabs, l)
        zx, qx, kvx = zx.reshape(b, l, GATE_OFF), qx.reshape(b, l, ATTN_W), kvx.reshape(b, l, 2 * KV_DUP_W)
        hy = (hy_conv_w[layer], conv_b, filt_params, d_skip, dft_x)
        ya, yh, yp = _mixers(zx, qx, kvx, kvc, sink[layer], True, hy, pool_w_b[layer], pool_scale[layer])
        xn, h2 = _merge_call(ya, yh, yp, gx, x2, ga1, norm2_g[layer], sc2, sh2, *merge_w, l)
        nxt = None if last else (norm1_g[layer + 1], *reversed(chunks(layer + 1, 0, b)[:2]))
        x2, hx = _mlp_call(h2, w1_b, w2_b, layer, xn, ga2, nxt, l)

        if not last:
            hyc = (hy_conv_w[layer], conv_b, filt_params, d_skip, dft_c)
            ya, yh, yp = _mixers(zc, qc, None, kvc, sink[layer], False, hyc, pool_w_b[layer], pool_scale[layer])
            cn, h2c = _merge_call(ya, yh, yp, gc, c2, cga1, norm2_g[layer], csc2, csh2, *merge_w, lc)
            nxt = (norm1_g[layer + 1], *reversed(chunks(layer + 1, b, b + 1)[:2]))
            c2, hc = _mlp_call(h2c, w1_b, w2_b, layer, cn, cga2, nxt, lc)

    return x2.reshape(b, l, d)
```

```python
import functools
import math

import jax
import jax.numpy as jnp
from jax import lax
from jax.experimental import pallas as pl
from jax.experimental.pallas import tpu as pltpu

D_MODEL = 2048
DEPTH = 2
GRID_W = 64
EPS = 1e-6
NEG_INF = -1e30

N_HEADS = 16
N_KV_HEADS = 4
GQA_GROUP = N_HEADS // N_KV_HEADS
HEAD_DIM = 64
ATTN_W = N_HEADS * HEAD_DIM
KV_W = N_KV_HEADS * HEAD_DIM
WINDOW = 128
ROPE_FREQS = HEAD_DIM // 4
ROPE_BASE = 10000.0

HYENA_W = D_MODEL // 4
HYENA_ORDER = 2
FILTER_BANDS = 16
FILTER_EMB = 1 + 2 * FILTER_BANDS
FILTER_HIDDEN = 64
FILTER_INNER = 2
DECAY_TARGET = 1e-2
FAST_DECAY_PCT = 0.3
SLOW_DECAY_PCT = 1.5

POOL_W = D_MODEL // 4
POOL_WINDOWS = (2, 4, 8, 16)
POOL_GROUP = POOL_W // len(POOL_WINDOWS)

N_BRANCH = 3
D_FF = 4 * D_MODEL

Q_OFF = 0
K_OFF = Q_OFF + ATTN_W
V_OFF = K_OFF + KV_W
HY_OFF = V_OFF + KV_W
POOL_OFF = HY_OFF + 3 * HYENA_W
GATE_OFF = POOL_OFF + POOL_W
IN_W = GATE_OFF + N_BRANCH * D_MODEL

V7X_LANES = 128
V7X_VMEM_LIMIT = 60 * 1024 * 1024
KV_DUP_W = N_KV_HEADS * V7X_LANES
MOD_ROWS = 24

F32 = jnp.float32
BF16 = jnp.bfloat16
HIGHEST = lax.Precision.HIGHEST


def _params(*semantics):
    return pltpu.CompilerParams(dimension_semantics=semantics, vmem_limit_bytes=V7X_VMEM_LIMIT)


def _const_spec(shape):
    zeros = (0,) * len(shape)
    return pl.BlockSpec(shape, lambda *_: zeros, pipeline_mode=pl.Buffered(1))


def _mod_spec(arr, rows_per_mod_tile):
    d = arr.shape[-1]
    if arr.shape[0] == 1:
        return pl.BlockSpec((1, 1, d), lambda i, *_: (0, 0, 0))
    return pl.BlockSpec((1, 1, d), lambda i, *_: (i // rows_per_mod_tile, 0, 0))


def _norm_mod(xf, g, sc, sh):
    y = xf * lax.rsqrt(jnp.mean(xf * xf, axis=-1, keepdims=True) + EPS)
    return (y * g) * (1.0 + sc) + sh


def _mod_body(c_ref, w_ref, b_ref, o_ref):
    c = c_ref[...]
    s = c * jax.nn.sigmoid(c)
    o_ref[0] = jnp.dot(s.astype(BF16), w_ref[0].astype(BF16), preferred_element_type=F32) + b_ref[0]


def _modulation(cc, w_mod, b_mod):
    depth, d, n = w_mod.shape
    tn = 1024
    return pl.pallas_call(
        _mod_body,
        grid=(depth, n // tn),
        in_specs=[
            pl.BlockSpec((MOD_ROWS, d), lambda l, j: (0, 0)),
            pl.BlockSpec((1, d, tn), lambda l, j: (l, 0, j)),
            pl.BlockSpec((1, 1, tn), lambda l, j: (l, 0, j)),
        ],
        out_specs=pl.BlockSpec((1, MOD_ROWS, tn), lambda l, j: (l, 0, j)),
        out_shape=jax.ShapeDtypeStruct((depth, MOD_ROWS, n), F32),
        compiler_params=_params("arbitrary", "arbitrary"),
        name="modulation",
    )(cc, w_mod, b_mod.reshape(depth, 1, n))


def _norm_body(x_ref, g_ref, sc_ref, sh_ref, o_ref):
    o_ref[...] = _norm_mod(x_ref[...], g_ref[...], sc_ref[0], sh_ref[0]).astype(o_ref.dtype)


def _norm_call(x2, g, sc, sh, rows_per_batch):
    m, d = x2.shape
    tm = min(1024, rows_per_batch if sc.shape[0] > 1 else m)
    return pl.pallas_call(
        _norm_body,
        grid=(m // tm,),
        in_specs=[
            pl.BlockSpec((tm, d), lambda i: (i, 0)),
            pl.BlockSpec((1, d), lambda i: (0, 0)),
            _mod_spec(sc, rows_per_batch // tm),
            _mod_spec(sh, rows_per_batch // tm),
        ],
        out_specs=pl.BlockSpec((tm, d), lambda i: (i, 0)),
        out_shape=jax.ShapeDtypeStruct((m, d), BF16),
        compiler_params=_params("arbitrary"),
        name="norm_mod",
    )(x2, g.reshape(1, d), sc, sh)


def _proj_body(a_ref, w_ref, o_ref):
    o_ref[...] = jnp.dot(a_ref[...], w_ref[...], preferred_element_type=F32).astype(o_ref.dtype)


def _proj_call(a, w, layer, col0, n, out_dtype):
    m, k = a.shape
    tm = min(4096, m)
    tn = 512
    c0 = col0 // tn
    return pl.pallas_call(
        _proj_body,
        grid=(m // tm, n // tn),
        in_specs=[
            pl.BlockSpec((tm, k), lambda i, j: (i, 0)),
            pl.BlockSpec((None, k, tn), lambda i, j: (layer, 0, c0 + j)),
        ],
        out_specs=pl.BlockSpec((tm, tn), lambda i, j: (i, j)),
        out_shape=jax.ShapeDtypeStruct((m, n), out_dtype),
        compiler_params=_params("arbitrary", "arbitrary"),
        name="in_proj",
    )(a, w)


Q_SLABS = ATTN_W // V7X_LANES
QK_SLABS = Q_SLABS + KV_W // V7X_LANES
QKV_SLABS = QK_SLABS + KV_W // V7X_LANES


def _gates_qkv_body(a_ref, w_ref, z_ref, gain_ref, cos_ref, sup_ref, sdn_ref, g_ref, q_ref, kv_ref):
    j = pl.program_id(1)
    rb = min(512, a_ref.shape[0])
    for r0 in range(0, a_ref.shape[0], rb):
        zg = jnp.dot(a_ref[r0:r0 + rb, :], w_ref[...], preferred_element_type=F32)
        g_ref[r0:r0 + rb, :] = (0.5 * jnp.tanh(0.5 * zg) + 0.5).astype(g_ref.dtype)

    x = z_ref[...].astype(F32)
    low = lax.broadcasted_iota(jnp.int32, (1, V7X_LANES), 1) < HEAD_DIM
    x2 = x * x
    ss = jnp.where(low, jnp.sum(jnp.where(low, x2, 0.0), axis=-1, keepdims=True),
                   jnp.sum(jnp.where(low, 0.0, x2), axis=-1, keepdims=True))
    inv = jnp.where(j < QK_SLABS, lax.rsqrt(ss * (1.0 / HEAD_DIM) + EPS), 1.0)
    y = _rope((x * inv) * gain_ref[0], cos_ref[...], sup_ref[...], sdn_ref[...])
    da, db = _dup_pair(y, low)

    q_ref[...] = y.astype(q_ref.dtype)
    kv_ref[:, 0:V7X_LANES] = da.astype(kv_ref.dtype)
    kv_ref[:, V7X_LANES:2 * V7X_LANES] = db.astype(kv_ref.dtype)


def _gates_qkv_call(a, w, layer, z2, gq, gk, rope_tabs, seq_len):
    m, k = a.shape
    d = D_MODEL
    tm = min(2048, m)
    tn = 512
    assert N_BRANCH * d // tn == QKV_SLABS and tm % seq_len == 0
    c0 = GATE_OFF // tn
    ones = jnp.ones((tm, V7X_LANES), F32)
    zeros = jnp.zeros((tm, V7X_LANES), F32)
    if rope_tabs is None:
        cos, sup, sdn = ones[None], zeros[None], zeros[None]
        tab_map = lambda i, j: (0, 0, 0)
    else:
        rep = lambda t: jnp.tile(t, (tm // seq_len, 1))
        cos, sup, sdn = (jnp.stack([rep(t), ident]) for t, ident in zip(rope_tabs, (ones, zeros, zeros)))
        tab_map = lambda i, j: ((j >= QK_SLABS).astype(jnp.int32), 0, 0)
    gains = jnp.stack([jnp.tile(gq, 2) * HEAD_DIM ** -0.5, jnp.tile(gk, 2), jnp.ones((V7X_LANES,), F32)])
    gain_map = lambda i, j: ((j >= Q_SLABS).astype(jnp.int32) + (j >= QK_SLABS).astype(jnp.int32), 0, 0)
    tab_spec = pl.BlockSpec((None, tm, V7X_LANES), tab_map)
    return pl.pallas_call(
        _gates_qkv_body,
        grid=(m // tm, QKV_SLABS),
        in_specs=[
            pl.BlockSpec((tm, k), lambda i, j: (i, 0)),
            pl.BlockSpec((None, k, tn), lambda i, j: (layer, 0, c0 + j)),
            pl.BlockSpec((tm, V7X_LANES), lambda i, j: (i, j)),
            pl.BlockSpec((None, 1, V7X_LANES), gain_map),
            tab_spec, tab_spec, tab_spec,
        ],
        out_specs=[
            pl.BlockSpec((tm, tn), lambda i, j: (i, j)),
            pl.BlockSpec((tm, V7X_LANES), lambda i, j: (i, jnp.minimum(j, Q_SLABS))),
            pl.BlockSpec((tm, 2 * V7X_LANES),
                         lambda i, j: (i, jnp.where(j >= Q_SLABS, j - Q_SLABS, QKV_SLABS - Q_SLABS))),
        ],
        out_shape=[
            jax.ShapeDtypeStruct((m, N_BRANCH * d), BF16),
            jax.ShapeDtypeStruct((m, ATTN_W + V7X_LANES), BF16),
            jax.ShapeDtypeStruct((m, 2 * KV_DUP_W + 2 * V7X_LANES), BF16),
        ],
        compiler_params=_params("arbitrary", "arbitrary"),
        name="gates_qkv",
    )(a, w, z2, gains.reshape(3, 1, V7X_LANES), cos, sup, sdn)


def _pair_block_diag():
    r = lax.broadcasted_iota(jnp.int32, (V7X_LANES, V7X_LANES), 0) // HEAD_DIM
    c = lax.broadcasted_iota(jnp.int32, (V7X_LANES, V7X_LANES), 1) // HEAD_DIM
    return (r == c).astype(F32)


def _head_norm(x, g, bd):
    ss = jnp.dot(x * x, bd, precision=HIGHEST, preferred_element_type=F32)
    return (x * lax.rsqrt(ss * (1.0 / HEAD_DIM) + EPS)) * g


def _rope(x, cos, sin_up, sin_dn):
    up = pltpu.roll(x, V7X_LANES - ROPE_FREQS, 1)
    dn = pltpu.roll(x, ROPE_FREQS, 1)
    return x * cos + up * sin_up + dn * sin_dn


def _dup_pair(x, low):
    r = pltpu.roll(x, HEAD_DIM, 1)
    return jnp.where(low, x, r), jnp.where(low, r, x)


def _kv_body(z_ref, gk_ref, kv_ref):
    bd = _pair_block_diag()
    low = lax.broadcasted_iota(jnp.int32, (1, V7X_LANES), 1) < HEAD_DIM
    for s in range(2 * KV_W // V7X_LANES):
        x = z_ref[:, s * V7X_LANES:(s + 1) * V7X_LANES].astype(F32)
        if s < KV_W // V7X_LANES:
            x = _head_norm(x, gk_ref[...], bd)
        a, b = _dup_pair(x, low)
        base = 2 * s * V7X_LANES
        kv_ref[:, base:base + V7X_LANES] = a.astype(kv_ref.dtype)
        kv_ref[:, base + V7X_LANES:base + 2 * V7X_LANES] = b.astype(kv_ref.dtype)


def _kv_call(z2, gk):
    m, nz = z2.shape
    tm = min(512, m)
    return pl.pallas_call(
        _kv_body,
        grid=(m // tm,),
        in_specs=[pl.BlockSpec((tm, nz), lambda i: (i, 0)),
                  pl.BlockSpec((1, V7X_LANES), lambda i: (0, 0))],
        out_specs=pl.BlockSpec((tm, 2 * KV_DUP_W), lambda i: (i, 0)),
        out_shape=jax.ShapeDtypeStruct((m, 2 * KV_DUP_W), BF16),
        compiler_params=_params("arbitrary"),
        name="kv_prep",
    )(z2, jnp.tile(gk, 2).reshape(1, V7X_LANES))


def _rope_tables(l):
    rows = l // GRID_W
    row = jnp.repeat(jnp.arange(rows, dtype=F32), GRID_W)
    col = jnp.tile(jnp.arange(GRID_W, dtype=F32), rows)
    inv = ROPE_BASE ** (-jnp.arange(ROPE_FREQS, dtype=F32) / ROPE_FREQS)
    ang = jnp.stack([row[:, None] * inv, col[:, None] * inv], axis=1)
    cos, sin = jnp.cos(ang), jnp.sin(ang)
    zero = jnp.zeros_like(sin)
    cos_h = jnp.stack([cos, cos], axis=2).reshape(l, HEAD_DIM)
    sup_h = jnp.stack([-sin, zero], axis=2).reshape(l, HEAD_DIM)
    sdn_h = jnp.stack([zero, sin], axis=2).reshape(l, HEAD_DIM)
    return tuple(jnp.tile(t, (1, 2)) for t in (cos_h, sup_h, sdn_h))


def _attn_body(*refs, local, tq):
    it = iter(refs)
    sink_ref = next(it)
    q_ref = next(it)
    if local:
        kp_ref, kc_ref, kn_ref, vp_ref, vc_ref, vn_ref = (next(it) for _ in range(6))
    kx_ref, vx_ref = next(it), next(it)
    o_ref = next(it)

    i = pl.program_id(1)
    nb = pl.num_programs(1)
    low = lax.broadcasted_iota(jnp.int32, (1, V7X_LANES), 1) < HEAD_DIM
    rows = GQA_GROUP * tq
    if local:
        qi = lax.broadcasted_iota(jnp.int32, (rows, tq), 0) % tq
        kj = lax.broadcasted_iota(jnp.int32, (rows, tq), 1)
        mask_prev = (kj >= qi) & (i > 0)
        mask_next = (kj <= qi) & (i < nb - 1)
    row_head = lax.broadcasted_iota(jnp.int32, (rows, 1), 0) // tq
    zero = jnp.zeros((), q_ref.dtype)

    for e, h in [(e, h) for e in range(q_ref.shape[0]) for h in range(N_KV_HEADS)]:
        hs = slice(h * V7X_LANES, (h + 1) * V7X_LANES)
        qa = q_ref[e, :, 2 * h * V7X_LANES:(2 * h + 1) * V7X_LANES]
        qb = q_ref[e, :, (2 * h + 1) * V7X_LANES:(2 * h + 2) * V7X_LANES]
        qs = jnp.concatenate([jnp.where(low, qa, zero), jnp.where(low, zero, qa),
                              jnp.where(low, qb, zero), jnp.where(low, zero, qb)], axis=0)
        kparts, vparts, masks = [kx_ref[e, :, hs]], [vx_ref[e, :, hs]], {}
        if local:
            kparts = [kp_ref[e, :, hs], kc_ref[e, :, hs], kn_ref[e, :, hs]] + kparts
            vparts = [vp_ref[e, :, hs], vc_ref[e, :, hs], vn_ref[e, :, hs]] + vparts
            masks = {0: mask_prev, 2: mask_next}
        k_all = jnp.concatenate(kparts, axis=0)
        v_all = jnp.concatenate(vparts, axis=0)

        sink = jnp.zeros((rows, 1), F32)
        for g in range(GQA_GROUP):
            sink = jnp.where(row_head == g, sink_ref[GQA_GROUP * h + g], sink)
        s_all = lax.dot_general(qs, k_all, (((1,), (1,)), ((), ())), preferred_element_type=F32)
        chunks = []
        for c in range(k_all.shape[0] // tq):
            s = s_all[:, c * tq:(c + 1) * tq]
            chunks.append(jnp.where(masks[c], s, NEG_INF) if c in masks else s)
        m = jnp.maximum(sink, jnp.max(functools.reduce(jnp.maximum, chunks), axis=-1, keepdims=True))
        probs = [jnp.exp(s - m) for s in chunks]
        denom = jnp.exp(sink - m) + jnp.sum(functools.reduce(jnp.add, probs), axis=-1, keepdims=True)
        p_all = jnp.concatenate([p.astype(v_all.dtype) for p in probs], axis=1)
        o = jnp.dot(p_all, v_all, preferred_element_type=F32) / denom
        oa = jnp.where(low, o[0:tq], o[tq:2 * tq])
        ob = jnp.where(low, o[2 * tq:3 * tq], o[3 * tq:4 * tq])
        o_ref[e, :, 2 * h * V7X_LANES:(2 * h + 1) * V7X_LANES] = oa.astype(o_ref.dtype)
        o_ref[e, :, (2 * h + 1) * V7X_LANES:(2 * h + 2) * V7X_LANES] = ob.astype(o_ref.dtype)


def _attn_call(q, kv, kvx, sink, local):
    b, l, _ = q.shape
    lx = kvx.shape[1]
    tq = WINDOW
    nb = l // tq
    ne = math.gcd(b, 4)
    blk = lambda w: (ne, tq, w)
    in_specs = [pl.BlockSpec(memory_space=pltpu.SMEM),
                pl.BlockSpec(blk(ATTN_W), lambda bi, i: (bi, i, 0))]
    args = [sink, q]
    if local:
        for half in (0, 1):
            for mp in (lambda bi, i, half=half: (bi, jnp.maximum(i - 1, 0), half),
                       lambda bi, i, half=half: (bi, i, half),
                       lambda bi, i, half=half: (bi, jnp.minimum(i + 1, nb - 1), half)):
                in_specs.append(pl.BlockSpec(blk(KV_DUP_W), mp))
                args.append(kv)
    for half in (0, 1):
        in_specs.append(pl.BlockSpec((ne, lx, KV_DUP_W), lambda bi, i, half=half: (bi, 0, half)))
        args.append(kvx)
    return pl.pallas_call(
        functools.partial(_attn_body, local=local, tq=tq),
        grid=(b // ne, nb),
        in_specs=in_specs,
        out_specs=pl.BlockSpec(blk(ATTN_W), lambda bi, i: (bi, i, 0)),
        out_shape=jax.ShapeDtypeStruct((b, l, ATTN_W), BF16),
        compiler_params=_params("arbitrary", "arbitrary"),
        name="attention",
    )(*args)


def _filter_body(z_ref, w0_ref, b0_ref, w1_ref, b1_ref, fr_ref, w2_ref, dec_ref, o_ref):
    fr = fr_ref[...]
    dot = functools.partial(jnp.dot, precision=HIGHEST, preferred_element_type=F32)
    h = jnp.sin(fr * (dot(z_ref[...], w0_ref[...]) + b0_ref[...]))
    for i in range(FILTER_INNER):
        h = jnp.sin(fr * (dot(h, w1_ref[i]) + b1_ref[i]))
    dec = dec_ref[...]
    for s in range(2 * HYENA_ORDER):
        sl = slice(s * HYENA_W, (s + 1) * HYENA_W)
        o_ref[:, sl] = dot(h, w2_ref[:, sl]) * dec


def _filter_features(l):
    t = jnp.linspace(0.0, 1.0, l, dtype=F32)[:, None]
    w = 2.0 * math.pi * jnp.arange(l, dtype=F32)[:, None] / l
    bands = jnp.linspace(1e-4, FILTER_BANDS - 1, FILTER_BANDS, dtype=F32)[None, :]
    z = jnp.concatenate([t, jnp.cos(bands * w), -jnp.sin(bands * w)], axis=-1)
    deltas = jnp.linspace(math.log(DECAY_TARGET) / SLOW_DECAY_PCT, math.log(DECAY_TARGET) / FAST_DECAY_PCT,
                          HYENA_W, dtype=F32)
    decay = jnp.exp(-t * jnp.abs(deltas))
    return jnp.pad(z, ((0, 0), (0, V7X_LANES - FILTER_EMB))), decay


def _filter_call(l, w0, b0, w1, b1, freq, w2):
    zfeat, decay = _filter_features(l)
    w0p = jnp.pad(w0, ((0, V7X_LANES - FILTER_EMB), (0, 0)))
    tl = min(512, l)
    nf = 2 * HYENA_ORDER * HYENA_W
    full = lambda shape: pl.BlockSpec(shape, lambda i: (0,) * len(shape))
    return pl.pallas_call(
        _filter_body,
        grid=(l // tl,),
        in_specs=[
            pl.BlockSpec((tl, V7X_LANES), lambda i: (i, 0)),
            full((V7X_LANES, FILTER_HIDDEN)),
            full((1, FILTER_HIDDEN)),
            full((FILTER_INNER, FILTER_HIDDEN, FILTER_HIDDEN)),
            full((FILTER_INNER, 1, FILTER_HIDDEN)),
            full((1, FILTER_HIDDEN)),
            full((FILTER_HIDDEN, nf)),
            pl.BlockSpec((tl, HYENA_W), lambda i: (i, 0)),
        ],
        out_specs=pl.BlockSpec((tl, nf), lambda i: (i, 0)),
        out_shape=jax.ShapeDtypeStruct((l, nf), F32),
        compiler_params=_params("arbitrary"),
        name="hyena_filter",
    )(zfeat, w0p, b0.reshape(1, -1), w1, b1.reshape(FILTER_INNER, 1, -1), freq.reshape(1, -1), w2, decay)


def _hyena_blocks(l):
    return max(1, min(4, l // V7X_LANES))


def _dft_matrices(blk):
    n = 2 * blk
    r = jnp.arange(blk, dtype=jnp.int32)
    ang = ((r[:, None] * r[None, :]) % n).astype(F32) * (2.0 * math.pi / n)
    return jnp.cos(ang).astype(BF16), jnp.sin(ang).astype(BF16)


def _alternating(l):
    row = lax.broadcasted_iota(jnp.int32, (l, 1), 0)
    return row, jnp.where(row % 2 == 0, 1.0, -1.0).astype(F32)


def _spectrum_body(hf_ref, hb_ref, fc_ref, fs_ref, ka_ref, kb_ref, kn_ref, *, nblk):
    l = hf_ref.shape[0]
    b = l // nblk
    n = 2 * b
    row = lax.broadcasted_iota(jnp.int32, (l, 1), 0)
    _, alt = _alternating(b)
    hf = hf_ref[...]
    hbs = jnp.where(row == 0, 0.0, pltpu.roll(hb_ref[...], 1, 0))
    fc, fs = fc_ref[...], fs_ref[...]

    def transforms(h):
        out = []
        for k in range(nblk):
            hk = h[k * b:(k + 1) * b]
            hk16 = hk.astype(BF16)
            out.append(dict(
                c=jnp.dot(fc, hk16, preferred_element_type=F32),
                s=jnp.dot(fs, hk16, preferred_element_type=F32),
                first16=hk16[0:1].astype(F32),
                first=hk[0:1],
                alt=jnp.sum(hk * alt, axis=0, keepdims=True)))
        return out

    tf, tb = transforms(hf), transforms(hbs)
    brow = lax.broadcasted_iota(jnp.int32, (b, 1), 0)
    w_re = jnp.where(brow == 0, 1.0 / n, 2.0 / n)
    for d in range(-(nblk - 1), nblk):
        idx = d + nblk - 1
        if d == 0:
            kre = tf[0]["c"] + tb[0]["c"]
            kim = tb[0]["s"] - tf[0]["s"]
            kn = tf[0]["alt"] + tb[0]["alt"]
        else:
            t, e, sg = (tf, d, -1.0) if d > 0 else (tb, -d, 1.0)
            kre = t[e]["c"] + alt * (t[e - 1]["c"] - t[e - 1]["first16"])
            kim = sg * (t[e]["s"] + alt * t[e - 1]["s"])
            kn = t[e]["alt"] + t[e - 1]["alt"] - t[e - 1]["first"]
        ka_ref[0, idx] = (kre * w_re).astype(ka_ref.dtype)
        kb_ref[0, idx] = (kim * (2.0 / n)).astype(kb_ref.dtype)
        kn_ref[0, idx] = kn * (1.0 / n)


def _spectrum_call(filt, fc, fs, nblk, tc):
    l = filt.shape[0]
    b = l // nblk
    nct = HYENA_W // tc
    nlag = 2 * nblk - 1
    return pl.pallas_call(
        functools.partial(_spectrum_body, nblk=nblk),
        grid=(HYENA_ORDER, nct),
        in_specs=[
            pl.BlockSpec((l, tc), lambda o, c: (0, 2 * nct * o + c)),
            pl.BlockSpec((l, tc), lambda o, c: (0, 2 * nct * o + nct + c)),
            _const_spec((b, b)),
            _const_spec((b, b)),
        ],
        out_specs=[
            pl.BlockSpec((1, nlag, b, tc), lambda o, c: (o, 0, 0, c)),
            pl.BlockSpec((1, nlag, b, tc), lambda o, c: (o, 0, 0, c)),
            pl.BlockSpec((1, nlag, 1, tc), lambda o, c: (o, 0, 0, c)),
        ],
        out_shape=[
            jax.ShapeDtypeStruct((HYENA_ORDER, nlag, b, HYENA_W), BF16),
            jax.ShapeDtypeStruct((HYENA_ORDER, nlag, b, HYENA_W), BF16),
            jax.ShapeDtypeStruct((HYENA_ORDER, nlag, 1, HYENA_W), F32),
        ],
        compiler_params=_params("arbitrary", "arbitrary"),
        name="hyena_spectrum",
    )(filt, filt, fc, fs)


def _conv3(x, w_ref, b_ref, row):
    l = x.shape[0]
    xm = jnp.where(row == 0, 0.0, pltpu.roll(x, 1, 0))
    xp = jnp.where(row == l - 1, 0.0, pltpu.roll(x, l - 1, 0))
    return xm * w_ref[0:1, :] + x * w_ref[1:2, :] + xp * w_ref[2:3, :] + b_ref[...]


def _fftconv_body(*refs, conv_u, nblk):
    it = iter(refs)
    u_ref = next(it)
    if conv_u:
        uw_ref, ub_ref = next(it), next(it)
    g_ref, gw_ref, gb_ref = next(it), next(it), next(it)
    ka_ref, kb_ref, kn_ref, d_ref, fc_ref, fs_ref, o_ref = (next(it) for _ in range(7))

    l = u_ref.shape[1]
    b = l // nblk
    row = lax.broadcasted_iota(jnp.int32, (l, 1), 0)
    _, alt = _alternating(b)
    u = u_ref[0].astype(F32)
    if conv_u:
        u = _conv3(u, uw_ref, ub_ref, row)
    gate = _conv3(g_ref[0].astype(F32), gw_ref, gb_ref, row)
    fc, fs = fc_ref[...], fs_ref[...]

    ps, qs, ns = [], [], []
    for j in range(nblk):
        uj = u[j * b:(j + 1) * b]
        uj16 = uj.astype(BF16)
        ps.append(jnp.dot(fc, uj16, preferred_element_type=F32).astype(BF16))
        qs.append(jnp.dot(fs, uj16, preferred_element_type=F32).astype(BF16))
        ns.append(jnp.sum(uj * alt, axis=0, keepdims=True))
    for i in range(nblk):
        r = t = nyq = None
        for j in range(nblk):
            lag = i - j + nblk - 1
            ka, kb = ka_ref[0, lag], kb_ref[0, lag]
            dr = ps[j] * ka + qs[j] * kb
            dt = qs[j] * ka - ps[j] * kb
            dn = ns[j] * kn_ref[0, lag]
            r, t, nyq = (dr, dt, dn) if j == 0 else (r + dr, t + dt, nyq + dn)
        y = jnp.dot(fc, r, preferred_element_type=F32) + jnp.dot(fs, t, preferred_element_type=F32)
        rows = slice(i * b, (i + 1) * b)
        y = y + alt * nyq + u[rows] * d_ref[0]
        o_ref[0, rows, :] = (gate[rows] * y).astype(o_ref.dtype)


def _fftconv_call(u, u_col0, z, gate_col0, conv_w, conv_b, spectra, d_skip, order, fc, fs, nblk, tc, out_dtype):
    b, l, _ = z.shape
    conv_u = u is z
    nct = HYENA_W // tc
    ka, kb, kn = spectra
    blk = l // nblk
    nlag = 2 * nblk - 1
    col = lambda c0: (lambda c, bi: (bi, 0, c0 // tc + c))
    wcol = lambda c0: (lambda c, bi: (0, (c0 - HY_OFF) // tc + c))
    in_specs = [pl.BlockSpec((1, l, tc), col(u_col0))]
    args = [u]
    if conv_u:
        in_specs += [pl.BlockSpec((3, tc), wcol(u_col0)), pl.BlockSpec((1, tc), wcol(u_col0))]
        args += [conv_w, conv_b]
    in_specs += [pl.BlockSpec((1, l, tc), col(gate_col0)),
                 pl.BlockSpec((3, tc), wcol(gate_col0)), pl.BlockSpec((1, tc), wcol(gate_col0))]
    args += [z, conv_w, conv_b]
    spec = lambda rows: pl.BlockSpec((1, nlag, rows, tc), lambda c, bi: (order, 0, 0, c),
                                     pipeline_mode=pl.Buffered(1))
    in_specs += [spec(blk), spec(blk), spec(1),
                 pl.BlockSpec((1, 1, tc), lambda c, bi: (order, 0, c), pipeline_mode=pl.Buffered(1)),
                 _const_spec((blk, blk)), _const_spec((blk, blk))]
    args += [ka, kb, kn, d_skip, fc, fs]
    return pl.pallas_call(
        functools.partial(_fftconv_body, conv_u=conv_u, nblk=nblk),
        grid=(nct, b),
        in_specs=in_specs,
        out_specs=pl.BlockSpec((1, l, tc), lambda c, bi: (bi, 0, c)),
        out_shape=jax.ShapeDtypeStruct((b, l, HYENA_W), out_dtype),
        compiler_params=_params("arbitrary", "arbitrary"),
        name="hyena_conv",
    )(*args)


def _pool_body(x_ref, w_ref, s_ref, o_ref):
    l = x_ref.shape[1]
    row = lax.broadcasted_iota(jnp.int32, (l, 1), 0)
    for g, win in enumerate(POOL_WINDOWS):
        half = win // 2
        sl = slice(g * POOL_GROUP, (g + 1) * POOL_GROUP)
        x = x_ref[0, :, sl].astype(F32)

        def shifted(a, k):
            return jnp.where((row >= k) & (row < l + k), pltpu.roll(a, k % l, 0), 0.0)

        back = fwd = x
        span = 1
        while span < half:
            back = back + shifted(back, span)
            fwd = fwd + shifted(fwd, -span)
            span *= 2
        acc = shifted(back, 1) + fwd
        cnt = (jnp.minimum(row + half, l) - jnp.maximum(row - half, 0)).astype(F32)
        d = acc / cnt - x
        y = jnp.dot(d.astype(BF16), w_ref[g], preferred_element_type=F32)
        o_ref[0, :, sl] = (y * s_ref[:, sl]).astype(o_ref.dtype)


def _pool_call(z, w_grp, scale):
    b, l, _ = z.shape
    ng = len(POOL_WINDOWS)
    return pl.pallas_call(
        _pool_body,
        grid=(b,),
        in_specs=[
            pl.BlockSpec((1, l, POOL_W), lambda bi: (bi, 0, POOL_OFF // POOL_W)),
            pl.BlockSpec((ng, POOL_GROUP, POOL_GROUP), lambda bi: (0, 0, 0)),
            pl.BlockSpec((1, POOL_W), lambda bi: (0, 0)),
        ],
        out_specs=pl.BlockSpec((1, l, POOL_W), lambda bi: (bi, 0, 0)),
        out_shape=jax.ShapeDtypeStruct((b, l, POOL_W), BF16),
        compiler_params=_params("arbitrary"),
        name="pool",
    )(z, w_grp, scale.reshape(1, POOL_W))


def _merge_body(ya_ref, yh_ref, yp_ref, gt_ref, x_ref, ga_ref, g2_ref, sc_ref, sh_ref,
                wa_ref, wh_ref, wp_ref, wo_ref, xn_ref, h2_ref):
    d = x_ref.shape[1]
    cj = 512
    ya, yh, yp = ya_ref[...], yh_ref[...], yp_ref[...]
    acc = jnp.zeros(x_ref.shape, F32)
    for j in range(d // cj):
        sl = slice(j * cj, (j + 1) * cj)
        gate = lambda br: gt_ref[:, br * d + j * cj:br * d + (j + 1) * cj].astype(F32)
        m = (gate(0) * jnp.dot(ya, wa_ref[:, sl], preferred_element_type=F32)
             + gate(1) * jnp.dot(yh, wh_ref[:, sl], preferred_element_type=F32)
             + gate(2) * jnp.dot(yp, wp_ref[:, sl], preferred_element_type=F32))
        acc = acc + jnp.dot(m.astype(BF16), wo_ref[sl, :], preferred_element_type=F32)
    xn = x_ref[...] + ga_ref[0] * acc
    xn_ref[...] = xn
    h2_ref[...] = _norm_mod(xn, g2_ref[...], sc_ref[0], sh_ref[0]).astype(h2_ref.dtype)


def _merge_call(ya, yh, yp, gates, x2, ga1, g2, sc2, sh2, wa, wh, wp, wo, layer, rows_per_batch):
    m, d = x2.shape
    tm = min(512, rows_per_batch)
    rpt = rows_per_batch // tm
    rows = lambda w: pl.BlockSpec((tm, w), lambda i: (i, 0))
    weight = lambda w: pl.BlockSpec((None,) + w.shape[1:], lambda i: (layer, 0, 0), pipeline_mode=pl.Buffered(1))
    return pl.pallas_call(
        _merge_body,
        grid=(m // tm,),
        in_specs=[rows(ATTN_W), rows(HYENA_W), rows(POOL_W), rows(N_BRANCH * d), rows(d),
                  _mod_spec(ga1, rpt), pl.BlockSpec((1, d), lambda i: (0, 0)),
                  _mod_spec(sc2, rpt), _mod_spec(sh2, rpt),
                  weight(wa), weight(wh), weight(wp), weight(wo)],
        out_specs=[rows(d), rows(d)],
        out_shape=[jax.ShapeDtypeStruct((m, d), F32), jax.ShapeDtypeStruct((m, d), BF16)],
        compiler_params=_params("arbitrary"),
        name="merge",
    )(ya, yh, yp, gates, x2, ga1, g2.reshape(1, d), sc2, sh2, wa, wh, wp, wo)


def _mlp_body(*refs, has_next):
    it = iter(refs)
    h_ref, w1_ref, w2_ref, x_ref, ga_ref = (next(it) for _ in range(5))
    if has_next:
        gn_ref, sc_ref, sh_ref = next(it), next(it), next(it)
    o_ref = next(it)
    hn_ref = next(it) if has_next else None
    xs_ref = next(it)

    f = pl.program_id(1)
    last = pl.num_programs(1) - 1
    tm, d = o_ref.shape
    cn = 512
    rb = min(256, tm)

    xs_ref[f] = x_ref[...]

    def hidden(rows):
        a = jnp.dot(h_ref[rows, :], w1_ref[...], preferred_element_type=F32)
        return jnp.square(jnp.maximum(a, 0.0)).astype(BF16)

    @pl.when(f == 0)
    def _():
        a = hidden(slice(None))
        for n0 in range(0, d, cn):
            o_ref[:, n0:n0 + cn] = jnp.dot(a, w2_ref[:, n0:n0 + cn], preferred_element_type=F32)

    @pl.when((f > 0) & (f < last))
    def _():
        a = hidden(slice(None))
        for n0 in range(0, d, cn):
            o_ref[:, n0:n0 + cn] += jnp.dot(a, w2_ref[:, n0:n0 + cn], preferred_element_type=F32)

    @pl.when(f == last)
    def _():
        for r0 in range(0, tm, rb):
            rows = slice(r0, r0 + rb)
            acc = o_ref[rows, :] + jnp.dot(hidden(rows), w2_ref[...], preferred_element_type=F32)
            x_rows = jnp.concatenate([xs_ref[c, rows, :] for c in range(xs_ref.shape[0])], axis=1)
            xo = x_rows + ga_ref[0] * acc
            o_ref[rows, :] = xo
            if has_next:
                hn_ref[rows, :] = _norm_mod(xo, gn_ref[...], sc_ref[0], sh_ref[0]).astype(hn_ref.dtype)


def _mlp_call(h2, w1, w2, layer, xn, ga2, nxt, rows_per_batch):
    m, d = xn.shape
    ff = w1.shape[2]
    tm, tf = min(1024, rows_per_batch if ga2.shape[0] > 1 else m), 512
    rpt = max(rows_per_batch // tm, 1)
    nf = ff // tf
    assert nf >= 2
    xw = d // nf
    assert xw % V7X_LANES == 0
    has_next = nxt is not None
    rows = pl.BlockSpec((tm, d), lambda i, f: (i, 0))
    in_specs = [rows, pl.BlockSpec((None, d, tf), lambda i, f: (layer, 0, f)),
                pl.BlockSpec((None, tf, d), lambda i, f: (layer, f, 0)),
                pl.BlockSpec((tm, xw), lambda i, f: (i, f)), _mod_spec(ga2, rpt)]
    args = [h2, w1, w2, xn, ga2]
    out_specs = [rows]
    out_shape = [jax.ShapeDtypeStruct((m, d), F32)]
    if has_next:
        gn, scn, shn = nxt
        in_specs += [pl.BlockSpec((1, d), lambda i, f: (0, 0)), _mod_spec(scn, rpt), _mod_spec(shn, rpt)]
        args += [gn.reshape(1, d), scn, shn]
        out_specs.append(rows)
        out_shape.append(jax.ShapeDtypeStruct((m, d), BF16))
    outs = pl.pallas_call(
        functools.partial(_mlp_body, has_next=has_next),
        grid=(m // tm, nf),
        in_specs=in_specs,
        out_specs=out_specs,
        out_shape=out_shape,
        scratch_shapes=[pltpu.VMEM((nf, tm, xw), F32)],
        compiler_params=_params("arbitrary", "arbitrary"),
        name="mlp",
    )(*args)
    return (outs[0], outs[1]) if has_next else (outs[0], None)


def _mixers(z, q, kv, kvx, sink, local, hy, pool_w, pool_scale):
    b, l, _ = z.shape
    y_att = _attn_call(q, kv, kvx, sink, local)
    conv_w, conv_b, filt_params, d_skip, (fc, fs) = hy
    nblk = _hyena_blocks(l)
    tc = 256
    spectra = _spectrum_call(_filter_call(l, *filt_params), fc, fs, nblk, tc)
    conv = functools.partial(_fftconv_call, conv_w=conv_w, conv_b=conv_b, spectra=spectra, d_skip=d_skip,
                             fc=fc, fs=fs, nblk=nblk, tc=tc)
    z1 = conv(z, HY_OFF, z, HY_OFF + HYENA_W, order=0, out_dtype=F32)
    y_hy = conv(z1, 0, z, HY_OFF + 2 * HYENA_W, order=1, out_dtype=BF16)
    y_pool = _pool_call(z, pool_w, pool_scale)
    return (y_att.reshape(b * l, ATTN_W), y_hy.reshape(b * l, HYENA_W), y_pool.reshape(b * l, POOL_W))


def kernel(x, c, ctx, c_ctx, norm1_g, norm2_g, w_mod, b_mod, w_in, q_norm_g, k_norm_g, sink, hy_conv_w, hy_conv_b, filt_w0, filt_b0, filt_w1, filt_b1, filt_freq, filt_w2, hy_bias, pool_w, pool_scale, w_att_o, w_hy_o, w_pool_o, w_out, mlp_w1, mlp_w2):
    b, l, d = x.shape
    lc = ctx.shape[1]
    depth = w_mod.shape[0]

    cc = jnp.concatenate([c, c_ctx[None, :], jnp.zeros((MOD_ROWS - b - 1, d), F32)], axis=0)
    mods = _modulation(cc, w_mod, b_mod)

    def chunks(layer, lo, hi):
        return [mods[layer, lo:hi, i * d:(i + 1) * d].reshape(hi - lo, 1, d) for i in range(6)]

    as_bf16 = lambda w: w.astype(BF16)
    w_in_b, w_att_b, w_hy_b, w_pool_b, w_out_b = map(as_bf16, (w_in, w_att_o, w_hy_o, w_pool_o, w_out))
    w1_b, w2_b, pool_w_b = map(as_bf16, (mlp_w1, mlp_w2, pool_w))

    rope_tabs = _rope_tables(l)
    dft_x = _dft_matrices(l // _hyena_blocks(l))
    dft_c = _dft_matrices(lc // _hyena_blocks(lc))

    x2 = x.reshape(b * l, d)
    c2 = ctx.reshape(b * lc, d)
    sh1, sc1 = chunks(0, 0, b)[:2]
    csh1, csc1 = chunks(0, b, b + 1)[:2]
    hx = _norm_call(x2, norm1_g[0], sc1, sh1, l)
    hc = _norm_call(c2, norm1_g[0], csc1, csh1, lc)

    for layer in range(depth):
        last = layer == depth - 1
        _, _, ga1, sh2, sc2, ga2 = chunks(layer, 0, b)
        _, _, cga1, csh2, csc2, cga2 = chunks(layer, b, b + 1)
        filt_params = (filt_w0[layer], filt_b0[layer], filt_w1[layer], filt_b1[layer], filt_freq[layer],
                       filt_w2[layer])
        conv_b = hy_conv_b[layer].reshape(1, -1)
        d_skip = hy_bias[layer].reshape(HYENA_ORDER, 1, HYENA_W)
        merge_w = (w_att_b, w_hy_b, w_pool_b, w_out_b, layer)

        gq, gk = q_norm_g[layer], k_norm_g[layer]
        if last:
            kvc = _kv_call(_proj_call(hc, w_in_b, layer, K_OFF, 2 * KV_W, BF16), gk)
        else:
            zc = _proj_call(hc, w_in_b, layer, 0, GATE_OFF, BF16)
            gc, qc, kvc = _gates_qkv_call(hc, w_in_b, layer, zc, gq, gk, None, lc)
            zc, qc = zc.reshape(b, lc, GATE_OFF), qc.reshape(b, lc, -1)
        kvc = kvc.reshape(b, lc, -1)

        zx = _proj_call(hx, w_in_b, layer, 0, GATE_OFF, BF16)
        gx, qx, kvx = _gates_qkv_call(hx, w_in_b, layer, zx, gq, gk, rope_tabs, l)
        zx, qx, kvx = zx.reshape(b, l, GATE_OFF), qx.reshape(b, l, -1), kvx.reshape(b, l, -1)
        hy = (hy_conv_w[layer], conv_b, filt_params, d_skip, dft_x)
        ya, yh, yp = _mixers(zx, qx, kvx, kvc, sink[layer], True, hy, pool_w_b[layer], pool_scale[layer])
        xn, h2 = _merge_call(ya, yh, yp, gx, x2, ga1, norm2_g[layer], sc2, sh2, *merge_w, l)
        nxt = None if last else (norm1_g[layer + 1], *reversed(chunks(layer + 1, 0, b)[:2]))
        x2, hx = _mlp_call(h2, w1_b, w2_b, layer, xn, ga2, nxt, l)

        if not last:
            hyc = (hy_conv_w[layer], conv_b, filt_params, d_skip, dft_c)
            ya, yh, yp = _mixers(zc, qc, None, kvc, sink[layer], False, hyc, pool_w_b[layer], pool_scale[layer])
            cn, h2c = _merge_call(ya, yh, yp, gc, c2, cga1, norm2_g[layer], csc2, csh2, *merge_w, lc)
            nxt = (norm1_g[layer + 1], *reversed(chunks(layer + 1, b, b + 1)[:2]))
            c2, hc = _mlp_call(h2c, w1_b, w2_b, layer, cn, cga2, nxt, lc)

    return x2.reshape(b, l, d)
```

```python
import functools
import math

import jax
import jax.numpy as jnp
from jax import lax
from jax.experimental import pallas as pl
from jax.experimental.pallas import tpu as pltpu

D_MODEL = 2048
DEPTH = 2
GRID_W = 64
EPS = 1e-6
NEG_INF = -1e30

N_HEADS = 16
N_KV_HEADS = 4
GQA_GROUP = N_HEADS // N_KV_HEADS
HEAD_DIM = 64
ATTN_W = N_HEADS * HEAD_DIM
KV_W = N_KV_HEADS * HEAD_DIM
WINDOW = 128
ROPE_FREQS = HEAD_DIM // 4
ROPE_BASE = 10000.0

HYENA_W = D_MODEL // 4
HYENA_ORDER = 2
FILTER_BANDS = 16
FILTER_EMB = 1 + 2 * FILTER_BANDS
FILTER_HIDDEN = 64
FILTER_INNER = 2
DECAY_TARGET = 1e-2
FAST_DECAY_PCT = 0.3
SLOW_DECAY_PCT = 1.5

POOL_W = D_MODEL // 4
POOL_WINDOWS = (2, 4, 8, 16)
POOL_GROUP = POOL_W // len(POOL_WINDOWS)

N_BRANCH = 3
D_FF = 4 * D_MODEL

Q_OFF = 0
K_OFF = Q_OFF + ATTN_W
V_OFF = K_OFF + KV_W
HY_OFF = V_OFF + KV_W
POOL_OFF = HY_OFF + 3 * HYENA_W
GATE_OFF = POOL_OFF + POOL_W
IN_W = GATE_OFF + N_BRANCH * D_MODEL

V7X_LANES = 128
V7X_VMEM_LIMIT = 62 * 1024 * 1024
KV_DUP_W = N_KV_HEADS * V7X_LANES
MOD_ROWS = 24

F32 = jnp.float32
BF16 = jnp.bfloat16
HIGHEST = lax.Precision.HIGHEST


def _params(*semantics):
    return pltpu.CompilerParams(dimension_semantics=semantics, vmem_limit_bytes=V7X_VMEM_LIMIT)


def _const_spec(shape):
    zeros = (0,) * len(shape)
    return pl.BlockSpec(shape, lambda *_: zeros, pipeline_mode=pl.Buffered(1))


def _mod_spec(arr, rows_per_mod_tile):
    d = arr.shape[-1]
    if arr.shape[0] == 1:
        return pl.BlockSpec((1, 1, d), lambda i, *_: (0, 0, 0))
    return pl.BlockSpec((1, 1, d), lambda i, *_: (i // rows_per_mod_tile, 0, 0))


def _norm_mod(xf, g, sc, sh):
    y = xf * lax.rsqrt(jnp.mean(xf * xf, axis=-1, keepdims=True) + EPS)
    return (y * g) * (1.0 + sc) + sh


def _mod_body(c_ref, w_ref, b_ref, o_ref):
    c = c_ref[...]
    s = c * jax.nn.sigmoid(c)
    o_ref[0] = jnp.dot(s.astype(BF16), w_ref[0].astype(BF16), preferred_element_type=F32) + b_ref[0]


def _modulation(cc, w_mod, b_mod):
    depth, d, n = w_mod.shape
    tn = 1024
    return pl.pallas_call(
        _mod_body,
        grid=(depth, n // tn),
        in_specs=[
            pl.BlockSpec((MOD_ROWS, d), lambda l, j: (0, 0)),
            pl.BlockSpec((1, d, tn), lambda l, j: (l, 0, j)),
            pl.BlockSpec((1, 1, tn), lambda l, j: (l, 0, j)),
        ],
        out_specs=pl.BlockSpec((1, MOD_ROWS, tn), lambda l, j: (l, 0, j)),
        out_shape=jax.ShapeDtypeStruct((depth, MOD_ROWS, n), F32),
        compiler_params=_params("arbitrary", "arbitrary"),
        name="modulation",
    )(cc, w_mod, b_mod.reshape(depth, 1, n))


def _norm_body(x_ref, g_ref, sc_ref, sh_ref, o_ref):
    o_ref[...] = _norm_mod(x_ref[...], g_ref[...], sc_ref[0], sh_ref[0]).astype(o_ref.dtype)


def _norm_call(x2, g, sc, sh, rows_per_batch):
    m, d = x2.shape
    tm = min(1024, rows_per_batch if sc.shape[0] > 1 else m)
    return pl.pallas_call(
        _norm_body,
        grid=(m // tm,),
        in_specs=[
            pl.BlockSpec((tm, d), lambda i: (i, 0)),
            pl.BlockSpec((1, d), lambda i: (0, 0)),
            _mod_spec(sc, rows_per_batch // tm),
            _mod_spec(sh, rows_per_batch // tm),
        ],
        out_specs=pl.BlockSpec((tm, d), lambda i: (i, 0)),
        out_shape=jax.ShapeDtypeStruct((m, d), BF16),
        compiler_params=_params("arbitrary"),
        name="norm_mod",
    )(x2, g.reshape(1, d), sc, sh)


def _proj_body(a_ref, w_ref, o_ref):
    o_ref[...] = jnp.dot(a_ref[...], w_ref[...], preferred_element_type=F32).astype(o_ref.dtype)


def _proj_call(a, w, layer, col0, n, out_dtype):
    m, k = a.shape
    tm = min(4096, m)
    tn = 512
    c0 = col0 // tn
    return pl.pallas_call(
        _proj_body,
        grid=(m // tm, n // tn),
        in_specs=[
            pl.BlockSpec((tm, k), lambda i, j: (i, 0)),
            pl.BlockSpec((None, k, tn), lambda i, j: (layer, 0, c0 + j)),
        ],
        out_specs=pl.BlockSpec((tm, tn), lambda i, j: (i, j)),
        out_shape=jax.ShapeDtypeStruct((m, n), out_dtype),
        compiler_params=_params("arbitrary", "arbitrary"),
        name="in_proj",
    )(a, w)


Q_SLABS = ATTN_W // V7X_LANES
QK_SLABS = Q_SLABS + KV_W // V7X_LANES
QKV_SLABS = QK_SLABS + KV_W // V7X_LANES


def _gates_qkv_body(a_ref, w_ref, z_ref, gain_ref, cos_ref, sup_ref, sdn_ref, g_ref, q_ref, kv_ref):
    j = pl.program_id(1)
    rb = min(512, a_ref.shape[0])
    for r0 in range(0, a_ref.shape[0], rb):
        zg = jnp.dot(a_ref[r0:r0 + rb, :], w_ref[...], preferred_element_type=F32)
        g_ref[r0:r0 + rb, :] = (0.5 * jnp.tanh(0.5 * zg) + 0.5).astype(g_ref.dtype)

    low = lax.broadcasted_iota(jnp.int32, (1, V7X_LANES), 1) < HEAD_DIM
    tr = cos_ref.shape[0]
    for r0 in range(0, z_ref.shape[0], tr):
        rows = slice(r0, r0 + tr)
        x = z_ref[rows, :].astype(F32)
        x2 = x * x
        ss = jnp.where(low, jnp.sum(jnp.where(low, x2, 0.0), axis=-1, keepdims=True),
                       jnp.sum(jnp.where(low, 0.0, x2), axis=-1, keepdims=True))
        inv = jnp.where(j < QK_SLABS, lax.rsqrt(ss * (1.0 / HEAD_DIM) + EPS), 1.0)
        y = _rope((x * inv) * gain_ref[0], cos_ref[...], sup_ref[...], sdn_ref[...])
        da, db = _dup_pair(y, low)
        q_ref[rows, :] = y.astype(q_ref.dtype)
        kv_ref[rows, 0:V7X_LANES] = da.astype(kv_ref.dtype)
        kv_ref[rows, V7X_LANES:2 * V7X_LANES] = db.astype(kv_ref.dtype)


def _gates_qkv_call(a, w, layer, z2, gq, gk, rope_tabs, seq_len):
    m, k = a.shape
    d = D_MODEL
    tm = min(4096, m)
    tn = 512
    tr = min(2048, tm)
    assert N_BRANCH * d // tn == QKV_SLABS and tr % seq_len == 0 and tm % tr == 0
    c0 = GATE_OFF // tn
    ones = jnp.ones((tr, V7X_LANES), F32)
    zeros = jnp.zeros((tr, V7X_LANES), F32)
    if rope_tabs is None:
        cos, sup, sdn = ones[None], zeros[None], zeros[None]
        tab_map = lambda i, j: (0, 0, 0)
    else:
        rep = lambda t: jnp.tile(t, (tr // seq_len, 1))
        cos, sup, sdn = (jnp.stack([rep(t), ident]) for t, ident in zip(rope_tabs, (ones, zeros, zeros)))
        tab_map = lambda i, j: ((j >= QK_SLABS).astype(jnp.int32), 0, 0)
    gains = jnp.stack([jnp.tile(gq, 2) * HEAD_DIM ** -0.5, jnp.tile(gk, 2), jnp.ones((V7X_LANES,), F32)])
    gain_map = lambda i, j: ((j >= Q_SLABS).astype(jnp.int32) + (j >= QK_SLABS).astype(jnp.int32), 0, 0)
    tab_spec = pl.BlockSpec((None, tr, V7X_LANES), tab_map)
    return pl.pallas_call(
        _gates_qkv_body,
        grid=(m // tm, QKV_SLABS),
        in_specs=[
            pl.BlockSpec((tm, k), lambda i, j: (i, 0)),
            pl.BlockSpec((None, k, tn), lambda i, j: (layer, 0, c0 + j)),
            pl.BlockSpec((tm, V7X_LANES), lambda i, j: (i, j)),
            pl.BlockSpec((None, 1, V7X_LANES), gain_map),
            tab_spec, tab_spec, tab_spec,
        ],
        out_specs=[
            pl.BlockSpec((tm, tn), lambda i, j: (i, j)),
            pl.BlockSpec((tm, V7X_LANES), lambda i, j: (i, jnp.minimum(j, Q_SLABS))),
            pl.BlockSpec((tm, 2 * V7X_LANES),
                         lambda i, j: (i, jnp.where(j >= Q_SLABS, j - Q_SLABS, QKV_SLABS - Q_SLABS))),
        ],
        out_shape=[
            jax.ShapeDtypeStruct((m, N_BRANCH * d), BF16),
            jax.ShapeDtypeStruct((m, ATTN_W + V7X_LANES), BF16),
            jax.ShapeDtypeStruct((m, 2 * KV_DUP_W + 2 * V7X_LANES), BF16),
        ],
        compiler_params=_params("arbitrary", "arbitrary"),
        name="gates_qkv",
    )(a, w, z2, gains.reshape(3, 1, V7X_LANES), cos, sup, sdn)


def _pair_block_diag():
    r = lax.broadcasted_iota(jnp.int32, (V7X_LANES, V7X_LANES), 0) // HEAD_DIM
    c = lax.broadcasted_iota(jnp.int32, (V7X_LANES, V7X_LANES), 1) // HEAD_DIM
    return (r == c).astype(F32)


def _head_norm(x, g, bd):
    ss = jnp.dot(x * x, bd, precision=HIGHEST, preferred_element_type=F32)
    return (x * lax.rsqrt(ss * (1.0 / HEAD_DIM) + EPS)) * g


def _rope(x, cos, sin_up, sin_dn):
    up = pltpu.roll(x, V7X_LANES - ROPE_FREQS, 1)
    dn = pltpu.roll(x, ROPE_FREQS, 1)
    return x * cos + up * sin_up + dn * sin_dn


def _dup_pair(x, low):
    r = pltpu.roll(x, HEAD_DIM, 1)
    return jnp.where(low, x, r), jnp.where(low, r, x)


def _kv_body(z_ref, gk_ref, kv_ref):
    bd = _pair_block_diag()
    low = lax.broadcasted_iota(jnp.int32, (1, V7X_LANES), 1) < HEAD_DIM
    for s in range(2 * KV_W // V7X_LANES):
        x = z_ref[:, s * V7X_LANES:(s + 1) * V7X_LANES].astype(F32)
        if s < KV_W // V7X_LANES:
            x = _head_norm(x, gk_ref[...], bd)
        a, b = _dup_pair(x, low)
        base = 2 * s * V7X_LANES
        kv_ref[:, base:base + V7X_LANES] = a.astype(kv_ref.dtype)
        kv_ref[:, base + V7X_LANES:base + 2 * V7X_LANES] = b.astype(kv_ref.dtype)


def _kv_call(z2, gk):
    m, nz = z2.shape
    tm = min(512, m)
    return pl.pallas_call(
        _kv_body,
        grid=(m // tm,),
        in_specs=[pl.BlockSpec((tm, nz), lambda i: (i, 0)),
                  pl.BlockSpec((1, V7X_LANES), lambda i: (0, 0))],
        out_specs=pl.BlockSpec((tm, 2 * KV_DUP_W), lambda i: (i, 0)),
        out_shape=jax.ShapeDtypeStruct((m, 2 * KV_DUP_W), BF16),
        compiler_params=_params("arbitrary"),
        name="kv_prep",
    )(z2, jnp.tile(gk, 2).reshape(1, V7X_LANES))


def _rope_tables(l):
    rows = l // GRID_W
    row = jnp.repeat(jnp.arange(rows, dtype=F32), GRID_W)
    col = jnp.tile(jnp.arange(GRID_W, dtype=F32), rows)
    inv = ROPE_BASE ** (-jnp.arange(ROPE_FREQS, dtype=F32) / ROPE_FREQS)
    ang = jnp.stack([row[:, None] * inv, col[:, None] * inv], axis=1)
    cos, sin = jnp.cos(ang), jnp.sin(ang)
    zero = jnp.zeros_like(sin)
    cos_h = jnp.stack([cos, cos], axis=2).reshape(l, HEAD_DIM)
    sup_h = jnp.stack([-sin, zero], axis=2).reshape(l, HEAD_DIM)
    sdn_h = jnp.stack([zero, sin], axis=2).reshape(l, HEAD_DIM)
    return tuple(jnp.tile(t, (1, 2)) for t in (cos_h, sup_h, sdn_h))


def _attn_body(*refs, local, tq):
    it = iter(refs)
    sink_ref = next(it)
    q_ref = next(it)
    if local:
        kp_ref, kc_ref, kn_ref, vp_ref, vc_ref, vn_ref = (next(it) for _ in range(6))
    kx_ref, vx_ref = next(it), next(it)
    o_ref = next(it)

    i = pl.program_id(1)
    nb = pl.num_programs(1)
    low = lax.broadcasted_iota(jnp.int32, (1, V7X_LANES), 1) < HEAD_DIM
    rows = GQA_GROUP * tq
    if local:
        qi = lax.broadcasted_iota(jnp.int32, (rows, tq), 0) % tq
        kj = lax.broadcasted_iota(jnp.int32, (rows, tq), 1)
        mask_prev = (kj >= qi) & (i > 0)
        mask_next = (kj <= qi) & (i < nb - 1)
    row_head = lax.broadcasted_iota(jnp.int32, (rows, 1), 0) // tq
    zero = jnp.zeros((), q_ref.dtype)

    for e, h in [(e, h) for e in range(q_ref.shape[0]) for h in range(N_KV_HEADS)]:
        hs = slice(h * V7X_LANES, (h + 1) * V7X_LANES)
        qa = q_ref[e, :, 2 * h * V7X_LANES:(2 * h + 1) * V7X_LANES]
        qb = q_ref[e, :, (2 * h + 1) * V7X_LANES:(2 * h + 2) * V7X_LANES]
        qs = jnp.concatenate([jnp.where(low, qa, zero), jnp.where(low, zero, qa),
                              jnp.where(low, qb, zero), jnp.where(low, zero, qb)], axis=0)
        kparts, vparts, masks = [kx_ref[e, :, hs]], [vx_ref[e, :, hs]], {}
        if local:
            kparts = [kp_ref[e, :, hs], kc_ref[e, :, hs], kn_ref[e, :, hs]] + kparts
            vparts = [vp_ref[e, :, hs], vc_ref[e, :, hs], vn_ref[e, :, hs]] + vparts
            masks = {0: mask_prev, 2: mask_next}
        k_all = jnp.concatenate(kparts, axis=0)
        v_all = jnp.concatenate(vparts, axis=0)

        sink = jnp.zeros((rows, 1), F32)
        for g in range(GQA_GROUP):
            sink = jnp.where(row_head == g, sink_ref[GQA_GROUP * h + g], sink)
        s_all = lax.dot_general(qs, k_all, (((1,), (1,)), ((), ())), preferred_element_type=F32)
        chunks = []
        for c in range(k_all.shape[0] // tq):
            s = s_all[:, c * tq:(c + 1) * tq]
            chunks.append(jnp.where(masks[c], s, NEG_INF) if c in masks else s)
        m = jnp.maximum(sink, jnp.max(functools.reduce(jnp.maximum, chunks), axis=-1, keepdims=True))
        probs = [jnp.exp(s - m) for s in chunks]
        denom = jnp.exp(sink - m) + jnp.sum(functools.reduce(jnp.add, probs), axis=-1, keepdims=True)
        p_all = jnp.concatenate([p.astype(v_all.dtype) for p in probs], axis=1)
        o = jnp.dot(p_all, v_all, preferred_element_type=F32) / denom
        oa = jnp.where(low, o[0:tq], o[tq:2 * tq])
        ob = jnp.where(low, o[2 * tq:3 * tq], o[3 * tq:4 * tq])
        o_ref[e, :, 2 * h * V7X_LANES:(2 * h + 1) * V7X_LANES] = oa.astype(o_ref.dtype)
        o_ref[e, :, (2 * h + 1) * V7X_LANES:(2 * h + 2) * V7X_LANES] = ob.astype(o_ref.dtype)


def _attn_call(q, kv, kvx, sink, local):
    b, l, _ = q.shape
    lx = kvx.shape[1]
    tq = WINDOW
    nb = l // tq
    ne = math.gcd(b, 4)
    blk = lambda w: (ne, tq, w)
    in_specs = [pl.BlockSpec(memory_space=pltpu.SMEM),
                pl.BlockSpec(blk(ATTN_W), lambda bi, i: (bi, i, 0))]
    args = [sink, q]
    if local:
        for half in (0, 1):
            for mp in (lambda bi, i, half=half: (bi, jnp.maximum(i - 1, 0), half),
                       lambda bi, i, half=half: (bi, i, half),
                       lambda bi, i, half=half: (bi, jnp.minimum(i + 1, nb - 1), half)):
                in_specs.append(pl.BlockSpec(blk(KV_DUP_W), mp))
                args.append(kv)
    for half in (0, 1):
        in_specs.append(pl.BlockSpec((ne, lx, KV_DUP_W), lambda bi, i, half=half: (bi, 0, half)))
        args.append(kvx)
    return pl.pallas_call(
        functools.partial(_attn_body, local=local, tq=tq),
        grid=(b // ne, nb),
        in_specs=in_specs,
        out_specs=pl.BlockSpec(blk(ATTN_W), lambda bi, i: (bi, i, 0)),
        out_shape=jax.ShapeDtypeStruct((b, l, ATTN_W), BF16),
        compiler_params=_params("arbitrary", "arbitrary"),
        name="attention",
    )(*args)


def _filter_body(z_ref, w0_ref, b0_ref, w1_ref, b1_ref, fr_ref, w2_ref, dec_ref, o_ref):
    fr = fr_ref[...]
    dot = functools.partial(jnp.dot, precision=HIGHEST, preferred_element_type=F32)
    h = jnp.sin(fr * (dot(z_ref[...], w0_ref[...]) + b0_ref[...]))
    for i in range(FILTER_INNER):
        h = jnp.sin(fr * (dot(h, w1_ref[i]) + b1_ref[i]))
    dec = dec_ref[...]
    for s in range(2 * HYENA_ORDER):
        sl = slice(s * HYENA_W, (s + 1) * HYENA_W)
        o_ref[:, sl] = dot(h, w2_ref[:, sl]) * dec


def _filter_features(l):
    t = jnp.linspace(0.0, 1.0, l, dtype=F32)[:, None]
    w = 2.0 * math.pi * jnp.arange(l, dtype=F32)[:, None] / l
    bands = jnp.linspace(1e-4, FILTER_BANDS - 1, FILTER_BANDS, dtype=F32)[None, :]
    z = jnp.concatenate([t, jnp.cos(bands * w), -jnp.sin(bands * w)], axis=-1)
    deltas = jnp.linspace(math.log(DECAY_TARGET) / SLOW_DECAY_PCT, math.log(DECAY_TARGET) / FAST_DECAY_PCT,
                          HYENA_W, dtype=F32)
    decay = jnp.exp(-t * jnp.abs(deltas))
    return jnp.pad(z, ((0, 0), (0, V7X_LANES - FILTER_EMB))), decay


def _filter_call(l, w0, b0, w1, b1, freq, w2):
    zfeat, decay = _filter_features(l)
    w0p = jnp.pad(w0, ((0, V7X_LANES - FILTER_EMB), (0, 0)))
    tl = min(512, l)
    nf = 2 * HYENA_ORDER * HYENA_W
    full = lambda shape: pl.BlockSpec(shape, lambda i: (0,) * len(shape))
    return pl.pallas_call(
        _filter_body,
        grid=(l // tl,),
        in_specs=[
            pl.BlockSpec((tl, V7X_LANES), lambda i: (i, 0)),
            full((V7X_LANES, FILTER_HIDDEN)),
            full((1, FILTER_HIDDEN)),
            full((FILTER_INNER, FILTER_HIDDEN, FILTER_HIDDEN)),
            full((FILTER_INNER, 1, FILTER_HIDDEN)),
            full((1, FILTER_HIDDEN)),
            full((FILTER_HIDDEN, nf)),
            pl.BlockSpec((tl, HYENA_W), lambda i: (i, 0)),
        ],
        out_specs=pl.BlockSpec((tl, nf), lambda i: (i, 0)),
        out_shape=jax.ShapeDtypeStruct((l, nf), F32),
        compiler_params=_params("arbitrary"),
        name="hyena_filter",
    )(zfeat, w0p, b0.reshape(1, -1), w1, b1.reshape(FILTER_INNER, 1, -1), freq.reshape(1, -1), w2, decay)


def _hyena_blocks(l):
    return max(1, min(4, l // V7X_LANES))


def _dft_matrices(blk):
    n = 2 * blk
    r = jnp.arange(blk, dtype=jnp.int32)
    ang = ((r[:, None] * r[None, :]) % n).astype(F32) * (2.0 * math.pi / n)
    return jnp.cos(ang).astype(BF16), jnp.sin(ang).astype(BF16)


def _alternating(l):
    row = lax.broadcasted_iota(jnp.int32, (l, 1), 0)
    return row, jnp.where(row % 2 == 0, 1.0, -1.0).astype(F32)


def _spectrum_body(hf_ref, hb_ref, fc_ref, fs_ref, ka_ref, kb_ref, kn_ref, *, nblk):
    l = hf_ref.shape[0]
    b = l // nblk
    n = 2 * b
    row = lax.broadcasted_iota(jnp.int32, (l, 1), 0)
    _, alt = _alternating(b)
    hf = hf_ref[...]
    hbs = jnp.where(row == 0, 0.0, pltpu.roll(hb_ref[...], 1, 0))
    fc, fs = fc_ref[...], fs_ref[...]

    def transforms(h):
        out = []
        for k in range(nblk):
            hk = h[k * b:(k + 1) * b]
            hk16 = hk.astype(BF16)
            out.append(dict(
                c=jnp.dot(fc, hk16, preferred_element_type=F32),
                s=jnp.dot(fs, hk16, preferred_element_type=F32),
                first16=hk16[0:1].astype(F32),
                first=hk[0:1],
                alt=jnp.sum(hk * alt, axis=0, keepdims=True)))
        return out

    tf, tb = transforms(hf), transforms(hbs)
    brow = lax.broadcasted_iota(jnp.int32, (b, 1), 0)
    w_re = jnp.where(brow == 0, 1.0 / n, 2.0 / n)
    for d in range(-(nblk - 1), nblk):
        idx = d + nblk - 1
        if d == 0:
            kre = tf[0]["c"] + tb[0]["c"]
            kim = tb[0]["s"] - tf[0]["s"]
            kn = tf[0]["alt"] + tb[0]["alt"]
        else:
            t, e, sg = (tf, d, -1.0) if d > 0 else (tb, -d, 1.0)
            kre = t[e]["c"] + alt * (t[e - 1]["c"] - t[e - 1]["first16"])
            kim = sg * (t[e]["s"] + alt * t[e - 1]["s"])
            kn = t[e]["alt"] + t[e - 1]["alt"] - t[e - 1]["first"]
        ka_ref[0, idx] = (kre * w_re).astype(ka_ref.dtype)
        kb_ref[0, idx] = (kim * (2.0 / n)).astype(kb_ref.dtype)
        kn_ref[0, idx] = kn * (1.0 / n)


def _spectrum_call(filt, fc, fs, nblk, tc):
    l = filt.shape[0]
    b = l // nblk
    nct = HYENA_W // tc
    nlag = 2 * nblk - 1
    return pl.pallas_call(
        functools.partial(_spectrum_body, nblk=nblk),
        grid=(HYENA_ORDER, nct),
        in_specs=[
            pl.BlockSpec((l, tc), lambda o, c: (0, 2 * nct * o + c)),
            pl.BlockSpec((l, tc), lambda o, c: (0, 2 * nct * o + nct + c)),
            _const_spec((b, b)),
            _const_spec((b, b)),
        ],
        out_specs=[
            pl.BlockSpec((1, nlag, b, tc), lambda o, c: (o, 0, 0, c)),
            pl.BlockSpec((1, nlag, b, tc), lambda o, c: (o, 0, 0, c)),
            pl.BlockSpec((1, nlag, 1, tc), lambda o, c: (o, 0, 0, c)),
        ],
        out_shape=[
            jax.ShapeDtypeStruct((HYENA_ORDER, nlag, b, HYENA_W), BF16),
            jax.ShapeDtypeStruct((HYENA_ORDER, nlag, b, HYENA_W), BF16),
            jax.ShapeDtypeStruct((HYENA_ORDER, nlag, 1, HYENA_W), F32),
        ],
        compiler_params=_params("arbitrary", "arbitrary"),
        name="hyena_spectrum",
    )(filt, filt, fc, fs)


def _conv3(x, w_ref, b_ref, row):
    l = x.shape[0]
    xm = jnp.where(row == 0, 0.0, pltpu.roll(x, 1, 0))
    xp = jnp.where(row == l - 1, 0.0, pltpu.roll(x, l - 1, 0))
    return xm * w_ref[0:1, :] + x * w_ref[1:2, :] + xp * w_ref[2:3, :] + b_ref[...]


def _fftconv_body(*refs, conv_u, nblk):
    it = iter(refs)
    u_ref = next(it)
    if conv_u:
        uw_ref, ub_ref = next(it), next(it)
    g_ref, gw_ref, gb_ref = next(it), next(it), next(it)
    ka_ref, kb_ref, kn_ref, d_ref, fc_ref, fs_ref, o_ref = (next(it) for _ in range(7))

    l = u_ref.shape[1]
    b = l // nblk
    row = lax.broadcasted_iota(jnp.int32, (l, 1), 0)
    _, alt = _alternating(b)
    u = u_ref[0].astype(F32)
    if conv_u:
        u = _conv3(u, uw_ref, ub_ref, row)
    gate = _conv3(g_ref[0].astype(F32), gw_ref, gb_ref, row)
    fc, fs = fc_ref[...], fs_ref[...]

    ps, qs, ns = [], [], []
    for j in range(nblk):
        uj = u[j * b:(j + 1) * b]
        uj16 = uj.astype(BF16)
        ps.append(jnp.dot(fc, uj16, preferred_element_type=F32).astype(BF16))
        qs.append(jnp.dot(fs, uj16, preferred_element_type=F32).astype(BF16))
        ns.append(jnp.sum(uj * alt, axis=0, keepdims=True))
    for i in range(nblk):
        r = t = nyq = None
        for j in range(nblk):
            lag = i - j + nblk - 1
            ka, kb = ka_ref[0, lag], kb_ref[0, lag]
            dr = ps[j] * ka + qs[j] * kb
            dt = qs[j] * ka - ps[j] * kb
            dn = ns[j] * kn_ref[0, lag]
            r, t, nyq = (dr, dt, dn) if j == 0 else (r + dr, t + dt, nyq + dn)
        y = jnp.dot(fc, r, preferred_element_type=F32) + jnp.dot(fs, t, preferred_element_type=F32)
        rows = slice(i * b, (i + 1) * b)
        y = y + alt * nyq + u[rows] * d_ref[0]
        o_ref[0, rows, :] = (gate[rows] * y).astype(o_ref.dtype)


def _fftconv_call(u, u_col0, z, gate_col0, conv_w, conv_b, spectra, d_skip, order, fc, fs, nblk, tc, out_dtype):
    b, l, _ = z.shape
    conv_u = u is z
    nct = HYENA_W // tc
    ka, kb, kn = spectra
    blk = l // nblk
    nlag = 2 * nblk - 1
    col = lambda c0: (lambda c, bi: (bi, 0, c0 // tc + c))
    wcol = lambda c0: (lambda c, bi: (0, (c0 - HY_OFF) // tc + c))
    in_specs = [pl.BlockSpec((1, l, tc), col(u_col0))]
    args = [u]
    if conv_u:
        in_specs += [pl.BlockSpec((3, tc), wcol(u_col0)), pl.BlockSpec((1, tc), wcol(u_col0))]
        args += [conv_w, conv_b]
    in_specs += [pl.BlockSpec((1, l, tc), col(gate_col0)),
                 pl.BlockSpec((3, tc), wcol(gate_col0)), pl.BlockSpec((1, tc), wcol(gate_col0))]
    args += [z, conv_w, conv_b]
    spec = lambda rows: pl.BlockSpec((1, nlag, rows, tc), lambda c, bi: (order, 0, 0, c),
                                     pipeline_mode=pl.Buffered(1))
    in_specs += [spec(blk), spec(blk), spec(1),
                 pl.BlockSpec((1, 1, tc), lambda c, bi: (order, 0, c), pipeline_mode=pl.Buffered(1)),
                 _const_spec((blk, blk)), _const_spec((blk, blk))]
    args += [ka, kb, kn, d_skip, fc, fs]
    return pl.pallas_call(
        functools.partial(_fftconv_body, conv_u=conv_u, nblk=nblk),
        grid=(nct, b),
        in_specs=in_specs,
        out_specs=pl.BlockSpec((1, l, tc), lambda c, bi: (bi, 0, c)),
        out_shape=jax.ShapeDtypeStruct((b, l, HYENA_W), out_dtype),
        compiler_params=_params("arbitrary", "arbitrary"),
        name="hyena_conv",
    )(*args)


def _pool_body(x_ref, w_ref, s_ref, o_ref):
    l = x_ref.shape[1]
    row = lax.broadcasted_iota(jnp.int32, (l, 1), 0)
    for g, win in enumerate(POOL_WINDOWS):
        half = win // 2
        sl = slice(g * POOL_GROUP, (g + 1) * POOL_GROUP)
        x = x_ref[0, :, sl].astype(F32)

        def shifted(a, k):
            return jnp.where((row >= k) & (row < l + k), pltpu.roll(a, k % l, 0), 0.0)

        back = fwd = x
        span = 1
        while span < half:
            back = back + shifted(back, span)
            fwd = fwd + shifted(fwd, -span)
            span *= 2
        acc = shifted(back, 1) + fwd
        cnt = (jnp.minimum(row + half, l) - jnp.maximum(row - half, 0)).astype(F32)
        d = acc / cnt - x
        y = jnp.dot(d.astype(BF16), w_ref[g], preferred_element_type=F32)
        o_ref[0, :, sl] = (y * s_ref[:, sl]).astype(o_ref.dtype)


def _pool_call(z, w_grp, scale):
    b, l, _ = z.shape
    ng = len(POOL_WINDOWS)
    return pl.pallas_call(
        _pool_body,
        grid=(b,),
        in_specs=[
            pl.BlockSpec((1, l, POOL_W), lambda bi: (bi, 0, POOL_OFF // POOL_W)),
            pl.BlockSpec((ng, POOL_GROUP, POOL_GROUP), lambda bi: (0, 0, 0)),
            pl.BlockSpec((1, POOL_W), lambda bi: (0, 0)),
        ],
        out_specs=pl.BlockSpec((1, l, POOL_W), lambda bi: (bi, 0, 0)),
        out_shape=jax.ShapeDtypeStruct((b, l, POOL_W), BF16),
        compiler_params=_params("arbitrary"),
        name="pool",
    )(z, w_grp, scale.reshape(1, POOL_W))


def _merge_body(ya_ref, yh_ref, yp_ref, gt_ref, x_ref, ga_ref, g2_ref, sc_ref, sh_ref,
                wa_ref, wh_ref, wp_ref, wo_ref, xn_ref, h2_ref):
    d = x_ref.shape[1]
    cj = 512
    ya, yh, yp = ya_ref[...], yh_ref[...], yp_ref[...]
    acc = jnp.zeros(x_ref.shape, F32)
    for j in range(d // cj):
        sl = slice(j * cj, (j + 1) * cj)
        gate = lambda br: gt_ref[:, br * d + j * cj:br * d + (j + 1) * cj].astype(F32)
        m = (gate(0) * jnp.dot(ya, wa_ref[:, sl], preferred_element_type=F32)
             + gate(1) * jnp.dot(yh, wh_ref[:, sl], preferred_element_type=F32)
             + gate(2) * jnp.dot(yp, wp_ref[:, sl], preferred_element_type=F32))
        acc = acc + jnp.dot(m.astype(BF16), wo_ref[sl, :], preferred_element_type=F32)
    xn = x_ref[...] + ga_ref[0] * acc
    xn_ref[...] = xn
    h2_ref[...] = _norm_mod(xn, g2_ref[...], sc_ref[0], sh_ref[0]).astype(h2_ref.dtype)


def _merge_call(ya, yh, yp, gates, x2, ga1, g2, sc2, sh2, wa, wh, wp, wo, layer, rows_per_batch):
    m, d = x2.shape
    tm = min(512, rows_per_batch)
    rpt = rows_per_batch // tm
    rows = lambda w: pl.BlockSpec((tm, w), lambda i: (i, 0))
    weight = lambda w: pl.BlockSpec((None,) + w.shape[1:], lambda i: (layer, 0, 0), pipeline_mode=pl.Buffered(1))
    return pl.pallas_call(
        _merge_body,
        grid=(m // tm,),
        in_specs=[rows(ATTN_W), rows(HYENA_W), rows(POOL_W), rows(N_BRANCH * d), rows(d),
                  _mod_spec(ga1, rpt), pl.BlockSpec((1, d), lambda i: (0, 0)),
                  _mod_spec(sc2, rpt), _mod_spec(sh2, rpt),
                  weight(wa), weight(wh), weight(wp), weight(wo)],
        out_specs=[rows(d), rows(d)],
        out_shape=[jax.ShapeDtypeStruct((m, d), F32), jax.ShapeDtypeStruct((m, d), BF16)],
        compiler_params=_params("arbitrary"),
        name="merge",
    )(ya, yh, yp, gates, x2, ga1, g2.reshape(1, d), sc2, sh2, wa, wh, wp, wo)


def _mlp_body(*refs, has_next):
    it = iter(refs)
    h_ref, w1_ref, w2_ref, x_ref, ga_ref = (next(it) for _ in range(5))
    if has_next:
        gn_ref, sc_ref, sh_ref = next(it), next(it), next(it)
    o_ref = next(it)
    hn_ref = next(it) if has_next else None
    xs_ref = next(it)

    f = pl.program_id(1)
    last = pl.num_programs(1) - 1
    tm, d = o_ref.shape
    cn = 512
    rb = min(256, tm)

    xs_ref[f] = x_ref[...]

    def hidden(rows):
        a = jnp.dot(h_ref[rows, :], w1_ref[...], preferred_element_type=F32)
        return jnp.square(jnp.maximum(a, 0.0)).astype(BF16)

    @pl.when(f == 0)
    def _():
        a = hidden(slice(None))
        for n0 in range(0, d, cn):
            o_ref[:, n0:n0 + cn] = jnp.dot(a, w2_ref[:, n0:n0 + cn], preferred_element_type=F32)

    @pl.when((f > 0) & (f < last))
    def _():
        a = hidden(slice(None))
        for n0 in range(0, d, cn):
            o_ref[:, n0:n0 + cn] += jnp.dot(a, w2_ref[:, n0:n0 + cn], preferred_element_type=F32)

    @pl.when(f == last)
    def _():
        for r0 in range(0, tm, rb):
            rows = slice(r0, r0 + rb)
            acc = o_ref[rows, :] + jnp.dot(hidden(rows), w2_ref[...], preferred_element_type=F32)
            x_rows = jnp.concatenate([xs_ref[c, rows, :] for c in range(xs_ref.shape[0])], axis=1)
            xo = x_rows + ga_ref[0] * acc
            o_ref[rows, :] = xo
            if has_next:
                hn_ref[rows, :] = _norm_mod(xo, gn_ref[...], sc_ref[0], sh_ref[0]).astype(hn_ref.dtype)


def _mlp_call(h2, w1, w2, layer, xn, ga2, nxt, rows_per_batch):
    m, d = xn.shape
    ff = w1.shape[2]
    tm, tf = min(1024, rows_per_batch if ga2.shape[0] > 1 else m), 512
    rpt = max(rows_per_batch // tm, 1)
    nf = ff // tf
    assert nf >= 2
    xw = d // nf
    assert xw % V7X_LANES == 0
    has_next = nxt is not None
    rows = pl.BlockSpec((tm, d), lambda i, f: (i, 0))
    in_specs = [rows, pl.BlockSpec((None, d, tf), lambda i, f: (layer, 0, f)),
                pl.BlockSpec((None, tf, d), lambda i, f: (layer, f, 0)),
                pl.BlockSpec((tm, xw), lambda i, f: (i, f)), _mod_spec(ga2, rpt)]
    args = [h2, w1, w2, xn, ga2]
    out_specs = [rows]
    out_shape = [jax.ShapeDtypeStruct((m, d), F32)]
    if has_next:
        gn, scn, shn = nxt
        in_specs += [pl.BlockSpec((1, d), lambda i, f: (0, 0)), _mod_spec(scn, rpt), _mod_spec(shn, rpt)]
        args += [gn.reshape(1, d), scn, shn]
        out_specs.append(rows)
        out_shape.append(jax.ShapeDtypeStruct((m, d), BF16))
    outs = pl.pallas_call(
        functools.partial(_mlp_body, has_next=has_next),
        grid=(m // tm, nf),
        in_specs=in_specs,
        out_specs=out_specs,
        out_shape=out_shape,
        scratch_shapes=[pltpu.VMEM((nf, tm, xw), F32)],
        compiler_params=_params("arbitrary", "arbitrary"),
        name="mlp",
    )(*args)
    return (outs[0], outs[1]) if has_next else (outs[0], None)


def _mixers(z, q, kv, kvx, sink, local, hy, pool_w, pool_scale):
    b, l, _ = z.shape
    y_att = _attn_call(q, kv, kvx, sink, local)
    conv_w, conv_b, filt_params, d_skip, (fc, fs) = hy
    nblk = _hyena_blocks(l)
    tc = 256
    spectra = _spectrum_call(_filter_call(l, *filt_params), fc, fs, nblk, tc)
    conv = functools.partial(_fftconv_call, conv_w=conv_w, conv_b=conv_b, spectra=spectra, d_skip=d_skip,
                             fc=fc, fs=fs, nblk=nblk, tc=tc)
    z1 = conv(z, HY_OFF, z, HY_OFF + HYENA_W, order=0, out_dtype=F32)
    y_hy = conv(z1, 0, z, HY_OFF + 2 * HYENA_W, order=1, out_dtype=BF16)
    y_pool = _pool_call(z, pool_w, pool_scale)
    return (y_att.reshape(b * l, ATTN_W), y_hy.reshape(b * l, HYENA_W), y_pool.reshape(b * l, POOL_W))


def kernel(x, c, ctx, c_ctx, norm1_g, norm2_g, w_mod, b_mod, w_in, q_norm_g, k_norm_g, sink, hy_conv_w, hy_conv_b, filt_w0, filt_b0, filt_w1, filt_b1, filt_freq, filt_w2, hy_bias, pool_w, pool_scale, w_att_o, w_hy_o, w_pool_o, w_out, mlp_w1, mlp_w2):
    b, l, d = x.shape
    lc = ctx.shape[1]
    depth = w_mod.shape[0]

    cc = jnp.concatenate([c, c_ctx[None, :], jnp.zeros((MOD_ROWS - b - 1, d), F32)], axis=0)
    mods = _modulation(cc, w_mod, b_mod)

    def chunks(layer, lo, hi):
        return [mods[layer, lo:hi, i * d:(i + 1) * d].reshape(hi - lo, 1, d) for i in range(6)]

    as_bf16 = lambda w: w.astype(BF16)
    w_in_b, w_att_b, w_hy_b, w_pool_b, w_out_b = map(as_bf16, (w_in, w_att_o, w_hy_o, w_pool_o, w_out))
    w1_b, w2_b, pool_w_b = map(as_bf16, (mlp_w1, mlp_w2, pool_w))

    rope_tabs = _rope_tables(l)
    dft_x = _dft_matrices(l // _hyena_blocks(l))
    dft_c = _dft_matrices(lc // _hyena_blocks(lc))

    x2 = x.reshape(b * l, d)
    c2 = ctx.reshape(b * lc, d)
    sh1, sc1 = chunks(0, 0, b)[:2]
    csh1, csc1 = chunks(0, b, b + 1)[:2]
    hx = _norm_call(x2, norm1_g[0], sc1, sh1, l)
    hc = _norm_call(c2, norm1_g[0], csc1, csh1, lc)

    for layer in range(depth):
        last = layer == depth - 1
        _, _, ga1, sh2, sc2, ga2 = chunks(layer, 0, b)
        _, _, cga1, csh2, csc2, cga2 = chunks(layer, b, b + 1)
        filt_params = (filt_w0[layer], filt_b0[layer], filt_w1[layer], filt_b1[layer], filt_freq[layer],
                       filt_w2[layer])
        conv_b = hy_conv_b[layer].reshape(1, -1)
        d_skip = hy_bias[layer].reshape(HYENA_ORDER, 1, HYENA_W)
        merge_w = (w_att_b, w_hy_b, w_pool_b, w_out_b, layer)

        gq, gk = q_norm_g[layer], k_norm_g[layer]
        if last:
            kvc = _kv_call(_proj_call(hc, w_in_b, layer, K_OFF, 2 * KV_W, BF16), gk)
        else:
            zc = _proj_call(hc, w_in_b, layer, 0, GATE_OFF, BF16)
            gc, qc, kvc = _gates_qkv_call(hc, w_in_b, layer, zc, gq, gk, None, lc)
            zc, qc = zc.reshape(b, lc, GATE_OFF), qc.reshape(b, lc, -1)
        kvc = kvc.reshape(b, lc, -1)

        zx = _proj_call(hx, w_in_b, layer, 0, GATE_OFF, BF16)
        gx, qx, kvx = _gates_qkv_call(hx, w_in_b, layer, zx, gq, gk, rope_tabs, l)
        zx, qx, kvx = zx.reshape(b, l, GATE_OFF), qx.reshape(b, l, -1), kvx.reshape(b, l, -1)
        hy = (hy_conv_w[layer], conv_b, filt_params, d_skip, dft_x)
        ya, yh, yp = _mixers(zx, qx, kvx, kvc, sink[layer], True, hy, pool_w_b[layer], pool_scale[layer])
        xn, h2 = _merge_call(ya, yh, yp, gx, x2, ga1, norm2_g[layer], sc2, sh2, *merge_w, l)
        nxt = None if last else (norm1_g[layer + 1], *reversed(chunks(layer + 1, 0, b)[:2]))
        x2, hx = _mlp_call(h2, w1_b, w2_b, layer, xn, ga2, nxt, l)

        if not last:
            hyc = (hy_conv_w[layer], conv_b, filt_params, d_skip, dft_c)
            ya, yh, yp = _mixers(zc, qc, None, kvc, sink[layer], False, hyc, pool_w_b[layer], pool_scale[layer])
            cn, h2c = _merge_call(ya, yh, yp, gc, c2, cga1, norm2_g[layer], csc2, csh2, *merge_w, lc)
            nxt = (norm1_g[layer + 1], *reversed(chunks(layer + 1, b, b + 1)[:2]))
            c2, hc = _mlp_call(h2c, w1_b, w2_b, layer, cn, cga2, nxt, lc)

    return x2.reshape(b, l, d)
```

```python
import functools
import math

import jax
import jax.numpy as jnp
from jax import lax
from jax.experimental import pallas as pl
from jax.experimental.pallas import tpu as pltpu

D_MODEL = 2048
DEPTH = 2
GRID_W = 64
EPS = 1e-6
NEG_INF = -1e30

N_HEADS = 16
N_KV_HEADS = 4
GQA_GROUP = N_HEADS // N_KV_HEADS
HEAD_DIM = 64
ATTN_W = N_HEADS * HEAD_DIM
KV_W = N_KV_HEADS * HEAD_DIM
WINDOW = 128
ROPE_FREQS = HEAD_DIM // 4
ROPE_BASE = 10000.0

HYENA_W = D_MODEL // 4
HYENA_ORDER = 2
FILTER_BANDS = 16
FILTER_EMB = 1 + 2 * FILTER_BANDS
FILTER_HIDDEN = 64
FILTER_INNER = 2
DECAY_TARGET = 1e-2
FAST_DECAY_PCT = 0.3
SLOW_DECAY_PCT = 1.5

POOL_W = D_MODEL // 4
POOL_WINDOWS = (2, 4, 8, 16)
POOL_GROUP = POOL_W // len(POOL_WINDOWS)

N_BRANCH = 3
D_FF = 4 * D_MODEL

Q_OFF = 0
K_OFF = Q_OFF + ATTN_W
V_OFF = K_OFF + KV_W
HY_OFF = V_OFF + KV_W
POOL_OFF = HY_OFF + 3 * HYENA_W
GATE_OFF = POOL_OFF + POOL_W
IN_W = GATE_OFF + N_BRANCH * D_MODEL

V7X_LANES = 128
V7X_VMEM_LIMIT = 62 * 1024 * 1024
KV_DUP_W = N_KV_HEADS * V7X_LANES
MOD_ROWS = 24

F32 = jnp.float32
BF16 = jnp.bfloat16
HIGHEST = lax.Precision.HIGHEST


def _params(*semantics):
    return pltpu.CompilerParams(dimension_semantics=semantics, vmem_limit_bytes=V7X_VMEM_LIMIT)


def _const_spec(shape):
    zeros = (0,) * len(shape)
    return pl.BlockSpec(shape, lambda *_: zeros, pipeline_mode=pl.Buffered(1))


def _mod_spec(arr, rows_per_mod_tile):
    d = arr.shape[-1]
    if arr.shape[0] == 1:
        return pl.BlockSpec((1, 1, d), lambda i, *_: (0, 0, 0))
    return pl.BlockSpec((1, 1, d), lambda i, *_: (i // rows_per_mod_tile, 0, 0))


def _norm_mod(xf, g, sc, sh):
    y = xf * lax.rsqrt(jnp.mean(xf * xf, axis=-1, keepdims=True) + EPS)
    return (y * g) * (1.0 + sc) + sh


def _mod_body(c_ref, w_ref, b_ref, o_ref):
    c = c_ref[...]
    s = c * jax.nn.sigmoid(c)
    o_ref[0] = jnp.dot(s.astype(BF16), w_ref[0].astype(BF16), preferred_element_type=F32) + b_ref[0]


def _modulation(cc, w_mod, b_mod):
    depth, d, n = w_mod.shape
    tn = 1024
    return pl.pallas_call(
        _mod_body,
        grid=(depth, n // tn),
        in_specs=[
            pl.BlockSpec((MOD_ROWS, d), lambda l, j: (0, 0)),
            pl.BlockSpec((1, d, tn), lambda l, j: (l, 0, j)),
            pl.BlockSpec((1, 1, tn), lambda l, j: (l, 0, j)),
        ],
        out_specs=pl.BlockSpec((1, MOD_ROWS, tn), lambda l, j: (l, 0, j)),
        out_shape=jax.ShapeDtypeStruct((depth, MOD_ROWS, n), F32),
        compiler_params=_params("arbitrary", "arbitrary"),
        name="modulation",
    )(cc, w_mod, b_mod.reshape(depth, 1, n))


def _norm_body(x_ref, g_ref, sc_ref, sh_ref, o_ref):
    o_ref[...] = _norm_mod(x_ref[...], g_ref[...], sc_ref[0], sh_ref[0]).astype(o_ref.dtype)


def _norm_call(x2, g, sc, sh, rows_per_batch):
    m, d = x2.shape
    tm = min(1024, rows_per_batch if sc.shape[0] > 1 else m)
    return pl.pallas_call(
        _norm_body,
        grid=(m // tm,),
        in_specs=[
            pl.BlockSpec((tm, d), lambda i: (i, 0)),
            pl.BlockSpec((1, d), lambda i: (0, 0)),
            _mod_spec(sc, rows_per_batch // tm),
            _mod_spec(sh, rows_per_batch // tm),
        ],
        out_specs=pl.BlockSpec((tm, d), lambda i: (i, 0)),
        out_shape=jax.ShapeDtypeStruct((m, d), BF16),
        compiler_params=_params("arbitrary"),
        name="norm_mod",
    )(x2, g.reshape(1, d), sc, sh)


def _proj_body(a_ref, w_ref, o_ref):
    o_ref[...] = jnp.dot(a_ref[...], w_ref[...], preferred_element_type=F32).astype(o_ref.dtype)


def _proj_call(a, w, layer, col0, n, out_dtype):
    m, k = a.shape
    tm = min(4096, m)
    tn = 512
    c0 = col0 // tn
    return pl.pallas_call(
        _proj_body,
        grid=(m // tm, n // tn),
        in_specs=[
            pl.BlockSpec((tm, k), lambda i, j: (i, 0)),
            pl.BlockSpec((None, k, tn), lambda i, j: (layer, 0, c0 + j)),
        ],
        out_specs=pl.BlockSpec((tm, tn), lambda i, j: (i, j)),
        out_shape=jax.ShapeDtypeStruct((m, n), out_dtype),
        compiler_params=_params("arbitrary", "arbitrary"),
        name="in_proj",
    )(a, w)


Q_SLABS = ATTN_W // V7X_LANES
QK_SLABS = Q_SLABS + KV_W // V7X_LANES
QKV_SLABS = QK_SLABS + KV_W // V7X_LANES


def _gates_qkv_body(a_ref, w_ref, z_ref, gain_ref, cos_ref, sup_ref, sdn_ref, g_ref, q_ref, kv_ref):
    j = pl.program_id(1)
    rb = min(1024, a_ref.shape[0])
    for r0 in range(0, a_ref.shape[0], rb):
        zg = jnp.dot(a_ref[r0:r0 + rb, :], w_ref[...], preferred_element_type=F32)
        g_ref[r0:r0 + rb, :] = (0.5 * jnp.tanh(0.5 * zg) + 0.5).astype(g_ref.dtype)

    low = lax.broadcasted_iota(jnp.int32, (1, V7X_LANES), 1) < HEAD_DIM
    tr = cos_ref.shape[0]
    for r0 in range(0, z_ref.shape[0], tr):
        rows = slice(r0, r0 + tr)
        x = z_ref[rows, :].astype(F32)
        x2 = x * x
        ss = jnp.where(low, jnp.sum(jnp.where(low, x2, 0.0), axis=-1, keepdims=True),
                       jnp.sum(jnp.where(low, 0.0, x2), axis=-1, keepdims=True))
        inv = jnp.where(j < QK_SLABS, lax.rsqrt(ss * (1.0 / HEAD_DIM) + EPS), 1.0)
        y = _rope((x * inv) * gain_ref[0], cos_ref[...], sup_ref[...], sdn_ref[...])
        da, db = _dup_pair(y, low)
        q_ref[rows, :] = y.astype(q_ref.dtype)
        kv_ref[rows, 0:V7X_LANES] = da.astype(kv_ref.dtype)
        kv_ref[rows, V7X_LANES:2 * V7X_LANES] = db.astype(kv_ref.dtype)


def _gates_qkv_call(a, w, layer, z2, gq, gk, rope_tabs, seq_len):
    m, k = a.shape
    d = D_MODEL
    tm = min(4096, m)
    tn = 512
    tr = min(2048, tm)
    assert N_BRANCH * d // tn == QKV_SLABS and tr % seq_len == 0 and tm % tr == 0
    c0 = GATE_OFF // tn
    ones = jnp.ones((tr, V7X_LANES), F32)
    zeros = jnp.zeros((tr, V7X_LANES), F32)
    if rope_tabs is None:
        cos, sup, sdn = ones[None], zeros[None], zeros[None]
        tab_map = lambda i, j: (0, 0, 0)
    else:
        rep = lambda t: jnp.tile(t, (tr // seq_len, 1))
        cos, sup, sdn = (jnp.stack([rep(t), ident]) for t, ident in zip(rope_tabs, (ones, zeros, zeros)))
        tab_map = lambda i, j: ((j >= QK_SLABS).astype(jnp.int32), 0, 0)
    gains = jnp.stack([jnp.tile(gq, 2) * HEAD_DIM ** -0.5, jnp.tile(gk, 2), jnp.ones((V7X_LANES,), F32)])
    gain_map = lambda i, j: ((j >= Q_SLABS).astype(jnp.int32) + (j >= QK_SLABS).astype(jnp.int32), 0, 0)
    tab_spec = pl.BlockSpec((None, tr, V7X_LANES), tab_map)
    return pl.pallas_call(
        _gates_qkv_body,
        grid=(m // tm, QKV_SLABS),
        in_specs=[
            pl.BlockSpec((tm, k), lambda i, j: (i, 0)),
            pl.BlockSpec((None, k, tn), lambda i, j: (layer, 0, c0 + j)),
            pl.BlockSpec((tm, V7X_LANES), lambda i, j: (i, j)),
            pl.BlockSpec((None, 1, V7X_LANES), gain_map),
            tab_spec, tab_spec, tab_spec,
        ],
        out_specs=[
            pl.BlockSpec((tm, tn), lambda i, j: (i, j)),
            pl.BlockSpec((tm, V7X_LANES), lambda i, j: (i, jnp.minimum(j, Q_SLABS))),
            pl.BlockSpec((tm, 2 * V7X_LANES),
                         lambda i, j: (i, jnp.where(j >= Q_SLABS, j - Q_SLABS, QKV_SLABS - Q_SLABS))),
        ],
        out_shape=[
            jax.ShapeDtypeStruct((m, N_BRANCH * d), BF16),
            jax.ShapeDtypeStruct((m, ATTN_W + V7X_LANES), BF16),
            jax.ShapeDtypeStruct((m, 2 * KV_DUP_W + 2 * V7X_LANES), BF16),
        ],
        compiler_params=_params("arbitrary", "arbitrary"),
        name="gates_qkv",
    )(a, w, z2, gains.reshape(3, 1, V7X_LANES), cos, sup, sdn)


def _pair_block_diag():
    r = lax.broadcasted_iota(jnp.int32, (V7X_LANES, V7X_LANES), 0) // HEAD_DIM
    c = lax.broadcasted_iota(jnp.int32, (V7X_LANES, V7X_LANES), 1) // HEAD_DIM
    return (r == c).astype(F32)


def _head_norm(x, g, bd):
    ss = jnp.dot(x * x, bd, precision=HIGHEST, preferred_element_type=F32)
    return (x * lax.rsqrt(ss * (1.0 / HEAD_DIM) + EPS)) * g


def _rope(x, cos, sin_up, sin_dn):
    up = pltpu.roll(x, V7X_LANES - ROPE_FREQS, 1)
    dn = pltpu.roll(x, ROPE_FREQS, 1)
    return x * cos + up * sin_up + dn * sin_dn


def _dup_pair(x, low):
    r = pltpu.roll(x, HEAD_DIM, 1)
    return jnp.where(low, x, r), jnp.where(low, r, x)


def _kv_body(z_ref, gk_ref, kv_ref):
    bd = _pair_block_diag()
    low = lax.broadcasted_iota(jnp.int32, (1, V7X_LANES), 1) < HEAD_DIM
    for s in range(2 * KV_W // V7X_LANES):
        x = z_ref[:, s * V7X_LANES:(s + 1) * V7X_LANES].astype(F32)
        if s < KV_W // V7X_LANES:
            x = _head_norm(x, gk_ref[...], bd)
        a, b = _dup_pair(x, low)
        base = 2 * s * V7X_LANES
        kv_ref[:, base:base + V7X_LANES] = a.astype(kv_ref.dtype)
        kv_ref[:, base + V7X_LANES:base + 2 * V7X_LANES] = b.astype(kv_ref.dtype)


def _kv_call(z2, gk):
    m, nz = z2.shape
    tm = min(512, m)
    return pl.pallas_call(
        _kv_body,
        grid=(m // tm,),
        in_specs=[pl.BlockSpec((tm, nz), lambda i: (i, 0)),
                  pl.BlockSpec((1, V7X_LANES), lambda i: (0, 0))],
        out_specs=pl.BlockSpec((tm, 2 * KV_DUP_W), lambda i: (i, 0)),
        out_shape=jax.ShapeDtypeStruct((m, 2 * KV_DUP_W), BF16),
        compiler_params=_params("arbitrary"),
        name="kv_prep",
    )(z2, jnp.tile(gk, 2).reshape(1, V7X_LANES))


def _rope_tables(l):
    rows = l // GRID_W
    row = jnp.repeat(jnp.arange(rows, dtype=F32), GRID_W)
    col = jnp.tile(jnp.arange(GRID_W, dtype=F32), rows)
    inv = ROPE_BASE ** (-jnp.arange(ROPE_FREQS, dtype=F32) / ROPE_FREQS)
    ang = jnp.stack([row[:, None] * inv, col[:, None] * inv], axis=1)
    cos, sin = jnp.cos(ang), jnp.sin(ang)
    zero = jnp.zeros_like(sin)
    cos_h = jnp.stack([cos, cos], axis=2).reshape(l, HEAD_DIM)
    sup_h = jnp.stack([-sin, zero], axis=2).reshape(l, HEAD_DIM)
    sdn_h = jnp.stack([zero, sin], axis=2).reshape(l, HEAD_DIM)
    return tuple(jnp.tile(t, (1, 2)) for t in (cos_h, sup_h, sdn_h))


def _attn_body(*refs, local, tq):
    it = iter(refs)
    sink_ref = next(it)
    q_ref = next(it)
    if local:
        kp_ref, kc_ref, kn_ref, vp_ref, vc_ref, vn_ref = (next(it) for _ in range(6))
    kx_ref, vx_ref = next(it), next(it)
    o_ref = next(it)

    i = pl.program_id(1)
    nb = pl.num_programs(1)
    low = lax.broadcasted_iota(jnp.int32, (1, V7X_LANES), 1) < HEAD_DIM
    rows = GQA_GROUP * tq
    if local:
        qi = lax.broadcasted_iota(jnp.int32, (rows, tq), 0) % tq
        kj = lax.broadcasted_iota(jnp.int32, (rows, tq), 1)
        mask_prev = (kj >= qi) & (i > 0)
        mask_next = (kj <= qi) & (i < nb - 1)
    row_head = lax.broadcasted_iota(jnp.int32, (rows, 1), 0) // tq
    zero = jnp.zeros((), q_ref.dtype)

    for e, h in [(e, h) for e in range(q_ref.shape[0]) for h in range(N_KV_HEADS)]:
        hs = slice(h * V7X_LANES, (h + 1) * V7X_LANES)
        qa = q_ref[e, :, 2 * h * V7X_LANES:(2 * h + 1) * V7X_LANES]
        qb = q_ref[e, :, (2 * h + 1) * V7X_LANES:(2 * h + 2) * V7X_LANES]
        qs = jnp.concatenate([jnp.where(low, qa, zero), jnp.where(low, zero, qa),
                              jnp.where(low, qb, zero), jnp.where(low, zero, qb)], axis=0)
        kparts, vparts, masks = [kx_ref[e, :, hs]], [vx_ref[e, :, hs]], {}
        if local:
            kparts = [kp_ref[e, :, hs], kc_ref[e, :, hs], kn_ref[e, :, hs]] + kparts
            vparts = [vp_ref[e, :, hs], vc_ref[e, :, hs], vn_ref[e, :, hs]] + vparts
            masks = {0: mask_prev, 2: mask_next}
        k_all = jnp.concatenate(kparts, axis=0)
        v_all = jnp.concatenate(vparts, axis=0)

        sink = jnp.zeros((rows, 1), F32)
        for g in range(GQA_GROUP):
            sink = jnp.where(row_head == g, sink_ref[GQA_GROUP * h + g], sink)
        s_all = lax.dot_general(qs, k_all, (((1,), (1,)), ((), ())), preferred_element_type=F32)
        chunks = []
        for c in range(k_all.shape[0] // tq):
            s = s_all[:, c * tq:(c + 1) * tq]
            chunks.append(jnp.where(masks[c], s, NEG_INF) if c in masks else s)
        m = jnp.maximum(sink, jnp.max(functools.reduce(jnp.maximum, chunks), axis=-1, keepdims=True))
        probs = [jnp.exp(s - m) for s in chunks]
        denom = jnp.exp(sink - m) + jnp.sum(functools.reduce(jnp.add, probs), axis=-1, keepdims=True)
        p_all = jnp.concatenate([p.astype(v_all.dtype) for p in probs], axis=1)
        o = jnp.dot(p_all, v_all, preferred_element_type=F32) / denom
        oa = jnp.where(low, o[0:tq], o[tq:2 * tq])
        ob = jnp.where(low, o[2 * tq:3 * tq], o[3 * tq:4 * tq])
        o_ref[e, :, 2 * h * V7X_LANES:(2 * h + 1) * V7X_LANES] = oa.astype(o_ref.dtype)
        o_ref[e, :, (2 * h + 1) * V7X_LANES:(2 * h + 2) * V7X_LANES] = ob.astype(o_ref.dtype)


def _attn_call(q, kv, kvx, sink, local):
    b, l, _ = q.shape
    lx = kvx.shape[1]
    tq = WINDOW
    nb = l // tq
    ne = math.gcd(b, 4)
    blk = lambda w: (ne, tq, w)
    in_specs = [pl.BlockSpec(memory_space=pltpu.SMEM),
                pl.BlockSpec(blk(ATTN_W), lambda bi, i: (bi, i, 0))]
    args = [sink, q]
    if local:
        for half in (0, 1):
            for mp in (lambda bi, i, half=half: (bi, jnp.maximum(i - 1, 0), half),
                       lambda bi, i, half=half: (bi, i, half),
                       lambda bi, i, half=half: (bi, jnp.minimum(i + 1, nb - 1), half)):
                in_specs.append(pl.BlockSpec(blk(KV_DUP_W), mp))
                args.append(kv)
    for half in (0, 1):
        in_specs.append(pl.BlockSpec((ne, lx, KV_DUP_W), lambda bi, i, half=half: (bi, 0, half)))
        args.append(kvx)
    return pl.pallas_call(
        functools.partial(_attn_body, local=local, tq=tq),
        grid=(b // ne, nb),
        in_specs=in_specs,
        out_specs=pl.BlockSpec(blk(ATTN_W), lambda bi, i: (bi, i, 0)),
        out_shape=jax.ShapeDtypeStruct((b, l, ATTN_W), BF16),
        compiler_params=_params("arbitrary", "arbitrary"),
        name="attention",
    )(*args)


def _filter_body(z_ref, w0_ref, b0_ref, w1_ref, b1_ref, fr_ref, w2_ref, dec_ref, o_ref):
    fr = fr_ref[...]
    dot = functools.partial(jnp.dot, precision=HIGHEST, preferred_element_type=F32)
    h = jnp.sin(fr * (dot(z_ref[...], w0_ref[...]) + b0_ref[...]))
    for i in range(FILTER_INNER):
        h = jnp.sin(fr * (dot(h, w1_ref[i]) + b1_ref[i]))
    dec = dec_ref[...]
    for s in range(2 * HYENA_ORDER):
        sl = slice(s * HYENA_W, (s + 1) * HYENA_W)
        o_ref[:, sl] = dot(h, w2_ref[:, sl]) * dec


def _filter_features(l):
    t = jnp.linspace(0.0, 1.0, l, dtype=F32)[:, None]
    w = 2.0 * math.pi * jnp.arange(l, dtype=F32)[:, None] / l
    bands = jnp.linspace(1e-4, FILTER_BANDS - 1, FILTER_BANDS, dtype=F32)[None, :]
    z = jnp.concatenate([t, jnp.cos(bands * w), -jnp.sin(bands * w)], axis=-1)
    deltas = jnp.linspace(math.log(DECAY_TARGET) / SLOW_DECAY_PCT, math.log(DECAY_TARGET) / FAST_DECAY_PCT,
                          HYENA_W, dtype=F32)
    decay = jnp.exp(-t * jnp.abs(deltas))
    return jnp.pad(z, ((0, 0), (0, V7X_LANES - FILTER_EMB))), decay


def _filter_call(l, w0, b0, w1, b1, freq, w2):
    zfeat, decay = _filter_features(l)
    w0p = jnp.pad(w0, ((0, V7X_LANES - FILTER_EMB), (0, 0)))
    tl = min(512, l)
    nf = 2 * HYENA_ORDER * HYENA_W
    full = lambda shape: pl.BlockSpec(shape, lambda i: (0,) * len(shape))
    return pl.pallas_call(
        _filter_body,
        grid=(l // tl,),
        in_specs=[
            pl.BlockSpec((tl, V7X_LANES), lambda i: (i, 0)),
            full((V7X_LANES, FILTER_HIDDEN)),
            full((1, FILTER_HIDDEN)),
            full((FILTER_INNER, FILTER_HIDDEN, FILTER_HIDDEN)),
            full((FILTER_INNER, 1, FILTER_HIDDEN)),
            full((1, FILTER_HIDDEN)),
            full((FILTER_HIDDEN, nf)),
            pl.BlockSpec((tl, HYENA_W), lambda i: (i, 0)),
        ],
        out_specs=pl.BlockSpec((tl, nf), lambda i: (i, 0)),
        out_shape=jax.ShapeDtypeStruct((l, nf), F32),
        compiler_params=_params("arbitrary"),
        name="hyena_filter",
    )(zfeat, w0p, b0.reshape(1, -1), w1, b1.reshape(FILTER_INNER, 1, -1), freq.reshape(1, -1), w2, decay)


def _hyena_blocks(l):
    return max(1, min(4, l // V7X_LANES))


def _dft_matrices(blk):
    n = 2 * blk
    r = jnp.arange(blk, dtype=jnp.int32)
    ang = ((r[:, None] * r[None, :]) % n).astype(F32) * (2.0 * math.pi / n)
    return jnp.cos(ang).astype(BF16), jnp.sin(ang).astype(BF16)


def _alternating(l):
    row = lax.broadcasted_iota(jnp.int32, (l, 1), 0)
    return row, jnp.where(row % 2 == 0, 1.0, -1.0).astype(F32)


def _spectrum_body(hf_ref, hb_ref, fc_ref, fs_ref, ka_ref, kb_ref, kn_ref, *, nblk):
    l = hf_ref.shape[0]
    b = l // nblk
    n = 2 * b
    row = lax.broadcasted_iota(jnp.int32, (l, 1), 0)
    _, alt = _alternating(b)
    hf = hf_ref[...]
    hbs = jnp.where(row == 0, 0.0, pltpu.roll(hb_ref[...], 1, 0))
    fc, fs = fc_ref[...], fs_ref[...]

    def transforms(h):
        out = []
        for k in range(nblk):
            hk = h[k * b:(k + 1) * b]
            hk16 = hk.astype(BF16)
            out.append(dict(
                c=jnp.dot(fc, hk16, preferred_element_type=F32),
                s=jnp.dot(fs, hk16, preferred_element_type=F32),
                first16=hk16[0:1].astype(F32),
                first=hk[0:1],
                alt=jnp.sum(hk * alt, axis=0, keepdims=True)))
        return out

    tf, tb = transforms(hf), transforms(hbs)
    brow = lax.broadcasted_iota(jnp.int32, (b, 1), 0)
    w_re = jnp.where(brow == 0, 1.0 / n, 2.0 / n)
    for d in range(-(nblk - 1), nblk):
        idx = d + nblk - 1
        if d == 0:
            kre = tf[0]["c"] + tb[0]["c"]
            kim = tb[0]["s"] - tf[0]["s"]
            kn = tf[0]["alt"] + tb[0]["alt"]
        else:
            t, e, sg = (tf, d, -1.0) if d > 0 else (tb, -d, 1.0)
            kre = t[e]["c"] + alt * (t[e - 1]["c"] - t[e - 1]["first16"])
            kim = sg * (t[e]["s"] + alt * t[e - 1]["s"])
            kn = t[e]["alt"] + t[e - 1]["alt"] - t[e - 1]["first"]
        ka_ref[0, idx] = (kre * w_re).astype(ka_ref.dtype)
        kb_ref[0, idx] = (kim * (2.0 / n)).astype(kb_ref.dtype)
        kn_ref[0, idx] = kn * (1.0 / n)


def _spectrum_call(filt, fc, fs, nblk, tc):
    l = filt.shape[0]
    b = l // nblk
    nct = HYENA_W // tc
    nlag = 2 * nblk - 1
    return pl.pallas_call(
        functools.partial(_spectrum_body, nblk=nblk),
        grid=(HYENA_ORDER, nct),
        in_specs=[
            pl.BlockSpec((l, tc), lambda o, c: (0, 2 * nct * o + c)),
            pl.BlockSpec((l, tc), lambda o, c: (0, 2 * nct * o + nct + c)),
            _const_spec((b, b)),
            _const_spec((b, b)),
        ],
        out_specs=[
            pl.BlockSpec((1, nlag, b, tc), lambda o, c: (o, 0, 0, c)),
            pl.BlockSpec((1, nlag, b, tc), lambda o, c: (o, 0, 0, c)),
            pl.BlockSpec((1, nlag, 1, tc), lambda o, c: (o, 0, 0, c)),
        ],
        out_shape=[
            jax.ShapeDtypeStruct((HYENA_ORDER, nlag, b, HYENA_W), BF16),
            jax.ShapeDtypeStruct((HYENA_ORDER, nlag, b, HYENA_W), BF16),
            jax.ShapeDtypeStruct((HYENA_ORDER, nlag, 1, HYENA_W), F32),
        ],
        compiler_params=_params("arbitrary", "arbitrary"),
        name="hyena_spectrum",
    )(filt, filt, fc, fs)


def _conv3(x, w_ref, b_ref, row):
    l = x.shape[0]
    xm = jnp.where(row == 0, 0.0, pltpu.roll(x, 1, 0))
    xp = jnp.where(row == l - 1, 0.0, pltpu.roll(x, l - 1, 0))
    return xm * w_ref[0:1, :] + x * w_ref[1:2, :] + xp * w_ref[2:3, :] + b_ref[...]


def _fftconv_body(*refs, conv_u, nblk):
    it = iter(refs)
    u_ref = next(it)
    if conv_u:
        uw_ref, ub_ref = next(it), next(it)
    g_ref, gw_ref, gb_ref = next(it), next(it), next(it)
    ka_ref, kb_ref, kn_ref, d_ref, fc_ref, fs_ref, o_ref = (next(it) for _ in range(7))

    l = u_ref.shape[1]
    b = l // nblk
    row = lax.broadcasted_iota(jnp.int32, (l, 1), 0)
    _, alt = _alternating(b)
    u = u_ref[0].astype(F32)
    if conv_u:
        u = _conv3(u, uw_ref, ub_ref, row)
    gate = _conv3(g_ref[0].astype(F32), gw_ref, gb_ref, row)
    fc, fs = fc_ref[...], fs_ref[...]

    ps, qs, ns = [], [], []
    for j in range(nblk):
        uj = u[j * b:(j + 1) * b]
        uj16 = uj.astype(BF16)
        ps.append(jnp.dot(fc, uj16, preferred_element_type=F32).astype(BF16))
        qs.append(jnp.dot(fs, uj16, preferred_element_type=F32).astype(BF16))
        ns.append(jnp.sum(uj * alt, axis=0, keepdims=True))
    for i in range(nblk):
        r = t = nyq = None
        for j in range(nblk):
            lag = i - j + nblk - 1
            ka, kb = ka_ref[0, lag], kb_ref[0, lag]
            dr = ps[j] * ka + qs[j] * kb
            dt = qs[j] * ka - ps[j] * kb
            dn = ns[j] * kn_ref[0, lag]
            r, t, nyq = (dr, dt, dn) if j == 0 else (r + dr, t + dt, nyq + dn)
        y = jnp.dot(fc, r, preferred_element_type=F32) + jnp.dot(fs, t, preferred_element_type=F32)
        rows = slice(i * b, (i + 1) * b)
        y = y + alt * nyq + u[rows] * d_ref[0]
        o_ref[0, rows, :] = (gate[rows] * y).astype(o_ref.dtype)


def _fftconv_call(u, u_col0, z, gate_col0, conv_w, conv_b, spectra, d_skip, order, fc, fs, nblk, tc, out_dtype):
    b, l, _ = z.shape
    conv_u = u is z
    nct = HYENA_W // tc
    ka, kb, kn = spectra
    blk = l // nblk
    nlag = 2 * nblk - 1
    col = lambda c0: (lambda c, bi: (bi, 0, c0 // tc + c))
    wcol = lambda c0: (lambda c, bi: (0, (c0 - HY_OFF) // tc + c))
    in_specs = [pl.BlockSpec((1, l, tc), col(u_col0))]
    args = [u]
    if conv_u:
        in_specs += [pl.BlockSpec((3, tc), wcol(u_col0)), pl.BlockSpec((1, tc), wcol(u_col0))]
        args += [conv_w, conv_b]
    in_specs += [pl.BlockSpec((1, l, tc), col(gate_col0)),
                 pl.BlockSpec((3, tc), wcol(gate_col0)), pl.BlockSpec((1, tc), wcol(gate_col0))]
    args += [z, conv_w, conv_b]
    spec = lambda rows: pl.BlockSpec((1, nlag, rows, tc), lambda c, bi: (order, 0, 0, c),
                                     pipeline_mode=pl.Buffered(1))
    in_specs += [spec(blk), spec(blk), spec(1),
                 pl.BlockSpec((1, 1, tc), lambda c, bi: (order, 0, c), pipeline_mode=pl.Buffered(1)),
                 _const_spec((blk, blk)), _const_spec((blk, blk))]
    args += [ka, kb, kn, d_skip, fc, fs]
    return pl.pallas_call(
        functools.partial(_fftconv_body, conv_u=conv_u, nblk=nblk),
        grid=(nct, b),
        in_specs=in_specs,
        out_specs=pl.BlockSpec((1, l, tc), lambda c, bi: (bi, 0, c)),
        out_shape=jax.ShapeDtypeStruct((b, l, HYENA_W), out_dtype),
        compiler_params=_params("arbitrary", "arbitrary"),
        name="hyena_conv",
    )(*args)


def _pool_body(x_ref, w_ref, s_ref, o_ref):
    l = x_ref.shape[1]
    row = lax.broadcasted_iota(jnp.int32, (l, 1), 0)
    for g, win in enumerate(POOL_WINDOWS):
        half = win // 2
        sl = slice(g * POOL_GROUP, (g + 1) * POOL_GROUP)
        x = x_ref[0, :, sl].astype(F32)

        def shifted(a, k):
            return jnp.where((row >= k) & (row < l + k), pltpu.roll(a, k % l, 0), 0.0)

        back = fwd = x
        span = 1
        while span < half:
            back = back + shifted(back, span)
            fwd = fwd + shifted(fwd, -span)
            span *= 2
        acc = shifted(back, 1) + fwd
        cnt = (jnp.minimum(row + half, l) - jnp.maximum(row - half, 0)).astype(F32)
        d = acc / cnt - x
        y = jnp.dot(d.astype(BF16), w_ref[g], preferred_element_type=F32)
        o_ref[0, :, sl] = (y * s_ref[:, sl]).astype(o_ref.dtype)


def _pool_call(z, w_grp, scale):
    b, l, _ = z.shape
    ng = len(POOL_WINDOWS)
    return pl.pallas_call(
        _pool_body,
        grid=(b,),
        in_specs=[
            pl.BlockSpec((1, l, POOL_W), lambda bi: (bi, 0, POOL_OFF // POOL_W)),
            pl.BlockSpec((ng, POOL_GROUP, POOL_GROUP), lambda bi: (0, 0, 0)),
            pl.BlockSpec((1, POOL_W), lambda bi: (0, 0)),
        ],
        out_specs=pl.BlockSpec((1, l, POOL_W), lambda bi: (bi, 0, 0)),
        out_shape=jax.ShapeDtypeStruct((b, l, POOL_W), BF16),
        compiler_params=_params("arbitrary"),
        name="pool",
    )(z, w_grp, scale.reshape(1, POOL_W))


def _merge_body(ya_ref, yh_ref, yp_ref, gt_ref, x_ref, ga_ref, g2_ref, sc_ref, sh_ref,
                wa_ref, wh_ref, wp_ref, wo_ref, xn_ref, h2_ref):
    d = x_ref.shape[1]
    cj = 512
    ya, yh, yp = ya_ref[...], yh_ref[...], yp_ref[...]
    acc = jnp.zeros(x_ref.shape, F32)
    for j in range(d // cj):
        sl = slice(j * cj, (j + 1) * cj)
        gate = lambda br: gt_ref[:, br * d + j * cj:br * d + (j + 1) * cj].astype(F32)
        m = (gate(0) * jnp.dot(ya, wa_ref[:, sl], preferred_element_type=F32)
             + gate(1) * jnp.dot(yh, wh_ref[:, sl], preferred_element_type=F32)
             + gate(2) * jnp.dot(yp, wp_ref[:, sl], preferred_element_type=F32))
        acc = acc + jnp.dot(m.astype(BF16), wo_ref[sl, :], preferred_element_type=F32)
    xn = x_ref[...] + ga_ref[0] * acc
    xn_ref[...] = xn
    h2_ref[...] = _norm_mod(xn, g2_ref[...], sc_ref[0], sh_ref[0]).astype(h2_ref.dtype)


def _merge_call(ya, yh, yp, gates, x2, ga1, g2, sc2, sh2, wa, wh, wp, wo, layer, rows_per_batch):
    m, d = x2.shape
    tm = min(512, rows_per_batch)
    rpt = rows_per_batch // tm
    rows = lambda w: pl.BlockSpec((tm, w), lambda i: (i, 0))
    weight = lambda w: pl.BlockSpec((None,) + w.shape[1:], lambda i: (layer, 0, 0), pipeline_mode=pl.Buffered(1))
    return pl.pallas_call(
        _merge_body,
        grid=(m // tm,),
        in_specs=[rows(ATTN_W), rows(HYENA_W), rows(POOL_W), rows(N_BRANCH * d), rows(d),
                  _mod_spec(ga1, rpt), pl.BlockSpec((1, d), lambda i: (0, 0)),
                  _mod_spec(sc2, rpt), _mod_spec(sh2, rpt),
                  weight(wa), weight(wh), weight(wp), weight(wo)],
        out_specs=[rows(d), rows(d)],
        out_shape=[jax.ShapeDtypeStruct((m, d), F32), jax.ShapeDtypeStruct((m, d), BF16)],
        compiler_params=_params("arbitrary"),
        name="merge",
    )(ya, yh, yp, gates, x2, ga1, g2.reshape(1, d), sc2, sh2, wa, wh, wp, wo)


def _mlp_body(*refs, has_next):
    it = iter(refs)
    h_ref, w1_ref, w2_ref, x_ref, ga_ref = (next(it) for _ in range(5))
    if has_next:
        gn_ref, sc_ref, sh_ref = next(it), next(it), next(it)
    o_ref = next(it)
    hn_ref = next(it) if has_next else None
    xs_ref = next(it)

    f = pl.program_id(1)
    last = pl.num_programs(1) - 1
    tm, d = o_ref.shape
    cn = 512
    rb = min(256, tm)

    xs_ref[f] = x_ref[...]

    def hidden(rows):
        a = jnp.dot(h_ref[rows, :], w1_ref[...], preferred_element_type=F32)
        return jnp.square(jnp.maximum(a, 0.0)).astype(BF16)

    @pl.when(f == 0)
    def _():
        a = hidden(slice(None))
        for n0 in range(0, d, cn):
            o_ref[:, n0:n0 + cn] = jnp.dot(a, w2_ref[:, n0:n0 + cn], preferred_element_type=F32)

    @pl.when((f > 0) & (f < last))
    def _():
        a = hidden(slice(None))
        for n0 in range(0, d, cn):
            o_ref[:, n0:n0 + cn] += jnp.dot(a, w2_ref[:, n0:n0 + cn], preferred_element_type=F32)

    @pl.when(f == last)
    def _():
        for r0 in range(0, tm, rb):
            rows = slice(r0, r0 + rb)
            acc = o_ref[rows, :] + jnp.dot(hidden(rows), w2_ref[...], preferred_element_type=F32)
            x_rows = jnp.concatenate([xs_ref[c, rows, :] for c in range(xs_ref.shape[0])], axis=1)
            xo = x_rows + ga_ref[0] * acc
            o_ref[rows, :] = xo
            if has_next:
                hn_ref[rows, :] = _norm_mod(xo, gn_ref[...], sc_ref[0], sh_ref[0]).astype(hn_ref.dtype)


def _mlp_call(h2, w1, w2, layer, xn, ga2, nxt, rows_per_batch):
    m, d = xn.shape
    ff = w1.shape[2]
    tm, tf = min(1024, rows_per_batch if ga2.shape[0] > 1 else m), 512
    rpt = max(rows_per_batch // tm, 1)
    nf = ff // tf
    assert nf >= 2
    xw = d // nf
    assert xw % V7X_LANES == 0
    has_next = nxt is not None
    rows = pl.BlockSpec((tm, d), lambda i, f: (i, 0))
    in_specs = [rows, pl.BlockSpec((None, d, tf), lambda i, f: (layer, 0, f)),
                pl.BlockSpec((None, tf, d), lambda i, f: (layer, f, 0)),
                pl.BlockSpec((tm, xw), lambda i, f: (i, f)), _mod_spec(ga2, rpt)]
    args = [h2, w1, w2, xn, ga2]
    out_specs = [rows]
    out_shape = [jax.ShapeDtypeStruct((m, d), F32)]
    if has_next:
        gn, scn, shn = nxt
        in_specs += [pl.BlockSpec((1, d), lambda i, f: (0, 0)), _mod_spec(scn, rpt), _mod_spec(shn, rpt)]
        args += [gn.reshape(1, d), scn, shn]
        out_specs.append(rows)
        out_shape.append(jax.ShapeDtypeStruct((m, d), BF16))
    outs = pl.pallas_call(
        functools.partial(_mlp_body, has_next=has_next),
        grid=(m // tm, nf),
        in_specs=in_specs,
        out_specs=out_specs,
        out_shape=out_shape,
        scratch_shapes=[pltpu.VMEM((nf, tm, xw), F32)],
        compiler_params=_params("arbitrary", "arbitrary"),
        name="mlp",
    )(*args)
    return (outs[0], outs[1]) if has_next else (outs[0], None)


def _mixers(z, q, kv, kvx, sink, local, hy, pool_w, pool_scale):
    b, l, _ = z.shape
    y_att = _attn_call(q, kv, kvx, sink, local)
    conv_w, conv_b, filt_params, d_skip, (fc, fs) = hy
    nblk = _hyena_blocks(l)
    tc = 256
    spectra = _spectrum_call(_filter_call(l, *filt_params), fc, fs, nblk, tc)
    conv = functools.partial(_fftconv_call, conv_w=conv_w, conv_b=conv_b, spectra=spectra, d_skip=d_skip,
                             fc=fc, fs=fs, nblk=nblk, tc=tc)
    z1 = conv(z, HY_OFF, z, HY_OFF + HYENA_W, order=0, out_dtype=F32)
    y_hy = conv(z1, 0, z, HY_OFF + 2 * HYENA_W, order=1, out_dtype=BF16)
    y_pool = _pool_call(z, pool_w, pool_scale)
    return (y_att.reshape(b * l, ATTN_W), y_hy.reshape(b * l, HYENA_W), y_pool.reshape(b * l, POOL_W))


def kernel(x, c, ctx, c_ctx, norm1_g, norm2_g, w_mod, b_mod, w_in, q_norm_g, k_norm_g, sink, hy_conv_w, hy_conv_b, filt_w0, filt_b0, filt_w1, filt_b1, filt_freq, filt_w2, hy_bias, pool_w, pool_scale, w_att_o, w_hy_o, w_pool_o, w_out, mlp_w1, mlp_w2):
    b, l, d = x.shape
    lc = ctx.shape[1]
    depth = w_mod.shape[0]

    cc = jnp.concatenate([c, c_ctx[None, :], jnp.zeros((MOD_ROWS - b - 1, d), F32)], axis=0)
    mods = _modulation(cc, w_mod, b_mod)

    def chunks(layer, lo, hi):
        return [mods[layer, lo:hi, i * d:(i + 1) * d].reshape(hi - lo, 1, d) for i in range(6)]

    as_bf16 = lambda w: w.astype(BF16)
    w_in_b, w_att_b, w_hy_b, w_pool_b, w_out_b = map(as_bf16, (w_in, w_att_o, w_hy_o, w_pool_o, w_out))
    w1_b, w2_b, pool_w_b = map(as_bf16, (mlp_w1, mlp_w2, pool_w))

    rope_tabs = _rope_tables(l)
    dft_x = _dft_matrices(l // _hyena_blocks(l))
    dft_c = _dft_matrices(lc // _hyena_blocks(lc))

    x2 = x.reshape(b * l, d)
    c2 = ctx.reshape(b * lc, d)
    sh1, sc1 = chunks(0, 0, b)[:2]
    csh1, csc1 = chunks(0, b, b + 1)[:2]
    hx = _norm_call(x2, norm1_g[0], sc1, sh1, l)
    hc = _norm_call(c2, norm1_g[0], csc1, csh1, lc)

    for layer in range(depth):
        last = layer == depth - 1
        _, _, ga1, sh2, sc2, ga2 = chunks(layer, 0, b)
        _, _, cga1, csh2, csc2, cga2 = chunks(layer, b, b + 1)
        filt_params = (filt_w0[layer], filt_b0[layer], filt_w1[layer], filt_b1[layer], filt_freq[layer],
                       filt_w2[layer])
        conv_b = hy_conv_b[layer].reshape(1, -1)
        d_skip = hy_bias[layer].reshape(HYENA_ORDER, 1, HYENA_W)
        merge_w = (w_att_b, w_hy_b, w_pool_b, w_out_b, layer)

        gq, gk = q_norm_g[layer], k_norm_g[layer]
        if last:
            kvc = _kv_call(_proj_call(hc, w_in_b, layer, K_OFF, 2 * KV_W, BF16), gk)
        else:
            zc = _proj_call(hc, w_in_b, layer, 0, GATE_OFF, BF16)
            gc, qc, kvc = _gates_qkv_call(hc, w_in_b, layer, zc, gq, gk, None, lc)
            zc, qc = zc.reshape(b, lc, GATE_OFF), qc.reshape(b, lc, -1)
        kvc = kvc.reshape(b, lc, -1)

        zx = _proj_call(hx, w_in_b, layer, 0, GATE_OFF, BF16)
        gx, qx, kvx = _gates_qkv_call(hx, w_in_b, layer, zx, gq, gk, rope_tabs, l)
        zx, qx, kvx = zx.reshape(b, l, GATE_OFF), qx.reshape(b, l, -1), kvx.reshape(b, l, -1)
        hy = (hy_conv_w[layer], conv_b, filt_params, d_skip, dft_x)
        ya, yh, yp = _mixers(zx, qx, kvx, kvc, sink[layer], True, hy, pool_w_b[layer], pool_scale[layer])
        xn, h2 = _merge_call(ya, yh, yp, gx, x2, ga1, norm2_g[layer], sc2, sh2, *merge_w, l)
        nxt = None if last else (norm1_g[layer + 1], *reversed(chunks(layer + 1, 0, b)[:2]))
        x2, hx = _mlp_call(h2, w1_b, w2_b, layer, xn, ga2, nxt, l)

        if not last:
            hyc = (hy_conv_w[layer], conv_b, filt_params, d_skip, dft_c)
            ya, yh, yp = _mixers(zc, qc, None, kvc, sink[layer], False, hyc, pool_w_b[layer], pool_scale[layer])
            cn, h2c = _merge_call(ya, yh, yp, gc, c2, cga1, norm2_g[layer], csc2, csh2, *merge_w, lc)
            nxt = (norm1_g[layer + 1], *reversed(chunks(layer + 1, b, b + 1)[:2]))
            c2, hc = _mlp_call(h2c, w1_b, w2_b, layer, cn, cga2, nxt, lc)

    return x2.reshape(b, l, d)
```

```python
import functools
import math

import jax
import jax.numpy as jnp
from jax import lax
from jax.experimental import pallas as pl
from jax.experimental.pallas import tpu as pltpu

D_MODEL = 2048
DEPTH = 2
GRID_W = 64
EPS = 1e-6
NEG_INF = -1e30

N_HEADS = 16
N_KV_HEADS = 4
GQA_GROUP = N_HEADS // N_KV_HEADS
HEAD_DIM = 64
ATTN_W = N_HEADS * HEAD_DIM
KV_W = N_KV_HEADS * HEAD_DIM
WINDOW = 128
ROPE_FREQS = HEAD_DIM // 4
ROPE_BASE = 10000.0

HYENA_W = D_MODEL // 4
HYENA_ORDER = 2
FILTER_BANDS = 16
FILTER_EMB = 1 + 2 * FILTER_BANDS
FILTER_HIDDEN = 64
FILTER_INNER = 2
DECAY_TARGET = 1e-2
FAST_DECAY_PCT = 0.3
SLOW_DECAY_PCT = 1.5

POOL_W = D_MODEL // 4
POOL_WINDOWS = (2, 4, 8, 16)
POOL_GROUP = POOL_W // len(POOL_WINDOWS)

N_BRANCH = 3
D_FF = 4 * D_MODEL

Q_OFF = 0
K_OFF = Q_OFF + ATTN_W
V_OFF = K_OFF + KV_W
HY_OFF = V_OFF + KV_W
POOL_OFF = HY_OFF + 3 * HYENA_W
GATE_OFF = POOL_OFF + POOL_W
IN_W = GATE_OFF + N_BRANCH * D_MODEL

V7X_LANES = 128
V7X_VMEM_LIMIT = 62 * 1024 * 1024
KV_DUP_W = N_KV_HEADS * V7X_LANES
MOD_ROWS = 24

F32 = jnp.float32
BF16 = jnp.bfloat16
HIGHEST = lax.Precision.HIGHEST


def _params(*semantics):
    return pltpu.CompilerParams(dimension_semantics=semantics, vmem_limit_bytes=V7X_VMEM_LIMIT)


def _const_spec(shape):
    zeros = (0,) * len(shape)
    return pl.BlockSpec(shape, lambda *_: zeros, pipeline_mode=pl.Buffered(1))


def _mod_spec(arr, rows_per_mod_tile):
    d = arr.shape[-1]
    if arr.shape[0] == 1:
        return pl.BlockSpec((1, 1, d), lambda i, *_: (0, 0, 0))
    return pl.BlockSpec((1, 1, d), lambda i, *_: (i // rows_per_mod_tile, 0, 0))


def _norm_mod(xf, g, sc, sh):
    y = xf * lax.rsqrt(jnp.mean(xf * xf, axis=-1, keepdims=True) + EPS)
    return (y * g) * (1.0 + sc) + sh


def _mod_body(c_ref, w_ref, b_ref, o_ref):
    c = c_ref[...]
    s = c * jax.nn.sigmoid(c)
    o_ref[0] = jnp.dot(s.astype(BF16), w_ref[0].astype(BF16), preferred_element_type=F32) + b_ref[0]


def _modulation(cc, w_mod, b_mod):
    depth, d, n = w_mod.shape
    tn = 1024
    return pl.pallas_call(
        _mod_body,
        grid=(depth, n // tn),
        in_specs=[
            pl.BlockSpec((MOD_ROWS, d), lambda l, j: (0, 0)),
            pl.BlockSpec((1, d, tn), lambda l, j: (l, 0, j)),
            pl.BlockSpec((1, 1, tn), lambda l, j: (l, 0, j)),
        ],
        out_specs=pl.BlockSpec((1, MOD_ROWS, tn), lambda l, j: (l, 0, j)),
        out_shape=jax.ShapeDtypeStruct((depth, MOD_ROWS, n), F32),
        compiler_params=_params("arbitrary", "arbitrary"),
        name="modulation",
    )(cc, w_mod, b_mod.reshape(depth, 1, n))


def _norm_body(x_ref, g_ref, sc_ref, sh_ref, o_ref):
    o_ref[...] = _norm_mod(x_ref[...], g_ref[...], sc_ref[0], sh_ref[0]).astype(o_ref.dtype)


def _norm_call(x2, g, sc, sh, rows_per_batch):
    m, d = x2.shape
    tm = min(1024, rows_per_batch if sc.shape[0] > 1 else m)
    return pl.pallas_call(
        _norm_body,
        grid=(m // tm,),
        in_specs=[
            pl.BlockSpec((tm, d), lambda i: (i, 0)),
            pl.BlockSpec((1, d), lambda i: (0, 0)),
            _mod_spec(sc, rows_per_batch // tm),
            _mod_spec(sh, rows_per_batch // tm),
        ],
        out_specs=pl.BlockSpec((tm, d), lambda i: (i, 0)),
        out_shape=jax.ShapeDtypeStruct((m, d), BF16),
        compiler_params=_params("arbitrary"),
        name="norm_mod",
    )(x2, g.reshape(1, d), sc, sh)


def _proj_body(a_ref, w_ref, o_ref):
    o_ref[...] = jnp.dot(a_ref[...], w_ref[...], preferred_element_type=F32).astype(o_ref.dtype)


def _proj_call(a, w, layer, col0, n, out_dtype):
    m, k = a.shape
    tm = min(4096, m)
    tn = 512
    c0 = col0 // tn
    return pl.pallas_call(
        _proj_body,
        grid=(m // tm, n // tn),
        in_specs=[
            pl.BlockSpec((tm, k), lambda i, j: (i, 0)),
            pl.BlockSpec((None, k, tn), lambda i, j: (layer, 0, c0 + j)),
        ],
        out_specs=pl.BlockSpec((tm, tn), lambda i, j: (i, j)),
        out_shape=jax.ShapeDtypeStruct((m, n), out_dtype),
        compiler_params=_params("arbitrary", "arbitrary"),
        name="in_proj",
    )(a, w)


Q_SLABS = ATTN_W // V7X_LANES
QK_SLABS = Q_SLABS + KV_W // V7X_LANES
QKV_SLABS = QK_SLABS + KV_W // V7X_LANES


def _gates_qkv_body(a_ref, w_ref, z_ref, gain_ref, cos_ref, sup_ref, sdn_ref, g_ref, q_ref, kv_ref):
    j = pl.program_id(1)
    rb = min(1024, a_ref.shape[0])
    for r0 in range(0, a_ref.shape[0], rb):
        zg = jnp.dot(a_ref[r0:r0 + rb, :], w_ref[...], preferred_element_type=F32)
        g_ref[r0:r0 + rb, :] = (0.5 * jnp.tanh(0.5 * zg) + 0.5).astype(g_ref.dtype)

    low = lax.broadcasted_iota(jnp.int32, (1, V7X_LANES), 1) < HEAD_DIM
    tr = cos_ref.shape[0]
    for r0 in range(0, z_ref.shape[0], tr):
        rows = slice(r0, r0 + tr)
        x = z_ref[rows, :].astype(F32)
        x2 = x * x
        ss = jnp.where(low, jnp.sum(jnp.where(low, x2, 0.0), axis=-1, keepdims=True),
                       jnp.sum(jnp.where(low, 0.0, x2), axis=-1, keepdims=True))
        inv = jnp.where(j < QK_SLABS, lax.rsqrt(ss * (1.0 / HEAD_DIM) + EPS), 1.0)
        y = _rope((x * inv) * gain_ref[0], cos_ref[...], sup_ref[...], sdn_ref[...])
        da, db = _dup_pair(y, low)
        q_ref[rows, :] = y.astype(q_ref.dtype)
        kv_ref[rows, 0:V7X_LANES] = da.astype(kv_ref.dtype)
        kv_ref[rows, V7X_LANES:2 * V7X_LANES] = db.astype(kv_ref.dtype)


def _gates_qkv_call(a, w, layer, z2, gq, gk, rope_tabs, seq_len):
    m, k = a.shape
    d = D_MODEL
    tm = min(4096, m)
    tn = 512
    tr = min(2048, tm)
    assert N_BRANCH * d // tn == QKV_SLABS and tr % seq_len == 0 and tm % tr == 0
    c0 = GATE_OFF // tn
    ones = jnp.ones((tr, V7X_LANES), F32)
    zeros = jnp.zeros((tr, V7X_LANES), F32)
    if rope_tabs is None:
        cos, sup, sdn = ones[None], zeros[None], zeros[None]
        tab_map = lambda i, j: (0, 0, 0)
    else:
        rep = lambda t: jnp.tile(t, (tr // seq_len, 1))
        cos, sup, sdn = (jnp.stack([rep(t), ident]) for t, ident in zip(rope_tabs, (ones, zeros, zeros)))
        tab_map = lambda i, j: ((j >= QK_SLABS).astype(jnp.int32), 0, 0)
    gains = jnp.stack([jnp.tile(gq, 2) * HEAD_DIM ** -0.5, jnp.tile(gk, 2), jnp.ones((V7X_LANES,), F32)])
    gain_map = lambda i, j: ((j >= Q_SLABS).astype(jnp.int32) + (j >= QK_SLABS).astype(jnp.int32), 0, 0)
    tab_spec = pl.BlockSpec((None, tr, V7X_LANES), tab_map)
    return pl.pallas_call(
        _gates_qkv_body,
        grid=(m // tm, QKV_SLABS),
        in_specs=[
            pl.BlockSpec((tm, k), lambda i, j: (i, 0)),
            pl.BlockSpec((None, k, tn), lambda i, j: (layer, 0, c0 + j)),
            pl.BlockSpec((tm, V7X_LANES), lambda i, j: (i, j)),
            pl.BlockSpec((None, 1, V7X_LANES), gain_map),
            tab_spec, tab_spec, tab_spec,
        ],
        out_specs=[
            pl.BlockSpec((tm, tn), lambda i, j: (i, j)),
            pl.BlockSpec((tm, V7X_LANES), lambda i, j: (i, jnp.minimum(j, Q_SLABS))),
            pl.BlockSpec((tm, 2 * V7X_LANES),
                         lambda i, j: (i, jnp.where(j >= Q_SLABS, j - Q_SLABS, QKV_SLABS - Q_SLABS))),
        ],
        out_shape=[
            jax.ShapeDtypeStruct((m, N_BRANCH * d), BF16),
            jax.ShapeDtypeStruct((m, ATTN_W + V7X_LANES), BF16),
            jax.ShapeDtypeStruct((m, 2 * KV_DUP_W + 2 * V7X_LANES), BF16),
        ],
        compiler_params=_params("arbitrary", "arbitrary"),
        name="gates_qkv",
    )(a, w, z2, gains.reshape(3, 1, V7X_LANES), cos, sup, sdn)


def _pair_block_diag():
    r = lax.broadcasted_iota(jnp.int32, (V7X_LANES, V7X_LANES), 0) // HEAD_DIM
    c = lax.broadcasted_iota(jnp.int32, (V7X_LANES, V7X_LANES), 1) // HEAD_DIM
    return (r == c).astype(F32)


def _head_norm(x, g, bd):
    ss = jnp.dot(x * x, bd, precision=HIGHEST, preferred_element_type=F32)
    return (x * lax.rsqrt(ss * (1.0 / HEAD_DIM) + EPS)) * g


def _rope(x, cos, sin_up, sin_dn):
    up = pltpu.roll(x, V7X_LANES - ROPE_FREQS, 1)
    dn = pltpu.roll(x, ROPE_FREQS, 1)
    return x * cos + up * sin_up + dn * sin_dn


def _dup_pair(x, low):
    r = pltpu.roll(x, HEAD_DIM, 1)
    return jnp.where(low, x, r), jnp.where(low, r, x)


def _kv_body(z_ref, gk_ref, kv_ref):
    bd = _pair_block_diag()
    low = lax.broadcasted_iota(jnp.int32, (1, V7X_LANES), 1) < HEAD_DIM
    for s in range(2 * KV_W // V7X_LANES):
        x = z_ref[:, s * V7X_LANES:(s + 1) * V7X_LANES].astype(F32)
        if s < KV_W // V7X_LANES:
            x = _head_norm(x, gk_ref[...], bd)
        a, b = _dup_pair(x, low)
        base = 2 * s * V7X_LANES
        kv_ref[:, base:base + V7X_LANES] = a.astype(kv_ref.dtype)
        kv_ref[:, base + V7X_LANES:base + 2 * V7X_LANES] = b.astype(kv_ref.dtype)


def _kv_call(z2, gk):
    m, nz = z2.shape
    tm = min(512, m)
    return pl.pallas_call(
        _kv_body,
        grid=(m // tm,),
        in_specs=[pl.BlockSpec((tm, nz), lambda i: (i, 0)),
                  pl.BlockSpec((1, V7X_LANES), lambda i: (0, 0))],
        out_specs=pl.BlockSpec((tm, 2 * KV_DUP_W), lambda i: (i, 0)),
        out_shape=jax.ShapeDtypeStruct((m, 2 * KV_DUP_W), BF16),
        compiler_params=_params("arbitrary"),
        name="kv_prep",
    )(z2, jnp.tile(gk, 2).reshape(1, V7X_LANES))


def _rope_tables(l):
    rows = l // GRID_W
    row = jnp.repeat(jnp.arange(rows, dtype=F32), GRID_W)
    col = jnp.tile(jnp.arange(GRID_W, dtype=F32), rows)
    inv = ROPE_BASE ** (-jnp.arange(ROPE_FREQS, dtype=F32) / ROPE_FREQS)
    ang = jnp.stack([row[:, None] * inv, col[:, None] * inv], axis=1)
    cos, sin = jnp.cos(ang), jnp.sin(ang)
    zero = jnp.zeros_like(sin)
    cos_h = jnp.stack([cos, cos], axis=2).reshape(l, HEAD_DIM)
    sup_h = jnp.stack([-sin, zero], axis=2).reshape(l, HEAD_DIM)
    sdn_h = jnp.stack([zero, sin], axis=2).reshape(l, HEAD_DIM)
    return tuple(jnp.tile(t, (1, 2)) for t in (cos_h, sup_h, sdn_h))


def _attn_body(*refs, local, tq):
    it = iter(refs)
    sink_ref = next(it)
    q_ref = next(it)
    if local:
        kp_ref, kc_ref, kn_ref, vp_ref, vc_ref, vn_ref = (next(it) for _ in range(6))
    kx_ref, vx_ref = next(it), next(it)
    o_ref = next(it)

    i = pl.program_id(1)
    nb = pl.num_programs(1)
    low = lax.broadcasted_iota(jnp.int32, (1, V7X_LANES), 1) < HEAD_DIM
    rows = GQA_GROUP * tq
    if local:
        qi = lax.broadcasted_iota(jnp.int32, (rows, tq), 0) % tq
        kj = lax.broadcasted_iota(jnp.int32, (rows, tq), 1)
        mask_prev = (kj >= qi) & (i > 0)
        mask_next = (kj <= qi) & (i < nb - 1)
    row_head = lax.broadcasted_iota(jnp.int32, (rows, 1), 0) // tq
    zero = jnp.zeros((), q_ref.dtype)

    for e, h in [(e, h) for e in range(q_ref.shape[0]) for h in range(N_KV_HEADS)]:
        hs = slice(h * V7X_LANES, (h + 1) * V7X_LANES)
        qa = q_ref[e, :, 2 * h * V7X_LANES:(2 * h + 1) * V7X_LANES]
        qb = q_ref[e, :, (2 * h + 1) * V7X_LANES:(2 * h + 2) * V7X_LANES]
        qs = jnp.concatenate([jnp.where(low, qa, zero), jnp.where(low, zero, qa),
                              jnp.where(low, qb, zero), jnp.where(low, zero, qb)], axis=0)
        kparts, vparts, masks = [kx_ref[e, :, hs]], [vx_ref[e, :, hs]], {}
        if local:
            kparts = [kp_ref[e, :, hs], kc_ref[e, :, hs], kn_ref[e, :, hs]] + kparts
            vparts = [vp_ref[e, :, hs], vc_ref[e, :, hs], vn_ref[e, :, hs]] + vparts
            masks = {0: mask_prev, 2: mask_next}
        k_all = jnp.concatenate(kparts, axis=0)
        v_all = jnp.concatenate(vparts, axis=0)

        sink = jnp.zeros((rows, 1), F32)
        for g in range(GQA_GROUP):
            sink = jnp.where(row_head == g, sink_ref[GQA_GROUP * h + g], sink)
        s_all = lax.dot_general(qs, k_all, (((1,), (1,)), ((), ())), preferred_element_type=F32)
        chunks = []
        for c in range(k_all.shape[0] // tq):
            s = s_all[:, c * tq:(c + 1) * tq]
            chunks.append(jnp.where(masks[c], s, NEG_INF) if c in masks else s)
        m = jnp.maximum(sink, jnp.max(functools.reduce(jnp.maximum, chunks), axis=-1, keepdims=True))
        probs = [jnp.exp(s - m) for s in chunks]
        denom = jnp.exp(sink - m) + jnp.sum(functools.reduce(jnp.add, probs), axis=-1, keepdims=True)
        p_all = jnp.concatenate([p.astype(v_all.dtype) for p in probs], axis=1)
        o = jnp.dot(p_all, v_all, preferred_element_type=F32) / denom
        oa = jnp.where(low, o[0:tq], o[tq:2 * tq])
        ob = jnp.where(low, o[2 * tq:3 * tq], o[3 * tq:4 * tq])
        o_ref[e, :, 2 * h * V7X_LANES:(2 * h + 1) * V7X_LANES] = oa.astype(o_ref.dtype)
        o_ref[e, :, (2 * h + 1) * V7X_LANES:(2 * h + 2) * V7X_LANES] = ob.astype(o_ref.dtype)


def _attn_call(q, kv, kvx, sink, local):
    b, l, _ = q.shape
    lx = kvx.shape[1]
    tq = WINDOW
    nb = l // tq
    ne = math.gcd(b, 4)
    blk = lambda w: (ne, tq, w)
    in_specs = [pl.BlockSpec(memory_space=pltpu.SMEM),
                pl.BlockSpec(blk(ATTN_W), lambda bi, i: (bi, i, 0))]
    args = [sink, q]
    if local:
        for half in (0, 1):
            for mp in (lambda bi, i, half=half: (bi, jnp.maximum(i - 1, 0), half),
                       lambda bi, i, half=half: (bi, i, half),
                       lambda bi, i, half=half: (bi, jnp.minimum(i + 1, nb - 1), half)):
                in_specs.append(pl.BlockSpec(blk(KV_DUP_W), mp))
                args.append(kv)
    for half in (0, 1):
        in_specs.append(pl.BlockSpec((ne, lx, KV_DUP_W), lambda bi, i, half=half: (bi, 0, half)))
        args.append(kvx)
    return pl.pallas_call(
        functools.partial(_attn_body, local=local, tq=tq),
        grid=(b // ne, nb),
        in_specs=in_specs,
        out_specs=pl.BlockSpec(blk(ATTN_W), lambda bi, i: (bi, i, 0)),
        out_shape=jax.ShapeDtypeStruct((b, l, ATTN_W), BF16),
        compiler_params=_params("arbitrary", "arbitrary"),
        name="attention",
    )(*args)


def _filter_body(z_ref, w0_ref, b0_ref, w1_ref, b1_ref, fr_ref, w2_ref, dec_ref, o_ref):
    fr = fr_ref[...]
    dot = functools.partial(jnp.dot, precision=HIGHEST, preferred_element_type=F32)
    h = jnp.sin(fr * (dot(z_ref[...], w0_ref[...]) + b0_ref[...]))
    for i in range(FILTER_INNER):
        h = jnp.sin(fr * (dot(h, w1_ref[i]) + b1_ref[i]))
    dec = dec_ref[...]
    for s in range(2 * HYENA_ORDER):
        sl = slice(s * HYENA_W, (s + 1) * HYENA_W)
        o_ref[:, sl] = dot(h, w2_ref[:, sl]) * dec


def _filter_features(l):
    t = jnp.linspace(0.0, 1.0, l, dtype=F32)[:, None]
    w = 2.0 * math.pi * jnp.arange(l, dtype=F32)[:, None] / l
    bands = jnp.linspace(1e-4, FILTER_BANDS - 1, FILTER_BANDS, dtype=F32)[None, :]
    z = jnp.concatenate([t, jnp.cos(bands * w), -jnp.sin(bands * w)], axis=-1)
    deltas = jnp.linspace(math.log(DECAY_TARGET) / SLOW_DECAY_PCT, math.log(DECAY_TARGET) / FAST_DECAY_PCT,
                          HYENA_W, dtype=F32)
    decay = jnp.exp(-t * jnp.abs(deltas))
    return jnp.pad(z, ((0, 0), (0, V7X_LANES - FILTER_EMB))), decay


def _filter_call(l, w0, b0, w1, b1, freq, w2):
    zfeat, decay = _filter_features(l)
    w0p = jnp.pad(w0, ((0, V7X_LANES - FILTER_EMB), (0, 0)))
    tl = min(512, l)
    nf = 2 * HYENA_ORDER * HYENA_W
    full = lambda shape: pl.BlockSpec(shape, lambda i: (0,) * len(shape))
    return pl.pallas_call(
        _filter_body,
        grid=(l // tl,),
        in_specs=[
            pl.BlockSpec((tl, V7X_LANES), lambda i: (i, 0)),
            full((V7X_LANES, FILTER_HIDDEN)),
            full((1, FILTER_HIDDEN)),
            full((FILTER_INNER, FILTER_HIDDEN, FILTER_HIDDEN)),
            full((FILTER_INNER, 1, FILTER_HIDDEN)),
            full((1, FILTER_HIDDEN)),
            full((FILTER_HIDDEN, nf)),
            pl.BlockSpec((tl, HYENA_W), lambda i: (i, 0)),
        ],
        out_specs=pl.BlockSpec((tl, nf), lambda i: (i, 0)),
        out_shape=jax.ShapeDtypeStruct((l, nf), F32),
        compiler_params=_params("arbitrary"),
        name="hyena_filter",
    )(zfeat, w0p, b0.reshape(1, -1), w1, b1.reshape(FILTER_INNER, 1, -1), freq.reshape(1, -1), w2, decay)


def _hyena_blocks(l):
    return max(1, min(4, l // V7X_LANES))


def _dft_matrices(blk):
    n = 2 * blk
    r = jnp.arange(blk, dtype=jnp.int32)
    ang = ((r[:, None] * r[None, :]) % n).astype(F32) * (2.0 * math.pi / n)
    return jnp.cos(ang).astype(BF16), jnp.sin(ang).astype(BF16)


def _alternating(l):
    row = lax.broadcasted_iota(jnp.int32, (l, 1), 0)
    return row, jnp.where(row % 2 == 0, 1.0, -1.0).astype(F32)


def _spectrum_body(hf_ref, hb_ref, fc_ref, fs_ref, ka_ref, kb_ref, kn_ref, *, nblk):
    l = hf_ref.shape[0]
    b = l // nblk
    n = 2 * b
    row = lax.broadcasted_iota(jnp.int32, (l, 1), 0)
    _, alt = _alternating(b)
    hf = hf_ref[...]
    hbs = jnp.where(row == 0, 0.0, pltpu.roll(hb_ref[...], 1, 0))
    fc, fs = fc_ref[...], fs_ref[...]

    def transforms(h):
        out = []
        for k in range(nblk):
            hk = h[k * b:(k + 1) * b]
            hk16 = hk.astype(BF16)
            out.append(dict(
                c=jnp.dot(fc, hk16, preferred_element_type=F32),
                s=jnp.dot(fs, hk16, preferred_element_type=F32),
                first16=hk16[0:1].astype(F32),
                first=hk[0:1],
                alt=jnp.sum(hk * alt, axis=0, keepdims=True)))
        return out

    tf, tb = transforms(hf), transforms(hbs)
    brow = lax.broadcasted_iota(jnp.int32, (b, 1), 0)
    w_re = jnp.where(brow == 0, 1.0 / n, 2.0 / n)
    for d in range(-(nblk - 1), nblk):
        idx = d + nblk - 1
        if d == 0:
            kre = tf[0]["c"] + tb[0]["c"]
            kim = tb[0]["s"] - tf[0]["s"]
            kn = tf[0]["alt"] + tb[0]["alt"]
        else:
            t, e, sg = (tf, d, -1.0) if d > 0 else (tb, -d, 1.0)
            kre = t[e]["c"] + alt * (t[e - 1]["c"] - t[e - 1]["first16"])
            kim = sg * (t[e]["s"] + alt * t[e - 1]["s"])
            kn = t[e]["alt"] + t[e - 1]["alt"] - t[e - 1]["first"]
        ka_ref[0, idx] = (kre * w_re).astype(ka_ref.dtype)
        kb_ref[0, idx] = (kim * (2.0 / n)).astype(kb_ref.dtype)
        kn_ref[0, idx] = kn * (1.0 / n)


def _spectrum_call(filt, fc, fs, nblk, tc):
    l = filt.shape[0]
    b = l // nblk
    nct = HYENA_W // tc
    nlag = 2 * nblk - 1
    return pl.pallas_call(
        functools.partial(_spectrum_body, nblk=nblk),
        grid=(HYENA_ORDER, nct),
        in_specs=[
            pl.BlockSpec((l, tc), lambda o, c: (0, 2 * nct * o + c)),
            pl.BlockSpec((l, tc), lambda o, c: (0, 2 * nct * o + nct + c)),
            _const_spec((b, b)),
            _const_spec((b, b)),
        ],
        out_specs=[
            pl.BlockSpec((1, nlag, b, tc), lambda o, c: (o, 0, 0, c)),
            pl.BlockSpec((1, nlag, b, tc), lambda o, c: (o, 0, 0, c)),
            pl.BlockSpec((1, nlag, 1, tc), lambda o, c: (o, 0, 0, c)),
        ],
        out_shape=[
            jax.ShapeDtypeStruct((HYENA_ORDER, nlag, b, HYENA_W), BF16),
            jax.ShapeDtypeStruct((HYENA_ORDER, nlag, b, HYENA_W), BF16),
            jax.ShapeDtypeStruct((HYENA_ORDER, nlag, 1, HYENA_W), F32),
        ],
        compiler_params=_params("arbitrary", "arbitrary"),
        name="hyena_spectrum",
    )(filt, filt, fc, fs)


def _conv3(x, w_ref, b_ref, row):
    l = x.shape[0]
    xm = jnp.where(row == 0, 0.0, pltpu.roll(x, 1, 0))
    xp = jnp.where(row == l - 1, 0.0, pltpu.roll(x, l - 1, 0))
    return xm * w_ref[0:1, :] + x * w_ref[1:2, :] + xp * w_ref[2:3, :] + b_ref[...]


def _fftconv_body(*refs, conv_u, nblk):
    it = iter(refs)
    u_ref = next(it)
    if conv_u:
        uw_ref, ub_ref = next(it), next(it)
    g_ref, gw_ref, gb_ref = next(it), next(it), next(it)
    ka_ref, kb_ref, kn_ref, d_ref, fc_ref, fs_ref, o_ref = (next(it) for _ in range(7))

    l = u_ref.shape[1]
    b = l // nblk
    row = lax.broadcasted_iota(jnp.int32, (l, 1), 0)
    _, alt = _alternating(b)
    u = u_ref[0].astype(F32)
    if conv_u:
        u = _conv3(u, uw_ref, ub_ref, row)
    gate = _conv3(g_ref[0].astype(F32), gw_ref, gb_ref, row)
    fc, fs = fc_ref[...], fs_ref[...]

    ps, qs, ns = [], [], []
    for j in range(nblk):
        uj = u[j * b:(j + 1) * b]
        uj16 = uj.astype(BF16)
        ps.append(jnp.dot(fc, uj16, preferred_element_type=F32).astype(BF16))
        qs.append(jnp.dot(fs, uj16, preferred_element_type=F32).astype(BF16))
        ns.append(jnp.sum(uj * alt, axis=0, keepdims=True))
    for i in range(nblk):
        r = t = nyq = None
        for j in range(nblk):
            lag = i - j + nblk - 1
            ka, kb = ka_ref[0, lag], kb_ref[0, lag]
            dr = ps[j] * ka + qs[j] * kb
            dt = qs[j] * ka - ps[j] * kb
            dn = ns[j] * kn_ref[0, lag]
            r, t, nyq = (dr, dt, dn) if j == 0 else (r + dr, t + dt, nyq + dn)
        y = jnp.dot(fc, r, preferred_element_type=F32) + jnp.dot(fs, t, preferred_element_type=F32)
        rows = slice(i * b, (i + 1) * b)
        y = y + alt * nyq + u[rows] * d_ref[0]
        o_ref[0, rows, :] = (gate[rows] * y).astype(o_ref.dtype)


def _hyena_body(v_ref, x1_ref, x2_ref, vw_ref, vb_ref, w1_ref, b1_ref, w2_ref, b2_ref,
                ka_ref, kb_ref, kn_ref, d_ref, fc_ref, fs_ref, o_ref, *, nblk):
    l = v_ref.shape[1]
    b = l // nblk
    row = lax.broadcasted_iota(jnp.int32, (l, 1), 0)
    _, alt = _alternating(b)
    fc, fs = fc_ref[...], fs_ref[...]

    def long_conv(u, gate, order):
        ps, qs, ns = [], [], []
        for j in range(nblk):
            uj = u[j * b:(j + 1) * b]
            uj16 = uj.astype(BF16)
            ps.append(jnp.dot(fc, uj16, preferred_element_type=F32).astype(BF16))
            qs.append(jnp.dot(fs, uj16, preferred_element_type=F32).astype(BF16))
            ns.append(jnp.sum(uj * alt, axis=0, keepdims=True))
        out = []
        for i in range(nblk):
            r = t = nyq = None
            for j in range(nblk):
                lag = i - j + nblk - 1
                ka, kb = ka_ref[order, lag], kb_ref[order, lag]
                dr = ps[j] * ka + qs[j] * kb
                dt = qs[j] * ka - ps[j] * kb
                dn = ns[j] * kn_ref[order, lag]
                r, t, nyq = (dr, dt, dn) if j == 0 else (r + dr, t + dt, nyq + dn)
            y = jnp.dot(fc, r, preferred_element_type=F32) + jnp.dot(fs, t, preferred_element_type=F32)
            rows = slice(i * b, (i + 1) * b)
            out.append(gate[rows] * (y + alt * nyq + u[rows] * d_ref[order]))
        return jnp.concatenate(out, axis=0)

    v = _conv3(v_ref[0].astype(F32), vw_ref, vb_ref, row)
    z1 = long_conv(v, _conv3(x1_ref[0].astype(F32), w1_ref, b1_ref, row), 0)
    o_ref[0] = long_conv(z1, _conv3(x2_ref[0].astype(F32), w2_ref, b2_ref, row), 1).astype(o_ref.dtype)


def _hyena_call(z, conv_w, conv_b, spectra, d_skip, fc, fs, nblk, tc):
    b, l, _ = z.shape
    nct = HYENA_W // tc
    ka, kb, kn = spectra
    blk = l // nblk
    nlag = 2 * nblk - 1
    col = lambda k: pl.BlockSpec((1, l, tc), lambda c, bi: (bi, 0, HY_OFF // tc + k * nct + c))
    wsp = lambda k: pl.BlockSpec((3, tc), lambda c, bi: (0, k * nct + c))
    bsp = lambda k: pl.BlockSpec((1, tc), lambda c, bi: (0, k * nct + c))
    spec = lambda rows: pl.BlockSpec((HYENA_ORDER, nlag, rows, tc), lambda c, bi: (0, 0, 0, c),
                                     pipeline_mode=pl.Buffered(1))
    return pl.pallas_call(
        functools.partial(_hyena_body, nblk=nblk),
        grid=(nct, b),
        in_specs=[col(0), col(1), col(2), wsp(0), bsp(0), wsp(1), bsp(1), wsp(2), bsp(2),
                  spec(blk), spec(blk), spec(1),
                  pl.BlockSpec((HYENA_ORDER, 1, tc), lambda c, bi: (0, 0, c), pipeline_mode=pl.Buffered(1)),
                  _const_spec((blk, blk)), _const_spec((blk, blk))],
        out_specs=pl.BlockSpec((1, l, tc), lambda c, bi: (bi, 0, c)),
        out_shape=jax.ShapeDtypeStruct((b, l, HYENA_W), BF16),
        compiler_params=_params("arbitrary", "arbitrary"),
        name="hyena_orders",
    )(z, z, z, conv_w, conv_b, conv_w, conv_b, conv_w, conv_b, ka, kb, kn, d_skip, fc, fs)


def _fftconv_call(u, u_col0, z, gate_col0, conv_w, conv_b, spectra, d_skip, order, fc, fs, nblk, tc, out_dtype):
    b, l, _ = z.shape
    conv_u = u is z
    nct = HYENA_W // tc
    ka, kb, kn = spectra
    blk = l // nblk
    nlag = 2 * nblk - 1
    col = lambda c0: (lambda c, bi: (bi, 0, c0 // tc + c))
    wcol = lambda c0: (lambda c, bi: (0, (c0 - HY_OFF) // tc + c))
    in_specs = [pl.BlockSpec((1, l, tc), col(u_col0))]
    args = [u]
    if conv_u:
        in_specs += [pl.BlockSpec((3, tc), wcol(u_col0)), pl.BlockSpec((1, tc), wcol(u_col0))]
        args += [conv_w, conv_b]
    in_specs += [pl.BlockSpec((1, l, tc), col(gate_col0)),
                 pl.BlockSpec((3, tc), wcol(gate_col0)), pl.BlockSpec((1, tc), wcol(gate_col0))]
    args += [z, conv_w, conv_b]
    spec = lambda rows: pl.BlockSpec((1, nlag, rows, tc), lambda c, bi: (order, 0, 0, c),
                                     pipeline_mode=pl.Buffered(1))
    in_specs += [spec(blk), spec(blk), spec(1),
                 pl.BlockSpec((1, 1, tc), lambda c, bi: (order, 0, c), pipeline_mode=pl.Buffered(1)),
                 _const_spec((blk, blk)), _const_spec((blk, blk))]
    args += [ka, kb, kn, d_skip, fc, fs]
    return pl.pallas_call(
        functools.partial(_fftconv_body, conv_u=conv_u, nblk=nblk),
        grid=(nct, b),
        in_specs=in_specs,
        out_specs=pl.BlockSpec((1, l, tc), lambda c, bi: (bi, 0, c)),
        out_shape=jax.ShapeDtypeStruct((b, l, HYENA_W), out_dtype),
        compiler_params=_params("arbitrary", "arbitrary"),
        name="hyena_conv",
    )(*args)


def _pool_body(x_ref, w_ref, s_ref, o_ref):
    l = x_ref.shape[1]
    row = lax.broadcasted_iota(jnp.int32, (l, 1), 0)
    for g, win in enumerate(POOL_WINDOWS):
        half = win // 2
        sl = slice(g * POOL_GROUP, (g + 1) * POOL_GROUP)
        x = x_ref[0, :, sl].astype(F32)

        def shifted(a, k):
            return jnp.where((row >= k) & (row < l + k), pltpu.roll(a, k % l, 0), 0.0)

        back = fwd = x
        span = 1
        while span < half:
            back = back + shifted(back, span)
            fwd = fwd + shifted(fwd, -span)
            span *= 2
        acc = shifted(back, 1) + fwd
        cnt = (jnp.minimum(row + half, l) - jnp.maximum(row - half, 0)).astype(F32)
        d = acc / cnt - x
        y = jnp.dot(d.astype(BF16), w_ref[g], preferred_element_type=F32)
        o_ref[0, :, sl] = (y * s_ref[:, sl]).astype(o_ref.dtype)


def _pool_call(z, w_grp, scale):
    b, l, _ = z.shape
    ng = len(POOL_WINDOWS)
    return pl.pallas_call(
        _pool_body,
        grid=(b,),
        in_specs=[
            pl.BlockSpec((1, l, POOL_W), lambda bi: (bi, 0, POOL_OFF // POOL_W)),
            pl.BlockSpec((ng, POOL_GROUP, POOL_GROUP), lambda bi: (0, 0, 0)),
            pl.BlockSpec((1, POOL_W), lambda bi: (0, 0)),
        ],
        out_specs=pl.BlockSpec((1, l, POOL_W), lambda bi: (bi, 0, 0)),
        out_shape=jax.ShapeDtypeStruct((b, l, POOL_W), BF16),
        compiler_params=_params("arbitrary"),
        name="pool",
    )(z, w_grp, scale.reshape(1, POOL_W))


def _merge_body(ya_ref, yh_ref, yp_ref, gt_ref, x_ref, ga_ref, g2_ref, sc_ref, sh_ref,
                wa_ref, wh_ref, wp_ref, wo_ref, xn_ref, h2_ref):
    d = x_ref.shape[1]
    cj = 512
    ya, yh, yp = ya_ref[...], yh_ref[...], yp_ref[...]
    acc = jnp.zeros(x_ref.shape, F32)
    for j in range(d // cj):
        sl = slice(j * cj, (j + 1) * cj)
        gate = lambda br: gt_ref[:, br * d + j * cj:br * d + (j + 1) * cj].astype(F32)
        m = (gate(0) * jnp.dot(ya, wa_ref[:, sl], preferred_element_type=F32)
             + gate(1) * jnp.dot(yh, wh_ref[:, sl], preferred_element_type=F32)
             + gate(2) * jnp.dot(yp, wp_ref[:, sl], preferred_element_type=F32))
        acc = acc + jnp.dot(m.astype(BF16), wo_ref[sl, :], preferred_element_type=F32)
    xn = x_ref[...] + ga_ref[0] * acc
    xn_ref[...] = xn
    h2_ref[...] = _norm_mod(xn, g2_ref[...], sc_ref[0], sh_ref[0]).astype(h2_ref.dtype)


def _merge_call(ya, yh, yp, gates, x2, ga1, g2, sc2, sh2, wa, wh, wp, wo, layer, rows_per_batch):
    m, d = x2.shape
    tm = min(512, rows_per_batch)
    rpt = rows_per_batch // tm
    rows = lambda w: pl.BlockSpec((tm, w), lambda i: (i, 0))
    weight = lambda w: pl.BlockSpec((None,) + w.shape[1:], lambda i: (layer, 0, 0), pipeline_mode=pl.Buffered(1))
    return pl.pallas_call(
        _merge_body,
        grid=(m // tm,),
        in_specs=[rows(ATTN_W), rows(HYENA_W), rows(POOL_W), rows(N_BRANCH * d), rows(d),
                  _mod_spec(ga1, rpt), pl.BlockSpec((1, d), lambda i: (0, 0)),
                  _mod_spec(sc2, rpt), _mod_spec(sh2, rpt),
                  weight(wa), weight(wh), weight(wp), weight(wo)],
        out_specs=[rows(d), rows(d)],
        out_shape=[jax.ShapeDtypeStruct((m, d), F32), jax.ShapeDtypeStruct((m, d), BF16)],
        compiler_params=_params("arbitrary"),
        name="merge",
    )(ya, yh, yp, gates, x2, ga1, g2.reshape(1, d), sc2, sh2, wa, wh, wp, wo)


def _mlp_body(*refs, has_next):
    it = iter(refs)
    h_ref, w1_ref, w2_ref, x_ref, ga_ref = (next(it) for _ in range(5))
    if has_next:
        gn_ref, sc_ref, sh_ref = next(it), next(it), next(it)
    o_ref = next(it)
    hn_ref = next(it) if has_next else None
    xs_ref = next(it)

    f = pl.program_id(1)
    last = pl.num_programs(1) - 1
    tm, d = o_ref.shape
    cn = 512
    rb = min(256, tm)

    xs_ref[f] = x_ref[...]

    def hidden(rows):
        a = jnp.dot(h_ref[rows, :], w1_ref[...], preferred_element_type=F32)
        return jnp.square(jnp.maximum(a, 0.0)).astype(BF16)

    @pl.when(f == 0)
    def _():
        a = hidden(slice(None))
        for n0 in range(0, d, cn):
            o_ref[:, n0:n0 + cn] = jnp.dot(a, w2_ref[:, n0:n0 + cn], preferred_element_type=F32)

    @pl.when((f > 0) & (f < last))
    def _():
        a = hidden(slice(None))
        for n0 in range(0, d, cn):
            o_ref[:, n0:n0 + cn] += jnp.dot(a, w2_ref[:, n0:n0 + cn], preferred_element_type=F32)

    @pl.when(f == last)
    def _():
        for r0 in range(0, tm, rb):
            rows = slice(r0, r0 + rb)
            acc = o_ref[rows, :] + jnp.dot(hidden(rows), w2_ref[...], preferred_element_type=F32)
            x_rows = jnp.concatenate([xs_ref[c, rows, :] for c in range(xs_ref.shape[0])], axis=1)
            xo = x_rows + ga_ref[0] * acc
            o_ref[rows, :] = xo
            if has_next:
                hn_ref[rows, :] = _norm_mod(xo, gn_ref[...], sc_ref[0], sh_ref[0]).astype(hn_ref.dtype)


def _mlp_call(h2, w1, w2, layer, xn, ga2, nxt, rows_per_batch):
    m, d = xn.shape
    ff = w1.shape[2]
    tm, tf = min(1024, rows_per_batch if ga2.shape[0] > 1 else m), 512
    rpt = max(rows_per_batch // tm, 1)
    nf = ff // tf
    assert nf >= 2
    xw = d // nf
    assert xw % V7X_LANES == 0
    has_next = nxt is not None
    rows = pl.BlockSpec((tm, d), lambda i, f: (i, 0))
    in_specs = [rows, pl.BlockSpec((None, d, tf), lambda i, f: (layer, 0, f)),
                pl.BlockSpec((None, tf, d), lambda i, f: (layer, f, 0)),
                pl.BlockSpec((tm, xw), lambda i, f: (i, f)), _mod_spec(ga2, rpt)]
    args = [h2, w1, w2, xn, ga2]
    out_specs = [rows]
    out_shape = [jax.ShapeDtypeStruct((m, d), F32)]
    if has_next:
        gn, scn, shn = nxt
        in_specs += [pl.BlockSpec((1, d), lambda i, f: (0, 0)), _mod_spec(scn, rpt), _mod_spec(shn, rpt)]
        args += [gn.reshape(1, d), scn, shn]
        out_specs.append(rows)
        out_shape.append(jax.ShapeDtypeStruct((m, d), BF16))
    outs = pl.pallas_call(
        functools.partial(_mlp_body, has_next=has_next),
        grid=(m // tm, nf),
        in_specs=in_specs,
        out_specs=out_specs,
        out_shape=out_shape,
        scratch_shapes=[pltpu.VMEM((nf, tm, xw), F32)],
        compiler_params=_params("arbitrary", "arbitrary"),
        name="mlp",
    )(*args)
    return (outs[0], outs[1]) if has_next else (outs[0], None)


def _mixers(z, q, kv, kvx, sink, local, hy, pool_w, pool_scale):
    b, l, _ = z.shape
    y_att = _attn_call(q, kv, kvx, sink, local)
    conv_w, conv_b, filt_params, d_skip, (fc, fs) = hy
    nblk = _hyena_blocks(l)
    tc = 256
    spectra = _spectrum_call(_filter_call(l, *filt_params), fc, fs, nblk, tc)
    y_hy = _hyena_call(z, conv_w, conv_b, spectra, d_skip, fc, fs, nblk, tc)
    y_pool = _pool_call(z, pool_w, pool_scale)
    return (y_att.reshape(b * l, ATTN_W), y_hy.reshape(b * l, HYENA_W), y_pool.reshape(b * l, POOL_W))


def kernel(x, c, ctx, c_ctx, norm1_g, norm2_g, w_mod, b_mod, w_in, q_norm_g, k_norm_g, sink, hy_conv_w, hy_conv_b, filt_w0, filt_b0, filt_w1, filt_b1, filt_freq, filt_w2, hy_bias, pool_w, pool_scale, w_att_o, w_hy_o, w_pool_o, w_out, mlp_w1, mlp_w2):
    b, l, d = x.shape
    lc = ctx.shape[1]
    depth = w_mod.shape[0]

    cc = jnp.concatenate([c, c_ctx[None, :], jnp.zeros((MOD_ROWS - b - 1, d), F32)], axis=0)
    mods = _modulation(cc, w_mod, b_mod)

    def chunks(layer, lo, hi):
        return [mods[layer, lo:hi, i * d:(i + 1) * d].reshape(hi - lo, 1, d) for i in range(6)]

    as_bf16 = lambda w: w.astype(BF16)
    w_in_b, w_att_b, w_hy_b, w_pool_b, w_out_b = map(as_bf16, (w_in, w_att_o, w_hy_o, w_pool_o, w_out))
    w1_b, w2_b, pool_w_b = map(as_bf16, (mlp_w1, mlp_w2, pool_w))

    rope_tabs = _rope_tables(l)
    dft_x = _dft_matrices(l // _hyena_blocks(l))
    dft_c = _dft_matrices(lc // _hyena_blocks(lc))

    x2 = x.reshape(b * l, d)
    c2 = ctx.reshape(b * lc, d)
    sh1, sc1 = chunks(0, 0, b)[:2]
    csh1, csc1 = chunks(0, b, b + 1)[:2]
    hx = _norm_call(x2, norm1_g[0], sc1, sh1, l)
    hc = _norm_call(c2, norm1_g[0], csc1, csh1, lc)

    for layer in range(depth):
        last = layer == depth - 1
        _, _, ga1, sh2, sc2, ga2 = chunks(layer, 0, b)
        _, _, cga1, csh2, csc2, cga2 = chunks(layer, b, b + 1)
        filt_params = (filt_w0[layer], filt_b0[layer], filt_w1[layer], filt_b1[layer], filt_freq[layer],
                       filt_w2[layer])
        conv_b = hy_conv_b[layer].reshape(1, -1)
        d_skip = hy_bias[layer].reshape(HYENA_ORDER, 1, HYENA_W)
        merge_w = (w_att_b, w_hy_b, w_pool_b, w_out_b, layer)

        gq, gk = q_norm_g[layer], k_norm_g[layer]
        if last:
            kvc = _kv_call(_proj_call(hc, w_in_b, layer, K_OFF, 2 * KV_W, BF16), gk)
        else:
            zc = _proj_call(hc, w_in_b, layer, 0, GATE_OFF, BF16)
            gc, qc, kvc = _gates_qkv_call(hc, w_in_b, layer, zc, gq, gk, None, lc)
            zc, qc = zc.reshape(b, lc, GATE_OFF), qc.reshape(b, lc, -1)
        kvc = kvc.reshape(b, lc, -1)

        zx = _proj_call(hx, w_in_b, layer, 0, GATE_OFF, BF16)
        gx, qx, kvx = _gates_qkv_call(hx, w_in_b, layer, zx, gq, gk, rope_tabs, l)
        zx, qx, kvx = zx.reshape(b, l, GATE_OFF), qx.reshape(b, l, -1), kvx.reshape(b, l, -1)
        hy = (hy_conv_w[layer], conv_b, filt_params, d_skip, dft_x)
        ya, yh, yp = _mixers(zx, qx, kvx, kvc, sink[layer], True, hy, pool_w_b[layer], pool_scale[layer])
        xn, h2 = _merge_call(ya, yh, yp, gx, x2, ga1, norm2_g[layer], sc2, sh2, *merge_w, l)
        nxt = None if last else (norm1_g[layer + 1], *reversed(chunks(layer + 1, 0, b)[:2]))
        x2, hx = _mlp_call(h2, w1_b, w2_b, layer, xn, ga2, nxt, l)

        if not last:
            hyc = (hy_conv_w[layer], conv_b, filt_params, d_skip, dft_c)
            ya, yh, yp = _mixers(zc, qc, None, kvc, sink[layer], False, hyc, pool_w_b[layer], pool_scale[layer])
            cn, h2c = _merge_call(ya, yh, yp, gc, c2, cga1, norm2_g[layer], csc2, csh2, *merge_w, lc)
            nxt = (norm1_g[layer + 1], *reversed(chunks(layer + 1, b, b + 1)[:2]))
            c2, hc = _mlp_call(h2c, w1_b, w2_b, layer, cn, cga2, nxt, lc)

    return x2.reshape(b, l, d)
```
